```python
import math
import jax
import jax.numpy as jnp
from jax import lax
import numpy as np

D_MODEL = 1024
BATCH = 32
SEQ = 256
DEPTH = 4
DEC_BATCH = 2
DEC_SEQ = 1024
PAST_LEN = 256

GRID_W = 64
HEAD_DIM = 64
N_EVEN = (DEPTH + 1) // 2
N_ODD = DEPTH // 2
N_MOD = 6

A_WIDTH = D_MODEL // 2
A_GROUP_CH = 16
A_GROUPS = A_WIDTH // A_GROUP_CH
A_STATE = 64
DT_MIN = 1e-3
DT_MAX = 1e-1

B_HEADS = (D_MODEL // 2) // HEAD_DIM
B_WIDTH = B_HEADS * HEAD_DIM
WIN_R_MAX = 8
WIN_C = 16
KB_W = 2 * WIN_C

C_HEADS = D_MODEL // HEAD_DIM
C_KV_HEADS = C_HEADS // 8
C_GROUP = C_HEADS // C_KV_HEADS
WINDOW = 128
W_BLOCK = 128
Q_BLOCK = 128
ROPE_BASE = 10000.0

N_EXPERTS = 32
TOP_K = 4
D_EXPERT = D_MODEL
SWIGLU_LIMIT = 7.0
SWIGLU_ALPHA = 1.702
MOE_BLOCK = 128

EPS = 1e-6
NEG_INF = -1e30

kernel_name = 'hybrid_s5_natten_swa_moe_diffusion_step'


def rms_norm(x, g):
    xf = x.astype(jnp.float32)
    y = xf * lax.rsqrt(jnp.mean(xf * xf, axis=-1, keepdims=True) + EPS)
    return (y * g.astype(jnp.float32)).astype(x.dtype)


def modulate(h, shift, scale):
    return h * (1 + scale) + shift


def softmax_with_sink(s, sink):
    if sink is None:
        return jax.nn.softmax(s, axis=-1)
    sink = sink.astype(jnp.float32)
    m = jnp.maximum(jnp.max(s, axis=-1, keepdims=True), sink)
    e = jnp.exp(s - m)
    return e / (jnp.sum(e, axis=-1, keepdims=True) + jnp.exp(sink - m))


def axial_rope(x):
    bsz, L, nh, dh = x.shape
    nf = dh // 4
    inv = ROPE_BASE ** (-jnp.arange(nf, dtype=jnp.float32) / nf)
    t = jnp.arange(L)
    pos = jnp.stack([t // GRID_W, t % GRID_W], axis=-1).astype(jnp.float32)
    ang = pos[:, None, :, None] * inv
    cos, sin = jnp.cos(ang), jnp.sin(ang)
    xr = x.astype(jnp.float32).reshape(bsz, L, nh, 2, 2, nf)
    x1, x2 = xr[..., 0, :], xr[..., 1, :]
    out = jnp.stack([x1 * cos - x2 * sin, x2 * cos + x1 * sin], axis=-2)
    return out.reshape(x.shape).astype(x.dtype)


def block_dense_attention(q, k, v, sink=None):
    bsz, lq, kvh, grp, dh = q.shape
    scale = dh ** -0.5
    qb = q.reshape(bsz, lq // Q_BLOCK, Q_BLOCK, kvh, grp, dh).transpose(1, 0, 2, 3, 4, 5)
    sink_b = None if sink is None else sink[None, :, :, None, None]

    def one_block(qblk):
        s = jnp.einsum('bqkgd,bmkd->bkgqm', qblk, k, preferred_element_type=jnp.float32) * scale
        p = softmax_with_sink(s, sink_b)
        return jnp.einsum('bkgqm,bmkd->bqkgd', p.astype(v.dtype), v)

    out = lax.map(one_block, qb)
    return out.transpose(1, 0, 2, 3, 4, 5).reshape(bsz, lq, kvh, grp, dh)


def neighborhood_attention(q, k, v, k_ctx, v_ctx, rpb):
    bsz, L, nh, dh = q.shape
    rows = L // GRID_W
    wr = min(WIN_R_MAX, rows)
    ncb = GRID_W // WIN_C
    scale = dh ** -0.5
    r = jnp.arange(rows)
    ri = jnp.clip(r - wr // 2, 0, rows - wr)[:, None] + jnp.arange(wr)[None, :]
    jc = jnp.arange(ncb)
    ci = jnp.clip(jc * WIN_C - WIN_C // 2, 0, GRID_W - KB_W)[:, None] + jnp.arange(KB_W)[None, :]
    qcol = jc[:, None] * WIN_C + jnp.arange(WIN_C)[None, :]
    cs = jnp.clip(qcol - WIN_C // 2, 0, GRID_W - WIN_C)
    col_ok = (ci[:, None, :] >= cs[:, :, None]) & (ci[:, None, :] < cs[:, :, None] + WIN_C)
    dr_idx = ri - r[:, None] + (WIN_R_MAX - 1)
    dc_idx = jnp.clip(ci[:, None, :] - qcol[:, :, None] + (WIN_C - 1), 0, 2 * WIN_C - 2)
    bias = rpb[:, dr_idx[:, None, None, :, None], dc_idx[None, :, :, None, :]].astype(jnp.float32)
    gr = ri[:, None, :, None]
    gc = ci[None, :, None, :]
    kg = k.reshape(bsz, rows, GRID_W, nh, dh)[:, gr, gc]
    vg = v.reshape(bsz, rows, GRID_W, nh, dh)[:, gr, gc]
    qb = q.reshape(bsz, rows, ncb, WIN_C, nh, dh)
    s_nb = jnp.einsum('brjqhd,brjmnhd->bhrjqmn', qb, kg, preferred_element_type=jnp.float32) * scale + bias[None]
    s_nb = jnp.where(col_ok[:, :, None, :], s_nb, NEG_INF)
    n_nb = wr * KB_W
    s_nb = s_nb.reshape(bsz, nh, rows, ncb, WIN_C, n_nb)
    s_ctx = jnp.einsum('brjqhd,bmhd->bhrjqm', qb, k_ctx, preferred_element_type=jnp.float32) * scale
    p = jax.nn.softmax(jnp.concatenate([s_nb, s_ctx], axis=-1), axis=-1).astype(v.dtype)
    p_nb = p[..., :n_nb].reshape(bsz, nh, rows, ncb, WIN_C, wr, KB_W)
    o = (jnp.einsum('bhrjqmn,brjmnhd->brjqhd', p_nb, vg)
         + jnp.einsum('bhrjqm,bmhd->brjqhd', p[..., n_nb:], v_ctx))
    return o.reshape(bsz, L, nh, dh)


def window_attention(q, k, v, k_ctx, v_ctx, sink):
    bsz, L, kvh, grp, dh = q.shape
    nb = L // W_BLOCK
    scale = dh ** -0.5
    kidx = jnp.arange(nb)[:, None] * W_BLOCK + jnp.arange(3 * W_BLOCK)[None, :]
    pad = ((0, 0), (W_BLOCK, W_BLOCK), (0, 0), (0, 0))
    kb = jnp.pad(k, pad)[:, kidx]
    vb = jnp.pad(v, pad)[:, kidx]
    qpos = jnp.arange(nb)[:, None] * W_BLOCK + jnp.arange(W_BLOCK)[None, :]
    kpos = (kidx - W_BLOCK)[:, None, :]
    ok = (jnp.abs(qpos[:, :, None] - kpos) <= WINDOW) & (kpos >= 0) & (kpos < L)
    qb = q.reshape(bsz, nb, W_BLOCK, kvh, grp, dh)
    s_win = jnp.einsum('bnqkgd,bnmkd->bkgnqm', qb, kb, preferred_element_type=jnp.float32) * scale
    s_win = jnp.where(ok, s_win, NEG_INF)
    s_ctx = jnp.einsum('bnqkgd,bmkd->bkgnqm', qb, k_ctx, preferred_element_type=jnp.float32) * scale
    p = softmax_with_sink(jnp.concatenate([s_win, s_ctx], axis=-1), sink[None, :, :, None, None, None]).astype(v.dtype)
    n_win = 3 * W_BLOCK
    o = (jnp.einsum('bkgnqm,bnmkd->bnqkgd', p[..., :n_win], vb)
         + jnp.einsum('bkgnqm,bmkd->bnqkgd', p[..., n_win:], v_ctx))
    return o.reshape(bsz, L, kvh, grp, dh)


def _cmul(ar, ai, br, bi):
    return ar * br - ai * bi, ar * bi + ai * br


def s5_direction(u, lam_re, lam_im, log_dt, b_re, b_im, c_re, c_im, s0):
    f32 = jnp.float32
    lr, li = lam_re.astype(f32), lam_im.astype(f32)
    dt = jnp.exp(log_dt.astype(f32))[:, None]
    mag = jnp.exp(lr * dt)
    ab_re, ab_im = mag * jnp.cos(li * dt), mag * jnp.sin(li * dt)
    den = lr * lr + li * li
    nr, ni = ab_re - 1.0, ab_im
    f_re = (nr * lr + ni * li) / den
    f_im = (ni * lr - nr * li) / den
    br, bi = b_re.astype(f32), b_im.astype(f32)
    bb_re = f_re[..., None] * br - f_im[..., None] * bi
    bb_im = f_re[..., None] * bi + f_im[..., None] * br
    x_re = jnp.einsum('blgh,gph->blgp', u, bb_re)
    x_im = jnp.einsum('blgh,gph->blgp', u, bb_im)
    a_re = jnp.broadcast_to(ab_re, x_re.shape)
    a_im = jnp.broadcast_to(ab_im, x_im.shape)

    def combine(e1, e2):
        a1r, a1i, b1r, b1i = e1
        a2r, a2i, b2r, b2i = e2
        ar, ai = _cmul(a2r, a2i, a1r, a1i)
        cr, ci = _cmul(a2r, a2i, b1r, b1i)
        return ar, ai, cr + b2r, ci + b2i

    a_cum_re, a_cum_im, s_re, s_im = lax.associative_scan(combine, (a_re, a_im, x_re, x_im), axis=1)
    if s0 is not None:
        ir, ii = _cmul(a_cum_re, a_cum_im, s0[0][:, None], s0[1][:, None])
        s_re, s_im = s_re + ir, s_im + ii
    y = (jnp.einsum('blgp,ghp->blgh', s_re, c_re.astype(f32))
         - jnp.einsum('blgp,ghp->blgh', s_im, c_im.astype(f32)))
    return y, s_re[:, -1], s_im[:, -1]


def s5_mixer(u, lam_re, lam_im, log_dt, b_re, b_im, c_re, c_im, d_skip, glu_w, glu_b, state0):
    f32 = jnp.float32
    bsz, L, _ = u.shape
    uf = u.astype(f32).reshape(bsz, L, A_GROUPS, A_GROUP_CH)
    ys, finals = [], []
    for dr in range(2):
        seq = uf if dr == 0 else uf[:, ::-1]
        s0 = None if state0 is None else (state0[:, dr, 0].astype(f32), state0[:, dr, 1].astype(f32))
        y, fr, fi = s5_direction(seq, lam_re[dr], lam_im[dr], log_dt[dr], b_re[dr], b_im[dr], c_re[dr], c_im[dr], s0)
        ys.append(y if dr == 0 else y[:, ::-1])
        finals.append(jnp.stack([fr, fi], axis=1))
    y = (ys[0] + ys[1]).reshape(bsz, L, A_WIDTH) + d_skip.astype(f32) * uf.reshape(bsz, L, A_WIDTH)
    g = jax.nn.gelu(y)
    out = g * jax.nn.sigmoid(g @ glu_w.astype(f32) + glu_b.astype(f32))
    return out.astype(u.dtype), jnp.stack(finals, axis=1).astype(u.dtype)


def even_mixer(h, w_in, w_out, lam_re, lam_im, log_dt, b_re, b_im, c_re, c_im, d_skip, glu_w, glu_b,
               qn, kn, rpb, ssm_state0, k_ctx, v_ctx):
    bsz, L, _ = h.shape
    z = h @ w_in
    u = z[..., :A_WIDTH]
    q, k, v = jnp.split(z[..., A_WIDTH:], 3, axis=-1)
    q = rms_norm(q.reshape(bsz, L, B_HEADS, HEAD_DIM), qn)
    k = rms_norm(k.reshape(bsz, L, B_HEADS, HEAD_DIM), kn)
    v = v.reshape(bsz, L, B_HEADS, HEAD_DIM)
    a_out, ssm_final = s5_mixer(u, lam_re, lam_im, log_dt, b_re, b_im, c_re, c_im, d_skip, glu_w, glu_b, ssm_state0)
    if k_ctx is None:
        b_out = block_dense_attention(q[:, :, :, None], k, v)[:, :, :, 0]
        cache = (ssm_final, k, v)
    else:
        b_out = neighborhood_attention(q, k, v, k_ctx, v_ctx, rpb)
        cache = None
    y = jnp.concatenate([a_out, b_out.reshape(bsz, L, B_WIDTH)], axis=-1) @ w_out
    return y, cache


def odd_mixer(h, w_qkv, w_o, qn, kn, sink, k_ctx, v_ctx):
    bsz, L, _ = h.shape
    nq, nkv = C_HEADS * HEAD_DIM, C_KV_HEADS * HEAD_DIM
    z = h @ w_qkv
    q = rms_norm(z[..., :nq].reshape(bsz, L, C_HEADS, HEAD_DIM), qn)
    k = rms_norm(z[..., nq:nq + nkv].reshape(bsz, L, C_KV_HEADS, HEAD_DIM), kn)
    v = z[..., nq + nkv:].reshape(bsz, L, C_KV_HEADS, HEAD_DIM)
    sink_g = sink.reshape(C_KV_HEADS, C_GROUP)
    if k_ctx is None:
        o = block_dense_attention(q.reshape(bsz, L, C_KV_HEADS, C_GROUP, HEAD_DIM), k, v, sink_g)
        cache = (k, v)
    else:
        q = axial_rope(q)
        k = axial_rope(k)
        o = window_attention(q.reshape(bsz, L, C_KV_HEADS, C_GROUP, HEAD_DIM), k, v, k_ctx, v_ctx, sink_g)
        cache = None
    return o.reshape(bsz, L, nq) @ w_o, cache


def moe_ffn(h, router_w, router_b, w_gu, b_gu, w_down, b_down):
    x = h.reshape(-1, D_MODEL)
    n_tok = x.shape[0]
    logits = (x @ router_w + router_b).astype(jnp.float32)
    top_val, top_idx = lax.top_k(logits, TOP_K)
    gates = jax.nn.softmax(top_val, axis=-1)
    n_asg = n_tok * TOP_K
    n_blocks = -(-n_asg // MOE_BLOCK) + N_EXPERTS
    n_rows = n_blocks * MOE_BLOCK
    flat_e = top_idx.reshape(n_asg)
    order = jnp.argsort(flat_e)
    sorted_e = flat_e[order]
    sorted_tok = (order // TOP_K).astype(jnp.int32)
    sorted_gate = gates.reshape(n_asg)[order]
    counts = jnp.bincount(flat_e, length=N_EXPERTS)
    start = jnp.cumsum(counts) - counts
    pad_counts = (counts + MOE_BLOCK - 1) // MOE_BLOCK * MOE_BLOCK
    pad_end = jnp.cumsum(pad_counts)
    pad_start = pad_end - pad_counts
    dest = pad_start[sorted_e] + jnp.arange(n_asg) - start[sorted_e]
    row_tok = jnp.zeros((n_rows,), jnp.int32).at[dest].set(sorted_tok)
    row_gate = jnp.zeros((n_rows,), jnp.float32).at[dest].set(sorted_gate)
    block_e = jnp.minimum(jnp.searchsorted(pad_end, jnp.arange(n_blocks) * MOE_BLOCK, side='right'), N_EXPERTS - 1)
    xb = x[row_tok].reshape(n_blocks, MOE_BLOCK, D_MODEL)

    def expert_block(args):
        xblk, e = args
        gu = xblk @ w_gu[e] + b_gu[e]
        g = jnp.minimum(gu[:, :D_EXPERT], SWIGLU_LIMIT)
        lin = jnp.clip(gu[:, D_EXPERT:], -SWIGLU_LIMIT, SWIGLU_LIMIT)
        act = g * jax.nn.sigmoid(SWIGLU_ALPHA * g) * (lin + 1)
        return act @ w_down[e] + b_down[e]

    y_rows = lax.map(expert_block, (xb, block_e)).reshape(n_rows, D_MODEL)
    out = jax.ops.segment_sum(y_rows * row_gate[:, None].astype(y_rows.dtype), row_tok, num_segments=n_tok)
    return out.reshape(h.shape).astype(h.dtype)


def setup_inputs(seed: int = 0) -> dict:
    key = jax.random.key(seed)
    ks = iter(jax.random.split(key, 64))
    f32 = jnp.float32

    def nrm(shape, scale=1.0):
        return jax.random.normal(next(ks), shape, f32) * scale

    d = D_MODEL
    ab_in = A_WIDTH + 3 * B_WIDTH
    c_qkv = (C_HEADS + 2 * C_KV_HEADS) * HEAD_DIM
    lam_im0 = jnp.pi * jnp.arange(A_STATE, dtype=f32)
    return {
        'x_prompt': nrm((BATCH, SEQ, d)),
        'x_sample': nrm((DEC_BATCH, DEC_SEQ, d)),
        'cache_nat_k': nrm((DEC_BATCH, N_EVEN, PAST_LEN, B_HEADS, HEAD_DIM)),
        'cache_nat_v': nrm((DEC_BATCH, N_EVEN, PAST_LEN, B_HEADS, HEAD_DIM)),
        'cache_swa_k': nrm((DEC_BATCH, N_ODD, PAST_LEN, C_KV_HEADS, HEAD_DIM)),
        'cache_swa_v': nrm((DEC_BATCH, N_ODD, PAST_LEN, C_KV_HEADS, HEAD_DIM)),
        'state_ssm': nrm((DEC_BATCH, N_EVEN, 2, 2, A_GROUPS, A_STATE), 0.5),
        'c': nrm((DEC_BATCH, d)),
        'c_ctx': nrm((d,)),
        'norm1_g': 1.0 + nrm((DEPTH, d), 0.02),
        'norm2_g': 1.0 + nrm((DEPTH, d), 0.02),
        'mod_w': nrm((DEPTH, d, N_MOD * d), 0.5 * d ** -0.5),
        'mod_b': nrm((DEPTH, N_MOD * d), 0.02),
        'ab_w_in': nrm((N_EVEN, d, ab_in), d ** -0.5),
        'ab_w_out': nrm((N_EVEN, A_WIDTH + B_WIDTH, d), (A_WIDTH + B_WIDTH) ** -0.5),
        'ssm_lam_re': -0.5 + nrm((N_EVEN, 2, A_GROUPS, A_STATE), 0.01),
        'ssm_lam_im': lam_im0 + nrm((N_EVEN, 2, A_GROUPS, A_STATE), 0.01),
        'ssm_log_dt': jax.random.uniform(next(ks), (N_EVEN, 2, A_GROUPS), f32, math.log(DT_MIN), math.log(DT_MAX)),
        'ssm_b_re': nrm((N_EVEN, 2, A_GROUPS, A_STATE, A_GROUP_CH), (2 * A_GROUP_CH) ** -0.5),
        'ssm_b_im': nrm((N_EVEN, 2, A_GROUPS, A_STATE, A_GROUP_CH), (2 * A_GROUP_CH) ** -0.5),
        'ssm_c_re': nrm((N_EVEN, 2, A_GROUPS, A_GROUP_CH, A_STATE), (2 * A_STATE) ** -0.5),
        'ssm_c_im': nrm((N_EVEN, 2, A_GROUPS, A_GROUP_CH, A_STATE), (2 * A_STATE) ** -0.5),
        'ssm_d': nrm((N_EVEN, A_WIDTH)),
        'ssm_glu_w': nrm((N_EVEN, A_WIDTH, A_WIDTH), A_WIDTH ** -0.5),
        'ssm_glu_b': nrm((N_EVEN, A_WIDTH), 0.02),
        'nat_qn': 1.0 + nrm((N_EVEN, HEAD_DIM), 0.02),
        'nat_kn': 1.0 + nrm((N_EVEN, HEAD_DIM), 0.02),
        'nat_rpb': nrm((N_EVEN, B_HEADS, 2 * WIN_R_MAX - 1, 2 * WIN_C - 1), 0.1),
        'swa_w_qkv': nrm((N_ODD, d, c_qkv), d ** -0.5),
        'swa_w_o': nrm((N_ODD, C_HEADS * HEAD_DIM, d), (C_HEADS * HEAD_DIM) ** -0.5),
        'swa_qn': 1.0 + nrm((N_ODD, HEAD_DIM), 0.02),
        'swa_kn': 1.0 + nrm((N_ODD, HEAD_DIM), 0.02),
        'swa_sink': nrm((N_ODD, C_HEADS), 0.5),
        'moe_router_w': nrm((DEPTH, d, N_EXPERTS), d ** -0.5),
        'moe_router_b': nrm((DEPTH, N_EXPERTS), 0.01),
        'moe_w_gu': nrm((DEPTH, N_EXPERTS, d, 2 * D_EXPERT), d ** -0.5),
        'moe_b_gu': nrm((DEPTH, N_EXPERTS, 2 * D_EXPERT), 0.01),
        'moe_w_down': nrm((DEPTH, N_EXPERTS, D_EXPERT, d), D_EXPERT ** -0.5),
        'moe_b_down': nrm((DEPTH, N_EXPERTS, d), 0.01),
    }


def reference(x_prompt, x_sample, cache_nat_k, cache_nat_v, cache_swa_k, cache_swa_v, state_ssm, c, c_ctx,
              norm1_g, norm2_g, mod_w, mod_b,
              ab_w_in, ab_w_out, ssm_lam_re, ssm_lam_im, ssm_log_dt, ssm_b_re, ssm_b_im, ssm_c_re, ssm_c_im,
              ssm_d, ssm_glu_w, ssm_glu_b, nat_qn, nat_kn, nat_rpb,
              swa_w_qkv, swa_w_o, swa_qn, swa_kn, swa_sink,
              moe_router_w, moe_router_b, moe_w_gu, moe_b_gu, moe_w_down, moe_b_down):
    xp, xs = x_prompt, x_sample
    cond_ctx = jax.nn.silu(c_ctx)
    cond_lat = jax.nn.silu(c)
    nat_k_out, nat_v_out, swa_k_out, swa_v_out, ssm_out = [], [], [], [], []
    for l in range(DEPTH):
        mp = jnp.split(cond_ctx @ mod_w[l] + mod_b[l], N_MOD, axis=-1)
        ms = [m[:, None, :] for m in jnp.split(cond_lat @ mod_w[l] + mod_b[l], N_MOD, axis=-1)]
        hp = modulate(rms_norm(xp, norm1_g[l]), mp[0], mp[1])
        hs = modulate(rms_norm(xs, norm1_g[l]), ms[0], ms[1])
        i = l // 2
        if l % 2 == 0:
            ev = (ab_w_in[i], ab_w_out[i], ssm_lam_re[i], ssm_lam_im[i], ssm_log_dt[i], ssm_b_re[i], ssm_b_im[i],
                  ssm_c_re[i], ssm_c_im[i], ssm_d[i], ssm_glu_w[i], ssm_glu_b[i], nat_qn[i], nat_kn[i], nat_rpb[i])
            mix_p, (st, kk, vv) = even_mixer(hp, *ev, None, None, None)
            mix_s, _ = even_mixer(hs, *ev, state_ssm[:, i], cache_nat_k[:, i], cache_nat_v[:, i])
            ssm_out.append(st)
            nat_k_out.append(kk)
            nat_v_out.append(vv)
        else:
            od = (swa_w_qkv[i], swa_w_o[i], swa_qn[i], swa_kn[i], swa_sink[i])
            mix_p, (kk, vv) = odd_mixer(hp, *od, None, None)
            mix_s, _ = odd_mixer(hs, *od, cache_swa_k[:, i], cache_swa_v[:, i])
            swa_k_out.append(kk)
            swa_v_out.append(vv)
        xp = xp + mp[2] * mix_p
        xs = xs + ms[2] * mix_s
        mo = (moe_router_w[l], moe_router_b[l], moe_w_gu[l], moe_b_gu[l], moe_w_down[l], moe_b_down[l])
        xp = xp + mp[5] * moe_ffn(modulate(rms_norm(xp, norm2_g[l]), mp[3], mp[4]), *mo)
        xs = xs + ms[5] * moe_ffn(modulate(rms_norm(xs, norm2_g[l]), ms[3], ms[4]), *mo)
    y_prompt, y_sample = xp, xs
    new_nat_k = jnp.stack(nat_k_out, axis=1)
    new_nat_v = jnp.stack(nat_v_out, axis=1)
    new_swa_k = jnp.stack(swa_k_out, axis=1)
    new_swa_v = jnp.stack(swa_v_out, axis=1)
    new_state_ssm = jnp.stack(ssm_out, axis=1)
    return (y_prompt, y_sample, new_nat_k, new_nat_v, new_swa_k, new_swa_v, new_state_ssm)
```

```python
import functools

import jax
import jax.numpy as jnp
import numpy as np
from jax import lax
from jax.experimental import pallas as pl
from jax.experimental.pallas import tpu as pltpu

F32 = jnp.float32
BF16 = jnp.bfloat16

D_MODEL = 1024
BATCH = 32
SEQ = 256
DEPTH = 4
DEC_BATCH = 2
DEC_SEQ = 1024
PAST_LEN = 256
GRID_W = 64
HEAD_DIM = 64
N_MOD = 6
A_WIDTH = 512
A_GROUP_CH = 16
A_GROUPS = 32
A_STATE = 64
B_HEADS = 8
B_WIDTH = 512
WIN_R_MAX = 8
WIN_C = 16
C_HEADS = 16
C_KV_HEADS = 2
C_GROUP = 8
WINDOW = 128
ROPE_BASE = 10000.0
N_EXPERTS = 32
TOP_K = 4
D_EXPERT = 1024
SWIGLU_LIMIT = 7.0
SWIGLU_ALPHA = 1.702
EPS = 1e-6
NEG_INF = -1e30

LANES = 128
SUBLANES = 8
TILE = 256
N_CTX_TOK = BATCH * SEQ
N_LAT_TOK = DEC_BATCH * DEC_SEQ
N_TOK = N_CTX_TOK + N_LAT_TOK
N_TILES = N_TOK // TILE
N_CTX_TILES = N_CTX_TOK // TILE
LAT_TILES_PER_SEQ = DEC_SEQ // TILE
SSM_LANES = A_GROUPS * A_STATE
SSM_SLABS = A_WIDTH // LANES
SSM_SLAB_STATES = SSM_LANES // SSM_SLABS
SCAN_ROWS = 8
SCAN_GROUPS = N_TILES // SCAN_ROWS
SCAN_CHUNK = 64
SCAN_LANE_PARTS = 2
MOE_TM = 256
N_ASG = N_TOK * TOP_K
MOE_BLOCKS = N_ASG // MOE_TM + N_EXPERTS
MOE_ROWS = MOE_BLOCKS * MOE_TM
ROUTER_PAD = LANES
VMEM_LIMIT = 56 * 1024 * 1024
ATT_SCALE = HEAD_DIM ** -0.5


def _cparams(*sem):
    return pltpu.CompilerParams(dimension_semantics=sem, vmem_limit_bytes=VMEM_LIMIT)


def _dot(a, b):
    return jnp.dot(a, b, preferred_element_type=F32)


def _dot_nt(a, b):
    return lax.dot_general(a, b, (((1,), (1,)), ((), ())), preferred_element_type=F32)


def _split(a):
    hi = a.astype(BF16)
    lo = (a - hi.astype(F32)).astype(BF16)
    return hi, lo


def _dot3(a, b):
    a_hi, a_lo = _split(a)
    b_hi, b_lo = _split(b)
    return _dot(a_hi, b_hi) + (_dot(a_hi, b_lo) + _dot(a_lo, b_hi))


def _silu(x):
    return x * jax.nn.sigmoid(x)


def _norm_mod(x, g, shift, scale):
    y = x * lax.rsqrt(jnp.mean(x * x, axis=-1, keepdims=True) + EPS)
    return (y * g) * (1.0 + scale) + shift


def _head_rms(x, bd, gain):
    sq_hi, sq_lo = _split(x * x)
    ms = _dot(sq_hi, bd) + _dot(sq_lo, bd)
    return x * lax.rsqrt(ms + EPS) * gain


def _lane_half(shape, half):
    lane = lax.broadcasted_iota(jnp.int32, shape, len(shape) - 1)
    return (lane < HEAD_DIM) if half == 0 else (lane >= HEAD_DIM)


def _mod_kernel(cond_ref, w_ref, b_ref, o_ref):
    o_ref[0] = _dot3(_silu(cond_ref[...]), w_ref[0]) + b_ref[0]


def _modulation(cond, mod_w, mod_b):
    nc = N_MOD
    return pl.pallas_call(
        _mod_kernel,
        out_shape=jax.ShapeDtypeStruct((DEPTH, SUBLANES, N_MOD * D_MODEL), F32),
        grid=(DEPTH, nc),
        in_specs=[
            pl.BlockSpec((SUBLANES, D_MODEL), lambda l, c: (0, 0)),
            pl.BlockSpec((1, D_MODEL, D_MODEL), lambda l, c: (l, 0, c)),
            pl.BlockSpec((1, 1, D_MODEL), lambda l, c: (l, 0, c)),
        ],
        out_specs=pl.BlockSpec((1, SUBLANES, D_MODEL), lambda l, c: (l, 0, c)),
        compiler_params=_cparams("arbitrary", "arbitrary"),
        name="modulation",
    )(cond, mod_w, mod_b.reshape(DEPTH, 1, N_MOD * D_MODEL))


def _mod_spec(col):
    return pl.BlockSpec((1, 1, D_MODEL), lambda i: (i, 0, col))


def _tile_spec(width):
    return pl.BlockSpec((TILE, width), lambda i: (i, 0))


def _full_spec(shape):
    nd = len(shape)
    return pl.BlockSpec(shape, lambda i: (0,) * nd)


def _even_in_kernel(x_ref, g_ref, sh_ref, sc_ref, w_ref, bd_ref, qn_ref, kn_ref,
                    u_ref, q_ref, k_ref, v_ref):
    h = _norm_mod(x_ref[...], g_ref[...], sh_ref[0], sc_ref[0]).astype(BF16)
    bd = bd_ref[...]
    u_ref[...] = _dot(h, w_ref[:, 0:A_WIDTH])
    for s in range(B_WIDTH // LANES):
        lo = A_WIDTH + s * LANES
        q = _dot(h, w_ref[:, lo:lo + LANES])
        q_ref[:, s * LANES:(s + 1) * LANES] = _head_rms(q, bd, qn_ref[...]).astype(BF16)
        lo = A_WIDTH + B_WIDTH + s * LANES
        k = _dot(h, w_ref[:, lo:lo + LANES])
        k_ref[:, s * LANES:(s + 1) * LANES] = _head_rms(k, bd, kn_ref[...])
    v_ref[...] = _dot(h, w_ref[:, A_WIDTH + 2 * B_WIDTH:])


def _even_in(x, g, modt, w, bd, qn, kn):
    n_out = A_WIDTH + 3 * B_WIDTH
    return pl.pallas_call(
        _even_in_kernel,
        out_shape=(
            jax.ShapeDtypeStruct((TILE, N_TILES * A_WIDTH), F32),
            jax.ShapeDtypeStruct((N_TOK, B_WIDTH), BF16),
            jax.ShapeDtypeStruct((N_TOK, B_WIDTH), F32),
            jax.ShapeDtypeStruct((N_TOK, B_WIDTH), F32),
        ),
        grid=(N_TILES,),
        in_specs=[
            _tile_spec(D_MODEL), _full_spec((1, D_MODEL)), _mod_spec(0), _mod_spec(1),
            _full_spec((D_MODEL, n_out)), _full_spec((LANES, LANES)),
            _full_spec((1, LANES)), _full_spec((1, LANES)),
        ],
        out_specs=(
            pl.BlockSpec((TILE, A_WIDTH), lambda i: (0, i)),
            _tile_spec(B_WIDTH), _tile_spec(B_WIDTH), _tile_spec(B_WIDTH),
        ),
        compiler_params=_cparams("arbitrary"),
        name="even_in",
    )(x, g, modt, modt, w, bd, qn, kn)


def _cmul(ar, ai, br, bi):
    return ar * br - ai * bi, ar * bi + ai * br


def _s5_kernel(u_ref, a_ref, w_ref, c_ref, s0_ref, y_ref, fin_ref, xr, xi, st_r, st_i):
    grp = pl.program_id(0)
    drn = pl.program_id(1)
    n_chunks = TILE // SCAN_CHUNK
    rows = SCAN_CHUNK * SCAN_ROWS
    part = SSM_LANES // SCAN_LANE_PARTS

    def run(store):
        @pl.loop(0, n_chunks)
        def _(c):
            cc = jnp.where(drn == 0, c, n_chunks - 1 - c)
            t0 = pl.multiple_of(cc * SCAN_CHUNK, SCAN_CHUNK)
            uu = u_ref[pl.ds(t0, SCAN_CHUNK), :, :]
            for s in range(SSM_SLABS):
                us = uu[:, :, s * LANES:(s + 1) * LANES].reshape(rows, LANES).astype(BF16)
                cols = slice(s * SSM_SLAB_STATES, (s + 1) * SSM_SLAB_STATES)
                xr[:, cols] = _dot(us, w_ref[0, 0, s])
                xi[:, cols] = _dot(us, w_ref[0, 1, s])
            for p in range(SCAN_LANE_PARTS):
                ls = slice(p * part, (p + 1) * part)
                ar = a_ref[0, 0, :, ls]
                ai = a_ref[0, 1, :, ls]

                def step(j, carry):
                    sr, si = carry
                    tt = jnp.where(drn == 0, j, SCAN_CHUNK - 1 - j)
                    r0 = pl.multiple_of(tt * SCAN_ROWS, SCAN_ROWS)
                    nr = ar * sr - ai * si + xr[pl.ds(r0, SCAN_ROWS), ls]
                    ni = ar * si + ai * sr + xi[pl.ds(r0, SCAN_ROWS), ls]
                    if store:
                        xr[pl.ds(r0, SCAN_ROWS), ls] = nr
                        xi[pl.ds(r0, SCAN_ROWS), ls] = ni
                    return nr, ni

                sr, si = lax.fori_loop(0, SCAN_CHUNK, step, (st_r[:, ls], st_i[:, ls]), unroll=4)
                st_r[:, ls] = sr
                st_i[:, ls] = si
            if store:
                for s in range(SSM_SLABS):
                    cols = slice(s * SSM_SLAB_STATES, (s + 1) * SSM_SLAB_STATES)
                    ys = (_dot(xr[:, cols].astype(BF16), c_ref[0, 0, s])
                          - _dot(xi[:, cols].astype(BF16), c_ref[0, 1, s]))
                    ys = ys.reshape(SCAN_CHUNK, SCAN_ROWS, LANES)
                    lanes = slice(s * LANES, (s + 1) * LANES)

                    @pl.when(drn == 0)
                    def _():
                        y_ref[pl.ds(t0, SCAN_CHUNK), :, lanes] = ys

                    @pl.when(drn != 0)
                    def _():
                        y_ref[pl.ds(t0, SCAN_CHUNK), :, lanes] += ys

    st_r[...] = s0_ref[0, 0]
    st_i[...] = s0_ref[0, 1]

    @pl.when(grp == SCAN_GROUPS - 1)
    def _():
        st_r[...] = jnp.zeros_like(st_r)
        st_i[...] = jnp.zeros_like(st_i)
        run(False)
        pr, pi = a_ref[0, 0], a_ref[0, 1]
        for _ in range(8):
            pr, pi = _cmul(pr, pi, pr, pi)
        fr, fi = st_r[...], st_i[...]
        s0r, s0i = s0_ref[0, 0], s0_ref[0, 1]
        row = lax.broadcasted_iota(jnp.int32, (SCAN_ROWS, SSM_LANES), 0)
        quarter = row % LAT_TILES_PER_SEQ
        fwd = drn == 0
        keep = quarter != jnp.where(fwd, 0, LAT_TILES_PER_SEQ - 1)
        ir, ii = s0r, s0i
        for _ in range(LAT_TILES_PER_SEQ - 1):
            nr, ni = _cmul(pr, pi, ir, ii)
            nr, ni = nr + fr, ni + fi
            nr = jnp.where(fwd, pltpu.roll(nr, 1, 0), pltpu.roll(nr, SCAN_ROWS - 1, 0))
            ni = jnp.where(fwd, pltpu.roll(ni, 1, 0), pltpu.roll(ni, SCAN_ROWS - 1, 0))
            ir = s0r + jnp.where(keep, nr, 0.0)
            ii = s0i + jnp.where(keep, ni, 0.0)
        st_r[...] = ir
        st_i[...] = ii

    run(True)
    fin_ref[0, 0] = st_r[...]
    fin_ref[0, 1] = st_i[...]


def _s5_scan(u_tb, a_b, w_bd, c_bd, s0):
    rows = SCAN_CHUNK * SCAN_ROWS
    return pl.pallas_call(
        _s5_kernel,
        out_shape=(
            jax.ShapeDtypeStruct((TILE, N_TILES, A_WIDTH), F32),
            jax.ShapeDtypeStruct((2, 2, N_TILES, SSM_LANES), F32),
        ),
        grid=(SCAN_GROUPS, 2),
        in_specs=[
            pl.BlockSpec((TILE, SCAN_ROWS, A_WIDTH), lambda g, d: (0, g, 0)),
            pl.BlockSpec((1, 2, SCAN_ROWS, SSM_LANES), lambda g, d: (d, 0, 0, 0)),
            pl.BlockSpec((1, 2, SSM_SLABS, LANES, SSM_SLAB_STATES), lambda g, d: (d, 0, 0, 0, 0)),
            pl.BlockSpec((1, 2, SSM_SLABS, SSM_SLAB_STATES, LANES), lambda g, d: (d, 0, 0, 0, 0)),
            pl.BlockSpec((1, 2, SCAN_ROWS, SSM_LANES), lambda g, d: (d, 0, g, 0)),
        ],
        out_specs=(
            pl.BlockSpec((TILE, SCAN_ROWS, A_WIDTH), lambda g, d: (0, g, 0)),
            pl.BlockSpec((1, 2, SCAN_ROWS, SSM_LANES), lambda g, d: (d, 0, g, 0)),
        ),
        scratch_shapes=[
            pltpu.VMEM((rows, SSM_LANES), F32), pltpu.VMEM((rows, SSM_LANES), F32),
            pltpu.VMEM((SCAN_ROWS, SSM_LANES), F32), pltpu.VMEM((SCAN_ROWS, SSM_LANES), F32),
        ],
        compiler_params=_cparams("arbitrary", "arbitrary"),
        name="s5_scan",
    )(u_tb, a_b, w_bd, c_bd, s0)


def _s5_params(lam_re, lam_im, log_dt, b_re, b_im, c_re, c_im):
    dt = jnp.exp(log_dt)[..., None]
    mag = jnp.exp(lam_re * dt)
    ab_re, ab_im = mag * jnp.cos(lam_im * dt), mag * jnp.sin(lam_im * dt)
    den = lam_re * lam_re + lam_im * lam_im
    nr, ni = ab_re - 1.0, ab_im
    f_re = (nr * lam_re + ni * lam_im) / den
    f_im = (ni * lam_re - nr * lam_im) / den
    bb_re = f_re[..., None] * b_re - f_im[..., None] * b_im
    bb_im = f_re[..., None] * b_im + f_im[..., None] * b_re
    a_b = jnp.stack([ab_re, ab_im], axis=1).reshape(2, 2, 1, SSM_LANES)
    a_b = jnp.broadcast_to(a_b, (2, 2, SCAN_ROWS, SSM_LANES))
    gps = A_GROUPS // SSM_SLABS
    eye = jnp.eye(gps, dtype=F32)

    def in_bd(bb):
        bb = bb.reshape(2, SSM_SLABS, gps, A_STATE, A_GROUP_CH)
        m = jnp.einsum('dsgph,gk->dsghkp', bb, eye)
        return m.reshape(2, SSM_SLABS, LANES, SSM_SLAB_STATES)

    def out_bd(cc):
        cc = cc.reshape(2, SSM_SLABS, gps, A_GROUP_CH, A_STATE)
        m = jnp.einsum('dsghp,gk->dsgpkh', cc, eye)
        return m.reshape(2, SSM_SLABS, SSM_SLAB_STATES, LANES)

    w_bd = jnp.stack([in_bd(bb_re), in_bd(bb_im)], axis=1).astype(BF16)
    c_bd = jnp.stack([out_bd(c_re), out_bd(c_im)], axis=1).astype(BF16)
    return a_b, w_bd, c_bd


def _softmax_pv(scores, values, sink=None):
    m = functools.reduce(jnp.maximum, [jnp.max(s, axis=-1, keepdims=True) for s in scores])
    if sink is not None:
        m = jnp.maximum(m, sink)
    den = None
    acc = None
    for s, v in zip(scores, values):
        e = jnp.exp(s - m)
        d = jnp.sum(e, axis=-1, keepdims=True)
        o = _dot(e.astype(BF16), v)
        den = d if den is None else den + d
        acc = o if acc is None else acc + o
    if sink is not None:
        den = den + jnp.exp(sink - m)
    return acc / den


def _nat_ctx_kernel(q_ref, k_ref, v_ref, o_ref):
    for p in range(B_WIDTH // LANES):
        ls = slice(p * LANES, (p + 1) * LANES)
        qs = q_ref[:, ls]
        ks = k_ref[:, ls].astype(BF16)
        vs = v_ref[:, ls].astype(BF16)
        out = None
        for half in range(2):
            msk = _lane_half(qs.shape, half)
            qm = jnp.where(msk, qs, jnp.zeros_like(qs))
            s = _dot_nt(qm, ks) * ATT_SCALE
            o = _softmax_pv([s], [vs])
            out = o if out is None else jnp.where(msk, o, out)
        o_ref[:, ls] = out.astype(BF16)


def _nat_lat_kernel(q_ref, k_ref, v_ref, kc_ref, vc_ref, bias_ref, o_ref):
    r = pl.program_id(1)
    wr = WIN_R_MAX
    rs = jnp.clip(r - wr // 2, 0, DEC_SEQ // GRID_W - wr)
    k0 = pl.multiple_of(rs * GRID_W, GRID_W)
    n_nb = wr * GRID_W
    for p in range(B_WIDTH // LANES):
        ls = slice(p * LANES, (p + 1) * LANES)
        qs = q_ref[:, ls]
        ks = k_ref[pl.ds(k0, n_nb), ls].astype(BF16)
        vs = v_ref[pl.ds(k0, n_nb), ls].astype(BF16)
        kc = kc_ref[0, :, ls].astype(BF16)
        vc = vc_ref[0, :, ls].astype(BF16)
        out = None
        for half in range(2):
            msk = _lane_half(qs.shape, half)
            qm = jnp.where(msk, qs, jnp.zeros_like(qs))
            s_nb = _dot_nt(qm, ks) * ATT_SCALE + bias_ref[2 * p + half, 0]
            s_ctx = _dot_nt(qm, kc) * ATT_SCALE
            o = _softmax_pv([s_nb, s_ctx], [vs, vc])
            out = o if out is None else jnp.where(msk, o, out)
        o_ref[:, ls] = out.astype(BF16)


def _nat_attention(q, k, v, kc, vc, bias):
    o_ctx = pl.pallas_call(
        _nat_ctx_kernel,
        out_shape=jax.ShapeDtypeStruct((N_CTX_TOK, B_WIDTH), BF16),
        grid=(N_CTX_TILES,),
        in_specs=[_tile_spec(B_WIDTH)] * 3,
        out_specs=_tile_spec(B_WIDTH),
        compiler_params=_cparams("arbitrary"),
        name="nat_ctx",
    )(q, k, v)
    rows = DEC_SEQ // GRID_W
    ctx_seqs = N_CTX_TOK // DEC_SEQ

    def bias_idx(b, r):
        return (0, r - jnp.clip(r - WIN_R_MAX // 2, 0, rows - WIN_R_MAX), 0, 0)

    o_lat = pl.pallas_call(
        _nat_lat_kernel,
        out_shape=jax.ShapeDtypeStruct((N_LAT_TOK, B_WIDTH), BF16),
        grid=(DEC_BATCH, rows),
        in_specs=[
            pl.BlockSpec((GRID_W, B_WIDTH), lambda b, r: (N_CTX_TOK // GRID_W + b * rows + r, 0)),
            pl.BlockSpec((DEC_SEQ, B_WIDTH), lambda b, r: (ctx_seqs + b, 0)),
            pl.BlockSpec((DEC_SEQ, B_WIDTH), lambda b, r: (ctx_seqs + b, 0)),
            pl.BlockSpec((1, PAST_LEN, B_WIDTH), lambda b, r: (b, 0, 0)),
            pl.BlockSpec((1, PAST_LEN, B_WIDTH), lambda b, r: (b, 0, 0)),
            pl.BlockSpec((B_HEADS, 1, GRID_W, WIN_R_MAX * GRID_W), bias_idx),
        ],
        out_specs=pl.BlockSpec((GRID_W, B_WIDTH), lambda b, r: (b * rows + r, 0)),
        compiler_params=_cparams("arbitrary", "arbitrary"),
        name="nat_lat",
    )(q, k, v, kc, vc, bias)
    return jnp.concatenate([o_ctx, o_lat], axis=0)


def _nat_bias(rpb):
    d = jnp.arange(WIN_R_MAX)
    j = jnp.arange(WIN_R_MAX)
    qc = jnp.arange(GRID_W)
    kc = jnp.arange(GRID_W)
    dr = (j[None, :] - d[:, None]) + (WIN_R_MAX - 1)
    cs = jnp.clip(qc - WIN_C // 2, 0, GRID_W - WIN_C)
    ok = (kc[None, :] >= cs[:, None]) & (kc[None, :] < cs[:, None] + WIN_C)
    dc = jnp.clip(kc[None, :] - qc[:, None] + (WIN_C - 1), 0, 2 * WIN_C - 2)
    b = rpb[:, dr[:, None, :, None], dc[None, :, None, :]]
    b = jnp.where(ok[None, None, :, None, :], b.astype(F32), NEG_INF)
    return b.reshape(B_HEADS, WIN_R_MAX, GRID_W, WIN_R_MAX * GRID_W)


def _gqa_heads(q_ref, sink_ref, k_segs, v_segs, masks, o_ref):
    k_rot = [pltpu.roll(k.astype(F32), HEAD_DIM, 1).astype(BF16) for k in k_segs]
    v_rot = [pltpu.roll(v.astype(F32), HEAD_DIM, 1).astype(BF16) for v in v_segs]
    for p in range(C_HEADS // 2):
        ls = slice(p * LANES, (p + 1) * LANES)
        qs = q_ref[:, ls]
        out = None
        for half in range(2):
            head = 2 * p + half
            kv = head // C_GROUP
            ks = k_segs if kv == half else k_rot
            vs = v_segs if kv == half else v_rot
            msk = _lane_half(qs.shape, half)
            qm = jnp.where(msk, qs, jnp.zeros_like(qs))
            scores = []
            for kseg, m in zip(ks, masks):
                s = _dot_nt(qm, kseg) * ATT_SCALE
                scores.append(s if m is None else jnp.where(m, s, NEG_INF))
            o = _softmax_pv(scores, vs, sink=sink_ref[head])
            out = o if out is None else jnp.where(msk, o, out)
        o_ref[:, ls] = out.astype(BF16)


def _swa_ctx_kernel(sink_ref, q_ref, k_ref, v_ref, o_ref):
    _gqa_heads(q_ref, sink_ref, [k_ref[...].astype(BF16)], [v_ref[...].astype(BF16)], [None], o_ref)


def _swa_lat_kernel(sink_ref, q_ref, k_ref, v_ref, kc_ref, vc_ref, o_ref):
    n = pl.program_id(1)
    n_win = 3 * WINDOW
    start = jnp.clip((n - 1) * WINDOW, 0, DEC_SEQ - n_win)
    k0 = pl.multiple_of(start, WINDOW)
    qpos = n * WINDOW + lax.broadcasted_iota(jnp.int32, (WINDOW, n_win), 0)
    kpos = start + lax.broadcasted_iota(jnp.int32, (WINDOW, n_win), 1)
    ok = jnp.abs(qpos - kpos) <= WINDOW
    _gqa_heads(q_ref, sink_ref,
               [k_ref[pl.ds(k0, n_win), :].astype(BF16), kc_ref[0].astype(BF16)],
               [v_ref[pl.ds(k0, n_win), :].astype(BF16), vc_ref[0].astype(BF16)],
               [ok, None], o_ref)


def _swa_attention(q, k, v, kc, vc, sink):
    nq, nkv = C_HEADS * HEAD_DIM, C_KV_HEADS * HEAD_DIM
    smem = pl.BlockSpec(memory_space=pltpu.SMEM)
    o_ctx = pl.pallas_call(
        _swa_ctx_kernel,
        out_shape=jax.ShapeDtypeStruct((N_CTX_TOK, nq), BF16),
        grid=(N_CTX_TILES,),
        in_specs=[smem, _tile_spec(nq), _tile_spec(nkv), _tile_spec(nkv)],
        out_specs=_tile_spec(nq),
        compiler_params=_cparams("arbitrary"),
        name="swa_ctx",
    )(sink, q, k, v)
    nb = DEC_SEQ // WINDOW
    ctx_seqs = N_CTX_TOK // DEC_SEQ
    o_lat = pl.pallas_call(
        _swa_lat_kernel,
        out_shape=jax.ShapeDtypeStruct((N_LAT_TOK, nq), BF16),
        grid=(DEC_BATCH, nb),
        in_specs=[
            smem,
            pl.BlockSpec((WINDOW, nq), lambda b, n: (N_CTX_TOK // WINDOW + b * nb + n, 0)),
            pl.BlockSpec((DEC_SEQ, nkv), lambda b, n: (ctx_seqs + b, 0)),
            pl.BlockSpec((DEC_SEQ, nkv), lambda b, n: (ctx_seqs + b, 0)),
            pl.BlockSpec((1, PAST_LEN, nkv), lambda b, n: (b, 0, 0)),
            pl.BlockSpec((1, PAST_LEN, nkv), lambda b, n: (b, 0, 0)),
        ],
        out_specs=pl.BlockSpec((WINDOW, nq), lambda b, n: (b * nb + n, 0)),
        compiler_params=_cparams("arbitrary", "arbitrary"),
        name="swa_lat",
    )(sink, q, k, v, kc, vc)
    return jnp.concatenate([o_ctx, o_lat], axis=0)


def _even_out_kernel(y_ref, u_ref, o_ref, x_ref, gate_ref, d_ref, gw_ref, gb_ref, w_ref, xo_ref):
    yy = y_ref[...] + d_ref[...] * u_ref[...]
    g = jax.nn.gelu(yy)
    a = g * jax.nn.sigmoid(_dot(g.astype(BF16), gw_ref[...]) + gb_ref[...])
    mix = _dot(a.astype(BF16), w_ref[0:A_WIDTH, :]) + _dot(o_ref[...], w_ref[A_WIDTH:, :])
    xo_ref[...] = x_ref[...] + gate_ref[0] * mix


def _even_out(y_t, u_t, o, x, modt, d_skip, glu_w, glu_b, w_out):
    tm_spec = pl.BlockSpec((TILE, A_WIDTH), lambda i: (0, i))
    return pl.pallas_call(
        _even_out_kernel,
        out_shape=jax.ShapeDtypeStruct((N_TOK, D_MODEL), F32),
        grid=(N_TILES,),
        in_specs=[
            tm_spec, tm_spec, _tile_spec(B_WIDTH), _tile_spec(D_MODEL), _mod_spec(2),
            _full_spec((1, A_WIDTH)), _full_spec((A_WIDTH, A_WIDTH)), _full_spec((1, A_WIDTH)),
            _full_spec((A_WIDTH + B_WIDTH, D_MODEL)),
        ],
        out_specs=_tile_spec(D_MODEL),
        compiler_params=_cparams("arbitrary"),
        name="even_out",
    )(y_t, u_t, o, x, modt, d_skip, glu_w, glu_b, w_out)


def _rope(x, cos, sin):
    lane = lax.broadcasted_iota(jnp.int32, x.shape, 1)
    first = (lane % (HEAD_DIM // 2)) < (HEAD_DIM // 4)
    partner = jnp.where(first, pltpu.roll(x, LANES - HEAD_DIM // 4, 1), pltpu.roll(x, HEAD_DIM // 4, 1))
    return x * cos + partner * sin


def _odd_in_kernel(x_ref, g_ref, sh_ref, sc_ref, w_ref, bd_ref, qn_ref, kn_ref, cos_ref, sin_ref,
                   q_ref, kc_ref, kr_ref, v_ref):
    i = pl.program_id(0)
    h = _norm_mod(x_ref[...], g_ref[...], sh_ref[0], sc_ref[0]).astype(BF16)
    bd = bd_ref[...]
    lat = i >= N_CTX_TILES
    cos = jnp.where(lat, cos_ref[...], 1.0)
    sin = jnp.where(lat, sin_ref[...], 0.0)
    nq = C_HEADS * HEAD_DIM
    for s in range(nq // LANES):
        q = _head_rms(_dot(h, w_ref[:, s * LANES:(s + 1) * LANES]), bd, qn_ref[...])
        q_ref[:, s * LANES:(s + 1) * LANES] = _rope(q, cos, sin).astype(BF16)
    k = _head_rms(_dot(h, w_ref[:, nq:nq + LANES]), bd, kn_ref[...])
    kc_ref[...] = k
    kr_ref[...] = _rope(k, cos, sin)
    v_ref[...] = _dot(h, w_ref[:, nq + LANES:])


def _odd_in(x, g, modt, w, bd, qn, kn, cos, sin):
    nq, nkv = C_HEADS * HEAD_DIM, C_KV_HEADS * HEAD_DIM
    lat_spec = pl.BlockSpec((TILE, LANES), lambda i: (jnp.maximum(i - N_CTX_TILES, 0) % LAT_TILES_PER_SEQ, 0))
    return pl.pallas_call(
        _odd_in_kernel,
        out_shape=(
            jax.ShapeDtypeStruct((N_TOK, nq), BF16),
            jax.ShapeDtypeStruct((N_TOK, nkv), F32),
            jax.ShapeDtypeStruct((N_TOK, nkv), F32),
            jax.ShapeDtypeStruct((N_TOK, nkv), F32),
        ),
        grid=(N_TILES,),
        in_specs=[
            _tile_spec(D_MODEL), _full_spec((1, D_MODEL)), _mod_spec(0), _mod_spec(1),
            _full_spec((D_MODEL, nq + 2 * nkv)), _full_spec((LANES, LANES)),
            _full_spec((1, LANES)), _full_spec((1, LANES)), lat_spec, lat_spec,
        ],
        out_specs=(_tile_spec(nq), _tile_spec(nkv), _tile_spec(nkv), _tile_spec(nkv)),
        compiler_params=_cparams("arbitrary"),
        name="odd_in",
    )(x, g, modt, modt, w, bd, qn, kn, cos, sin)


def _rope_tables():
    nf = HEAD_DIM // 4
    inv = ROPE_BASE ** (-jnp.arange(nf, dtype=F32) / nf)
    t = jnp.arange(DEC_SEQ)
    pos = jnp.stack([t // GRID_W, t % GRID_W], axis=-1).astype(F32)
    ang = pos[:, :, None] * inv
    cos, sin = jnp.cos(ang), jnp.sin(ang)
    cos_h = jnp.stack([cos, cos], axis=2).reshape(DEC_SEQ, HEAD_DIM)
    sin_h = jnp.stack([-sin, sin], axis=2).reshape(DEC_SEQ, HEAD_DIM)
    return jnp.tile(cos_h, (1, 2)), jnp.tile(sin_h, (1, 2))


def _odd_out_kernel(o_ref, x_ref, gate_ref, w_ref, xo_ref):
    xo_ref[...] = x_ref[...] + gate_ref[0] * _dot(o_ref[...], w_ref[...])


def _odd_out(o, x, modt, w_o):
    nq = C_HEADS * HEAD_DIM
    return pl.pallas_call(
        _odd_out_kernel,
        out_shape=jax.ShapeDtypeStruct((N_TOK, D_MODEL), F32),
        grid=(N_TILES,),
        in_specs=[_tile_spec(nq), _tile_spec(D_MODEL), _mod_spec(2), _full_spec((nq, D_MODEL))],
        out_specs=_tile_spec(D_MODEL),
        compiler_params=_cparams("arbitrary"),
        name="odd_out",
    )(o, x, modt, w_o)


def _moe_in_kernel(x_ref, g_ref, sh_ref, sc_ref, rw_ref, rb_ref, h_ref, lg_ref):
    h = _norm_mod(x_ref[...], g_ref[...], sh_ref[0], sc_ref[0])
    h_ref[...] = h.astype(BF16)
    lg_ref[...] = _dot3(h, rw_ref[...]) + rb_ref[...]


def _moe_in(x, g, modt, rw, rb):
    return pl.pallas_call(
        _moe_in_kernel,
        out_shape=(
            jax.ShapeDtypeStruct((N_TOK, D_MODEL), BF16),
            jax.ShapeDtypeStruct((N_TOK, ROUTER_PAD), F32),
        ),
        grid=(N_TILES,),
        in_specs=[
            _tile_spec(D_MODEL), _full_spec((1, D_MODEL)), _mod_spec(3), _mod_spec(4),
            _full_spec((D_MODEL, ROUTER_PAD)), _full_spec((1, ROUTER_PAD)),
        ],
        out_specs=(_tile_spec(D_MODEL), _tile_spec(ROUTER_PAD)),
        compiler_params=_cparams("arbitrary"),
        name="moe_in",
    )(x, g, modt, modt, rw, rb)


def _moe_expert_kernel(be_ref, nb_ref, x_ref, wgu_ref, bgu_ref, wd_ref, bd_ref, o_ref):
    i = pl.program_id(0)

    @pl.when(i < nb_ref[0])
    def _():
        x = x_ref[...]
        g = _dot(x, wgu_ref[0, :, 0:D_EXPERT].astype(BF16)) + bgu_ref[0, :, 0:D_EXPERT]
        lin = _dot(x, wgu_ref[0, :, D_EXPERT:].astype(BF16)) + bgu_ref[0, :, D_EXPERT:]
        g = jnp.minimum(g, SWIGLU_LIMIT)
        lin = jnp.clip(lin, -SWIGLU_LIMIT, SWIGLU_LIMIT)
        act = g * jax.nn.sigmoid(SWIGLU_ALPHA * g) * (lin + 1.0)
        o_ref[...] = _dot(act.astype(BF16), wd_ref[0].astype(BF16)) + bd_ref[0]

    @pl.when(i >= nb_ref[0])
    def _():
        o_ref[...] = jnp.zeros_like(o_ref)


def _moe_experts(block_e, n_used, xs, w_gu, b_gu, w_down, b_down):
    grid_spec = pltpu.PrefetchScalarGridSpec(
        num_scalar_prefetch=2,
        grid=(MOE_BLOCKS,),
        in_specs=[
            pl.BlockSpec((MOE_TM, D_MODEL), lambda i, be, nb: (jnp.minimum(i, nb[0] - 1), 0)),
            pl.BlockSpec((1, D_MODEL, 2 * D_EXPERT), lambda i, be, nb: (be[i], 0, 0)),
            pl.BlockSpec((1, 1, 2 * D_EXPERT), lambda i, be, nb: (be[i], 0, 0)),
            pl.BlockSpec((1, D_EXPERT, D_MODEL), lambda i, be, nb: (be[i], 0, 0)),
            pl.BlockSpec((1, 1, D_MODEL), lambda i, be, nb: (be[i], 0, 0)),
        ],
        out_specs=pl.BlockSpec((MOE_TM, D_MODEL), lambda i, be, nb: (i, 0)),
    )
    return pl.pallas_call(
        _moe_expert_kernel,
        out_shape=jax.ShapeDtypeStruct((MOE_ROWS, D_MODEL), F32),
        grid_spec=grid_spec,
        compiler_params=_cparams("arbitrary"),
        name="moe_experts",
    )(block_e, n_used, xs, w_gu, b_gu.reshape(N_EXPERTS, 1, 2 * D_EXPERT),
      w_down, b_down.reshape(N_EXPERTS, 1, D_MODEL))


def _moe(x, g, modt, gate_rows, rw, rb, w_gu, b_gu, w_down, b_down):
    h, logits = _moe_in(x, g, modt, rw, rb)
    top_val, top_idx = lax.top_k(logits[:, :N_EXPERTS], TOP_K)
    gates = jax.nn.softmax(top_val, axis=-1)
    flat_e = top_idx.reshape(N_ASG)
    onehot = (flat_e[:, None] == jnp.arange(N_EXPERTS)[None, :]).astype(jnp.int32)
    csum = jnp.cumsum(onehot, axis=0)
    rank = jnp.take_along_axis(csum, flat_e[:, None], axis=1)[:, 0] - 1
    counts = csum[-1]
    pad_counts = (counts + MOE_TM - 1) // MOE_TM * MOE_TM
    pad_end = jnp.cumsum(pad_counts)
    pad_start = pad_end - pad_counts
    dest = pad_start[flat_e] + rank
    row_tok = jnp.zeros((MOE_ROWS,), jnp.int32).at[dest].set(jnp.arange(N_ASG, dtype=jnp.int32) // TOP_K)
    n_used = (pad_end[-1] // MOE_TM).astype(jnp.int32)
    blk = jnp.arange(MOE_BLOCKS, dtype=jnp.int32)
    block_e = jnp.minimum(jnp.searchsorted(pad_end, blk * MOE_TM, side='right'), N_EXPERTS - 1).astype(jnp.int32)
    last_e = block_e[jnp.maximum(n_used - 1, 0)]
    block_e = jnp.where(blk < n_used, block_e, last_e)
    xs = h[row_tok]
    y_rows = _moe_experts(block_e, n_used.reshape(1), xs, w_gu, b_gu, w_down, b_down)
    picked = y_rows[dest].reshape(N_TOK, TOP_K, D_MODEL)
    out = jnp.sum(picked * gates[:, :, None], axis=1)
    return x + gate_rows * out


def kernel(x_prompt, x_sample, cache_nat_k, cache_nat_v, cache_swa_k, cache_swa_v, state_ssm, c, c_ctx,
           norm1_g, norm2_g, mod_w, mod_b,
           ab_w_in, ab_w_out, ssm_lam_re, ssm_lam_im, ssm_log_dt, ssm_b_re, ssm_b_im, ssm_c_re, ssm_c_im,
           ssm_d, ssm_glu_w, ssm_glu_b, nat_qn, nat_kn, nat_rpb,
           swa_w_qkv, swa_w_o, swa_qn, swa_kn, swa_sink,
           moe_router_w, moe_router_b, moe_w_gu, moe_b_gu, moe_w_down, moe_b_down):
    x = jnp.concatenate([x_prompt.reshape(N_CTX_TOK, D_MODEL), x_sample.reshape(N_LAT_TOK, D_MODEL)], axis=0)
    cond = jnp.zeros((SUBLANES, D_MODEL), F32).at[0].set(c_ctx).at[1:1 + DEC_BATCH].set(c)
    mod = _modulation(cond, mod_w, mod_b)
    tile_row = np.concatenate([np.zeros(N_CTX_TILES, np.int32),
                               1 + np.arange(N_TILES - N_CTX_TILES, dtype=np.int32) // LAT_TILES_PER_SEQ])
    tok_row = np.repeat(tile_row, TILE)
    head_gain = lambda gn: jnp.tile(gn, 2).reshape(1, LANES)
    bd = jnp.asarray(np.kron(np.eye(2, dtype=np.float32), np.full((HEAD_DIM, HEAD_DIM), 1.0 / HEAD_DIM, np.float32)), BF16)
    rope_cos, rope_sin = _rope_tables()
    router_w = jnp.pad(moe_router_w, ((0, 0), (0, 0), (0, ROUTER_PAD - N_EXPERTS)))
    router_b = jnp.pad(moe_router_b, ((0, 0), (0, ROUTER_PAD - N_EXPERTS))).reshape(DEPTH, 1, ROUTER_PAD)
    nkv = C_KV_HEADS * HEAD_DIM

    nat_k_out, nat_v_out, swa_k_out, swa_v_out, ssm_out = [], [], [], [], []
    for l in range(DEPTH):
        modt = mod[l][tile_row].reshape(N_TILES, 1, N_MOD * D_MODEL)
        g1 = norm1_g[l].reshape(1, D_MODEL)
        i = l // 2
        if l % 2 == 0:
            u_t, q, k, v = _even_in(x, g1, modt, ab_w_in[i].astype(BF16), bd,
                                    head_gain(nat_qn[i]), head_gain(nat_kn[i]))
            a_b, w_bd, c_bd = _s5_params(ssm_lam_re[i], ssm_lam_im[i], ssm_log_dt[i], ssm_b_re[i], ssm_b_im[i],
                                         ssm_c_re[i], ssm_c_im[i])
            st = state_ssm[:, i].reshape(DEC_BATCH, 2, 2, SSM_LANES).transpose(1, 2, 0, 3)
            s0 = jnp.zeros((2, 2, N_TILES, SSM_LANES), F32)
            first = N_CTX_TILES + LAT_TILES_PER_SEQ * np.arange(DEC_BATCH)
            s0 = s0.at[0, :, first].set(st[0].transpose(1, 0, 2))
            s0 = s0.at[1, :, first + LAT_TILES_PER_SEQ - 1].set(st[1].transpose(1, 0, 2))
            y_t, fin = _s5_scan(u_t.reshape(TILE, N_TILES, A_WIDTH), a_b, w_bd, c_bd, s0)
            o = _nat_attention(q, k, v,
                               cache_nat_k[:, i].reshape(DEC_BATCH, PAST_LEN, B_WIDTH),
                               cache_nat_v[:, i].reshape(DEC_BATCH, PAST_LEN, B_WIDTH),
                               _nat_bias(nat_rpb[i]))
            x = _even_out(y_t.reshape(TILE, N_TILES * A_WIDTH), u_t, o, x, modt,
                          ssm_d[i].reshape(1, A_WIDTH), ssm_glu_w[i].astype(BF16),
                          ssm_glu_b[i].reshape(1, A_WIDTH), ab_w_out[i].astype(BF16))
            ssm_out.append(fin[:, :, :N_CTX_TILES].transpose(2, 0, 1, 3).reshape(BATCH, 2, 2, A_GROUPS, A_STATE))
            nat_k_out.append(k[:N_CTX_TOK].reshape(BATCH, SEQ, B_HEADS, HEAD_DIM))
            nat_v_out.append(v[:N_CTX_TOK].reshape(BATCH, SEQ, B_HEADS, HEAD_DIM))
        else:
            q, k_plain, k_rot, v = _odd_in(x, g1, modt, swa_w_qkv[i].astype(BF16), bd,
                                           head_gain(swa_qn[i]), head_gain(swa_kn[i]), rope_cos, rope_sin)
            o = _swa_attention(q, k_rot, v,
                               cache_swa_k[:, i].reshape(DEC_BATCH, PAST_LEN, nkv),
                               cache_swa_v[:, i].reshape(DEC_BATCH, PAST_LEN, nkv), swa_sink[i])
            x = _odd_out(o, x, modt, swa_w_o[i].astype(BF16))
            swa_k_out.append(k_plain[:N_CTX_TOK].reshape(BATCH, SEQ, C_KV_HEADS, HEAD_DIM))
            swa_v_out.append(v[:N_CTX_TOK].reshape(BATCH, SEQ, C_KV_HEADS, HEAD_DIM))
        gate_rows = mod[l][tok_row][:, 5 * D_MODEL:]
        x = _moe(x, norm2_g[l].reshape(1, D_MODEL), modt, gate_rows, router_w[l], router_b[l],
                 moe_w_gu[l], moe_b_gu[l], moe_w_down[l], moe_b_down[l])

    y_prompt = x[:N_CTX_TOK].reshape(BATCH, SEQ, D_MODEL)
    y_sample = x[N_CTX_TOK:].reshape(DEC_BATCH, DEC_SEQ, D_MODEL)
    return (y_prompt, y_sample,
            jnp.stack(nat_k_out, axis=1), jnp.stack(nat_v_out, axis=1),
            jnp.stack(swa_k_out, axis=1), jnp.stack(swa_v_out, axis=1),
            jnp.stack(ssm_out, axis=1))
```

```python
import functools

import jax
import jax.numpy as jnp
import numpy as np
from jax import lax
from jax.experimental import pallas as pl
from jax.experimental.pallas import tpu as pltpu

F32 = jnp.float32
BF16 = jnp.bfloat16

D_MODEL = 1024
BATCH = 32
SEQ = 256
DEPTH = 4
DEC_BATCH = 2
DEC_SEQ = 1024
PAST_LEN = 256
GRID_W = 64
HEAD_DIM = 64
N_MOD = 6
A_WIDTH = 512
A_GROUP_CH = 16
A_GROUPS = 32
A_STATE = 64
B_HEADS = 8
B_WIDTH = 512
WIN_R_MAX = 8
WIN_C = 16
C_HEADS = 16
C_KV_HEADS = 2
C_GROUP = 8
WINDOW = 128
ROPE_BASE = 10000.0
N_EXPERTS = 32
TOP_K = 4
D_EXPERT = 1024
SWIGLU_LIMIT = 7.0
SWIGLU_ALPHA = 1.702
EPS = 1e-6
NEG_INF = -1e30

LANES = 128
SUBLANES = 8
TILE = 256
N_CTX_TOK = BATCH * SEQ
N_LAT_TOK = DEC_BATCH * DEC_SEQ
N_TOK = N_CTX_TOK + N_LAT_TOK
N_TILES = N_TOK // TILE
N_CTX_TILES = N_CTX_TOK // TILE
LAT_TILES_PER_SEQ = DEC_SEQ // TILE
SSM_LANES = A_GROUPS * A_STATE
SSM_SLABS = A_WIDTH // LANES
SSM_SLAB_STATES = SSM_LANES // SSM_SLABS
SCAN_ROWS = 8
SCAN_GROUPS = N_TILES // SCAN_ROWS
SCAN_CHUNK = 64
SCAN_LANE_PARTS = 2
MOE_TM = 256
N_ASG = N_TOK * TOP_K
ASG_STRIDE = 1 << 16
MOE_BLOCKS = N_ASG // MOE_TM + N_EXPERTS
MOE_ROWS = MOE_BLOCKS * MOE_TM
ROUTER_PAD = LANES
VMEM_LIMIT = 56 * 1024 * 1024
ATT_SCALE = HEAD_DIM ** -0.5


def _cparams(*sem):
    return pltpu.CompilerParams(dimension_semantics=sem, vmem_limit_bytes=VMEM_LIMIT)


def _dot(a, b):
    return jnp.dot(a, b, preferred_element_type=F32)


def _dot_nt(a, b):
    return lax.dot_general(a, b, (((1,), (1,)), ((), ())), preferred_element_type=F32)


def _split(a):
    hi = a.astype(BF16)
    lo = (a - hi.astype(F32)).astype(BF16)
    return hi, lo


def _dot3(a, b):
    a_hi, a_lo = _split(a)
    b_hi, b_lo = _split(b)
    return _dot(a_hi, b_hi) + (_dot(a_hi, b_lo) + _dot(a_lo, b_hi))


def _silu(x):
    return x * jax.nn.sigmoid(x)


def _norm_mod(x, g, shift, scale):
    y = x * lax.rsqrt(jnp.mean(x * x, axis=-1, keepdims=True) + EPS)
    return (y * g) * (1.0 + scale) + shift


def _head_rms(x, bd, gain):
    sq_hi, sq_lo = _split(x * x)
    ms = _dot(sq_hi, bd) + _dot(sq_lo, bd)
    return x * lax.rsqrt(ms + EPS) * gain


def _lane_half(shape, half):
    lane = lax.broadcasted_iota(jnp.int32, shape, len(shape) - 1)
    return (lane < HEAD_DIM) if half == 0 else (lane >= HEAD_DIM)


def _mod_kernel(cond_ref, w_ref, b_ref, o_ref):
    o_ref[0] = _dot3(_silu(cond_ref[...]), w_ref[0]) + b_ref[0]


def _modulation(cond, mod_w, mod_b):
    nc = N_MOD
    return pl.pallas_call(
        _mod_kernel,
        out_shape=jax.ShapeDtypeStruct((DEPTH, SUBLANES, N_MOD * D_MODEL), F32),
        grid=(DEPTH, nc),
        in_specs=[
            pl.BlockSpec((SUBLANES, D_MODEL), lambda l, c: (0, 0)),
            pl.BlockSpec((1, D_MODEL, D_MODEL), lambda l, c: (l, 0, c)),
            pl.BlockSpec((1, 1, D_MODEL), lambda l, c: (l, 0, c)),
        ],
        out_specs=pl.BlockSpec((1, SUBLANES, D_MODEL), lambda l, c: (l, 0, c)),
        compiler_params=_cparams("arbitrary", "arbitrary"),
        name="modulation",
    )(cond, mod_w, mod_b.reshape(DEPTH, 1, N_MOD * D_MODEL))


def _mod_spec(col):
    return pl.BlockSpec((1, 1, D_MODEL), lambda i: (i, 0, col))


def _tile_spec(width):
    return pl.BlockSpec((TILE, width), lambda i: (i, 0))


def _full_spec(shape):
    nd = len(shape)
    return pl.BlockSpec(shape, lambda i: (0,) * nd)


def _even_in_kernel(x_ref, g_ref, sh_ref, sc_ref, w_ref, bd_ref, qn_ref, kn_ref,
                    u_ref, q_ref, k_ref, v_ref):
    h = _norm_mod(x_ref[...], g_ref[...], sh_ref[0], sc_ref[0]).astype(BF16)
    bd = bd_ref[...]
    u_ref[...] = _dot(h, w_ref[:, 0:A_WIDTH])
    for s in range(B_WIDTH // LANES):
        lo = A_WIDTH + s * LANES
        q = _dot(h, w_ref[:, lo:lo + LANES])
        q_ref[:, s * LANES:(s + 1) * LANES] = _head_rms(q, bd, qn_ref[...]).astype(BF16)
        lo = A_WIDTH + B_WIDTH + s * LANES
        k = _dot(h, w_ref[:, lo:lo + LANES])
        k_ref[:, s * LANES:(s + 1) * LANES] = _head_rms(k, bd, kn_ref[...])
    v_ref[...] = _dot(h, w_ref[:, A_WIDTH + 2 * B_WIDTH:])


def _even_in(x, g, modt, w, bd, qn, kn):
    n_out = A_WIDTH + 3 * B_WIDTH
    return pl.pallas_call(
        _even_in_kernel,
        out_shape=(
            jax.ShapeDtypeStruct((TILE, N_TILES * A_WIDTH), F32),
            jax.ShapeDtypeStruct((N_TOK, B_WIDTH), BF16),
            jax.ShapeDtypeStruct((N_TOK, B_WIDTH), F32),
            jax.ShapeDtypeStruct((N_TOK, B_WIDTH), F32),
        ),
        grid=(N_TILES,),
        in_specs=[
            _tile_spec(D_MODEL), _full_spec((1, D_MODEL)), _mod_spec(0), _mod_spec(1),
            _full_spec((D_MODEL, n_out)), _full_spec((LANES, LANES)),
            _full_spec((1, LANES)), _full_spec((1, LANES)),
        ],
        out_specs=(
            pl.BlockSpec((TILE, A_WIDTH), lambda i: (0, i)),
            _tile_spec(B_WIDTH), _tile_spec(B_WIDTH), _tile_spec(B_WIDTH),
        ),
        compiler_params=_cparams("arbitrary"),
        name="even_in",
    )(x, g, modt, modt, w, bd, qn, kn)


def _cmul(ar, ai, br, bi):
    return ar * br - ai * bi, ar * bi + ai * br


def _s5_kernel(u_ref, a_ref, w_ref, c_ref, s0_ref, y_ref, fin_ref, xr, xi, st_r, st_i):
    grp = pl.program_id(0)
    drn = pl.program_id(1)
    n_chunks = TILE // SCAN_CHUNK
    rows = SCAN_CHUNK * SCAN_ROWS
    part = SSM_LANES // SCAN_LANE_PARTS

    def run(store):
        @pl.loop(0, n_chunks)
        def _(c):
            cc = jnp.where(drn == 0, c, n_chunks - 1 - c)
            t0 = pl.multiple_of(cc * SCAN_CHUNK, SCAN_CHUNK)
            uu = u_ref[pl.ds(t0, SCAN_CHUNK), :, :]
            for s in range(SSM_SLABS):
                us = uu[:, :, s * LANES:(s + 1) * LANES].reshape(rows, LANES).astype(BF16)
                cols = slice(s * SSM_SLAB_STATES, (s + 1) * SSM_SLAB_STATES)
                xr[:, cols] = _dot(us, w_ref[0, 0, s])
                xi[:, cols] = _dot(us, w_ref[0, 1, s])
            for p in range(SCAN_LANE_PARTS):
                ls = slice(p * part, (p + 1) * part)
                ar = a_ref[0, 0, :, ls]
                ai = a_ref[0, 1, :, ls]

                def step(j, carry):
                    sr, si = carry
                    tt = jnp.where(drn == 0, j, SCAN_CHUNK - 1 - j)
                    r0 = pl.multiple_of(tt * SCAN_ROWS, SCAN_ROWS)
                    nr = ar * sr - ai * si + xr[pl.ds(r0, SCAN_ROWS), ls]
                    ni = ar * si + ai * sr + xi[pl.ds(r0, SCAN_ROWS), ls]
                    if store:
                        xr[pl.ds(r0, SCAN_ROWS), ls] = nr
                        xi[pl.ds(r0, SCAN_ROWS), ls] = ni
                    return nr, ni

                sr, si = lax.fori_loop(0, SCAN_CHUNK, step, (st_r[:, ls], st_i[:, ls]), unroll=4)
                st_r[:, ls] = sr
                st_i[:, ls] = si
            if store:
                for s in range(SSM_SLABS):
                    cols = slice(s * SSM_SLAB_STATES, (s + 1) * SSM_SLAB_STATES)
                    ys = (_dot(xr[:, cols].astype(BF16), c_ref[0, 0, s])
                          - _dot(xi[:, cols].astype(BF16), c_ref[0, 1, s]))
                    ys = ys.reshape(SCAN_CHUNK, SCAN_ROWS, LANES)
                    lanes = slice(s * LANES, (s + 1) * LANES)

                    @pl.when(drn == 0)
                    def _():
                        y_ref[pl.ds(t0, SCAN_CHUNK), :, lanes] = ys

                    @pl.when(drn != 0)
                    def _():
                        y_ref[pl.ds(t0, SCAN_CHUNK), :, lanes] += ys

    st_r[...] = s0_ref[0, 0]
    st_i[...] = s0_ref[0, 1]

    @pl.when(grp == SCAN_GROUPS - 1)
    def _():
        st_r[...] = jnp.zeros_like(st_r)
        st_i[...] = jnp.zeros_like(st_i)
        run(False)
        pr, pi = a_ref[0, 0], a_ref[0, 1]
        for _ in range(8):
            pr, pi = _cmul(pr, pi, pr, pi)
        fr, fi = st_r[...], st_i[...]
        s0r, s0i = s0_ref[0, 0], s0_ref[0, 1]
        row = lax.broadcasted_iota(jnp.int32, (SCAN_ROWS, SSM_LANES), 0)
        quarter = row % LAT_TILES_PER_SEQ
        fwd = drn == 0
        keep = quarter != jnp.where(fwd, 0, LAT_TILES_PER_SEQ - 1)
        ir, ii = s0r, s0i
        for _ in range(LAT_TILES_PER_SEQ - 1):
            nr, ni = _cmul(pr, pi, ir, ii)
            nr, ni = nr + fr, ni + fi
            nr = jnp.where(fwd, pltpu.roll(nr, 1, 0), pltpu.roll(nr, SCAN_ROWS - 1, 0))
            ni = jnp.where(fwd, pltpu.roll(ni, 1, 0), pltpu.roll(ni, SCAN_ROWS - 1, 0))
            ir = s0r + jnp.where(keep, nr, 0.0)
            ii = s0i + jnp.where(keep, ni, 0.0)
        st_r[...] = ir
        st_i[...] = ii

    run(True)
    fin_ref[0, 0] = st_r[...]
    fin_ref[0, 1] = st_i[...]


def _s5_scan(u_tb, a_b, w_bd, c_bd, s0):
    rows = SCAN_CHUNK * SCAN_ROWS
    return pl.pallas_call(
        _s5_kernel,
        out_shape=(
            jax.ShapeDtypeStruct((TILE, N_TILES, A_WIDTH), F32),
            jax.ShapeDtypeStruct((2, 2, N_TILES, SSM_LANES), F32),
        ),
        grid=(SCAN_GROUPS, 2),
        in_specs=[
            pl.BlockSpec((TILE, SCAN_ROWS, A_WIDTH), lambda g, d: (0, g, 0)),
            pl.BlockSpec((1, 2, SCAN_ROWS, SSM_LANES), lambda g, d: (d, 0, 0, 0)),
            pl.BlockSpec((1, 2, SSM_SLABS, LANES, SSM_SLAB_STATES), lambda g, d: (d, 0, 0, 0, 0)),
            pl.BlockSpec((1, 2, SSM_SLABS, SSM_SLAB_STATES, LANES), lambda g, d: (d, 0, 0, 0, 0)),
            pl.BlockSpec((1, 2, SCAN_ROWS, SSM_LANES), lambda g, d: (d, 0, g, 0)),
        ],
        out_specs=(
            pl.BlockSpec((TILE, SCAN_ROWS, A_WIDTH), lambda g, d: (0, g, 0)),
            pl.BlockSpec((1, 2, SCAN_ROWS, SSM_LANES), lambda g, d: (d, 0, g, 0)),
        ),
        scratch_shapes=[
            pltpu.VMEM((rows, SSM_LANES), F32), pltpu.VMEM((rows, SSM_LANES), F32),
            pltpu.VMEM((SCAN_ROWS, SSM_LANES), F32), pltpu.VMEM((SCAN_ROWS, SSM_LANES), F32),
        ],
        compiler_params=_cparams("arbitrary", "arbitrary"),
        name="s5_scan",
    )(u_tb, a_b, w_bd, c_bd, s0)


def _s5_params(lam_re, lam_im, log_dt, b_re, b_im, c_re, c_im):
    dt = jnp.exp(log_dt)[..., None]
    mag = jnp.exp(lam_re * dt)
    ab_re, ab_im = mag * jnp.cos(lam_im * dt), mag * jnp.sin(lam_im * dt)
    den = lam_re * lam_re + lam_im * lam_im
    nr, ni = ab_re - 1.0, ab_im
    f_re = (nr * lam_re + ni * lam_im) / den
    f_im = (ni * lam_re - nr * lam_im) / den
    bb_re = f_re[..., None] * b_re - f_im[..., None] * b_im
    bb_im = f_re[..., None] * b_im + f_im[..., None] * b_re
    a_b = jnp.stack([ab_re, ab_im], axis=1).reshape(2, 2, 1, SSM_LANES)
    a_b = jnp.broadcast_to(a_b, (2, 2, SCAN_ROWS, SSM_LANES))
    gps = A_GROUPS // SSM_SLABS
    eye = jnp.eye(gps, dtype=F32)

    def in_bd(bb):
        bb = bb.reshape(2, SSM_SLABS, gps, A_STATE, A_GROUP_CH)
        m = jnp.einsum('dsgph,gk->dsghkp', bb, eye)
        return m.reshape(2, SSM_SLABS, LANES, SSM_SLAB_STATES)

    def out_bd(cc):
        cc = cc.reshape(2, SSM_SLABS, gps, A_GROUP_CH, A_STATE)
        m = jnp.einsum('dsghp,gk->dsgpkh', cc, eye)
        return m.reshape(2, SSM_SLABS, SSM_SLAB_STATES, LANES)

    w_bd = jnp.stack([in_bd(bb_re), in_bd(bb_im)], axis=1).astype(BF16)
    c_bd = jnp.stack([out_bd(c_re), out_bd(c_im)], axis=1).astype(BF16)
    return a_b, w_bd, c_bd


def _softmax_pv(scores, values, sink=None):
    m = functools.reduce(jnp.maximum, [jnp.max(s, axis=-1, keepdims=True) for s in scores])
    if sink is not None:
        m = jnp.maximum(m, sink)
    den = None
    acc = None
    for s, v in zip(scores, values):
        e = jnp.exp(s - m)
        d = jnp.sum(e, axis=-1, keepdims=True)
        o = _dot(e.astype(BF16), v)
        den = d if den is None else den + d
        acc = o if acc is None else acc + o
    if sink is not None:
        den = den + jnp.exp(sink - m)
    return acc / den


def _nat_ctx_kernel(q_ref, k_ref, v_ref, o_ref):
    for p in range(B_WIDTH // LANES):
        ls = slice(p * LANES, (p + 1) * LANES)
        qs = q_ref[:, ls]
        ks = k_ref[:, ls].astype(BF16)
        vs = v_ref[:, ls].astype(BF16)
        out = None
        for half in range(2):
            msk = _lane_half(qs.shape, half)
            qm = jnp.where(msk, qs, jnp.zeros_like(qs))
            s = _dot_nt(qm, ks) * ATT_SCALE
            o = _softmax_pv([s], [vs])
            out = o if out is None else jnp.where(msk, o, out)
        o_ref[:, ls] = out.astype(BF16)


def _nat_lat_kernel(q_ref, k_ref, v_ref, kc_ref, vc_ref, bias_ref, o_ref):
    r = pl.program_id(1)
    wr = WIN_R_MAX
    rs = jnp.clip(r - wr // 2, 0, DEC_SEQ // GRID_W - wr)
    k0 = pl.multiple_of(rs * GRID_W, GRID_W)
    n_nb = wr * GRID_W
    for p in range(B_WIDTH // LANES):
        ls = slice(p * LANES, (p + 1) * LANES)
        qs = q_ref[:, ls]
        ks = k_ref[pl.ds(k0, n_nb), ls].astype(BF16)
        vs = v_ref[pl.ds(k0, n_nb), ls].astype(BF16)
        kc = kc_ref[0, :, ls].astype(BF16)
        vc = vc_ref[0, :, ls].astype(BF16)
        out = None
        for half in range(2):
            msk = _lane_half(qs.shape, half)
            qm = jnp.where(msk, qs, jnp.zeros_like(qs))
            s_nb = _dot_nt(qm, ks) * ATT_SCALE + bias_ref[2 * p + half, 0]
            s_ctx = _dot_nt(qm, kc) * ATT_SCALE
            o = _softmax_pv([s_nb, s_ctx], [vs, vc])
            out = o if out is None else jnp.where(msk, o, out)
        o_ref[:, ls] = out.astype(BF16)


def _nat_attention(q, k, v, kc, vc, bias):
    o_ctx = pl.pallas_call(
        _nat_ctx_kernel,
        out_shape=jax.ShapeDtypeStruct((N_CTX_TOK, B_WIDTH), BF16),
        grid=(N_CTX_TILES,),
        in_specs=[_tile_spec(B_WIDTH)] * 3,
        out_specs=_tile_spec(B_WIDTH),
        compiler_params=_cparams("arbitrary"),
        name="nat_ctx",
    )(q, k, v)
    rows = DEC_SEQ // GRID_W
    ctx_seqs = N_CTX_TOK // DEC_SEQ

    def bias_idx(b, r):
        return (0, r - jnp.clip(r - WIN_R_MAX // 2, 0, rows - WIN_R_MAX), 0, 0)

    o_lat = pl.pallas_call(
        _nat_lat_kernel,
        out_shape=jax.ShapeDtypeStruct((N_LAT_TOK, B_WIDTH), BF16),
        grid=(DEC_BATCH, rows),
        in_specs=[
            pl.BlockSpec((GRID_W, B_WIDTH), lambda b, r: (N_CTX_TOK // GRID_W + b * rows + r, 0)),
            pl.BlockSpec((DEC_SEQ, B_WIDTH), lambda b, r: (ctx_seqs + b, 0)),
            pl.BlockSpec((DEC_SEQ, B_WIDTH), lambda b, r: (ctx_seqs + b, 0)),
            pl.BlockSpec((1, PAST_LEN, B_WIDTH), lambda b, r: (b, 0, 0)),
            pl.BlockSpec((1, PAST_LEN, B_WIDTH), lambda b, r: (b, 0, 0)),
            pl.BlockSpec((B_HEADS, 1, GRID_W, WIN_R_MAX * GRID_W), bias_idx),
        ],
        out_specs=pl.BlockSpec((GRID_W, B_WIDTH), lambda b, r: (b * rows + r, 0)),
        compiler_params=_cparams("arbitrary", "arbitrary"),
        name="nat_lat",
    )(q, k, v, kc, vc, bias)
    return jnp.concatenate([o_ctx, o_lat], axis=0)


def _nat_bias(rpb):
    qc = np.arange(GRID_W)
    kc = np.arange(GRID_W)
    cs = np.clip(qc - WIN_C // 2, 0, GRID_W - WIN_C)
    ok = (kc[None, :] >= cs[:, None]) & (kc[None, :] < cs[:, None] + WIN_C)
    dc = np.clip(kc[None, :] - qc[:, None] + (WIN_C - 1), 0, 2 * WIN_C - 2)
    pick = (dc[:, :, None] == np.arange(2 * WIN_C - 1)).astype(np.float32)
    t = jnp.einsum('hrc,qkc->hrqk', rpb.astype(F32), pick, precision=lax.Precision.HIGHEST)
    t = jnp.where(ok[None, None], t, NEG_INF)
    per_d = [t[:, WIN_R_MAX - 1 - d:2 * WIN_R_MAX - 1 - d].transpose(0, 2, 1, 3)
             .reshape(B_HEADS, GRID_W, WIN_R_MAX * GRID_W) for d in range(WIN_R_MAX)]
    return jnp.stack(per_d, axis=1)


def _gqa_heads(q_ref, sink_ref, k_segs, v_segs, masks, o_ref):
    k_rot = [pltpu.roll(k.astype(F32), HEAD_DIM, 1).astype(BF16) for k in k_segs]
    v_rot = [pltpu.roll(v.astype(F32), HEAD_DIM, 1).astype(BF16) for v in v_segs]
    for p in range(C_HEADS // 2):
        ls = slice(p * LANES, (p + 1) * LANES)
        qs = q_ref[:, ls]
        out = None
        for half in range(2):
            head = 2 * p + half
            kv = head // C_GROUP
            ks = k_segs if kv == half else k_rot
            vs = v_segs if kv == half else v_rot
            msk = _lane_half(qs.shape, half)
            qm = jnp.where(msk, qs, jnp.zeros_like(qs))
            scores = []
            for kseg, m in zip(ks, masks):
                s = _dot_nt(qm, kseg) * ATT_SCALE
                scores.append(s if m is None else jnp.where(m, s, NEG_INF))
            o = _softmax_pv(scores, vs, sink=sink_ref[head])
            out = o if out is None else jnp.where(msk, o, out)
        o_ref[:, ls] = out.astype(BF16)


def _swa_ctx_kernel(sink_ref, q_ref, k_ref, v_ref, o_ref):
    _gqa_heads(q_ref, sink_ref, [k_ref[...].astype(BF16)], [v_ref[...].astype(BF16)], [None], o_ref)


def _swa_lat_kernel(sink_ref, q_ref, k_ref, v_ref, kc_ref, vc_ref, o_ref):
    n = pl.program_id(1)
    n_win = 3 * WINDOW
    start = jnp.clip((n - 1) * WINDOW, 0, DEC_SEQ - n_win)
    k0 = pl.multiple_of(start, WINDOW)
    qpos = n * WINDOW + lax.broadcasted_iota(jnp.int32, (WINDOW, n_win), 0)
    kpos = start + lax.broadcasted_iota(jnp.int32, (WINDOW, n_win), 1)
    ok = jnp.abs(qpos - kpos) <= WINDOW
    _gqa_heads(q_ref, sink_ref,
               [k_ref[pl.ds(k0, n_win), :].astype(BF16), kc_ref[0].astype(BF16)],
               [v_ref[pl.ds(k0, n_win), :].astype(BF16), vc_ref[0].astype(BF16)],
               [ok, None], o_ref)


def _swa_attention(q, k, v, kc, vc, sink):
    nq, nkv = C_HEADS * HEAD_DIM, C_KV_HEADS * HEAD_DIM
    smem = pl.BlockSpec(memory_space=pltpu.SMEM)
    o_ctx = pl.pallas_call(
        _swa_ctx_kernel,
        out_shape=jax.ShapeDtypeStruct((N_CTX_TOK, nq), BF16),
        grid=(N_CTX_TILES,),
        in_specs=[smem, _tile_spec(nq), _tile_spec(nkv), _tile_spec(nkv)],
        out_specs=_tile_spec(nq),
        compiler_params=_cparams("arbitrary"),
        name="swa_ctx",
    )(sink, q, k, v)
    nb = DEC_SEQ // WINDOW
    ctx_seqs = N_CTX_TOK // DEC_SEQ
    o_lat = pl.pallas_call(
        _swa_lat_kernel,
        out_shape=jax.ShapeDtypeStruct((N_LAT_TOK, nq), BF16),
        grid=(DEC_BATCH, nb),
        in_specs=[
            smem,
            pl.BlockSpec((WINDOW, nq), lambda b, n: (N_CTX_TOK // WINDOW + b * nb + n, 0)),
            pl.BlockSpec((DEC_SEQ, nkv), lambda b, n: (ctx_seqs + b, 0)),
            pl.BlockSpec((DEC_SEQ, nkv), lambda b, n: (ctx_seqs + b, 0)),
            pl.BlockSpec((1, PAST_LEN, nkv), lambda b, n: (b, 0, 0)),
            pl.BlockSpec((1, PAST_LEN, nkv), lambda b, n: (b, 0, 0)),
        ],
        out_specs=pl.BlockSpec((WINDOW, nq), lambda b, n: (b * nb + n, 0)),
        compiler_params=_cparams("arbitrary", "arbitrary"),
        name="swa_lat",
    )(sink, q, k, v, kc, vc)
    return jnp.concatenate([o_ctx, o_lat], axis=0)


def _even_out_kernel(y_ref, u_ref, o_ref, x_ref, gate_ref, d_ref, gw_ref, gb_ref, w_ref, xo_ref):
    yy = y_ref[...] + d_ref[...] * u_ref[...]
    g = jax.nn.gelu(yy)
    a = g * jax.nn.sigmoid(_dot(g.astype(BF16), gw_ref[...]) + gb_ref[...])
    mix = _dot(a.astype(BF16), w_ref[0:A_WIDTH, :]) + _dot(o_ref[...], w_ref[A_WIDTH:, :])
    xo_ref[...] = x_ref[...] + gate_ref[0] * mix


def _even_out(y_t, u_t, o, x, modt, d_skip, glu_w, glu_b, w_out):
    tm_spec = pl.BlockSpec((TILE, A_WIDTH), lambda i: (0, i))
    return pl.pallas_call(
        _even_out_kernel,
        out_shape=jax.ShapeDtypeStruct((N_TOK, D_MODEL), F32),
        grid=(N_TILES,),
        in_specs=[
            tm_spec, tm_spec, _tile_spec(B_WIDTH), _tile_spec(D_MODEL), _mod_spec(2),
            _full_spec((1, A_WIDTH)), _full_spec((A_WIDTH, A_WIDTH)), _full_spec((1, A_WIDTH)),
            _full_spec((A_WIDTH + B_WIDTH, D_MODEL)),
        ],
        out_specs=_tile_spec(D_MODEL),
        compiler_params=_cparams("arbitrary"),
        name="even_out",
    )(y_t, u_t, o, x, modt, d_skip, glu_w, glu_b, w_out)


def _rope(x, cos, sin):
    lane = lax.broadcasted_iota(jnp.int32, x.shape, 1)
    first = (lane % (HEAD_DIM // 2)) < (HEAD_DIM // 4)
    partner = jnp.where(first, pltpu.roll(x, LANES - HEAD_DIM // 4, 1), pltpu.roll(x, HEAD_DIM // 4, 1))
    return x * cos + partner * sin


def _odd_in_kernel(x_ref, g_ref, sh_ref, sc_ref, w_ref, bd_ref, qn_ref, kn_ref, cos_ref, sin_ref,
                   q_ref, kc_ref, kr_ref, v_ref):
    i = pl.program_id(0)
    h = _norm_mod(x_ref[...], g_ref[...], sh_ref[0], sc_ref[0]).astype(BF16)
    bd = bd_ref[...]
    lat = i >= N_CTX_TILES
    cos = jnp.where(lat, cos_ref[...], 1.0)
    sin = jnp.where(lat, sin_ref[...], 0.0)
    nq = C_HEADS * HEAD_DIM
    for s in range(nq // LANES):
        q = _head_rms(_dot(h, w_ref[:, s * LANES:(s + 1) * LANES]), bd, qn_ref[...])
        q_ref[:, s * LANES:(s + 1) * LANES] = _rope(q, cos, sin).astype(BF16)
    k = _head_rms(_dot(h, w_ref[:, nq:nq + LANES]), bd, kn_ref[...])
    kc_ref[...] = k
    kr_ref[...] = _rope(k, cos, sin)
    v_ref[...] = _dot(h, w_ref[:, nq + LANES:])


def _odd_in(x, g, modt, w, bd, qn, kn, cos, sin):
    nq, nkv = C_HEADS * HEAD_DIM, C_KV_HEADS * HEAD_DIM
    lat_spec = pl.BlockSpec((TILE, LANES), lambda i: (jnp.maximum(i - N_CTX_TILES, 0) % LAT_TILES_PER_SEQ, 0))
    return pl.pallas_call(
        _odd_in_kernel,
        out_shape=(
            jax.ShapeDtypeStruct((N_TOK, nq), BF16),
            jax.ShapeDtypeStruct((N_TOK, nkv), F32),
            jax.ShapeDtypeStruct((N_TOK, nkv), F32),
            jax.ShapeDtypeStruct((N_TOK, nkv), F32),
        ),
        grid=(N_TILES,),
        in_specs=[
            _tile_spec(D_MODEL), _full_spec((1, D_MODEL)), _mod_spec(0), _mod_spec(1),
            _full_spec((D_MODEL, nq + 2 * nkv)), _full_spec((LANES, LANES)),
            _full_spec((1, LANES)), _full_spec((1, LANES)), lat_spec, lat_spec,
        ],
        out_specs=(_tile_spec(nq), _tile_spec(nkv), _tile_spec(nkv), _tile_spec(nkv)),
        compiler_params=_cparams("arbitrary"),
        name="odd_in",
    )(x, g, modt, modt, w, bd, qn, kn, cos, sin)


def _rope_tables():
    nf = HEAD_DIM // 4
    inv = ROPE_BASE ** (-jnp.arange(nf, dtype=F32) / nf)
    t = jnp.arange(DEC_SEQ)
    pos = jnp.stack([t // GRID_W, t % GRID_W], axis=-1).astype(F32)
    ang = pos[:, :, None] * inv
    cos, sin = jnp.cos(ang), jnp.sin(ang)
    cos_h = jnp.stack([cos, cos], axis=2).reshape(DEC_SEQ, HEAD_DIM)
    sin_h = jnp.stack([-sin, sin], axis=2).reshape(DEC_SEQ, HEAD_DIM)
    return jnp.tile(cos_h, (1, 2)), jnp.tile(sin_h, (1, 2))


def _odd_out_kernel(o_ref, x_ref, gate_ref, w_ref, xo_ref):
    xo_ref[...] = x_ref[...] + gate_ref[0] * _dot(o_ref[...], w_ref[...])


def _odd_out(o, x, modt, w_o):
    nq = C_HEADS * HEAD_DIM
    return pl.pallas_call(
        _odd_out_kernel,
        out_shape=jax.ShapeDtypeStruct((N_TOK, D_MODEL), F32),
        grid=(N_TILES,),
        in_specs=[_tile_spec(nq), _tile_spec(D_MODEL), _mod_spec(2), _full_spec((nq, D_MODEL))],
        out_specs=_tile_spec(D_MODEL),
        compiler_params=_cparams("arbitrary"),
        name="odd_out",
    )(o, x, modt, w_o)


def _moe_in_kernel(x_ref, g_ref, sh_ref, sc_ref, rw_ref, rb_ref, h_ref, lg_ref):
    h = _norm_mod(x_ref[...], g_ref[...], sh_ref[0], sc_ref[0])
    h_ref[...] = h.astype(BF16)
    lg_ref[...] = _dot3(h, rw_ref[...]) + rb_ref[...]


def _moe_in(x, g, modt, rw, rb):
    return pl.pallas_call(
        _moe_in_kernel,
        out_shape=(
            jax.ShapeDtypeStruct((N_TOK, D_MODEL), BF16),
            jax.ShapeDtypeStruct((N_TOK, ROUTER_PAD), F32),
        ),
        grid=(N_TILES,),
        in_specs=[
            _tile_spec(D_MODEL), _full_spec((1, D_MODEL)), _mod_spec(3), _mod_spec(4),
            _full_spec((D_MODEL, ROUTER_PAD)), _full_spec((1, ROUTER_PAD)),
        ],
        out_specs=(_tile_spec(D_MODEL), _tile_spec(ROUTER_PAD)),
        compiler_params=_cparams("arbitrary"),
        name="moe_in",
    )(x, g, modt, modt, rw, rb)


def _moe_expert_kernel(be_ref, nb_ref, x_ref, wgu_ref, bgu_ref, wd_ref, bd_ref, o_ref):
    i = pl.program_id(0)

    @pl.when(i < nb_ref[0])
    def _():
        x = x_ref[...]
        g = _dot(x, wgu_ref[0, 0, :, 0:D_EXPERT].astype(BF16)) + bgu_ref[0, 0, :, 0:D_EXPERT]
        lin = _dot(x, wgu_ref[0, 0, :, D_EXPERT:].astype(BF16)) + bgu_ref[0, 0, :, D_EXPERT:]
        g = jnp.minimum(g, SWIGLU_LIMIT)
        lin = jnp.clip(lin, -SWIGLU_LIMIT, SWIGLU_LIMIT)
        act = g * jax.nn.sigmoid(SWIGLU_ALPHA * g) * (lin + 1.0)
        o_ref[...] = _dot(act.astype(BF16), wd_ref[0, 0].astype(BF16)) + bd_ref[0, 0]

    @pl.when(i >= nb_ref[0])
    def _():
        o_ref[...] = jnp.zeros_like(o_ref)


def _moe_experts(layer, block_e, n_used, xs, w_gu, b_gu, w_down, b_down):
    grid_spec = pltpu.PrefetchScalarGridSpec(
        num_scalar_prefetch=2,
        grid=(MOE_BLOCKS,),
        in_specs=[
            pl.BlockSpec((MOE_TM, D_MODEL), lambda i, be, nb: (jnp.minimum(i, nb[0] - 1), 0)),
            pl.BlockSpec((1, 1, D_MODEL, 2 * D_EXPERT), lambda i, be, nb: (layer, be[i], 0, 0)),
            pl.BlockSpec((1, 1, 1, 2 * D_EXPERT), lambda i, be, nb: (layer, be[i], 0, 0)),
            pl.BlockSpec((1, 1, D_EXPERT, D_MODEL), lambda i, be, nb: (layer, be[i], 0, 0)),
            pl.BlockSpec((1, 1, 1, D_MODEL), lambda i, be, nb: (layer, be[i], 0, 0)),
        ],
        out_specs=pl.BlockSpec((MOE_TM, D_MODEL), lambda i, be, nb: (i, 0)),
    )
    return pl.pallas_call(
        _moe_expert_kernel,
        out_shape=jax.ShapeDtypeStruct((MOE_ROWS, D_MODEL), F32),
        grid_spec=grid_spec,
        compiler_params=_cparams("arbitrary"),
        name="moe_experts",
    )(block_e, n_used, xs, w_gu, b_gu.reshape(DEPTH, N_EXPERTS, 1, 2 * D_EXPERT),
      w_down, b_down.reshape(DEPTH, N_EXPERTS, 1, D_MODEL))


def _moe_combine_kernel(x_ref, gmod_ref, gates_ref, y0_ref, y1_ref, y2_ref, y3_ref, o_ref):
    gates = gates_ref[...]
    acc = gates[:, 0:1] * y0_ref[...]
    for k, y_ref in enumerate((y1_ref, y2_ref, y3_ref), start=1):
        acc = acc + gates[:, k:k + 1] * y_ref[...]
    o_ref[...] = x_ref[...] + gmod_ref[0] * acc


def _moe_combine(x, modt, gates, ys):
    return pl.pallas_call(
        _moe_combine_kernel,
        out_shape=jax.ShapeDtypeStruct((N_TOK, D_MODEL), F32),
        grid=(N_TILES,),
        in_specs=[_tile_spec(D_MODEL), _mod_spec(5), _tile_spec(TOP_K)] + [_tile_spec(D_MODEL)] * TOP_K,
        out_specs=_tile_spec(D_MODEL),
        compiler_params=_cparams("arbitrary"),
        name="moe_combine",
    )(x, modt, gates, *ys)


def _lut(idx, table):
    n = table.shape[0]
    hit = idx[:, None] == jnp.arange(n, dtype=idx.dtype)[None, :]
    return jnp.sum(jnp.where(hit, table[None, :], 0), axis=1)


def _moe_route(logits):
    i32 = jnp.int32
    top_val, top_idx = lax.top_k(logits[:, :N_EXPERTS], TOP_K)
    gates = jax.nn.softmax(top_val, axis=-1)
    flat_e = top_idx.reshape(N_ASG).astype(i32)
    asg = jnp.arange(N_ASG, dtype=i32)
    skey = jnp.sort(flat_e * ASG_STRIDE + asg)
    sorted_e = skey // ASG_STRIDE
    sorted_asg = skey % ASG_STRIDE
    start = jnp.searchsorted(skey, jnp.arange(N_EXPERTS + 1, dtype=i32) * ASG_STRIDE).astype(i32)
    counts = start[1:] - start[:-1]
    pad_counts = (counts + MOE_TM - 1) // MOE_TM * MOE_TM
    pad_end = jnp.cumsum(pad_counts)
    pad_start = pad_end - pad_counts
    n_used = pad_end[-1] // MOE_TM
    pos = asg + _lut(sorted_e, pad_start - start[:-1])
    _, dest = lax.sort((sorted_asg, pos), num_keys=1)
    blk = jnp.arange(MOE_BLOCKS, dtype=i32)
    block_e = jnp.minimum(jnp.searchsorted(pad_end, blk * MOE_TM, side='right'), N_EXPERTS - 1).astype(i32)
    block_e = jnp.where(blk < n_used, block_e, block_e[jnp.maximum(n_used - 1, 0)])
    row = jnp.arange(MOE_ROWS, dtype=i32)
    e_row = jnp.repeat(block_e, MOE_TM)
    off = row - _lut(e_row, pad_start)
    valid = (off < _lut(e_row, counts)) & (row < n_used * MOE_TM)
    src = jnp.clip(_lut(e_row, start[:-1]) + off, 0, N_ASG - 1)
    row_tok = jnp.where(valid, sorted_asg[src] // TOP_K, 0)
    return gates, dest.reshape(N_TOK, TOP_K), row_tok, block_e, n_used.reshape(1).astype(i32)


def _moe(layer, x, g, modt, rw, rb, w_gu, b_gu, w_down, b_down):
    h, logits = _moe_in(x, g, modt, rw, rb)
    gates, dest, row_tok, block_e, n_used = _moe_route(logits)
    y_rows = _moe_experts(layer, block_e, n_used, h[row_tok], w_gu, b_gu, w_down, b_down)
    return _moe_combine(x, modt, gates, [y_rows[dest[:, k]] for k in range(TOP_K)])


def kernel(x_prompt, x_sample, cache_nat_k, cache_nat_v, cache_swa_k, cache_swa_v, state_ssm, c, c_ctx,
           norm1_g, norm2_g, mod_w, mod_b,
           ab_w_in, ab_w_out, ssm_lam_re, ssm_lam_im, ssm_log_dt, ssm_b_re, ssm_b_im, ssm_c_re, ssm_c_im,
           ssm_d, ssm_glu_w, ssm_glu_b, nat_qn, nat_kn, nat_rpb,
           swa_w_qkv, swa_w_o, swa_qn, swa_kn, swa_sink,
           moe_router_w, moe_router_b, moe_w_gu, moe_b_gu, moe_w_down, moe_b_down):
    x = jnp.concatenate([x_prompt.reshape(N_CTX_TOK, D_MODEL), x_sample.reshape(N_LAT_TOK, D_MODEL)], axis=0)
    cond = jnp.zeros((SUBLANES, D_MODEL), F32).at[0].set(c_ctx).at[1:1 + DEC_BATCH].set(c)
    mod = _modulation(cond, mod_w, mod_b)
    tile_row = np.concatenate([np.zeros(N_CTX_TILES, np.int32),
                               1 + np.arange(N_TILES - N_CTX_TILES, dtype=np.int32) // LAT_TILES_PER_SEQ])
    head_gain = lambda gn: jnp.tile(gn, 2).reshape(1, LANES)
    bd = jnp.asarray(np.kron(np.eye(2, dtype=np.float32), np.full((HEAD_DIM, HEAD_DIM), 1.0 / HEAD_DIM, np.float32)), BF16)
    rope_cos, rope_sin = _rope_tables()
    router_w = jnp.pad(moe_router_w, ((0, 0), (0, 0), (0, ROUTER_PAD - N_EXPERTS)))
    router_b = jnp.pad(moe_router_b, ((0, 0), (0, ROUTER_PAD - N_EXPERTS))).reshape(DEPTH, 1, ROUTER_PAD)
    nkv = C_KV_HEADS * HEAD_DIM

    nat_k_out, nat_v_out, swa_k_out, swa_v_out, ssm_out = [], [], [], [], []
    for l in range(DEPTH):
        modt = mod[l][tile_row].reshape(N_TILES, 1, N_MOD * D_MODEL)
        g1 = norm1_g[l].reshape(1, D_MODEL)
        i = l // 2
        if l % 2 == 0:
            u_t, q, k, v = _even_in(x, g1, modt, ab_w_in[i].astype(BF16), bd,
                                    head_gain(nat_qn[i]), head_gain(nat_kn[i]))
            a_b, w_bd, c_bd = _s5_params(ssm_lam_re[i], ssm_lam_im[i], ssm_log_dt[i], ssm_b_re[i], ssm_b_im[i],
                                         ssm_c_re[i], ssm_c_im[i])
            st = state_ssm[:, i].reshape(DEC_BATCH, 2, 2, SSM_LANES).transpose(1, 2, 0, 3)
            s0 = jnp.zeros((2, 2, N_TILES, SSM_LANES), F32)
            first = N_CTX_TILES + LAT_TILES_PER_SEQ * np.arange(DEC_BATCH)
            s0 = s0.at[0, :, first].set(st[0].transpose(1, 0, 2))
            s0 = s0.at[1, :, first + LAT_TILES_PER_SEQ - 1].set(st[1].transpose(1, 0, 2))
            y_t, fin = _s5_scan(u_t.reshape(TILE, N_TILES, A_WIDTH), a_b, w_bd, c_bd, s0)
            o = _nat_attention(q, k, v,
                               cache_nat_k[:, i].reshape(DEC_BATCH, PAST_LEN, B_WIDTH),
                               cache_nat_v[:, i].reshape(DEC_BATCH, PAST_LEN, B_WIDTH),
                               _nat_bias(nat_rpb[i]))
            x = _even_out(y_t.reshape(TILE, N_TILES * A_WIDTH), u_t, o, x, modt,
                          ssm_d[i].reshape(1, A_WIDTH), ssm_glu_w[i].astype(BF16),
                          ssm_glu_b[i].reshape(1, A_WIDTH), ab_w_out[i].astype(BF16))
            ssm_out.append(fin[:, :, :N_CTX_TILES].transpose(2, 0, 1, 3).reshape(BATCH, 2, 2, A_GROUPS, A_STATE))
            nat_k_out.append(k[:N_CTX_TOK].reshape(BATCH, SEQ, B_HEADS, HEAD_DIM))
            nat_v_out.append(v[:N_CTX_TOK].reshape(BATCH, SEQ, B_HEADS, HEAD_DIM))
        else:
            q, k_plain, k_rot, v = _odd_in(x, g1, modt, swa_w_qkv[i].astype(BF16), bd,
                                           head_gain(swa_qn[i]), head_gain(swa_kn[i]), rope_cos, rope_sin)
            o = _swa_attention(q, k_rot, v,
                               cache_swa_k[:, i].reshape(DEC_BATCH, PAST_LEN, nkv),
                               cache_swa_v[:, i].reshape(DEC_BATCH, PAST_LEN, nkv), swa_sink[i])
            x = _odd_out(o, x, modt, swa_w_o[i].astype(BF16))
            swa_k_out.append(k_plain[:N_CTX_TOK].reshape(BATCH, SEQ, C_KV_HEADS, HEAD_DIM))
            swa_v_out.append(v[:N_CTX_TOK].reshape(BATCH, SEQ, C_KV_HEADS, HEAD_DIM))
        x = _moe(l, x, norm2_g[l].reshape(1, D_MODEL), modt, router_w[l], router_b[l],
                 moe_w_gu, moe_b_gu, moe_w_down, moe_b_down)

    y_prompt = x[:N_CTX_TOK].reshape(BATCH, SEQ, D_MODEL)
    y_sample = x[N_CTX_TOK:].reshape(DEC_BATCH, DEC_SEQ, D_MODEL)
    return (y_prompt, y_sample,
            jnp.stack(nat_k_out, axis=1), jnp.stack(nat_v_out, axis=1),
            jnp.stack(swa_k_out, axis=1), jnp.stack(swa_v_out, axis=1),
            jnp.stack(ssm_out, axis=1))
```

```python
import functools

import jax
import jax.numpy as jnp
import numpy as np
from jax import lax
from jax.experimental import pallas as pl
from jax.experimental.pallas import tpu as pltpu

F32 = jnp.float32
BF16 = jnp.bfloat16
U32 = jnp.uint32

D_MODEL = 1024
BATCH = 32
SEQ = 256
DEPTH = 4
DEC_BATCH = 2
DEC_SEQ = 1024
PAST_LEN = 256
GRID_W = 64
HEAD_DIM = 64
N_MOD = 6
A_WIDTH = 512
A_GROUP_CH = 16
A_GROUPS = 32
A_STATE = 64
B_HEADS = 8
B_WIDTH = 512
WIN_R_MAX = 8
WIN_C = 16
C_HEADS = 16
C_KV_HEADS = 2
C_GROUP = 8
WINDOW = 128
ROPE_BASE = 10000.0
N_EXPERTS = 32
TOP_K = 4
D_EXPERT = 1024
SWIGLU_LIMIT = 7.0
SWIGLU_ALPHA = 1.702
EPS = 1e-6
NEG_INF = -1e30

LANES = 128
SUBLANES = 8
TILE = 256
N_CTX_TOK = BATCH * SEQ
N_LAT_TOK = DEC_BATCH * DEC_SEQ
N_TOK = N_CTX_TOK + N_LAT_TOK
N_TILES = N_TOK // TILE
N_CTX_TILES = N_CTX_TOK // TILE
LAT_TILES_PER_SEQ = DEC_SEQ // TILE
SSM_LANES = A_GROUPS * A_STATE
SSM_SLABS = A_WIDTH // LANES
SSM_SLAB_STATES = SSM_LANES // SSM_SLABS
SCAN_ROWS = 8
SCAN_GROUPS = N_TILES // SCAN_ROWS
SCAN_CHUNK = 64
SCAN_LANE_PARTS = 2
MOE_TM = 256
N_ASG = N_TOK * TOP_K
ASG_STRIDE = 1 << 16
MOE_BLOCKS = N_ASG // MOE_TM + N_EXPERTS
MOE_ROWS = MOE_BLOCKS * MOE_TM
ROUTER_PAD = LANES
VMEM_LIMIT = 56 * 1024 * 1024
MOE_VMEM_LIMIT = 60 * 1024 * 1024
ATT_SCALE = HEAD_DIM ** -0.5


def _cparams(*sem):
    return pltpu.CompilerParams(dimension_semantics=sem, vmem_limit_bytes=VMEM_LIMIT)


def _dot(a, b):
    return jnp.dot(a, b, preferred_element_type=F32)


def _dot_nt(a, b):
    return lax.dot_general(a, b, (((1,), (1,)), ((), ())), preferred_element_type=F32)


def _split(a):
    hi = a.astype(BF16)
    lo = (a - hi.astype(F32)).astype(BF16)
    return hi, lo


def _dot3(a, b):
    a_hi, a_lo = _split(a)
    b_hi, b_lo = _split(b)
    return _dot(a_hi, b_hi) + (_dot(a_hi, b_lo) + _dot(a_lo, b_hi))


def _silu(x):
    return x * jax.nn.sigmoid(x)


def _norm_mod(x, g, shift, scale):
    y = x * lax.rsqrt(jnp.mean(x * x, axis=-1, keepdims=True) + EPS)
    return (y * g) * (1.0 + scale) + shift


def _head_rms(x, bd, gain):
    sq_hi, sq_lo = _split(x * x)
    ms = _dot(sq_hi, bd) + _dot(sq_lo, bd)
    return x * lax.rsqrt(ms + EPS) * gain


def _lane_half(shape, half):
    lane = lax.broadcasted_iota(jnp.int32, shape, len(shape) - 1)
    return (lane < HEAD_DIM) if half == 0 else (lane >= HEAD_DIM)


def _mod_kernel(cond_ref, w_ref, b_ref, o_ref):
    o_ref[0] = _dot3(_silu(cond_ref[...]), w_ref[0]) + b_ref[0]


def _modulation(cond, mod_w, mod_b):
    nc = N_MOD
    return pl.pallas_call(
        _mod_kernel,
        out_shape=jax.ShapeDtypeStruct((DEPTH, SUBLANES, N_MOD * D_MODEL), F32),
        grid=(DEPTH, nc),
        in_specs=[
            pl.BlockSpec((SUBLANES, D_MODEL), lambda l, c: (0, 0)),
            pl.BlockSpec((1, D_MODEL, D_MODEL), lambda l, c: (l, 0, c)),
            pl.BlockSpec((1, 1, D_MODEL), lambda l, c: (l, 0, c)),
        ],
        out_specs=pl.BlockSpec((1, SUBLANES, D_MODEL), lambda l, c: (l, 0, c)),
        compiler_params=_cparams("arbitrary", "arbitrary"),
        name="modulation",
    )(cond, mod_w, mod_b.reshape(DEPTH, 1, N_MOD * D_MODEL))


def _mod_spec(col):
    return pl.BlockSpec((1, 1, D_MODEL), lambda i: (i, 0, col))


def _tile_spec(width):
    return pl.BlockSpec((TILE, width), lambda i: (i, 0))


def _ctx_tile_spec(width):
    return pl.BlockSpec((TILE, width), lambda i: (jnp.minimum(i, N_CTX_TILES - 1), 0))


def _full_spec(shape):
    nd = len(shape)
    return pl.BlockSpec(shape, lambda i: (0,) * nd)


def _even_in_kernel(x_ref, g_ref, sh_ref, sc_ref, w_ref, bd_ref, qn_ref, kn_ref,
                    u_ref, q_ref, k_ref, v_ref):
    h = _norm_mod(x_ref[...], g_ref[...], sh_ref[0], sc_ref[0]).astype(BF16)
    bd = bd_ref[...]
    u_ref[...] = _dot(h, w_ref[:, 0:A_WIDTH])
    for s in range(B_WIDTH // LANES):
        lo = A_WIDTH + s * LANES
        q = _dot(h, w_ref[:, lo:lo + LANES])
        q_ref[:, s * LANES:(s + 1) * LANES] = _head_rms(q, bd, qn_ref[...]).astype(BF16)
        lo = A_WIDTH + B_WIDTH + s * LANES
        k = _dot(h, w_ref[:, lo:lo + LANES])
        k_ref[:, s * LANES:(s + 1) * LANES] = _head_rms(k, bd, kn_ref[...])
    v_ref[...] = _dot(h, w_ref[:, A_WIDTH + 2 * B_WIDTH:])


def _even_in(x, g, modt, w, bd, qn, kn):
    n_out = A_WIDTH + 3 * B_WIDTH
    return pl.pallas_call(
        _even_in_kernel,
        out_shape=(
            jax.ShapeDtypeStruct((TILE, N_TILES * A_WIDTH), F32),
            jax.ShapeDtypeStruct((N_TOK, B_WIDTH), BF16),
            jax.ShapeDtypeStruct((N_TOK, B_WIDTH), F32),
            jax.ShapeDtypeStruct((N_TOK, B_WIDTH), F32),
        ),
        grid=(N_TILES,),
        in_specs=[
            _tile_spec(D_MODEL), _full_spec((1, D_MODEL)), _mod_spec(0), _mod_spec(1),
            _full_spec((D_MODEL, n_out)), _full_spec((LANES, LANES)),
            _full_spec((1, LANES)), _full_spec((1, LANES)),
        ],
        out_specs=(
            pl.BlockSpec((TILE, A_WIDTH), lambda i: (0, i)),
            _tile_spec(B_WIDTH), _tile_spec(B_WIDTH), _tile_spec(B_WIDTH),
        ),
        compiler_params=_cparams("arbitrary"),
        name="even_in",
    )(x, g, modt, modt, w, bd, qn, kn)


def _cmul(ar, ai, br, bi):
    return ar * br - ai * bi, ar * bi + ai * br


def _s5_kernel(u_ref, a_ref, w_ref, c_ref, s0_ref, y_ref, fin_ref, xr, xi, st_r, st_i):
    grp = pl.program_id(0)
    drn = pl.program_id(1)
    n_chunks = TILE // SCAN_CHUNK
    rows = SCAN_CHUNK * SCAN_ROWS
    part = SSM_LANES // SCAN_LANE_PARTS

    def run(store):
        @pl.loop(0, n_chunks)
        def _(c):
            cc = jnp.where(drn == 0, c, n_chunks - 1 - c)
            t0 = pl.multiple_of(cc * SCAN_CHUNK, SCAN_CHUNK)
            uu = u_ref[pl.ds(t0, SCAN_CHUNK), :, :]
            for s in range(SSM_SLABS):
                us = uu[:, :, s * LANES:(s + 1) * LANES].reshape(rows, LANES).astype(BF16)
                cols = slice(s * SSM_SLAB_STATES, (s + 1) * SSM_SLAB_STATES)
                xr[:, cols] = _dot(us, w_ref[0, 0, s])
                xi[:, cols] = _dot(us, w_ref[0, 1, s])
            for p in range(SCAN_LANE_PARTS):
                ls = slice(p * part, (p + 1) * part)
                ar = a_ref[0, 0, :, ls]
                ai = a_ref[0, 1, :, ls]

                def step(j, carry):
                    sr, si = carry
                    tt = jnp.where(drn == 0, j, SCAN_CHUNK - 1 - j)
                    r0 = pl.multiple_of(tt * SCAN_ROWS, SCAN_ROWS)
                    nr = ar * sr - ai * si + xr[pl.ds(r0, SCAN_ROWS), ls]
                    ni = ar * si + ai * sr + xi[pl.ds(r0, SCAN_ROWS), ls]
                    if store:
                        xr[pl.ds(r0, SCAN_ROWS), ls] = nr
                        xi[pl.ds(r0, SCAN_ROWS), ls] = ni
                    return nr, ni

                sr, si = lax.fori_loop(0, SCAN_CHUNK, step, (st_r[:, ls], st_i[:, ls]), unroll=4)
                st_r[:, ls] = sr
                st_i[:, ls] = si
            if store:
                for s in range(SSM_SLABS):
                    cols = slice(s * SSM_SLAB_STATES, (s + 1) * SSM_SLAB_STATES)
                    ys = (_dot(xr[:, cols].astype(BF16), c_ref[0, 0, s])
                          - _dot(xi[:, cols].astype(BF16), c_ref[0, 1, s]))
                    ys = ys.reshape(SCAN_CHUNK, SCAN_ROWS, LANES)
                    lanes = slice(s * LANES, (s + 1) * LANES)

                    @pl.when(drn == 0)
                    def _():
                        y_ref[pl.ds(t0, SCAN_CHUNK), :, lanes] = ys

                    @pl.when(drn != 0)
                    def _():
                        y_ref[pl.ds(t0, SCAN_CHUNK), :, lanes] += ys

    st_r[...] = s0_ref[0, 0]
    st_i[...] = s0_ref[0, 1]

    @pl.when(grp == SCAN_GROUPS - 1)
    def _():
        st_r[...] = jnp.zeros_like(st_r)
        st_i[...] = jnp.zeros_like(st_i)
        run(False)
        pr, pi = a_ref[0, 0], a_ref[0, 1]
        for _ in range(8):
            pr, pi = _cmul(pr, pi, pr, pi)
        fr, fi = st_r[...], st_i[...]
        s0r, s0i = s0_ref[0, 0], s0_ref[0, 1]
        row = lax.broadcasted_iota(jnp.int32, (SCAN_ROWS, SSM_LANES), 0)
        quarter = row % LAT_TILES_PER_SEQ
        fwd = drn == 0
        keep = quarter != jnp.where(fwd, 0, LAT_TILES_PER_SEQ - 1)
        ir, ii = s0r, s0i
        for _ in range(LAT_TILES_PER_SEQ - 1):
            nr, ni = _cmul(pr, pi, ir, ii)
            nr, ni = nr + fr, ni + fi
            nr = jnp.where(fwd, pltpu.roll(nr, 1, 0), pltpu.roll(nr, SCAN_ROWS - 1, 0))
            ni = jnp.where(fwd, pltpu.roll(ni, 1, 0), pltpu.roll(ni, SCAN_ROWS - 1, 0))
            ir = s0r + jnp.where(keep, nr, 0.0)
            ii = s0i + jnp.where(keep, ni, 0.0)
        st_r[...] = ir
        st_i[...] = ii

    run(True)
    fin_ref[0, 0] = st_r[...]
    fin_ref[0, 1] = st_i[...]


def _s5_scan(u_tb, a_b, w_bd, c_bd, s0):
    rows = SCAN_CHUNK * SCAN_ROWS
    return pl.pallas_call(
        _s5_kernel,
        out_shape=(
            jax.ShapeDtypeStruct((TILE, N_TILES, A_WIDTH), F32),
            jax.ShapeDtypeStruct((2, 2, N_TILES, SSM_LANES), F32),
        ),
        grid=(SCAN_GROUPS, 2),
        in_specs=[
            pl.BlockSpec((TILE, SCAN_ROWS, A_WIDTH), lambda g, d: (0, g, 0)),
            pl.BlockSpec((1, 2, SCAN_ROWS, SSM_LANES), lambda g, d: (d, 0, 0, 0)),
            pl.BlockSpec((1, 2, SSM_SLABS, LANES, SSM_SLAB_STATES), lambda g, d: (d, 0, 0, 0, 0)),
            pl.BlockSpec((1, 2, SSM_SLABS, SSM_SLAB_STATES, LANES), lambda g, d: (d, 0, 0, 0, 0)),
            pl.BlockSpec((1, 2, SCAN_ROWS, SSM_LANES), lambda g, d: (d, 0, g, 0)),
        ],
        out_specs=(
            pl.BlockSpec((TILE, SCAN_ROWS, A_WIDTH), lambda g, d: (0, g, 0)),
            pl.BlockSpec((1, 2, SCAN_ROWS, SSM_LANES), lambda g, d: (d, 0, g, 0)),
        ),
        scratch_shapes=[
            pltpu.VMEM((rows, SSM_LANES), F32), pltpu.VMEM((rows, SSM_LANES), F32),
            pltpu.VMEM((SCAN_ROWS, SSM_LANES), F32), pltpu.VMEM((SCAN_ROWS, SSM_LANES), F32),
        ],
        compiler_params=_cparams("arbitrary", "arbitrary"),
        name="s5_scan",
    )(u_tb, a_b, w_bd, c_bd, s0)


def _s5_params(lam_re, lam_im, log_dt, b_re, b_im, c_re, c_im):
    dt = jnp.exp(log_dt)[..., None]
    mag = jnp.exp(lam_re * dt)
    ab_re, ab_im = mag * jnp.cos(lam_im * dt), mag * jnp.sin(lam_im * dt)
    den = lam_re * lam_re + lam_im * lam_im
    nr, ni = ab_re - 1.0, ab_im
    f_re = (nr * lam_re + ni * lam_im) / den
    f_im = (ni * lam_re - nr * lam_im) / den
    bb_re = f_re[..., None] * b_re - f_im[..., None] * b_im
    bb_im = f_re[..., None] * b_im + f_im[..., None] * b_re
    a_b = jnp.stack([ab_re, ab_im], axis=1).reshape(2, 2, 1, SSM_LANES)
    a_b = jnp.broadcast_to(a_b, (2, 2, SCAN_ROWS, SSM_LANES))
    gps = A_GROUPS // SSM_SLABS
    eye = jnp.eye(gps, dtype=F32)

    def in_bd(bb):
        bb = bb.reshape(2, SSM_SLABS, gps, A_STATE, A_GROUP_CH)
        m = jnp.einsum('dsgph,gk->dsghkp', bb, eye)
        return m.reshape(2, SSM_SLABS, LANES, SSM_SLAB_STATES)

    def out_bd(cc):
        cc = cc.reshape(2, SSM_SLABS, gps, A_GROUP_CH, A_STATE)
        m = jnp.einsum('dsghp,gk->dsgpkh', cc, eye)
        return m.reshape(2, SSM_SLABS, SSM_SLAB_STATES, LANES)

    w_bd = jnp.stack([in_bd(bb_re), in_bd(bb_im)], axis=1).astype(BF16)
    c_bd = jnp.stack([out_bd(c_re), out_bd(c_im)], axis=1).astype(BF16)
    return a_b, w_bd, c_bd


def _softmax_pv(scores, values, sink=None):
    m = functools.reduce(jnp.maximum, [jnp.max(s, axis=-1, keepdims=True) for s in scores])
    if sink is not None:
        m = jnp.maximum(m, sink)
    den = None
    acc = None
    for s, v in zip(scores, values):
        e = jnp.exp(s - m)
        d = jnp.sum(e, axis=-1, keepdims=True)
        o = _dot(e.astype(BF16), v)
        den = d if den is None else den + d
        acc = o if acc is None else acc + o
    if sink is not None:
        den = den + jnp.exp(sink - m)
    return acc / den


def _nat_ctx_kernel(q_ref, k_ref, v_ref, o_ref):
    @pl.when(pl.program_id(0) >= N_CTX_TILES)
    def _():
        o_ref[...] = jnp.zeros_like(o_ref)

    @pl.when(pl.program_id(0) < N_CTX_TILES)
    def _():
        for p in range(B_WIDTH // LANES):
            ls = slice(p * LANES, (p + 1) * LANES)
            qs = q_ref[:, ls]
            ks = k_ref[:, ls].astype(BF16)
            vs = v_ref[:, ls].astype(BF16)
            out = None
            for half in range(2):
                msk = _lane_half(qs.shape, half)
                qm = jnp.where(msk, qs, jnp.zeros_like(qs))
                s = _dot_nt(qm, ks) * ATT_SCALE
                o = _softmax_pv([s], [vs])
                out = o if out is None else jnp.where(msk, o, out)
            o_ref[:, ls] = out.astype(BF16)


def _nat_lat_kernel(o_all_ref, q_ref, k_ref, v_ref, kc_ref, vc_ref, bias_ref, o_ref):
    del o_all_ref
    r = pl.program_id(1)
    wr = WIN_R_MAX
    rs = jnp.clip(r - wr // 2, 0, DEC_SEQ // GRID_W - wr)
    k0 = pl.multiple_of(rs * GRID_W, GRID_W)
    n_nb = wr * GRID_W
    for p in range(B_WIDTH // LANES):
        ls = slice(p * LANES, (p + 1) * LANES)
        qs = q_ref[:, ls]
        ks = k_ref[pl.ds(k0, n_nb), ls].astype(BF16)
        vs = v_ref[pl.ds(k0, n_nb), ls].astype(BF16)
        kc = kc_ref[0, :, ls].astype(BF16)
        vc = vc_ref[0, :, ls].astype(BF16)
        out = None
        for half in range(2):
            msk = _lane_half(qs.shape, half)
            qm = jnp.where(msk, qs, jnp.zeros_like(qs))
            s_nb = _dot_nt(qm, ks) * ATT_SCALE + bias_ref[2 * p + half, 0]
            s_ctx = _dot_nt(qm, kc) * ATT_SCALE
            o = _softmax_pv([s_nb, s_ctx], [vs, vc])
            out = o if out is None else jnp.where(msk, o, out)
        o_ref[:, ls] = out.astype(BF16)


def _nat_attention(q, k, v, kc, vc, bias):
    o_ctx = pl.pallas_call(
        _nat_ctx_kernel,
        out_shape=jax.ShapeDtypeStruct((N_TOK, B_WIDTH), BF16),
        grid=(N_TILES,),
        in_specs=[_ctx_tile_spec(B_WIDTH)] * 3,
        out_specs=_tile_spec(B_WIDTH),
        compiler_params=_cparams("arbitrary"),
        name="nat_ctx",
    )(q, k, v)
    rows = DEC_SEQ // GRID_W
    ctx_seqs = N_CTX_TOK // DEC_SEQ
    lat_row = lambda b, r: (N_CTX_TOK // GRID_W + b * rows + r, 0)

    def bias_idx(b, r):
        return (0, r - jnp.clip(r - WIN_R_MAX // 2, 0, rows - WIN_R_MAX), 0, 0)

    return pl.pallas_call(
        _nat_lat_kernel,
        out_shape=jax.ShapeDtypeStruct((N_TOK, B_WIDTH), BF16),
        grid=(DEC_BATCH, rows),
        in_specs=[
            pl.BlockSpec(memory_space=pl.ANY),
            pl.BlockSpec((GRID_W, B_WIDTH), lat_row),
            pl.BlockSpec((DEC_SEQ, B_WIDTH), lambda b, r: (ctx_seqs + b, 0)),
            pl.BlockSpec((DEC_SEQ, B_WIDTH), lambda b, r: (ctx_seqs + b, 0)),
            pl.BlockSpec((1, PAST_LEN, B_WIDTH), lambda b, r: (b, 0, 0)),
            pl.BlockSpec((1, PAST_LEN, B_WIDTH), lambda b, r: (b, 0, 0)),
            pl.BlockSpec((B_HEADS, 1, GRID_W, WIN_R_MAX * GRID_W), bias_idx),
        ],
        out_specs=pl.BlockSpec((GRID_W, B_WIDTH), lat_row),
        input_output_aliases={0: 0},
        compiler_params=_cparams("arbitrary", "arbitrary"),
        name="nat_lat",
    )(o_ctx, q, k, v, kc, vc, bias)


def _nat_bias(rpb):
    qc = np.arange(GRID_W)
    kc = np.arange(GRID_W)
    cs = np.clip(qc - WIN_C // 2, 0, GRID_W - WIN_C)
    ok = (kc[None, :] >= cs[:, None]) & (kc[None, :] < cs[:, None] + WIN_C)
    dc = np.clip(kc[None, :] - qc[:, None] + (WIN_C - 1), 0, 2 * WIN_C - 2)
    pick = (dc[:, :, None] == np.arange(2 * WIN_C - 1)).astype(np.float32)
    t = jnp.einsum('hrc,qkc->hrqk', rpb.astype(F32), pick, precision=lax.Precision.HIGHEST)
    t = jnp.where(ok[None, None], t, NEG_INF)
    per_d = [t[:, WIN_R_MAX - 1 - d:2 * WIN_R_MAX - 1 - d].transpose(0, 2, 1, 3)
             .reshape(B_HEADS, GRID_W, WIN_R_MAX * GRID_W) for d in range(WIN_R_MAX)]
    return jnp.stack(per_d, axis=1)


def _gqa_heads(q_ref, sink_ref, k_segs, v_segs, masks, o_ref):
    k_rot = [pltpu.roll(k.astype(F32), HEAD_DIM, 1).astype(BF16) for k in k_segs]
    v_rot = [pltpu.roll(v.astype(F32), HEAD_DIM, 1).astype(BF16) for v in v_segs]
    for p in range(C_HEADS // 2):
        ls = slice(p * LANES, (p + 1) * LANES)
        qs = q_ref[:, ls]
        out = None
        for half in range(2):
            head = 2 * p + half
            kv = head // C_GROUP
            ks = k_segs if kv == half else k_rot
            vs = v_segs if kv == half else v_rot
            msk = _lane_half(qs.shape, half)
            qm = jnp.where(msk, qs, jnp.zeros_like(qs))
            scores = []
            for kseg, m in zip(ks, masks):
                s = _dot_nt(qm, kseg) * ATT_SCALE
                scores.append(s if m is None else jnp.where(m, s, NEG_INF))
            o = _softmax_pv(scores, vs, sink=sink_ref[head])
            out = o if out is None else jnp.where(msk, o, out)
        o_ref[:, ls] = out.astype(BF16)


def _swa_ctx_kernel(sink_ref, q_ref, k_ref, v_ref, o_ref):
    @pl.when(pl.program_id(0) >= N_CTX_TILES)
    def _():
        o_ref[...] = jnp.zeros_like(o_ref)

    @pl.when(pl.program_id(0) < N_CTX_TILES)
    def _():
        _gqa_heads(q_ref, sink_ref, [k_ref[...].astype(BF16)], [v_ref[...].astype(BF16)], [None], o_ref)


def _swa_lat_kernel(sink_ref, o_all_ref, q_ref, k_ref, v_ref, kc_ref, vc_ref, o_ref):
    del o_all_ref
    n = pl.program_id(1)
    n_win = 3 * WINDOW
    start = jnp.clip((n - 1) * WINDOW, 0, DEC_SEQ - n_win)
    k0 = pl.multiple_of(start, WINDOW)
    qpos = n * WINDOW + lax.broadcasted_iota(jnp.int32, (WINDOW, n_win), 0)
    kpos = start + lax.broadcasted_iota(jnp.int32, (WINDOW, n_win), 1)
    ok = jnp.abs(qpos - kpos) <= WINDOW
    _gqa_heads(q_ref, sink_ref,
               [k_ref[pl.ds(k0, n_win), :].astype(BF16), kc_ref[0].astype(BF16)],
               [v_ref[pl.ds(k0, n_win), :].astype(BF16), vc_ref[0].astype(BF16)],
               [ok, None], o_ref)


def _swa_attention(q, k, v, kc, vc, sink):
    nq, nkv = C_HEADS * HEAD_DIM, C_KV_HEADS * HEAD_DIM
    smem = pl.BlockSpec(memory_space=pltpu.SMEM)
    o_ctx = pl.pallas_call(
        _swa_ctx_kernel,
        out_shape=jax.ShapeDtypeStruct((N_TOK, nq), BF16),
        grid=(N_TILES,),
        in_specs=[smem, _ctx_tile_spec(nq), _ctx_tile_spec(nkv), _ctx_tile_spec(nkv)],
        out_specs=_tile_spec(nq),
        compiler_params=_cparams("arbitrary"),
        name="swa_ctx",
    )(sink, q, k, v)
    nb = DEC_SEQ // WINDOW
    ctx_seqs = N_CTX_TOK // DEC_SEQ
    lat_row = lambda b, n: (N_CTX_TOK // WINDOW + b * nb + n, 0)
    return pl.pallas_call(
        _swa_lat_kernel,
        out_shape=jax.ShapeDtypeStruct((N_TOK, nq), BF16),
        grid=(DEC_BATCH, nb),
        in_specs=[
            smem,
            pl.BlockSpec(memory_space=pl.ANY),
            pl.BlockSpec((WINDOW, nq), lat_row),
            pl.BlockSpec((DEC_SEQ, nkv), lambda b, n: (ctx_seqs + b, 0)),
            pl.BlockSpec((DEC_SEQ, nkv), lambda b, n: (ctx_seqs + b, 0)),
            pl.BlockSpec((1, PAST_LEN, nkv), lambda b, n: (b, 0, 0)),
            pl.BlockSpec((1, PAST_LEN, nkv), lambda b, n: (b, 0, 0)),
        ],
        out_specs=pl.BlockSpec((WINDOW, nq), lat_row),
        input_output_aliases={1: 0},
        compiler_params=_cparams("arbitrary", "arbitrary"),
        name="swa_lat",
    )(sink, o_ctx, q, k, v, kc, vc)


def _even_out_kernel(y_ref, u_ref, o_ref, x_ref, gate_ref, d_ref, gw_ref, gb_ref, w_ref, xo_ref):
    yy = y_ref[...] + d_ref[...] * u_ref[...]
    g = jax.nn.gelu(yy)
    a = g * jax.nn.sigmoid(_dot(g.astype(BF16), gw_ref[...]) + gb_ref[...])
    mix = _dot(a.astype(BF16), w_ref[0:A_WIDTH, :]) + _dot(o_ref[...], w_ref[A_WIDTH:, :])
    xo_ref[...] = x_ref[...] + gate_ref[0] * mix


def _even_out(y_t, u_t, o, x, modt, d_skip, glu_w, glu_b, w_out):
    tm_spec = pl.BlockSpec((TILE, A_WIDTH), lambda i: (0, i))
    return pl.pallas_call(
        _even_out_kernel,
        out_shape=jax.ShapeDtypeStruct((N_TOK, D_MODEL), F32),
        grid=(N_TILES,),
        in_specs=[
            tm_spec, tm_spec, _tile_spec(B_WIDTH), _tile_spec(D_MODEL), _mod_spec(2),
            _full_spec((1, A_WIDTH)), _full_spec((A_WIDTH, A_WIDTH)), _full_spec((1, A_WIDTH)),
            _full_spec((A_WIDTH + B_WIDTH, D_MODEL)),
        ],
        out_specs=_tile_spec(D_MODEL),
        compiler_params=_cparams("arbitrary"),
        name="even_out",
    )(y_t, u_t, o, x, modt, d_skip, glu_w, glu_b, w_out)


def _rope(x, cos, sin):
    lane = lax.broadcasted_iota(jnp.int32, x.shape, 1)
    first = (lane % (HEAD_DIM // 2)) < (HEAD_DIM // 4)
    partner = jnp.where(first, pltpu.roll(x, LANES - HEAD_DIM // 4, 1), pltpu.roll(x, HEAD_DIM // 4, 1))
    return x * cos + partner * sin


def _odd_in_kernel(x_ref, g_ref, sh_ref, sc_ref, w_ref, bd_ref, qn_ref, kn_ref, cos_ref, sin_ref,
                   q_ref, kc_ref, kr_ref, v_ref):
    i = pl.program_id(0)
    h = _norm_mod(x_ref[...], g_ref[...], sh_ref[0], sc_ref[0]).astype(BF16)
    bd = bd_ref[...]
    lat = i >= N_CTX_TILES
    cos = jnp.where(lat, cos_ref[...], 1.0)
    sin = jnp.where(lat, sin_ref[...], 0.0)
    nq = C_HEADS * HEAD_DIM
    for s in range(nq // LANES):
        q = _head_rms(_dot(h, w_ref[:, s * LANES:(s + 1) * LANES]), bd, qn_ref[...])
        q_ref[:, s * LANES:(s + 1) * LANES] = _rope(q, cos, sin).astype(BF16)
    k = _head_rms(_dot(h, w_ref[:, nq:nq + LANES]), bd, kn_ref[...])
    kc_ref[...] = k
    kr_ref[...] = _rope(k, cos, sin)
    v_ref[...] = _dot(h, w_ref[:, nq + LANES:])


def _odd_in(x, g, modt, w, bd, qn, kn, cos, sin):
    nq, nkv = C_HEADS * HEAD_DIM, C_KV_HEADS * HEAD_DIM
    lat_spec = pl.BlockSpec((TILE, LANES), lambda i: (jnp.maximum(i - N_CTX_TILES, 0) % LAT_TILES_PER_SEQ, 0))
    return pl.pallas_call(
        _odd_in_kernel,
        out_shape=(
            jax.ShapeDtypeStruct((N_TOK, nq), BF16),
            jax.ShapeDtypeStruct((N_TOK, nkv), F32),
            jax.ShapeDtypeStruct((N_TOK, nkv), F32),
            jax.ShapeDtypeStruct((N_TOK, nkv), F32),
        ),
        grid=(N_TILES,),
        in_specs=[
            _tile_spec(D_MODEL), _full_spec((1, D_MODEL)), _mod_spec(0), _mod_spec(1),
            _full_spec((D_MODEL, nq + 2 * nkv)), _full_spec((LANES, LANES)),
            _full_spec((1, LANES)), _full_spec((1, LANES)), lat_spec, lat_spec,
        ],
        out_specs=(_tile_spec(nq), _tile_spec(nkv), _tile_spec(nkv), _tile_spec(nkv)),
        compiler_params=_cparams("arbitrary"),
        name="odd_in",
    )(x, g, modt, modt, w, bd, qn, kn, cos, sin)


def _rope_tables():
    nf = HEAD_DIM // 4
    inv = ROPE_BASE ** (-jnp.arange(nf, dtype=F32) / nf)
    t = jnp.arange(DEC_SEQ)
    pos = jnp.stack([t // GRID_W, t % GRID_W], axis=-1).astype(F32)
    ang = pos[:, :, None] * inv
    cos, sin = jnp.cos(ang), jnp.sin(ang)
    cos_h = jnp.stack([cos, cos], axis=2).reshape(DEC_SEQ, HEAD_DIM)
    sin_h = jnp.stack([-sin, sin], axis=2).reshape(DEC_SEQ, HEAD_DIM)
    return jnp.tile(cos_h, (1, 2)), jnp.tile(sin_h, (1, 2))


def _odd_out_kernel(o_ref, x_ref, gate_ref, w_ref, xo_ref):
    xo_ref[...] = x_ref[...] + gate_ref[0] * _dot(o_ref[...], w_ref[...])


def _odd_out(o, x, modt, w_o):
    nq = C_HEADS * HEAD_DIM
    return pl.pallas_call(
        _odd_out_kernel,
        out_shape=jax.ShapeDtypeStruct((N_TOK, D_MODEL), F32),
        grid=(N_TILES,),
        in_specs=[_tile_spec(nq), _tile_spec(D_MODEL), _mod_spec(2), _full_spec((nq, D_MODEL))],
        out_specs=_tile_spec(D_MODEL),
        compiler_params=_cparams("arbitrary"),
        name="odd_out",
    )(o, x, modt, w_o)


def _pack_bf16_pair(lo, hi):
    lo_bits = lax.bitcast_convert_type(lo.astype(BF16).astype(F32), U32)
    hi_bits = lax.bitcast_convert_type(hi.astype(BF16).astype(F32), U32)
    return (hi_bits & jnp.uint32(0xFFFF0000)) | (lo_bits >> 16)


def _unpack_bf16_pair(packed):
    lo = lax.bitcast_convert_type(packed << 16, F32).astype(BF16)
    hi = lax.bitcast_convert_type(packed & jnp.uint32(0xFFFF0000), F32).astype(BF16)
    return lo, hi


def _moe_in_kernel(x_ref, g_ref, sh_ref, sc_ref, rw_ref, rb_ref, h_ref, lg_ref):
    h = _norm_mod(x_ref[...], g_ref[...], sh_ref[0], sc_ref[0])
    h_ref[...] = _pack_bf16_pair(h[:, 0:D_MODEL // 2], h[:, D_MODEL // 2:])
    lg_ref[...] = _dot3(h, rw_ref[...]) + rb_ref[...]


def _moe_in(x, g, modt, rw, rb):
    return pl.pallas_call(
        _moe_in_kernel,
        out_shape=(
            jax.ShapeDtypeStruct((N_TOK, D_MODEL // 2), U32),
            jax.ShapeDtypeStruct((N_TOK, ROUTER_PAD), F32),
        ),
        grid=(N_TILES,),
        in_specs=[
            _tile_spec(D_MODEL), _full_spec((1, D_MODEL)), _mod_spec(3), _mod_spec(4),
            _full_spec((D_MODEL, ROUTER_PAD)), _full_spec((1, ROUTER_PAD)),
        ],
        out_specs=(_tile_spec(D_MODEL // 2), _tile_spec(ROUTER_PAD)),
        compiler_params=_cparams("arbitrary"),
        name="moe_in",
    )(x, g, modt, modt, rw, rb)


def _moe_expert_kernel(layer, be_ref, nb_ref, first_ref, slot_ref, nxt_ref, src_cur_ref, src_nxt_ref,
                       h_ref, wgu_hbm, bgu_ref, wd_hbm, bd_ref, o_ref, xg0, xg1, wgu_buf, wd_buf, wsem):
    i = pl.program_id(0)
    nb = nb_ref[0]
    half = D_MODEL // 2

    def gather(idx_ref, dst):
        for r in range(MOE_TM):
            dst[pl.ds(r, 1), :] = h_ref[pl.ds(idx_ref[0, 0, r], 1), :]

    def weight_copies(e, slot):
        return (pltpu.make_async_copy(wgu_hbm.at[layer, e], wgu_buf.at[slot], wsem.at[0, slot]),
                pltpu.make_async_copy(wd_hbm.at[layer, e], wd_buf.at[slot], wsem.at[1, slot]))

    @pl.when(i == 0)
    def _():
        for cp in weight_copies(be_ref[0], 0):
            cp.start()
        gather(src_cur_ref, xg0)

    def block(x_cur, x_nxt):
        ws = slot_ref[i]

        @pl.when(first_ref[i] == 1)
        def _():
            for cp in weight_copies(be_ref[i], ws):
                cp.wait()

            @pl.when(nxt_ref[i] >= 0)
            def _():
                for cp in weight_copies(nxt_ref[i], 1 - ws):
                    cp.start()

        x_lo, x_hi = _unpack_bf16_pair(x_cur[...])
        gu = (_dot(x_lo, wgu_buf[ws, 0:half, :].astype(BF16)) + _dot(x_hi, wgu_buf[ws, half:, :].astype(BF16))
              + bgu_ref[0, 0])
        gather(src_nxt_ref, x_nxt)
        g = jnp.minimum(gu[:, 0:D_EXPERT], SWIGLU_LIMIT)
        lin = jnp.clip(gu[:, D_EXPERT:], -SWIGLU_LIMIT, SWIGLU_LIMIT)
        act = g * jax.nn.sigmoid(SWIGLU_ALPHA * g) * (lin + 1.0)
        o_ref[...] = _dot(act.astype(BF16), wd_buf[ws].astype(BF16)) + bd_ref[0, 0]

    @pl.when((i < nb) & (i % 2 == 0))
    def _():
        block(xg0, xg1)

    @pl.when((i < nb) & (i % 2 == 1))
    def _():
        block(xg1, xg0)

    @pl.when(i >= nb)
    def _():
        o_ref[...] = jnp.zeros_like(o_ref)


def _moe_experts(layer, plan, h_packed, w_gu, b_gu, w_down, b_down):
    block_e, n_used, first, slot, nxt, row_tok = plan
    half = D_MODEL // 2
    idx_spec = lambda f: pl.BlockSpec((1, 1, MOE_TM), f, memory_space=pltpu.SMEM)
    hbm = pl.BlockSpec(memory_space=pl.ANY)
    grid_spec = pltpu.PrefetchScalarGridSpec(
        num_scalar_prefetch=5,
        grid=(MOE_BLOCKS,),
        in_specs=[
            idx_spec(lambda i, *_: (i, 0, 0)),
            idx_spec(lambda i, *_: (jnp.minimum(i + 1, MOE_BLOCKS - 1), 0, 0)),
            pl.BlockSpec((N_TOK, half), lambda i, *_: (0, 0), pipeline_mode=pl.Buffered(1)),
            hbm,
            pl.BlockSpec((1, 1, 1, 2 * D_EXPERT), lambda i, be, *_: (layer, be[i], 0, 0)),
            hbm,
            pl.BlockSpec((1, 1, 1, D_MODEL), lambda i, be, *_: (layer, be[i], 0, 0)),
        ],
        out_specs=pl.BlockSpec((MOE_TM, D_MODEL), lambda i, *_: (i, 0)),
        scratch_shapes=[
            pltpu.VMEM((MOE_TM, half), U32), pltpu.VMEM((MOE_TM, half), U32),
            pltpu.VMEM((2, D_MODEL, 2 * D_EXPERT), F32), pltpu.VMEM((2, D_EXPERT, D_MODEL), F32),
            pltpu.SemaphoreType.DMA((2, 2)),
        ],
    )
    rows = row_tok.reshape(MOE_BLOCKS, 1, MOE_TM)
    return pl.pallas_call(
        functools.partial(_moe_expert_kernel, layer),
        out_shape=jax.ShapeDtypeStruct((MOE_ROWS, D_MODEL), F32),
        grid_spec=grid_spec,
        compiler_params=pltpu.CompilerParams(dimension_semantics=("arbitrary",), vmem_limit_bytes=MOE_VMEM_LIMIT),
        name="moe_experts",
    )(block_e, n_used, first, slot, nxt, rows, rows, h_packed, w_gu,
      b_gu.reshape(DEPTH, N_EXPERTS, 1, 2 * D_EXPERT), w_down, b_down.reshape(DEPTH, N_EXPERTS, 1, D_MODEL))


def _moe_combine_kernel(x_ref, gmod_ref, gates_ref, y0_ref, y1_ref, y2_ref, y3_ref, o_ref):
    gates = gates_ref[...]
    acc = gates[:, 0:1] * y0_ref[...]
    for k, y_ref in enumerate((y1_ref, y2_ref, y3_ref), start=1):
        acc = acc + gates[:, k:k + 1] * y_ref[...]
    o_ref[...] = x_ref[...] + gmod_ref[0] * acc


def _moe_combine(x, modt, gates, ys):
    return pl.pallas_call(
        _moe_combine_kernel,
        out_shape=jax.ShapeDtypeStruct((N_TOK, D_MODEL), F32),
        grid=(N_TILES,),
        in_specs=[_tile_spec(D_MODEL), _mod_spec(5), _tile_spec(TOP_K)] + [_tile_spec(D_MODEL)] * TOP_K,
        out_specs=_tile_spec(D_MODEL),
        compiler_params=_cparams("arbitrary"),
        name="moe_combine",
    )(x, modt, gates, *ys)


def _lut(idx, table):
    n = table.shape[0]
    hit = idx[:, None] == jnp.arange(n, dtype=idx.dtype)[None, :]
    return jnp.sum(jnp.where(hit, table[None, :], 0), axis=1)


def _moe_route(logits):
    i32 = jnp.int32
    top_val, top_idx = lax.top_k(logits[:, :N_EXPERTS], TOP_K)
    gates = jax.nn.softmax(top_val, axis=-1)
    flat_e = top_idx.reshape(N_ASG).astype(i32)
    asg = jnp.arange(N_ASG, dtype=i32)
    experts = jnp.arange(N_EXPERTS, dtype=i32)
    skey = jnp.sort(flat_e * ASG_STRIDE + asg)
    sorted_e = skey // ASG_STRIDE
    sorted_asg = skey % ASG_STRIDE
    counts = jnp.sum((flat_e[:, None] == experts[None, :]).astype(i32), axis=0)
    start = jnp.cumsum(counts) - counts
    pad_counts = (counts + MOE_TM - 1) // MOE_TM * MOE_TM
    pad_end = jnp.cumsum(pad_counts)
    pad_start = pad_end - pad_counts
    n_used = pad_end[-1] // MOE_TM
    pos = asg + _lut(sorted_e, pad_start - start)
    _, dest = lax.sort((sorted_asg, pos), num_keys=1)
    blk = jnp.arange(MOE_BLOCKS, dtype=i32)
    block_e = jnp.sum((blk[:, None] * MOE_TM >= pad_end[None, :]).astype(i32), axis=1)
    last_e = jnp.max(jnp.where(counts > 0, experts, 0))
    block_e = jnp.where(blk < n_used, jnp.minimum(block_e, N_EXPERTS - 1), last_e)
    first = jnp.concatenate([jnp.ones((1,), i32), (block_e[1:] != block_e[:-1]).astype(i32)])
    slot = (jnp.cumsum(first) - 1) % 2
    later = (experts[None, :] > experts[:, None]) & (counts[None, :] > 0)
    nxt_of = jnp.min(jnp.where(later, experts[None, :], N_EXPERTS), axis=1)
    nxt = _lut(block_e, jnp.where(nxt_of < N_EXPERTS, nxt_of, -1))
    row = jnp.arange(MOE_ROWS, dtype=i32)
    e_row = jnp.repeat(block_e, MOE_TM)
    off = row - _lut(e_row, pad_start)
    valid = (off < _lut(e_row, counts)) & (row < n_used * MOE_TM)
    src = jnp.clip(_lut(e_row, start) + off, 0, N_ASG - 1)
    row_tok = jnp.where(valid, sorted_asg[src] // TOP_K, 0)
    plan = (block_e, n_used.reshape(1).astype(i32), first, slot.astype(i32), nxt.astype(i32), row_tok)
    return gates, dest.reshape(N_TOK, TOP_K), plan


def _moe(layer, x, g, modt, rw, rb, w_gu, b_gu, w_down, b_down):
    h_packed, logits = _moe_in(x, g, modt, rw, rb)
    gates, dest, plan = _moe_route(logits)
    y_rows = _moe_experts(layer, plan, h_packed, w_gu, b_gu, w_down, b_down)
    return _moe_combine(x, modt, gates, [y_rows[dest[:, k]] for k in range(TOP_K)])


def kernel(x_prompt, x_sample, cache_nat_k, cache_nat_v, cache_swa_k, cache_swa_v, state_ssm, c, c_ctx,
           norm1_g, norm2_g, mod_w, mod_b,
           ab_w_in, ab_w_out, ssm_lam_re, ssm_lam_im, ssm_log_dt, ssm_b_re, ssm_b_im, ssm_c_re, ssm_c_im,
           ssm_d, ssm_glu_w, ssm_glu_b, nat_qn, nat_kn, nat_rpb,
           swa_w_qkv, swa_w_o, swa_qn, swa_kn, swa_sink,
           moe_router_w, moe_router_b, moe_w_gu, moe_b_gu, moe_w_down, moe_b_down):
    x = jnp.concatenate([x_prompt.reshape(N_CTX_TOK, D_MODEL), x_sample.reshape(N_LAT_TOK, D_MODEL)], axis=0)
    cond = jnp.zeros((SUBLANES, D_MODEL), F32).at[0].set(c_ctx).at[1:1 + DEC_BATCH].set(c)
    mod = _modulation(cond, mod_w, mod_b)
    tile_row = np.concatenate([np.zeros(N_CTX_TILES, np.int32),
                               1 + np.arange(N_TILES - N_CTX_TILES, dtype=np.int32) // LAT_TILES_PER_SEQ])
    head_gain = lambda gn: jnp.tile(gn, 2).reshape(1, LANES)
    bd = jnp.asarray(np.kron(np.eye(2, dtype=np.float32), np.full((HEAD_DIM, HEAD_DIM), 1.0 / HEAD_DIM, np.float32)), BF16)
    rope_cos, rope_sin = _rope_tables()
    router_w = jnp.pad(moe_router_w, ((0, 0), (0, 0), (0, ROUTER_PAD - N_EXPERTS)))
    router_b = jnp.pad(moe_router_b, ((0, 0), (0, ROUTER_PAD - N_EXPERTS))).reshape(DEPTH, 1, ROUTER_PAD)
    nkv = C_KV_HEADS * HEAD_DIM

    nat_k_out, nat_v_out, swa_k_out, swa_v_out, ssm_out = [], [], [], [], []
    for l in range(DEPTH):
        modt = mod[l][tile_row].reshape(N_TILES, 1, N_MOD * D_MODEL)
        g1 = norm1_g[l].reshape(1, D_MODEL)
        i = l // 2
        if l % 2 == 0:
            u_t, q, k, v = _even_in(x, g1, modt, ab_w_in[i].astype(BF16), bd,
                                    head_gain(nat_qn[i]), head_gain(nat_kn[i]))
            a_b, w_bd, c_bd = _s5_params(ssm_lam_re[i], ssm_lam_im[i], ssm_log_dt[i], ssm_b_re[i], ssm_b_im[i],
                                         ssm_c_re[i], ssm_c_im[i])
            st = state_ssm[:, i].reshape(DEC_BATCH, 2, 2, SSM_LANES).transpose(1, 2, 0, 3)
            s0 = jnp.zeros((2, 2, N_TILES, SSM_LANES), F32)
            first = N_CTX_TILES + LAT_TILES_PER_SEQ * np.arange(DEC_BATCH)
            s0 = s0.at[0, :, first].set(st[0].transpose(1, 0, 2))
            s0 = s0.at[1, :, first + LAT_TILES_PER_SEQ - 1].set(st[1].transpose(1, 0, 2))
            y_t, fin = _s5_scan(u_t.reshape(TILE, N_TILES, A_WIDTH), a_b, w_bd, c_bd, s0)
            o = _nat_attention(q, k, v,
                               cache_nat_k[:, i].reshape(DEC_BATCH, PAST_LEN, B_WIDTH),
                               cache_nat_v[:, i].reshape(DEC_BATCH, PAST_LEN, B_WIDTH),
                               _nat_bias(nat_rpb[i]))
            x = _even_out(y_t.reshape(TILE, N_TILES * A_WIDTH), u_t, o, x, modt,
                          ssm_d[i].reshape(1, A_WIDTH), ssm_glu_w[i].astype(BF16),
                          ssm_glu_b[i].reshape(1, A_WIDTH), ab_w_out[i].astype(BF16))
            ssm_out.append(fin[:, :, :N_CTX_TILES].transpose(2, 0, 1, 3).reshape(BATCH, 2, 2, A_GROUPS, A_STATE))
            nat_k_out.append(k[:N_CTX_TOK].reshape(BATCH, SEQ, B_HEADS, HEAD_DIM))
            nat_v_out.append(v[:N_CTX_TOK].reshape(BATCH, SEQ, B_HEADS, HEAD_DIM))
        else:
            q, k_plain, k_rot, v = _odd_in(x, g1, modt, swa_w_qkv[i].astype(BF16), bd,
                                           head_gain(swa_qn[i]), head_gain(swa_kn[i]), rope_cos, rope_sin)
            o = _swa_attention(q, k_rot, v,
                               cache_swa_k[:, i].reshape(DEC_BATCH, PAST_LEN, nkv),
                               cache_swa_v[:, i].reshape(DEC_BATCH, PAST_LEN, nkv), swa_sink[i])
            x = _odd_out(o, x, modt, swa_w_o[i].astype(BF16))
            swa_k_out.append(k_plain[:N_CTX_TOK].reshape(BATCH, SEQ, C_KV_HEADS, HEAD_DIM))
            swa_v_out.append(v[:N_CTX_TOK].reshape(BATCH, SEQ, C_KV_HEADS, HEAD_DIM))
        x = _moe(l, x, norm2_g[l].reshape(1, D_MODEL), modt, router_w[l], router_b[l],
                 moe_w_gu, moe_b_gu, moe_w_down, moe_b_down)

    y_prompt = x[:N_CTX_TOK].reshape(BATCH, SEQ, D_MODEL)
    y_sample = x[N_CTX_TOK:].reshape(DEC_BATCH, DEC_SEQ, D_MODEL)
    return (y_prompt, y_sample,
            jnp.stack(nat_k_out, axis=1), jnp.stack(nat_v_out, axis=1),
            jnp.stack(swa_k_out, axis=1), jnp.stack(swa_v_out, axis=1),
            jnp.stack(ssm_out, axis=1))
```

```python
import functools

import jax
import jax.numpy as jnp
import numpy as np
from jax import lax
from jax.experimental import pallas as pl
from jax.experimental.pallas import tpu as pltpu

F32 = jnp.float32
BF16 = jnp.bfloat16
U32 = jnp.uint32

D_MODEL = 1024
BATCH = 32
SEQ = 256
DEPTH = 4
DEC_BATCH = 2
DEC_SEQ = 1024
PAST_LEN = 256
GRID_W = 64
HEAD_DIM = 64
N_MOD = 6
A_WIDTH = 512
A_GROUP_CH = 16
A_GROUPS = 32
A_STATE = 64
B_HEADS = 8
B_WIDTH = 512
WIN_R_MAX = 8
WIN_C = 16
C_HEADS = 16
C_KV_HEADS = 2
C_GROUP = 8
WINDOW = 128
ROPE_BASE = 10000.0
N_EXPERTS = 32
TOP_K = 4
D_EXPERT = 1024
SWIGLU_LIMIT = 7.0
SWIGLU_ALPHA = 1.702
EPS = 1e-6
NEG_INF = -1e30

LANES = 128
SUBLANES = 8
TILE = 256
N_CTX_TOK = BATCH * SEQ
N_LAT_TOK = DEC_BATCH * DEC_SEQ
N_TOK = N_CTX_TOK + N_LAT_TOK
N_TILES = N_TOK // TILE
N_CTX_TILES = N_CTX_TOK // TILE
LAT_TILES_PER_SEQ = DEC_SEQ // TILE
SSM_LANES = A_GROUPS * A_STATE
SSM_SLABS = A_WIDTH // LANES
SSM_SLAB_STATES = SSM_LANES // SSM_SLABS
SCAN_ROWS = 8
SCAN_GROUPS = N_TILES // SCAN_ROWS
SCAN_CHUNK = 64
SCAN_LANE_PARTS = 2
MOE_TM = 256
N_ASG = N_TOK * TOP_K
ASG_STRIDE = 1 << 16
MOE_BLOCKS = N_ASG // MOE_TM + N_EXPERTS
MOE_ROWS = MOE_BLOCKS * MOE_TM
ROUTER_PAD = LANES
VMEM_LIMIT = 56 * 1024 * 1024
MOE_VMEM_LIMIT = 60 * 1024 * 1024
ATT_SCALE = HEAD_DIM ** -0.5


def _cparams(*sem):
    return pltpu.CompilerParams(dimension_semantics=sem, vmem_limit_bytes=VMEM_LIMIT)


def _dot(a, b):
    return jnp.dot(a, b, preferred_element_type=F32)


def _dot_nt(a, b):
    return lax.dot_general(a, b, (((1,), (1,)), ((), ())), preferred_element_type=F32)


def _split(a):
    hi = a.astype(BF16)
    lo = (a - hi.astype(F32)).astype(BF16)
    return hi, lo


def _dot3(a, b):
    a_hi, a_lo = _split(a)
    b_hi, b_lo = _split(b)
    return _dot(a_hi, b_hi) + (_dot(a_hi, b_lo) + _dot(a_lo, b_hi))


def _silu(x):
    return x * jax.nn.sigmoid(x)


def _norm_mod(x, g, shift, scale):
    y = x * lax.rsqrt(jnp.mean(x * x, axis=-1, keepdims=True) + EPS)
    return (y * g) * (1.0 + scale) + shift


def _head_rms(x, bd2, gain):
    sq_hi, sq_lo = _split(x * x)
    ms = _dot(jnp.concatenate([sq_hi, sq_lo], axis=1), bd2)
    return x * lax.rsqrt(ms + EPS) * gain


def _combined(x_ref, comb_refs):
    x = x_ref[...]
    if not comb_refs:
        return x
    gmod_ref, gates_ref = comb_refs[0], comb_refs[1]
    gates = gates_ref[...]
    acc = gates[:, 0:1] * comb_refs[2][...]
    for k in range(1, TOP_K):
        acc = acc + gates[:, k:k + 1] * comb_refs[2 + k][...]
    return x + gmod_ref[0] * acc


def _comb_specs():
    return [_mod_spec(5), _tile_spec(TOP_K)] + [_tile_spec(D_MODEL)] * TOP_K


N_COMB = 2 + TOP_K


def _lane_half(shape, half):
    lane = lax.broadcasted_iota(jnp.int32, shape, len(shape) - 1)
    return (lane < HEAD_DIM) if half == 0 else (lane >= HEAD_DIM)


def _mod_kernel(cond_ref, w_ref, b_ref, o_ref):
    o_ref[0] = _dot3(_silu(cond_ref[...]), w_ref[0]) + b_ref[0]


def _modulation(cond, mod_w, mod_b):
    nc = N_MOD
    return pl.pallas_call(
        _mod_kernel,
        out_shape=jax.ShapeDtypeStruct((DEPTH, SUBLANES, N_MOD * D_MODEL), F32),
        grid=(DEPTH, nc),
        in_specs=[
            pl.BlockSpec((SUBLANES, D_MODEL), lambda l, c: (0, 0)),
            pl.BlockSpec((1, D_MODEL, D_MODEL), lambda l, c: (l, 0, c)),
            pl.BlockSpec((1, 1, D_MODEL), lambda l, c: (l, 0, c)),
        ],
        out_specs=pl.BlockSpec((1, SUBLANES, D_MODEL), lambda l, c: (l, 0, c)),
        compiler_params=_cparams("arbitrary", "arbitrary"),
        name="modulation",
    )(cond, mod_w, mod_b.reshape(DEPTH, 1, N_MOD * D_MODEL))


def _mod_spec(col):
    return pl.BlockSpec((1, 1, D_MODEL), lambda i: (i, 0, col))


def _tile_spec(width):
    return pl.BlockSpec((TILE, width), lambda i: (i, 0))


def _ctx_tile_spec(width):
    return pl.BlockSpec((TILE, width), lambda i: (jnp.minimum(i, N_CTX_TILES - 1), 0))


def _full_spec(shape):
    nd = len(shape)
    return pl.BlockSpec(shape, lambda i: (0,) * nd)


def _even_in_kernel(n_comb, x_ref, *refs):
    comb, refs = refs[:n_comb], refs[n_comb:]
    g_ref, sh_ref, sc_ref, w_ref, bd_ref, qn_ref, kn_ref = refs[:7]
    outs = refs[7:]
    x = _combined(x_ref, comb)
    if n_comb:
        outs[0][...] = x
        outs = outs[1:]
    u_ref, q_ref, k_ref, v_ref = outs
    h = _norm_mod(x, g_ref[...], sh_ref[0], sc_ref[0]).astype(BF16)
    bd2 = bd_ref[...]
    z = _dot(h, w_ref[...])
    u_ref[...] = z[:, 0:A_WIDTH]
    for s in range(B_WIDTH // LANES):
        lo = A_WIDTH + s * LANES
        q_ref[:, s * LANES:(s + 1) * LANES] = _head_rms(z[:, lo:lo + LANES], bd2, qn_ref[...]).astype(BF16)
        lo = A_WIDTH + B_WIDTH + s * LANES
        k_ref[:, s * LANES:(s + 1) * LANES] = _head_rms(z[:, lo:lo + LANES], bd2, kn_ref[...])
    v_ref[...] = z[:, A_WIDTH + 2 * B_WIDTH:]


def _even_in(x, comb, g, modt, w, bd2, qn, kn):
    n_out = A_WIDTH + 3 * B_WIDTH
    out_shape = [
        jax.ShapeDtypeStruct((TILE, N_TILES * A_WIDTH), F32),
        jax.ShapeDtypeStruct((N_TOK, B_WIDTH), BF16),
        jax.ShapeDtypeStruct((N_TOK, B_WIDTH), F32),
        jax.ShapeDtypeStruct((N_TOK, B_WIDTH), F32),
    ]
    out_specs = [pl.BlockSpec((TILE, A_WIDTH), lambda i: (0, i)),
                 _tile_spec(B_WIDTH), _tile_spec(B_WIDTH), _tile_spec(B_WIDTH)]
    if comb:
        out_shape.insert(0, jax.ShapeDtypeStruct((N_TOK, D_MODEL), F32))
        out_specs.insert(0, _tile_spec(D_MODEL))
    return pl.pallas_call(
        functools.partial(_even_in_kernel, len(comb)),
        out_shape=tuple(out_shape),
        grid=(N_TILES,),
        in_specs=[_tile_spec(D_MODEL)] + (_comb_specs() if comb else []) + [
            _full_spec((1, D_MODEL)), _mod_spec(0), _mod_spec(1),
            _full_spec((D_MODEL, n_out)), _full_spec((2 * LANES, LANES)),
            _full_spec((1, LANES)), _full_spec((1, LANES)),
        ],
        out_specs=tuple(out_specs),
        compiler_params=_cparams("arbitrary"),
        name="even_in",
    )(x, *comb, g, modt, modt, w, bd2, qn, kn)


def _cmul(ar, ai, br, bi):
    return ar * br - ai * bi, ar * bi + ai * br


def _s5_kernel(u_ref, a_ref, w_ref, c_ref, s0_ref, y_ref, fin_ref, xr, xi, st_r, st_i):
    grp = pl.program_id(0)
    drn = pl.program_id(1)
    n_chunks = TILE // SCAN_CHUNK
    rows = SCAN_CHUNK * SCAN_ROWS
    part = SSM_LANES // SCAN_LANE_PARTS

    def run(store):
        @pl.loop(0, n_chunks)
        def _(c):
            cc = jnp.where(drn == 0, c, n_chunks - 1 - c)
            t0 = pl.multiple_of(cc * SCAN_CHUNK, SCAN_CHUNK)
            uu = u_ref[pl.ds(t0, SCAN_CHUNK), :, :]
            for s in range(SSM_SLABS):
                us = uu[:, :, s * LANES:(s + 1) * LANES].reshape(rows, LANES).astype(BF16)
                cols = slice(s * SSM_SLAB_STATES, (s + 1) * SSM_SLAB_STATES)
                xr[:, cols] = _dot(us, w_ref[0, 0, s])
                xi[:, cols] = _dot(us, w_ref[0, 1, s])
            for p in range(SCAN_LANE_PARTS):
                ls = slice(p * part, (p + 1) * part)
                ar = a_ref[0, 0, :, ls]
                ai = a_ref[0, 1, :, ls]

                def step(j, carry):
                    sr, si = carry
                    tt = jnp.where(drn == 0, j, SCAN_CHUNK - 1 - j)
                    r0 = pl.multiple_of(tt * SCAN_ROWS, SCAN_ROWS)
                    nr = ar * sr - ai * si + xr[pl.ds(r0, SCAN_ROWS), ls]
                    ni = ar * si + ai * sr + xi[pl.ds(r0, SCAN_ROWS), ls]
                    if store:
                        xr[pl.ds(r0, SCAN_ROWS), ls] = nr
                        xi[pl.ds(r0, SCAN_ROWS), ls] = ni
                    return nr, ni

                sr, si = lax.fori_loop(0, SCAN_CHUNK, step, (st_r[:, ls], st_i[:, ls]), unroll=4)
                st_r[:, ls] = sr
                st_i[:, ls] = si
            if store:
                for s in range(SSM_SLABS):
                    cols = slice(s * SSM_SLAB_STATES, (s + 1) * SSM_SLAB_STATES)
                    ys = (_dot(xr[:, cols].astype(BF16), c_ref[0, 0, s])
                          - _dot(xi[:, cols].astype(BF16), c_ref[0, 1, s]))
                    ys = ys.reshape(SCAN_CHUNK, SCAN_ROWS, LANES)
                    lanes = slice(s * LANES, (s + 1) * LANES)

                    @pl.when(drn == 0)
                    def _():
                        y_ref[pl.ds(t0, SCAN_CHUNK), :, lanes] = ys

                    @pl.when(drn != 0)
                    def _():
                        y_ref[pl.ds(t0, SCAN_CHUNK), :, lanes] += ys

    st_r[...] = s0_ref[0, 0]
    st_i[...] = s0_ref[0, 1]

    @pl.when(grp == SCAN_GROUPS - 1)
    def _():
        st_r[...] = jnp.zeros_like(st_r)
        st_i[...] = jnp.zeros_like(st_i)
        run(False)
        pr, pi = a_ref[0, 0], a_ref[0, 1]
        for _ in range(8):
            pr, pi = _cmul(pr, pi, pr, pi)
        fr, fi = st_r[...], st_i[...]
        s0r, s0i = s0_ref[0, 0], s0_ref[0, 1]
        row = lax.broadcasted_iota(jnp.int32, (SCAN_ROWS, SSM_LANES), 0)
        quarter = row % LAT_TILES_PER_SEQ
        fwd = drn == 0
        keep = quarter != jnp.where(fwd, 0, LAT_TILES_PER_SEQ - 1)
        ir, ii = s0r, s0i
        for _ in range(LAT_TILES_PER_SEQ - 1):
            nr, ni = _cmul(pr, pi, ir, ii)
            nr, ni = nr + fr, ni + fi
            nr = jnp.where(fwd, pltpu.roll(nr, 1, 0), pltpu.roll(nr, SCAN_ROWS - 1, 0))
            ni = jnp.where(fwd, pltpu.roll(ni, 1, 0), pltpu.roll(ni, SCAN_ROWS - 1, 0))
            ir = s0r + jnp.where(keep, nr, 0.0)
            ii = s0i + jnp.where(keep, ni, 0.0)
        st_r[...] = ir
        st_i[...] = ii

    run(True)
    fin_ref[0, 0] = st_r[...]
    fin_ref[0, 1] = st_i[...]


def _s5_scan(u_tb, a_b, w_bd, c_bd, s0):
    rows = SCAN_CHUNK * SCAN_ROWS
    return pl.pallas_call(
        _s5_kernel,
        out_shape=(
            jax.ShapeDtypeStruct((TILE, N_TILES, A_WIDTH), F32),
            jax.ShapeDtypeStruct((2, 2, N_TILES, SSM_LANES), F32),
        ),
        grid=(SCAN_GROUPS, 2),
        in_specs=[
            pl.BlockSpec((TILE, SCAN_ROWS, A_WIDTH), lambda g, d: (0, g, 0)),
            pl.BlockSpec((1, 2, SCAN_ROWS, SSM_LANES), lambda g, d: (d, 0, 0, 0)),
            pl.BlockSpec((1, 2, SSM_SLABS, LANES, SSM_SLAB_STATES), lambda g, d: (d, 0, 0, 0, 0)),
            pl.BlockSpec((1, 2, SSM_SLABS, SSM_SLAB_STATES, LANES), lambda g, d: (d, 0, 0, 0, 0)),
            pl.BlockSpec((1, 2, SCAN_ROWS, SSM_LANES), lambda g, d: (d, 0, g, 0)),
        ],
        out_specs=(
            pl.BlockSpec((TILE, SCAN_ROWS, A_WIDTH), lambda g, d: (0, g, 0)),
            pl.BlockSpec((1, 2, SCAN_ROWS, SSM_LANES), lambda g, d: (d, 0, g, 0)),
        ),
        scratch_shapes=[
            pltpu.VMEM((rows, SSM_LANES), F32), pltpu.VMEM((rows, SSM_LANES), F32),
            pltpu.VMEM((SCAN_ROWS, SSM_LANES), F32), pltpu.VMEM((SCAN_ROWS, SSM_LANES), F32),
        ],
        compiler_params=_cparams("arbitrary", "arbitrary"),
        name="s5_scan",
    )(u_tb, a_b, w_bd, c_bd, s0)


def _s5_params(lam_re, lam_im, log_dt, b_re, b_im, c_re, c_im):
    dt = jnp.exp(log_dt)[..., None]
    mag = jnp.exp(lam_re * dt)
    ab_re, ab_im = mag * jnp.cos(lam_im * dt), mag * jnp.sin(lam_im * dt)
    den = lam_re * lam_re + lam_im * lam_im
    nr, ni = ab_re - 1.0, ab_im
    f_re = (nr * lam_re + ni * lam_im) / den
    f_im = (ni * lam_re - nr * lam_im) / den
    bb_re = f_re[..., None] * b_re - f_im[..., None] * b_im
    bb_im = f_re[..., None] * b_im + f_im[..., None] * b_re
    a_b = jnp.stack([ab_re, ab_im], axis=1).reshape(2, 2, 1, SSM_LANES)
    a_b = jnp.broadcast_to(a_b, (2, 2, SCAN_ROWS, SSM_LANES))
    gps = A_GROUPS // SSM_SLABS
    eye = jnp.eye(gps, dtype=F32)

    def in_bd(bb):
        bb = bb.reshape(2, SSM_SLABS, gps, A_STATE, A_GROUP_CH)
        m = jnp.einsum('dsgph,gk->dsghkp', bb, eye)
        return m.reshape(2, SSM_SLABS, LANES, SSM_SLAB_STATES)

    def out_bd(cc):
        cc = cc.reshape(2, SSM_SLABS, gps, A_GROUP_CH, A_STATE)
        m = jnp.einsum('dsghp,gk->dsgpkh', cc, eye)
        return m.reshape(2, SSM_SLABS, SSM_SLAB_STATES, LANES)

    w_bd = jnp.stack([in_bd(bb_re), in_bd(bb_im)], axis=1).astype(BF16)
    c_bd = jnp.stack([out_bd(c_re), out_bd(c_im)], axis=1).astype(BF16)
    return a_b, w_bd, c_bd


def _softmax_pv(scores, values, sink=None):
    m = functools.reduce(jnp.maximum, [jnp.max(s, axis=-1, keepdims=True) for s in scores])
    if sink is not None:
        m = jnp.maximum(m, sink)
    den = None
    acc = None
    for s, v in zip(scores, values):
        e = jnp.exp(s - m)
        d = jnp.sum(e, axis=-1, keepdims=True)
        o = _dot(e.astype(BF16), v)
        den = d if den is None else den + d
        acc = o if acc is None else acc + o
    if sink is not None:
        den = den + jnp.exp(sink - m)
    return acc / den


def _nat_ctx_kernel(q_ref, k_ref, v_ref, o_ref):
    @pl.when(pl.program_id(0) >= N_CTX_TILES)
    def _():
        o_ref[...] = jnp.zeros_like(o_ref)

    @pl.when(pl.program_id(0) < N_CTX_TILES)
    def _():
        for p in range(B_WIDTH // LANES):
            ls = slice(p * LANES, (p + 1) * LANES)
            qs = q_ref[:, ls]
            ks = k_ref[:, ls].astype(BF16)
            vs = v_ref[:, ls].astype(BF16)
            out = None
            for half in range(2):
                msk = _lane_half(qs.shape, half)
                qm = jnp.where(msk, qs, jnp.zeros_like(qs))
                s = _dot_nt(qm, ks) * ATT_SCALE
                o = _softmax_pv([s], [vs])
                out = o if out is None else jnp.where(msk, o, out)
            o_ref[:, ls] = out.astype(BF16)


def _nat_lat_kernel(o_all_ref, q_ref, k_ref, v_ref, kc_ref, vc_ref, bias_ref, o_ref):
    del o_all_ref
    r = pl.program_id(1)
    wr = WIN_R_MAX
    rs = jnp.clip(r - wr // 2, 0, DEC_SEQ // GRID_W - wr)
    k0 = pl.multiple_of(rs * GRID_W, GRID_W)
    n_nb = wr * GRID_W
    for p in range(B_WIDTH // LANES):
        ls = slice(p * LANES, (p + 1) * LANES)
        qs = q_ref[:, ls]
        ks = k_ref[pl.ds(k0, n_nb), ls].astype(BF16)
        vs = v_ref[pl.ds(k0, n_nb), ls].astype(BF16)
        kc = kc_ref[0, :, ls].astype(BF16)
        vc = vc_ref[0, :, ls].astype(BF16)
        out = None
        for half in range(2):
            msk = _lane_half(qs.shape, half)
            qm = jnp.where(msk, qs, jnp.zeros_like(qs))
            s_nb = _dot_nt(qm, ks) * ATT_SCALE + bias_ref[2 * p + half, 0]
            s_ctx = _dot_nt(qm, kc) * ATT_SCALE
            o = _softmax_pv([s_nb, s_ctx], [vs, vc])
            out = o if out is None else jnp.where(msk, o, out)
        o_ref[:, ls] = out.astype(BF16)


def _nat_attention(q, k, v, kc, vc, bias):
    o_ctx = pl.pallas_call(
        _nat_ctx_kernel,
        out_shape=jax.ShapeDtypeStruct((N_TOK, B_WIDTH), BF16),
        grid=(N_TILES,),
        in_specs=[_ctx_tile_spec(B_WIDTH)] * 3,
        out_specs=_tile_spec(B_WIDTH),
        compiler_params=_cparams("arbitrary"),
        name="nat_ctx",
    )(q, k, v)
    rows = DEC_SEQ // GRID_W
    ctx_seqs = N_CTX_TOK // DEC_SEQ
    lat_row = lambda b, r: (N_CTX_TOK // GRID_W + b * rows + r, 0)

    def bias_idx(b, r):
        return (0, r - jnp.clip(r - WIN_R_MAX // 2, 0, rows - WIN_R_MAX), 0, 0)

    return pl.pallas_call(
        _nat_lat_kernel,
        out_shape=jax.ShapeDtypeStruct((N_TOK, B_WIDTH), BF16),
        grid=(DEC_BATCH, rows),
        in_specs=[
            pl.BlockSpec(memory_space=pl.ANY),
            pl.BlockSpec((GRID_W, B_WIDTH), lat_row),
            pl.BlockSpec((DEC_SEQ, B_WIDTH), lambda b, r: (ctx_seqs + b, 0)),
            pl.BlockSpec((DEC_SEQ, B_WIDTH), lambda b, r: (ctx_seqs + b, 0)),
            pl.BlockSpec((1, PAST_LEN, B_WIDTH), lambda b, r: (b, 0, 0)),
            pl.BlockSpec((1, PAST_LEN, B_WIDTH), lambda b, r: (b, 0, 0)),
            pl.BlockSpec((B_HEADS, 1, GRID_W, WIN_R_MAX * GRID_W), bias_idx),
        ],
        out_specs=pl.BlockSpec((GRID_W, B_WIDTH), lat_row),
        input_output_aliases={0: 0},
        compiler_params=_cparams("arbitrary", "arbitrary"),
        name="nat_lat",
    )(o_ctx, q, k, v, kc, vc, bias)


def _nat_bias(rpb):
    qc = np.arange(GRID_W)
    kc = np.arange(GRID_W)
    cs = np.clip(qc - WIN_C // 2, 0, GRID_W - WIN_C)
    ok = (kc[None, :] >= cs[:, None]) & (kc[None, :] < cs[:, None] + WIN_C)
    dc = np.clip(kc[None, :] - qc[:, None] + (WIN_C - 1), 0, 2 * WIN_C - 2)
    pick = (dc[:, :, None] == np.arange(2 * WIN_C - 1)).astype(np.float32)
    t = jnp.einsum('hrc,qkc->hrqk', rpb.astype(F32), pick, precision=lax.Precision.HIGHEST)
    t = jnp.where(ok[None, None], t, NEG_INF)
    per_d = [t[:, WIN_R_MAX - 1 - d:2 * WIN_R_MAX - 1 - d].transpose(0, 2, 1, 3)
             .reshape(B_HEADS, GRID_W, WIN_R_MAX * GRID_W) for d in range(WIN_R_MAX)]
    return jnp.stack(per_d, axis=1)


def _gqa_heads(q_ref, sink_ref, k_segs, v_segs, masks, o_ref):
    k_rot = [pltpu.roll(k.astype(F32), HEAD_DIM, 1).astype(BF16) for k in k_segs]
    v_rot = [pltpu.roll(v.astype(F32), HEAD_DIM, 1).astype(BF16) for v in v_segs]
    for p in range(C_HEADS // 2):
        ls = slice(p * LANES, (p + 1) * LANES)
        qs = q_ref[:, ls]
        out = None
        for half in range(2):
            head = 2 * p + half
            kv = head // C_GROUP
            ks = k_segs if kv == half else k_rot
            vs = v_segs if kv == half else v_rot
            msk = _lane_half(qs.shape, half)
            qm = jnp.where(msk, qs, jnp.zeros_like(qs))
            scores = []
            for kseg, m in zip(ks, masks):
                s = _dot_nt(qm, kseg) * ATT_SCALE
                scores.append(s if m is None else jnp.where(m, s, NEG_INF))
            o = _softmax_pv(scores, vs, sink=sink_ref[head])
            out = o if out is None else jnp.where(msk, o, out)
        o_ref[:, ls] = out.astype(BF16)


def _swa_ctx_kernel(sink_ref, q_ref, k_ref, v_ref, o_ref):
    @pl.when(pl.program_id(0) >= N_CTX_TILES)
    def _():
        o_ref[...] = jnp.zeros_like(o_ref)

    @pl.when(pl.program_id(0) < N_CTX_TILES)
    def _():
        _gqa_heads(q_ref, sink_ref, [k_ref[...].astype(BF16)], [v_ref[...].astype(BF16)], [None], o_ref)


def _swa_lat_kernel(sink_ref, o_all_ref, q_ref, k_ref, v_ref, kc_ref, vc_ref, o_ref):
    del o_all_ref
    n = pl.program_id(1)
    n_win = 3 * WINDOW
    start = jnp.clip((n - 1) * WINDOW, 0, DEC_SEQ - n_win)
    k0 = pl.multiple_of(start, WINDOW)
    qpos = n * WINDOW + lax.broadcasted_iota(jnp.int32, (WINDOW, n_win), 0)
    kpos = start + lax.broadcasted_iota(jnp.int32, (WINDOW, n_win), 1)
    ok = jnp.abs(qpos - kpos) <= WINDOW
    _gqa_heads(q_ref, sink_ref,
               [k_ref[pl.ds(k0, n_win), :].astype(BF16), kc_ref[0].astype(BF16)],
               [v_ref[pl.ds(k0, n_win), :].astype(BF16), vc_ref[0].astype(BF16)],
               [ok, None], o_ref)


def _swa_attention(q, k, v, kc, vc, sink):
    nq, nkv = C_HEADS * HEAD_DIM, C_KV_HEADS * HEAD_DIM
    smem = pl.BlockSpec(memory_space=pltpu.SMEM)
    o_ctx = pl.pallas_call(
        _swa_ctx_kernel,
        out_shape=jax.ShapeDtypeStruct((N_TOK, nq), BF16),
        grid=(N_TILES,),
        in_specs=[smem, _ctx_tile_spec(nq), _ctx_tile_spec(nkv), _ctx_tile_spec(nkv)],
        out_specs=_tile_spec(nq),
        compiler_params=_cparams("arbitrary"),
        name="swa_ctx",
    )(sink, q, k, v)
    nb = DEC_SEQ // WINDOW
    ctx_seqs = N_CTX_TOK // DEC_SEQ
    lat_row = lambda b, n: (N_CTX_TOK // WINDOW + b * nb + n, 0)
    return pl.pallas_call(
        _swa_lat_kernel,
        out_shape=jax.ShapeDtypeStruct((N_TOK, nq), BF16),
        grid=(DEC_BATCH, nb),
        in_specs=[
            smem,
            pl.BlockSpec(memory_space=pl.ANY),
            pl.BlockSpec((WINDOW, nq), lat_row),
            pl.BlockSpec((DEC_SEQ, nkv), lambda b, n: (ctx_seqs + b, 0)),
            pl.BlockSpec((DEC_SEQ, nkv), lambda b, n: (ctx_seqs + b, 0)),
            pl.BlockSpec((1, PAST_LEN, nkv), lambda b, n: (b, 0, 0)),
            pl.BlockSpec((1, PAST_LEN, nkv), lambda b, n: (b, 0, 0)),
        ],
        out_specs=pl.BlockSpec((WINDOW, nq), lat_row),
        input_output_aliases={1: 0},
        compiler_params=_cparams("arbitrary", "arbitrary"),
        name="swa_lat",
    )(sink, o_ctx, q, k, v, kc, vc)


def _moe_input(x, g_ref, sh_ref, sc_ref, rw_ref, rb_ref, h_ref, lg_ref):
    h = _norm_mod(x, g_ref[...], sh_ref[0], sc_ref[0])
    h_ref[...] = _pack_bf16_pair(h[:, 0:D_MODEL // 2], h[:, D_MODEL // 2:])
    lg_ref[...] = _dot3(h, rw_ref[...]) + rb_ref[...]


def _moe_input_specs():
    return [_full_spec((1, D_MODEL)), _mod_spec(3), _mod_spec(4),
            _full_spec((D_MODEL, ROUTER_PAD)), _full_spec((1, ROUTER_PAD))]


def _mixer_out_shapes():
    shapes = (jax.ShapeDtypeStruct((N_TOK, D_MODEL), F32),
              jax.ShapeDtypeStruct((N_TOK, D_MODEL // 2), U32),
              jax.ShapeDtypeStruct((N_TOK, ROUTER_PAD), F32))
    specs = (_tile_spec(D_MODEL), _tile_spec(D_MODEL // 2), _tile_spec(ROUTER_PAD))
    return shapes, specs


def _even_out_kernel(y_ref, u_ref, o_ref, x_ref, gate_ref, d_ref, gw_ref, gb_ref, w_ref,
                     g2_ref, sh_ref, sc_ref, rw_ref, rb_ref, xo_ref, h_ref, lg_ref):
    yy = y_ref[...] + d_ref[...] * u_ref[...]
    g = jax.nn.gelu(yy)
    a = g * jax.nn.sigmoid(_dot(g.astype(BF16), gw_ref[...]) + gb_ref[...])
    mix = _dot(a.astype(BF16), w_ref[0:A_WIDTH, :]) + _dot(o_ref[...], w_ref[A_WIDTH:, :])
    x = x_ref[...] + gate_ref[0] * mix
    xo_ref[...] = x
    _moe_input(x, g2_ref, sh_ref, sc_ref, rw_ref, rb_ref, h_ref, lg_ref)


def _even_out(y_t, u_t, o, x, modt, d_skip, glu_w, glu_b, w_out, g2, rw, rb):
    tm_spec = pl.BlockSpec((TILE, A_WIDTH), lambda i: (0, i))
    out_shape, out_specs = _mixer_out_shapes()
    return pl.pallas_call(
        _even_out_kernel,
        out_shape=out_shape,
        grid=(N_TILES,),
        in_specs=[
            tm_spec, tm_spec, _tile_spec(B_WIDTH), _tile_spec(D_MODEL), _mod_spec(2),
            _full_spec((1, A_WIDTH)), _full_spec((A_WIDTH, A_WIDTH)), _full_spec((1, A_WIDTH)),
            _full_spec((A_WIDTH + B_WIDTH, D_MODEL)),
        ] + _moe_input_specs(),
        out_specs=out_specs,
        compiler_params=_cparams("arbitrary"),
        name="even_out",
    )(y_t, u_t, o, x, modt, d_skip, glu_w, glu_b, w_out, g2, modt, modt, rw, rb)


def _rope(x, cos, sin):
    lane = lax.broadcasted_iota(jnp.int32, x.shape, 1)
    first = (lane % (HEAD_DIM // 2)) < (HEAD_DIM // 4)
    partner = jnp.where(first, pltpu.roll(x, LANES - HEAD_DIM // 4, 1), pltpu.roll(x, HEAD_DIM // 4, 1))
    return x * cos + partner * sin


def _odd_in_kernel(n_comb, x_ref, *refs):
    comb, refs = refs[:n_comb], refs[n_comb:]
    g_ref, sh_ref, sc_ref, w_ref, bd_ref, qn_ref, kn_ref, cos_ref, sin_ref = refs[:9]
    outs = refs[9:]
    x = _combined(x_ref, comb)
    if n_comb:
        outs[0][...] = x
        outs = outs[1:]
    q_ref, kc_ref, kr_ref, v_ref = outs
    i = pl.program_id(0)
    h = _norm_mod(x, g_ref[...], sh_ref[0], sc_ref[0]).astype(BF16)
    bd2 = bd_ref[...]
    lat = i >= N_CTX_TILES
    cos = jnp.where(lat, cos_ref[...], 1.0)
    sin = jnp.where(lat, sin_ref[...], 0.0)
    nq = C_HEADS * HEAD_DIM
    z = _dot(h, w_ref[...])
    for s in range(nq // LANES):
        q = _head_rms(z[:, s * LANES:(s + 1) * LANES], bd2, qn_ref[...])
        q_ref[:, s * LANES:(s + 1) * LANES] = _rope(q, cos, sin).astype(BF16)
    k = _head_rms(z[:, nq:nq + LANES], bd2, kn_ref[...])
    kc_ref[...] = k
    kr_ref[...] = _rope(k, cos, sin)
    v_ref[...] = z[:, nq + LANES:]


def _odd_in(x, comb, g, modt, w, bd2, qn, kn, cos, sin):
    nq, nkv = C_HEADS * HEAD_DIM, C_KV_HEADS * HEAD_DIM
    lat_spec = pl.BlockSpec((TILE, LANES), lambda i: (jnp.maximum(i - N_CTX_TILES, 0) % LAT_TILES_PER_SEQ, 0))
    out_shape = [
        jax.ShapeDtypeStruct((N_TOK, nq), BF16),
        jax.ShapeDtypeStruct((N_TOK, nkv), F32),
        jax.ShapeDtypeStruct((N_TOK, nkv), F32),
        jax.ShapeDtypeStruct((N_TOK, nkv), F32),
    ]
    out_specs = [_tile_spec(nq), _tile_spec(nkv), _tile_spec(nkv), _tile_spec(nkv)]
    if comb:
        out_shape.insert(0, jax.ShapeDtypeStruct((N_TOK, D_MODEL), F32))
        out_specs.insert(0, _tile_spec(D_MODEL))
    return pl.pallas_call(
        functools.partial(_odd_in_kernel, len(comb)),
        out_shape=tuple(out_shape),
        grid=(N_TILES,),
        in_specs=[_tile_spec(D_MODEL)] + (_comb_specs() if comb else []) + [
            _full_spec((1, D_MODEL)), _mod_spec(0), _mod_spec(1),
            _full_spec((D_MODEL, nq + 2 * nkv)), _full_spec((2 * LANES, LANES)),
            _full_spec((1, LANES)), _full_spec((1, LANES)), lat_spec, lat_spec,
        ],
        out_specs=tuple(out_specs),
        compiler_params=_cparams("arbitrary"),
        name="odd_in",
    )(x, *comb, g, modt, modt, w, bd2, qn, kn, cos, sin)


def _rope_tables():
    nf = HEAD_DIM // 4
    inv = ROPE_BASE ** (-jnp.arange(nf, dtype=F32) / nf)
    t = jnp.arange(DEC_SEQ)
    pos = jnp.stack([t // GRID_W, t % GRID_W], axis=-1).astype(F32)
    ang = pos[:, :, None] * inv
    cos, sin = jnp.cos(ang), jnp.sin(ang)
    cos_h = jnp.stack([cos, cos], axis=2).reshape(DEC_SEQ, HEAD_DIM)
    sin_h = jnp.stack([-sin, sin], axis=2).reshape(DEC_SEQ, HEAD_DIM)
    return jnp.tile(cos_h, (1, 2)), jnp.tile(sin_h, (1, 2))


def _odd_out_kernel(o_ref, x_ref, gate_ref, w_ref, g2_ref, sh_ref, sc_ref, rw_ref, rb_ref,
                    xo_ref, h_ref, lg_ref):
    x = x_ref[...] + gate_ref[0] * _dot(o_ref[...], w_ref[...])
    xo_ref[...] = x
    _moe_input(x, g2_ref, sh_ref, sc_ref, rw_ref, rb_ref, h_ref, lg_ref)


def _odd_out(o, x, modt, w_o, g2, rw, rb):
    nq = C_HEADS * HEAD_DIM
    out_shape, out_specs = _mixer_out_shapes()
    return pl.pallas_call(
        _odd_out_kernel,
        out_shape=out_shape,
        grid=(N_TILES,),
        in_specs=[_tile_spec(nq), _tile_spec(D_MODEL), _mod_spec(2), _full_spec((nq, D_MODEL))]
        + _moe_input_specs(),
        out_specs=out_specs,
        compiler_params=_cparams("arbitrary"),
        name="odd_out",
    )(o, x, modt, w_o, g2, modt, modt, rw, rb)


def _pack_bf16_pair(lo, hi):
    lo_bits = lax.bitcast_convert_type(lo.astype(BF16).astype(F32), U32)
    hi_bits = lax.bitcast_convert_type(hi.astype(BF16).astype(F32), U32)
    return (hi_bits & jnp.uint32(0xFFFF0000)) | (lo_bits >> 16)


def _unpack_bf16_pair(packed):
    lo = lax.bitcast_convert_type(packed << 16, F32).astype(BF16)
    hi = lax.bitcast_convert_type(packed & jnp.uint32(0xFFFF0000), F32).astype(BF16)
    return lo, hi


def _moe_expert_kernel(layer, be_ref, nb_ref, first_ref, slot_ref, nxt_ref, src_cur_ref, src_nxt_ref,
                       h_ref, wgu_hbm, bgu_ref, wd_hbm, bd_ref, o_ref, xg0, xg1, wgu_buf, wd_buf, wsem):
    i = pl.program_id(0)
    nb = nb_ref[0]
    half = D_MODEL // 2

    def gather(idx_ref, dst):
        for r in range(MOE_TM):
            dst[pl.ds(r, 1), :] = h_ref[pl.ds(idx_ref[0, 0, r], 1), :]

    def weight_copies(e, slot):
        return (pltpu.make_async_copy(wgu_hbm.at[layer, e], wgu_buf.at[slot], wsem.at[0, slot]),
                pltpu.make_async_copy(wd_hbm.at[layer, e], wd_buf.at[slot], wsem.at[1, slot]))

    @pl.when(i == 0)
    def _():
        for cp in weight_copies(be_ref[0], 0):
            cp.start()
        gather(src_cur_ref, xg0)

    def block(x_cur, x_nxt):
        ws = slot_ref[i]

        @pl.when(first_ref[i] == 1)
        def _():
            for cp in weight_copies(be_ref[i], ws):
                cp.wait()

            @pl.when(nxt_ref[i] >= 0)
            def _():
                for cp in weight_copies(nxt_ref[i], 1 - ws):
                    cp.start()

        x_lo, x_hi = _unpack_bf16_pair(x_cur[...])
        gu = (_dot(x_lo, wgu_buf[ws, 0:half, :].astype(BF16)) + _dot(x_hi, wgu_buf[ws, half:, :].astype(BF16))
              + bgu_ref[0, 0])
        gather(src_nxt_ref, x_nxt)
        g = jnp.minimum(gu[:, 0:D_EXPERT], SWIGLU_LIMIT)
        lin = jnp.clip(gu[:, D_EXPERT:], -SWIGLU_LIMIT, SWIGLU_LIMIT)
        act = g * jax.nn.sigmoid(SWIGLU_ALPHA * g) * (lin + 1.0)
        o_ref[...] = _dot(act.astype(BF16), wd_buf[ws].astype(BF16)) + bd_ref[0, 0]

    @pl.when((i < nb) & (i % 2 == 0))
    def _():
        block(xg0, xg1)

    @pl.when((i < nb) & (i % 2 == 1))
    def _():
        block(xg1, xg0)

    @pl.when(i >= nb)
    def _():
        o_ref[...] = jnp.zeros_like(o_ref)


def _moe_experts(layer, plan, h_packed, w_gu, b_gu, w_down, b_down):
    block_e, n_used, first, slot, nxt, row_tok = plan
    half = D_MODEL // 2
    idx_spec = lambda f: pl.BlockSpec((1, 1, MOE_TM), f, memory_space=pltpu.SMEM)
    hbm = pl.BlockSpec(memory_space=pl.ANY)
    grid_spec = pltpu.PrefetchScalarGridSpec(
        num_scalar_prefetch=5,
        grid=(MOE_BLOCKS,),
        in_specs=[
            idx_spec(lambda i, *_: (i, 0, 0)),
            idx_spec(lambda i, *_: (jnp.minimum(i + 1, MOE_BLOCKS - 1), 0, 0)),
            pl.BlockSpec((N_TOK, half), lambda i, *_: (0, 0), pipeline_mode=pl.Buffered(1)),
            hbm,
            pl.BlockSpec((1, 1, 1, 2 * D_EXPERT), lambda i, be, *_: (layer, be[i], 0, 0)),
            hbm,
            pl.BlockSpec((1, 1, 1, D_MODEL), lambda i, be, *_: (layer, be[i], 0, 0)),
        ],
        out_specs=pl.BlockSpec((MOE_TM, D_MODEL), lambda i, *_: (i, 0)),
        scratch_shapes=[
            pltpu.VMEM((MOE_TM, half), U32), pltpu.VMEM((MOE_TM, half), U32),
            pltpu.VMEM((2, D_MODEL, 2 * D_EXPERT), F32), pltpu.VMEM((2, D_EXPERT, D_MODEL), F32),
            pltpu.SemaphoreType.DMA((2, 2)),
        ],
    )
    rows = row_tok.reshape(MOE_BLOCKS, 1, MOE_TM)
    return pl.pallas_call(
        functools.partial(_moe_expert_kernel, layer),
        out_shape=jax.ShapeDtypeStruct((MOE_ROWS, D_MODEL), F32),
        grid_spec=grid_spec,
        compiler_params=pltpu.CompilerParams(dimension_semantics=("arbitrary",), vmem_limit_bytes=MOE_VMEM_LIMIT),
        name="moe_experts",
    )(block_e, n_used, first, slot, nxt, rows, rows, h_packed, w_gu,
      b_gu.reshape(DEPTH, N_EXPERTS, 1, 2 * D_EXPERT), w_down, b_down.reshape(DEPTH, N_EXPERTS, 1, D_MODEL))


def _moe_combine_kernel(x_ref, *refs):
    refs[N_COMB][...] = _combined(x_ref, refs[:N_COMB])


def _moe_combine(x, comb):
    return pl.pallas_call(
        _moe_combine_kernel,
        out_shape=jax.ShapeDtypeStruct((N_TOK, D_MODEL), F32),
        grid=(N_TILES,),
        in_specs=[_tile_spec(D_MODEL)] + _comb_specs(),
        out_specs=_tile_spec(D_MODEL),
        compiler_params=_cparams("arbitrary"),
        name="moe_combine",
    )(x, *comb)


def _lut(idx, table):
    n = table.shape[0]
    hit = idx[:, None] == jnp.arange(n, dtype=idx.dtype)[None, :]
    return jnp.sum(jnp.where(hit, table[None, :], 0), axis=1)


def _moe_route(logits):
    i32 = jnp.int32
    top_val, top_idx = lax.top_k(logits[:, :N_EXPERTS], TOP_K)
    gates = jax.nn.softmax(top_val, axis=-1)
    flat_e = top_idx.reshape(N_ASG).astype(i32)
    asg = jnp.arange(N_ASG, dtype=i32)
    experts = jnp.arange(N_EXPERTS, dtype=i32)
    skey = jnp.sort(flat_e * ASG_STRIDE + asg)
    sorted_e = skey // ASG_STRIDE
    sorted_asg = skey % ASG_STRIDE
    counts = jnp.sum((flat_e[:, None] == experts[None, :]).astype(i32), axis=0)
    start = jnp.cumsum(counts) - counts
    pad_counts = (counts + MOE_TM - 1) // MOE_TM * MOE_TM
    pad_end = jnp.cumsum(pad_counts)
    pad_start = pad_end - pad_counts
    n_used = pad_end[-1] // MOE_TM
    pos = asg + _lut(sorted_e, pad_start - start)
    _, dest = lax.sort((sorted_asg, pos), num_keys=1)
    blk = jnp.arange(MOE_BLOCKS, dtype=i32)
    block_e = jnp.sum((blk[:, None] * MOE_TM >= pad_end[None, :]).astype(i32), axis=1)
    last_e = jnp.max(jnp.where(counts > 0, experts, 0))
    block_e = jnp.where(blk < n_used, jnp.minimum(block_e, N_EXPERTS - 1), last_e)
    first = jnp.concatenate([jnp.ones((1,), i32), (block_e[1:] != block_e[:-1]).astype(i32)])
    slot = (jnp.cumsum(first) - 1) % 2
    later = (experts[None, :] > experts[:, None]) & (counts[None, :] > 0)
    nxt_of = jnp.min(jnp.where(later, experts[None, :], N_EXPERTS), axis=1)
    nxt = _lut(block_e, jnp.where(nxt_of < N_EXPERTS, nxt_of, -1))
    row = jnp.arange(MOE_ROWS, dtype=i32)
    e_row = jnp.repeat(block_e, MOE_TM)
    off = row - _lut(e_row, pad_start)
    valid = (off < _lut(e_row, counts)) & (row < n_used * MOE_TM)
    src = jnp.clip(_lut(e_row, start) + off, 0, N_ASG - 1)
    row_tok = jnp.where(valid, sorted_asg[src] // TOP_K, 0)
    plan = (block_e, n_used.reshape(1).astype(i32), first, slot.astype(i32), nxt.astype(i32), row_tok)
    return gates, dest.reshape(N_TOK, TOP_K), plan


def _moe(layer, h_packed, logits, modt, w_gu, b_gu, w_down, b_down):
    gates, dest, plan = _moe_route(logits)
    y_rows = _moe_experts(layer, plan, h_packed, w_gu, b_gu, w_down, b_down)
    return (modt, gates) + tuple(y_rows[dest[:, k]] for k in range(TOP_K))


def kernel(x_prompt, x_sample, cache_nat_k, cache_nat_v, cache_swa_k, cache_swa_v, state_ssm, c, c_ctx,
           norm1_g, norm2_g, mod_w, mod_b,
           ab_w_in, ab_w_out, ssm_lam_re, ssm_lam_im, ssm_log_dt, ssm_b_re, ssm_b_im, ssm_c_re, ssm_c_im,
           ssm_d, ssm_glu_w, ssm_glu_b, nat_qn, nat_kn, nat_rpb,
           swa_w_qkv, swa_w_o, swa_qn, swa_kn, swa_sink,
           moe_router_w, moe_router_b, moe_w_gu, moe_b_gu, moe_w_down, moe_b_down):
    x = jnp.concatenate([x_prompt.reshape(N_CTX_TOK, D_MODEL), x_sample.reshape(N_LAT_TOK, D_MODEL)], axis=0)
    cond = jnp.zeros((SUBLANES, D_MODEL), F32).at[0].set(c_ctx).at[1:1 + DEC_BATCH].set(c)
    mod = _modulation(cond, mod_w, mod_b)
    tile_row = np.concatenate([np.zeros(N_CTX_TILES, np.int32),
                               1 + np.arange(N_TILES - N_CTX_TILES, dtype=np.int32) // LAT_TILES_PER_SEQ])
    head_gain = lambda gn: jnp.tile(gn, 2).reshape(1, LANES)
    bd = np.kron(np.eye(2, dtype=np.float32), np.full((HEAD_DIM, HEAD_DIM), 1.0 / HEAD_DIM, np.float32))
    bd2 = jnp.asarray(np.concatenate([bd, bd], axis=0), BF16)
    rope_cos, rope_sin = _rope_tables()
    router_w = jnp.pad(moe_router_w, ((0, 0), (0, 0), (0, ROUTER_PAD - N_EXPERTS)))
    router_b = jnp.pad(moe_router_b, ((0, 0), (0, ROUTER_PAD - N_EXPERTS))).reshape(DEPTH, 1, ROUTER_PAD)
    nkv = C_KV_HEADS * HEAD_DIM

    nat_k_out, nat_v_out, swa_k_out, swa_v_out, ssm_out = [], [], [], [], []
    comb = ()
    for l in range(DEPTH):
        modt = mod[l][tile_row].reshape(N_TILES, 1, N_MOD * D_MODEL)
        g1 = norm1_g[l].reshape(1, D_MODEL)
        moe_in = (norm2_g[l].reshape(1, D_MODEL), router_w[l], router_b[l])
        i = l // 2
        if l % 2 == 0:
            outs = _even_in(x, comb, g1, modt, ab_w_in[i].astype(BF16), bd2,
                            head_gain(nat_qn[i]), head_gain(nat_kn[i]))
            if comb:
                x, outs = outs[0], outs[1:]
            u_t, q, k, v = outs
            a_b, w_bd, c_bd = _s5_params(ssm_lam_re[i], ssm_lam_im[i], ssm_log_dt[i], ssm_b_re[i], ssm_b_im[i],
                                         ssm_c_re[i], ssm_c_im[i])
            st = state_ssm[:, i].reshape(DEC_BATCH, 2, 2, SSM_LANES).transpose(1, 2, 0, 3)
            s0 = jnp.zeros((2, 2, N_TILES, SSM_LANES), F32)
            first = N_CTX_TILES + LAT_TILES_PER_SEQ * np.arange(DEC_BATCH)
            s0 = s0.at[0, :, first].set(st[0].transpose(1, 0, 2))
            s0 = s0.at[1, :, first + LAT_TILES_PER_SEQ - 1].set(st[1].transpose(1, 0, 2))
            y_t, fin = _s5_scan(u_t.reshape(TILE, N_TILES, A_WIDTH), a_b, w_bd, c_bd, s0)
            o = _nat_attention(q, k, v,
                               cache_nat_k[:, i].reshape(DEC_BATCH, PAST_LEN, B_WIDTH),
                               cache_nat_v[:, i].reshape(DEC_BATCH, PAST_LEN, B_WIDTH),
                               _nat_bias(nat_rpb[i]))
            x, h_packed, logits = _even_out(y_t.reshape(TILE, N_TILES * A_WIDTH), u_t, o, x, modt,
                                            ssm_d[i].reshape(1, A_WIDTH), ssm_glu_w[i].astype(BF16),
                                            ssm_glu_b[i].reshape(1, A_WIDTH), ab_w_out[i].astype(BF16), *moe_in)
            ssm_out.append(fin[:, :, :N_CTX_TILES].transpose(2, 0, 1, 3).reshape(BATCH, 2, 2, A_GROUPS, A_STATE))
            nat_k_out.append(k[:N_CTX_TOK].reshape(BATCH, SEQ, B_HEADS, HEAD_DIM))
            nat_v_out.append(v[:N_CTX_TOK].reshape(BATCH, SEQ, B_HEADS, HEAD_DIM))
        else:
            outs = _odd_in(x, comb, g1, modt, swa_w_qkv[i].astype(BF16), bd2,
                           head_gain(swa_qn[i]), head_gain(swa_kn[i]), rope_cos, rope_sin)
            if comb:
                x, outs = outs[0], outs[1:]
            q, k_plain, k_rot, v = outs
            o = _swa_attention(q, k_rot, v,
                               cache_swa_k[:, i].reshape(DEC_BATCH, PAST_LEN, nkv),
                               cache_swa_v[:, i].reshape(DEC_BATCH, PAST_LEN, nkv), swa_sink[i])
            x, h_packed, logits = _odd_out(o, x, modt, swa_w_o[i].astype(BF16), *moe_in)
            swa_k_out.append(k_plain[:N_CTX_TOK].reshape(BATCH, SEQ, C_KV_HEADS, HEAD_DIM))
            swa_v_out.append(v[:N_CTX_TOK].reshape(BATCH, SEQ, C_KV_HEADS, HEAD_DIM))
        comb = _moe(l, h_packed, logits, modt, moe_w_gu, moe_b_gu, moe_w_down, moe_b_down)

    x = _moe_combine(x, comb)
    y_prompt = x[:N_CTX_TOK].reshape(BATCH, SEQ, D_MODEL)
    y_sample = x[N_CTX_TOK:].reshape(DEC_BATCH, DEC_SEQ, D_MODEL)
    return (y_prompt, y_sample,
            jnp.stack(nat_k_out, axis=1), jnp.stack(nat_v_out, axis=1),
            jnp.stack(swa_k_out, axis=1), jnp.stack(swa_v_out, axis=1),
            jnp.stack(ssm_out, axis=1))
```

```python
import functools

import jax
import jax.numpy as jnp
import numpy as np
from jax import lax
from jax.experimental import pallas as pl
from jax.experimental.pallas import tpu as pltpu

F32 = jnp.float32
BF16 = jnp.bfloat16
U32 = jnp.uint32

D_MODEL = 1024
BATCH = 32
SEQ = 256
DEPTH = 4
DEC_BATCH = 2
DEC_SEQ = 1024
PAST_LEN = 256
GRID_W = 64
HEAD_DIM = 64
N_MOD = 6
N_EVEN = (DEPTH + 1) // 2
N_ODD = DEPTH // 2
A_WIDTH = 512
A_GROUP_CH = 16
A_GROUPS = 32
A_STATE = 64
B_HEADS = 8
B_WIDTH = 512
WIN_R_MAX = 8
WIN_C = 16
C_HEADS = 16
C_KV_HEADS = 2
C_GROUP = 8
WINDOW = 128
ROPE_BASE = 10000.0
N_EXPERTS = 32
TOP_K = 4
D_EXPERT = 1024
SWIGLU_LIMIT = 7.0
SWIGLU_ALPHA = 1.702
EPS = 1e-6
NEG_INF = -1e30

LANES = 128
SUBLANES = 8
TILE = 256
N_CTX_TOK = BATCH * SEQ
N_LAT_TOK = DEC_BATCH * DEC_SEQ
N_TOK = N_CTX_TOK + N_LAT_TOK
N_TILES = N_TOK // TILE
N_CTX_TILES = N_CTX_TOK // TILE
LAT_TILES_PER_SEQ = DEC_SEQ // TILE
SSM_LANES = A_GROUPS * A_STATE
SSM_SLABS = A_WIDTH // LANES
SSM_SLAB_STATES = SSM_LANES // SSM_SLABS
SCAN_ROWS = 8
SCAN_GROUPS = N_TILES // SCAN_ROWS
SCAN_CHUNK = 64
SCAN_LANE_PARTS = 2
MOE_TM = 256
MOE_STEP_BLOCKS = 4
N_ASG = N_TOK * TOP_K
ASG_STRIDE = 1 << 16
MOE_BLOCKS = N_ASG // MOE_TM + N_EXPERTS
MOE_ROWS = MOE_BLOCKS * MOE_TM
ROUTER_PAD = LANES
VMEM_LIMIT = 56 * 1024 * 1024
MOE_VMEM_LIMIT = 60 * 1024 * 1024
ATT_SCALE = HEAD_DIM ** -0.5


def _cparams(*sem):
    return pltpu.CompilerParams(dimension_semantics=sem, vmem_limit_bytes=VMEM_LIMIT)


def _dot(a, b):
    return jnp.dot(a, b, preferred_element_type=F32)


def _dot_nt(a, b):
    return lax.dot_general(a, b, (((1,), (1,)), ((), ())), preferred_element_type=F32)


def _split(a):
    hi = a.astype(BF16)
    lo = (a - hi.astype(F32)).astype(BF16)
    return hi, lo


def _dot3(a, b):
    a_hi, a_lo = _split(a)
    b_hi, b_lo = _split(b)
    return _dot(a_hi, b_hi) + (_dot(a_hi, b_lo) + _dot(a_lo, b_hi))


def _silu(x):
    return x * jax.nn.sigmoid(x)


def _norm_mod(x, g, shift, scale):
    y = x * lax.rsqrt(jnp.mean(x * x, axis=-1, keepdims=True) + EPS)
    return (y * g) * (1.0 + scale) + shift


def _head_rms(x, bd2, gain):
    sq_hi, sq_lo = _split(x * x)
    ms = _dot(jnp.concatenate([sq_hi, sq_lo], axis=1), bd2)
    return x * lax.rsqrt(ms + EPS) * gain


def _combined(x_ref, comb_refs):
    x = x_ref[...]
    if not comb_refs:
        return x
    gmod_ref, gates_ref = comb_refs[0], comb_refs[1]
    gates = gates_ref[...]
    acc = gates[:, 0:1] * comb_refs[2][...]
    for k in range(1, TOP_K):
        acc = acc + gates[:, k:k + 1] * comb_refs[2 + k][...]
    return x + gmod_ref[0] * acc


def _comb_specs():
    return [_mod_spec(5), _tile_spec(TOP_K)] + [_tile_spec(D_MODEL)] * TOP_K


N_COMB = 2 + TOP_K


def _lane_half(shape, half):
    lane = lax.broadcasted_iota(jnp.int32, shape, len(shape) - 1)
    return (lane < HEAD_DIM) if half == 0 else (lane >= HEAD_DIM)


def _mod_kernel(cond_ref, w_ref, b_ref, o_ref):
    o_ref[0] = _dot3(_silu(cond_ref[...]), w_ref[0]) + b_ref[0]


def _modulation(cond, mod_w, mod_b):
    nc = N_MOD
    return pl.pallas_call(
        _mod_kernel,
        out_shape=jax.ShapeDtypeStruct((DEPTH, SUBLANES, N_MOD * D_MODEL), F32),
        grid=(DEPTH, nc),
        in_specs=[
            pl.BlockSpec((SUBLANES, D_MODEL), lambda l, c: (0, 0)),
            pl.BlockSpec((1, D_MODEL, D_MODEL), lambda l, c: (l, 0, c)),
            pl.BlockSpec((1, 1, D_MODEL), lambda l, c: (l, 0, c)),
        ],
        out_specs=pl.BlockSpec((1, SUBLANES, D_MODEL), lambda l, c: (l, 0, c)),
        compiler_params=_cparams("arbitrary", "arbitrary"),
        name="modulation",
    )(cond, mod_w, mod_b.reshape(DEPTH, 1, N_MOD * D_MODEL))


def _mod_spec(col):
    return pl.BlockSpec((1, 1, D_MODEL), lambda i: (i, 0, col))


def _tile_spec(width):
    return pl.BlockSpec((TILE, width), lambda i: (i, 0))


def _ctx_tile_spec(width):
    return pl.BlockSpec((TILE, width), lambda i: (jnp.minimum(i, N_CTX_TILES - 1), 0))


def _full_spec(shape):
    nd = len(shape)
    return pl.BlockSpec(shape, lambda i: (0,) * nd)


def _cache_store(slot, n_slots, ref, value):
    @pl.when(pl.program_id(0) < N_CTX_TILES)
    def _():
        if slot == 0:
            ref[0, 0] = value
            for other in range(1, n_slots):
                ref[0, other] = jnp.zeros_like(value)
        else:
            ref[0, 0] = value


def _cache_specs(slot, n_slots, width):
    shape = jax.ShapeDtypeStruct((BATCH, n_slots, SEQ, width), F32)
    seq = lambda i: jnp.minimum(i, N_CTX_TILES - 1)
    if slot == 0:
        return shape, pl.BlockSpec((1, n_slots, SEQ, width), lambda i: (seq(i), 0, 0, 0))
    return shape, pl.BlockSpec((1, 1, SEQ, width), lambda i: (seq(i), slot, 0, 0))


def _even_in_kernel(n_comb, slot, n_prev, x_ref, *refs):
    comb, refs = refs[:n_comb], refs[n_comb:]
    g_ref, sh_ref, sc_ref, w_ref, bd_ref, qn_ref, kn_ref = refs[:7]
    outs = refs[7 + n_prev:]
    x = _combined(x_ref, comb)
    if n_comb:
        outs[0][...] = x
        outs = outs[1:]
    u_ref, q_ref, k_ref, v_ref, kc_ref, vc_ref = outs
    h = _norm_mod(x, g_ref[...], sh_ref[0], sc_ref[0]).astype(BF16)
    bd2 = bd_ref[...]
    z = _dot(h, w_ref[...])
    u_ref[...] = z[:, 0:A_WIDTH]
    ks = []
    for s in range(B_WIDTH // LANES):
        lo = A_WIDTH + s * LANES
        q_ref[:, s * LANES:(s + 1) * LANES] = _head_rms(z[:, lo:lo + LANES], bd2, qn_ref[...]).astype(BF16)
        lo = A_WIDTH + B_WIDTH + s * LANES
        ks.append(_head_rms(z[:, lo:lo + LANES], bd2, kn_ref[...]))
    k = jnp.concatenate(ks, axis=1)
    v = z[:, A_WIDTH + 2 * B_WIDTH:]
    k_ref[...] = k
    v_ref[...] = v
    _cache_store(slot, N_EVEN, kc_ref, k)
    _cache_store(slot, N_EVEN, vc_ref, v)


def _even_in(x, comb, prev_caches, slot, g, modt, w, bd2, qn, kn):
    n_out = A_WIDTH + 3 * B_WIDTH
    cache_shape, cache_spec = _cache_specs(slot, N_EVEN, B_WIDTH)
    out_shape = [
        jax.ShapeDtypeStruct((TILE, N_TILES * A_WIDTH), F32),
        jax.ShapeDtypeStruct((N_TOK, B_WIDTH), BF16),
        jax.ShapeDtypeStruct((N_TOK, B_WIDTH), F32),
        jax.ShapeDtypeStruct((N_TOK, B_WIDTH), F32),
        cache_shape, cache_shape,
    ]
    out_specs = [pl.BlockSpec((TILE, A_WIDTH), lambda i: (0, i)),
                 _tile_spec(B_WIDTH), _tile_spec(B_WIDTH), _tile_spec(B_WIDTH), cache_spec, cache_spec]
    if comb:
        out_shape.insert(0, jax.ShapeDtypeStruct((N_TOK, D_MODEL), F32))
        out_specs.insert(0, _tile_spec(D_MODEL))
    n_in = 1 + len(comb) + 7
    aliases = {n_in + j: len(out_shape) - 2 + j for j in range(len(prev_caches))}
    return pl.pallas_call(
        functools.partial(_even_in_kernel, len(comb), slot, len(prev_caches)),
        out_shape=tuple(out_shape),
        grid=(N_TILES,),
        in_specs=[_tile_spec(D_MODEL)] + (_comb_specs() if comb else []) + [
            _full_spec((1, D_MODEL)), _mod_spec(0), _mod_spec(1),
            _full_spec((D_MODEL, n_out)), _full_spec((2 * LANES, LANES)),
            _full_spec((1, LANES)), _full_spec((1, LANES)),
        ] + [pl.BlockSpec(memory_space=pl.ANY)] * len(prev_caches),
        out_specs=tuple(out_specs),
        input_output_aliases=aliases,
        compiler_params=_cparams("arbitrary"),
        name="even_in",
    )(x, *comb, g, modt, modt, w, bd2, qn, kn, *prev_caches)


def _cmul(ar, ai, br, bi):
    return ar * br - ai * bi, ar * bi + ai * br


def _s5_kernel(u_ref, a_ref, w_ref, c_ref, s0_ref, y_ref, fin_ref, xr, xi, st_r, st_i):
    grp = pl.program_id(0)
    drn = pl.program_id(1)
    n_chunks = TILE // SCAN_CHUNK
    rows = SCAN_CHUNK * SCAN_ROWS
    part = SSM_LANES // SCAN_LANE_PARTS

    def run(store):
        @pl.loop(0, n_chunks)
        def _(c):
            cc = jnp.where(drn == 0, c, n_chunks - 1 - c)
            t0 = pl.multiple_of(cc * SCAN_CHUNK, SCAN_CHUNK)
            uu = u_ref[pl.ds(t0, SCAN_CHUNK), :, :]
            for s in range(SSM_SLABS):
                us = uu[:, :, s * LANES:(s + 1) * LANES].reshape(rows, LANES).astype(BF16)
                cols = slice(s * SSM_SLAB_STATES, (s + 1) * SSM_SLAB_STATES)
                xr[:, cols] = _dot(us, w_ref[0, 0, s])
                xi[:, cols] = _dot(us, w_ref[0, 1, s])
            for p in range(SCAN_LANE_PARTS):
                ls = slice(p * part, (p + 1) * part)
                ar = a_ref[0, 0, :, ls]
                ai = a_ref[0, 1, :, ls]

                def step(j, carry):
                    sr, si = carry
                    tt = jnp.where(drn == 0, j, SCAN_CHUNK - 1 - j)
                    r0 = pl.multiple_of(tt * SCAN_ROWS, SCAN_ROWS)
                    nr = ar * sr - ai * si + xr[pl.ds(r0, SCAN_ROWS), ls]
                    ni = ar * si + ai * sr + xi[pl.ds(r0, SCAN_ROWS), ls]
                    if store:
                        xr[pl.ds(r0, SCAN_ROWS), ls] = nr
                        xi[pl.ds(r0, SCAN_ROWS), ls] = ni
                    return nr, ni

                sr, si = lax.fori_loop(0, SCAN_CHUNK, step, (st_r[:, ls], st_i[:, ls]), unroll=4)
                st_r[:, ls] = sr
                st_i[:, ls] = si
            if store:
                for s in range(SSM_SLABS):
                    cols = slice(s * SSM_SLAB_STATES, (s + 1) * SSM_SLAB_STATES)
                    ys = (_dot(xr[:, cols].astype(BF16), c_ref[0, 0, s])
                          - _dot(xi[:, cols].astype(BF16), c_ref[0, 1, s]))
                    ys = ys.reshape(SCAN_CHUNK, SCAN_ROWS, LANES)
                    lanes = slice(s * LANES, (s + 1) * LANES)

                    @pl.when(drn == 0)
                    def _():
                        y_ref[pl.ds(t0, SCAN_CHUNK), :, lanes] = ys

                    @pl.when(drn != 0)
                    def _():
                        y_ref[pl.ds(t0, SCAN_CHUNK), :, lanes] += ys

    st_r[...] = s0_ref[0, 0]
    st_i[...] = s0_ref[0, 1]

    @pl.when(grp == SCAN_GROUPS - 1)
    def _():
        st_r[...] = jnp.zeros_like(st_r)
        st_i[...] = jnp.zeros_like(st_i)
        run(False)
        pr, pi = a_ref[0, 0], a_ref[0, 1]
        for _ in range(8):
            pr, pi = _cmul(pr, pi, pr, pi)
        fr, fi = st_r[...], st_i[...]
        s0r, s0i = s0_ref[0, 0], s0_ref[0, 1]
        row = lax.broadcasted_iota(jnp.int32, (SCAN_ROWS, SSM_LANES), 0)
        quarter = row % LAT_TILES_PER_SEQ
        fwd = drn == 0
        keep = quarter != jnp.where(fwd, 0, LAT_TILES_PER_SEQ - 1)
        ir, ii = s0r, s0i
        for _ in range(LAT_TILES_PER_SEQ - 1):
            nr, ni = _cmul(pr, pi, ir, ii)
            nr, ni = nr + fr, ni + fi
            nr = jnp.where(fwd, pltpu.roll(nr, 1, 0), pltpu.roll(nr, SCAN_ROWS - 1, 0))
            ni = jnp.where(fwd, pltpu.roll(ni, 1, 0), pltpu.roll(ni, SCAN_ROWS - 1, 0))
            ir = s0r + jnp.where(keep, nr, 0.0)
            ii = s0i + jnp.where(keep, ni, 0.0)
        st_r[...] = ir
        st_i[...] = ii

    run(True)
    fin_ref[0, 0] = st_r[...]
    fin_ref[0, 1] = st_i[...]


def _s5_scan(u_tb, a_b, w_bd, c_bd, s0):
    rows = SCAN_CHUNK * SCAN_ROWS
    return pl.pallas_call(
        _s5_kernel,
        out_shape=(
            jax.ShapeDtypeStruct((TILE, N_TILES, A_WIDTH), F32),
            jax.ShapeDtypeStruct((2, 2, N_TILES, SSM_LANES), F32),
        ),
        grid=(SCAN_GROUPS, 2),
        in_specs=[
            pl.BlockSpec((TILE, SCAN_ROWS, A_WIDTH), lambda g, d: (0, g, 0)),
            pl.BlockSpec((1, 2, SCAN_ROWS, SSM_LANES), lambda g, d: (d, 0, 0, 0)),
            pl.BlockSpec((1, 2, SSM_SLABS, LANES, SSM_SLAB_STATES), lambda g, d: (d, 0, 0, 0, 0)),
            pl.BlockSpec((1, 2, SSM_SLABS, SSM_SLAB_STATES, LANES), lambda g, d: (d, 0, 0, 0, 0)),
            pl.BlockSpec((1, 2, SCAN_ROWS, SSM_LANES), lambda g, d: (d, 0, g, 0)),
        ],
        out_specs=(
            pl.BlockSpec((TILE, SCAN_ROWS, A_WIDTH), lambda g, d: (0, g, 0)),
            pl.BlockSpec((1, 2, SCAN_ROWS, SSM_LANES), lambda g, d: (d, 0, g, 0)),
        ),
        scratch_shapes=[
            pltpu.VMEM((rows, SSM_LANES), F32), pltpu.VMEM((rows, SSM_LANES), F32),
            pltpu.VMEM((SCAN_ROWS, SSM_LANES), F32), pltpu.VMEM((SCAN_ROWS, SSM_LANES), F32),
        ],
        compiler_params=_cparams("arbitrary", "arbitrary"),
        name="s5_scan",
    )(u_tb, a_b, w_bd, c_bd, s0)


def _s5_params(lam_re, lam_im, log_dt, b_re, b_im, c_re, c_im):
    dt = jnp.exp(log_dt)[..., None]
    mag = jnp.exp(lam_re * dt)
    ab_re, ab_im = mag * jnp.cos(lam_im * dt), mag * jnp.sin(lam_im * dt)
    den = lam_re * lam_re + lam_im * lam_im
    nr, ni = ab_re - 1.0, ab_im
    f_re = (nr * lam_re + ni * lam_im) / den
    f_im = (ni * lam_re - nr * lam_im) / den
    bb_re = f_re[..., None] * b_re - f_im[..., None] * b_im
    bb_im = f_re[..., None] * b_im + f_im[..., None] * b_re
    a_b = jnp.stack([ab_re, ab_im], axis=1).reshape(2, 2, 1, SSM_LANES)
    a_b = jnp.broadcast_to(a_b, (2, 2, SCAN_ROWS, SSM_LANES))
    gps = A_GROUPS // SSM_SLABS
    eye = jnp.eye(gps, dtype=F32)

    def in_bd(bb):
        bb = bb.reshape(2, SSM_SLABS, gps, A_STATE, A_GROUP_CH)
        m = jnp.einsum('dsgph,gk->dsghkp', bb, eye)
        return m.reshape(2, SSM_SLABS, LANES, SSM_SLAB_STATES)

    def out_bd(cc):
        cc = cc.reshape(2, SSM_SLABS, gps, A_GROUP_CH, A_STATE)
        m = jnp.einsum('dsghp,gk->dsgpkh', cc, eye)
        return m.reshape(2, SSM_SLABS, SSM_SLAB_STATES, LANES)

    w_bd = jnp.stack([in_bd(bb_re), in_bd(bb_im)], axis=1).astype(BF16)
    c_bd = jnp.stack([out_bd(c_re), out_bd(c_im)], axis=1).astype(BF16)
    return a_b, w_bd, c_bd


def _softmax_pv(scores, values, sink=None):
    m = functools.reduce(jnp.maximum, [jnp.max(s, axis=-1, keepdims=True) for s in scores])
    if sink is not None:
        m = jnp.maximum(m, sink)
    den = None
    acc = None
    for s, v in zip(scores, values):
        e = jnp.exp(s - m)
        d = jnp.sum(e, axis=-1, keepdims=True)
        o = _dot(e.astype(BF16), v)
        den = d if den is None else den + d
        acc = o if acc is None else acc + o
    if sink is not None:
        den = den + jnp.exp(sink - m)
    return acc / den


def _nat_ctx_kernel(q_ref, k_ref, v_ref, o_ref):
    @pl.when(pl.program_id(0) >= N_CTX_TILES)
    def _():
        o_ref[...] = jnp.zeros_like(o_ref)

    @pl.when(pl.program_id(0) < N_CTX_TILES)
    def _():
        for p in range(B_WIDTH // LANES):
            ls = slice(p * LANES, (p + 1) * LANES)
            qs = q_ref[:, ls]
            ks = k_ref[:, ls].astype(BF16)
            vs = v_ref[:, ls].astype(BF16)
            out = None
            for half in range(2):
                msk = _lane_half(qs.shape, half)
                qm = jnp.where(msk, qs, jnp.zeros_like(qs))
                s = _dot_nt(qm, ks) * ATT_SCALE
                o = _softmax_pv([s], [vs])
                out = o if out is None else jnp.where(msk, o, out)
            o_ref[:, ls] = out.astype(BF16)


def _nat_lat_kernel(o_all_ref, q_ref, k_ref, v_ref, kc_ref, vc_ref, bias_ref, o_ref):
    del o_all_ref
    r = pl.program_id(1)
    wr = WIN_R_MAX
    rs = jnp.clip(r - wr // 2, 0, DEC_SEQ // GRID_W - wr)
    k0 = pl.multiple_of(rs * GRID_W, GRID_W)
    n_nb = wr * GRID_W
    for p in range(B_WIDTH // LANES):
        ls = slice(p * LANES, (p + 1) * LANES)
        qs = q_ref[:, ls]
        ks = k_ref[pl.ds(k0, n_nb), ls].astype(BF16)
        vs = v_ref[pl.ds(k0, n_nb), ls].astype(BF16)
        kc = kc_ref[0, :, ls].astype(BF16)
        vc = vc_ref[0, :, ls].astype(BF16)
        out = None
        for half in range(2):
            msk = _lane_half(qs.shape, half)
            qm = jnp.where(msk, qs, jnp.zeros_like(qs))
            s_nb = _dot_nt(qm, ks) * ATT_SCALE + bias_ref[2 * p + half, 0]
            s_ctx = _dot_nt(qm, kc) * ATT_SCALE
            o = _softmax_pv([s_nb, s_ctx], [vs, vc])
            out = o if out is None else jnp.where(msk, o, out)
        o_ref[:, ls] = out.astype(BF16)


def _nat_attention(q, k, v, kc, vc, bias):
    o_ctx = pl.pallas_call(
        _nat_ctx_kernel,
        out_shape=jax.ShapeDtypeStruct((N_TOK, B_WIDTH), BF16),
        grid=(N_TILES,),
        in_specs=[_ctx_tile_spec(B_WIDTH)] * 3,
        out_specs=_tile_spec(B_WIDTH),
        compiler_params=_cparams("arbitrary"),
        name="nat_ctx",
    )(q, k, v)
    rows = DEC_SEQ // GRID_W
    ctx_seqs = N_CTX_TOK // DEC_SEQ
    lat_row = lambda b, r: (N_CTX_TOK // GRID_W + b * rows + r, 0)

    def bias_idx(b, r):
        return (0, r - jnp.clip(r - WIN_R_MAX // 2, 0, rows - WIN_R_MAX), 0, 0)

    return pl.pallas_call(
        _nat_lat_kernel,
        out_shape=jax.ShapeDtypeStruct((N_TOK, B_WIDTH), BF16),
        grid=(DEC_BATCH, rows),
        in_specs=[
            pl.BlockSpec(memory_space=pl.ANY),
            pl.BlockSpec((GRID_W, B_WIDTH), lat_row),
            pl.BlockSpec((DEC_SEQ, B_WIDTH), lambda b, r: (ctx_seqs + b, 0)),
            pl.BlockSpec((DEC_SEQ, B_WIDTH), lambda b, r: (ctx_seqs + b, 0)),
            pl.BlockSpec((1, PAST_LEN, B_WIDTH), lambda b, r: (b, 0, 0)),
            pl.BlockSpec((1, PAST_LEN, B_WIDTH), lambda b, r: (b, 0, 0)),
            pl.BlockSpec((B_HEADS, 1, GRID_W, WIN_R_MAX * GRID_W), bias_idx),
        ],
        out_specs=pl.BlockSpec((GRID_W, B_WIDTH), lat_row),
        input_output_aliases={0: 0},
        compiler_params=_cparams("arbitrary", "arbitrary"),
        name="nat_lat",
    )(o_ctx, q, k, v, kc, vc, bias)


def _nat_bias(rpb):
    qc = np.arange(GRID_W)
    kc = np.arange(GRID_W)
    cs = np.clip(qc - WIN_C // 2, 0, GRID_W - WIN_C)
    ok = (kc[None, :] >= cs[:, None]) & (kc[None, :] < cs[:, None] + WIN_C)
    dc = np.clip(kc[None, :] - qc[:, None] + (WIN_C - 1), 0, 2 * WIN_C - 2)
    pick = (dc[:, :, None] == np.arange(2 * WIN_C - 1)).astype(np.float32)
    t = jnp.einsum('hrc,qkc->hrqk', rpb.astype(F32), pick, precision=lax.Precision.HIGHEST)
    t = jnp.where(ok[None, None], t, NEG_INF)
    per_d = [t[:, WIN_R_MAX - 1 - d:2 * WIN_R_MAX - 1 - d].transpose(0, 2, 1, 3)
             .reshape(B_HEADS, GRID_W, WIN_R_MAX * GRID_W) for d in range(WIN_R_MAX)]
    return jnp.stack(per_d, axis=1)


def _gqa_heads(q_ref, sink_ref, k_segs, v_segs, masks, o_ref):
    k_rot = [pltpu.roll(k.astype(F32), HEAD_DIM, 1).astype(BF16) for k in k_segs]
    v_rot = [pltpu.roll(v.astype(F32), HEAD_DIM, 1).astype(BF16) for v in v_segs]
    for p in range(C_HEADS // 2):
        ls = slice(p * LANES, (p + 1) * LANES)
        qs = q_ref[:, ls]
        out = None
        for half in range(2):
            head = 2 * p + half
            kv = head // C_GROUP
            ks = k_segs if kv == half else k_rot
            vs = v_segs if kv == half else v_rot
            msk = _lane_half(qs.shape, half)
            qm = jnp.where(msk, qs, jnp.zeros_like(qs))
            scores = []
            for kseg, m in zip(ks, masks):
                s = _dot_nt(qm, kseg) * ATT_SCALE
                scores.append(s if m is None else jnp.where(m, s, NEG_INF))
            o = _softmax_pv(scores, vs, sink=sink_ref[head])
            out = o if out is None else jnp.where(msk, o, out)
        o_ref[:, ls] = out.astype(BF16)


def _swa_ctx_kernel(sink_ref, q_ref, k_ref, v_ref, o_ref):
    @pl.when(pl.program_id(0) >= N_CTX_TILES)
    def _():
        o_ref[...] = jnp.zeros_like(o_ref)

    @pl.when(pl.program_id(0) < N_CTX_TILES)
    def _():
        _gqa_heads(q_ref, sink_ref, [k_ref[...].astype(BF16)], [v_ref[...].astype(BF16)], [None], o_ref)


def _swa_lat_kernel(sink_ref, o_all_ref, q_ref, k_ref, v_ref, kc_ref, vc_ref, o_ref):
    del o_all_ref
    n = pl.program_id(1)
    n_win = 3 * WINDOW
    start = jnp.clip((n - 1) * WINDOW, 0, DEC_SEQ - n_win)
    k0 = pl.multiple_of(start, WINDOW)
    qpos = n * WINDOW + lax.broadcasted_iota(jnp.int32, (WINDOW, n_win), 0)
    kpos = start + lax.broadcasted_iota(jnp.int32, (WINDOW, n_win), 1)
    ok = jnp.abs(qpos - kpos) <= WINDOW
    _gqa_heads(q_ref, sink_ref,
               [k_ref[pl.ds(k0, n_win), :].astype(BF16), kc_ref[0].astype(BF16)],
               [v_ref[pl.ds(k0, n_win), :].astype(BF16), vc_ref[0].astype(BF16)],
               [ok, None], o_ref)


def _swa_attention(q, k, v, kc, vc, sink):
    nq, nkv = C_HEADS * HEAD_DIM, C_KV_HEADS * HEAD_DIM
    smem = pl.BlockSpec(memory_space=pltpu.SMEM)
    o_ctx = pl.pallas_call(
        _swa_ctx_kernel,
        out_shape=jax.ShapeDtypeStruct((N_TOK, nq), BF16),
        grid=(N_TILES,),
        in_specs=[smem, _ctx_tile_spec(nq), _ctx_tile_spec(nkv), _ctx_tile_spec(nkv)],
        out_specs=_tile_spec(nq),
        compiler_params=_cparams("arbitrary"),
        name="swa_ctx",
    )(sink, q, k, v)
    nb = DEC_SEQ // WINDOW
    ctx_seqs = N_CTX_TOK // DEC_SEQ
    lat_row = lambda b, n: (N_CTX_TOK // WINDOW + b * nb + n, 0)
    return pl.pallas_call(
        _swa_lat_kernel,
        out_shape=jax.ShapeDtypeStruct((N_TOK, nq), BF16),
        grid=(DEC_BATCH, nb),
        in_specs=[
            smem,
            pl.BlockSpec(memory_space=pl.ANY),
            pl.BlockSpec((WINDOW, nq), lat_row),
            pl.BlockSpec((DEC_SEQ, nkv), lambda b, n: (ctx_seqs + b, 0)),
            pl.BlockSpec((DEC_SEQ, nkv), lambda b, n: (ctx_seqs + b, 0)),
            pl.BlockSpec((1, PAST_LEN, nkv), lambda b, n: (b, 0, 0)),
            pl.BlockSpec((1, PAST_LEN, nkv), lambda b, n: (b, 0, 0)),
        ],
        out_specs=pl.BlockSpec((WINDOW, nq), lat_row),
        input_output_aliases={1: 0},
        compiler_params=_cparams("arbitrary", "arbitrary"),
        name="swa_lat",
    )(sink, o_ctx, q, k, v, kc, vc)


def _moe_input(x, g_ref, sh_ref, sc_ref, rw_ref, rb_ref, h_ref, lg_ref):
    h = _norm_mod(x, g_ref[...], sh_ref[0], sc_ref[0])
    h_ref[...] = _pack_bf16_pair(h[:, 0:D_MODEL // 2], h[:, D_MODEL // 2:])
    lg_ref[...] = _dot3(h, rw_ref[...]) + rb_ref[...]


def _moe_input_specs():
    return [_full_spec((1, D_MODEL)), _mod_spec(3), _mod_spec(4),
            _full_spec((D_MODEL, ROUTER_PAD)), _full_spec((1, ROUTER_PAD))]


def _mixer_out_shapes():
    shapes = (jax.ShapeDtypeStruct((N_TOK, D_MODEL), F32),
              jax.ShapeDtypeStruct((N_TOK, D_MODEL // 2), U32),
              jax.ShapeDtypeStruct((N_TOK, ROUTER_PAD), F32))
    specs = (_tile_spec(D_MODEL), _tile_spec(D_MODEL // 2), _tile_spec(ROUTER_PAD))
    return shapes, specs


def _even_out_kernel(y_ref, u_ref, o_ref, x_ref, gate_ref, d_ref, gw_ref, gb_ref, w_ref,
                     g2_ref, sh_ref, sc_ref, rw_ref, rb_ref, xo_ref, h_ref, lg_ref):
    yy = y_ref[...] + d_ref[...] * u_ref[...]
    g = jax.nn.gelu(yy)
    a = g * jax.nn.sigmoid(_dot(g.astype(BF16), gw_ref[...]) + gb_ref[...])
    mix = _dot(a.astype(BF16), w_ref[0:A_WIDTH, :]) + _dot(o_ref[...], w_ref[A_WIDTH:, :])
    x = x_ref[...] + gate_ref[0] * mix
    xo_ref[...] = x
    _moe_input(x, g2_ref, sh_ref, sc_ref, rw_ref, rb_ref, h_ref, lg_ref)


def _even_out(y_t, u_t, o, x, modt, d_skip, glu_w, glu_b, w_out, g2, rw, rb):
    tm_spec = pl.BlockSpec((TILE, A_WIDTH), lambda i: (0, i))
    out_shape, out_specs = _mixer_out_shapes()
    return pl.pallas_call(
        _even_out_kernel,
        out_shape=out_shape,
        grid=(N_TILES,),
        in_specs=[
            tm_spec, tm_spec, _tile_spec(B_WIDTH), _tile_spec(D_MODEL), _mod_spec(2),
            _full_spec((1, A_WIDTH)), _full_spec((A_WIDTH, A_WIDTH)), _full_spec((1, A_WIDTH)),
            _full_spec((A_WIDTH + B_WIDTH, D_MODEL)),
        ] + _moe_input_specs(),
        out_specs=out_specs,
        compiler_params=_cparams("arbitrary"),
        name="even_out",
    )(y_t, u_t, o, x, modt, d_skip, glu_w, glu_b, w_out, g2, modt, modt, rw, rb)


def _rope(x, cos, sin):
    lane = lax.broadcasted_iota(jnp.int32, x.shape, 1)
    first = (lane % (HEAD_DIM // 2)) < (HEAD_DIM // 4)
    partner = jnp.where(first, pltpu.roll(x, LANES - HEAD_DIM // 4, 1), pltpu.roll(x, HEAD_DIM // 4, 1))
    return x * cos + partner * sin


def _odd_in_kernel(n_comb, slot, n_prev, x_ref, *refs):
    comb, refs = refs[:n_comb], refs[n_comb:]
    g_ref, sh_ref, sc_ref, w_ref, bd_ref, qn_ref, kn_ref, cos_ref, sin_ref = refs[:9]
    outs = refs[9 + n_prev:]
    x = _combined(x_ref, comb)
    if n_comb:
        outs[0][...] = x
        outs = outs[1:]
    q_ref, kr_ref, v_ref, kcache_ref, vcache_ref = outs
    i = pl.program_id(0)
    h = _norm_mod(x, g_ref[...], sh_ref[0], sc_ref[0]).astype(BF16)
    bd2 = bd_ref[...]
    lat = i >= N_CTX_TILES
    cos = jnp.where(lat, cos_ref[...], 1.0)
    sin = jnp.where(lat, sin_ref[...], 0.0)
    nq = C_HEADS * HEAD_DIM
    z = _dot(h, w_ref[...])
    for s in range(nq // LANES):
        q = _head_rms(z[:, s * LANES:(s + 1) * LANES], bd2, qn_ref[...])
        q_ref[:, s * LANES:(s + 1) * LANES] = _rope(q, cos, sin).astype(BF16)
    k = _head_rms(z[:, nq:nq + LANES], bd2, kn_ref[...])
    v = z[:, nq + LANES:]
    kr_ref[...] = _rope(k, cos, sin)
    v_ref[...] = v
    _cache_store(slot, N_ODD, kcache_ref, k)
    _cache_store(slot, N_ODD, vcache_ref, v)


def _odd_in(x, comb, prev_caches, slot, g, modt, w, bd2, qn, kn, cos, sin):
    nq, nkv = C_HEADS * HEAD_DIM, C_KV_HEADS * HEAD_DIM
    lat_spec = pl.BlockSpec((TILE, LANES), lambda i: (jnp.maximum(i - N_CTX_TILES, 0) % LAT_TILES_PER_SEQ, 0))
    cache_shape, cache_spec = _cache_specs(slot, N_ODD, nkv)
    out_shape = [
        jax.ShapeDtypeStruct((N_TOK, nq), BF16),
        jax.ShapeDtypeStruct((N_TOK, nkv), F32),
        jax.ShapeDtypeStruct((N_TOK, nkv), F32),
        cache_shape, cache_shape,
    ]
    out_specs = [_tile_spec(nq), _tile_spec(nkv), _tile_spec(nkv), cache_spec, cache_spec]
    if comb:
        out_shape.insert(0, jax.ShapeDtypeStruct((N_TOK, D_MODEL), F32))
        out_specs.insert(0, _tile_spec(D_MODEL))
    n_in = 1 + len(comb) + 9
    aliases = {n_in + j: len(out_shape) - 2 + j for j in range(len(prev_caches))}
    return pl.pallas_call(
        functools.partial(_odd_in_kernel, len(comb), slot, len(prev_caches)),
        out_shape=tuple(out_shape),
        grid=(N_TILES,),
        in_specs=[_tile_spec(D_MODEL)] + (_comb_specs() if comb else []) + [
            _full_spec((1, D_MODEL)), _mod_spec(0), _mod_spec(1),
            _full_spec((D_MODEL, nq + 2 * nkv)), _full_spec((2 * LANES, LANES)),
            _full_spec((1, LANES)), _full_spec((1, LANES)), lat_spec, lat_spec,
        ] + [pl.BlockSpec(memory_space=pl.ANY)] * len(prev_caches),
        out_specs=tuple(out_specs),
        input_output_aliases=aliases,
        compiler_params=_cparams("arbitrary"),
        name="odd_in",
    )(x, *comb, g, modt, modt, w, bd2, qn, kn, cos, sin, *prev_caches)


def _rope_tables():
    nf = HEAD_DIM // 4
    inv = ROPE_BASE ** (-jnp.arange(nf, dtype=F32) / nf)
    t = jnp.arange(DEC_SEQ)
    pos = jnp.stack([t // GRID_W, t % GRID_W], axis=-1).astype(F32)
    ang = pos[:, :, None] * inv
    cos, sin = jnp.cos(ang), jnp.sin(ang)
    cos_h = jnp.stack([cos, cos], axis=2).reshape(DEC_SEQ, HEAD_DIM)
    sin_h = jnp.stack([-sin, sin], axis=2).reshape(DEC_SEQ, HEAD_DIM)
    return jnp.tile(cos_h, (1, 2)), jnp.tile(sin_h, (1, 2))


def _odd_out_kernel(o_ref, x_ref, gate_ref, w_ref, g2_ref, sh_ref, sc_ref, rw_ref, rb_ref,
                    xo_ref, h_ref, lg_ref):
    x = x_ref[...] + gate_ref[0] * _dot(o_ref[...], w_ref[...])
    xo_ref[...] = x
    _moe_input(x, g2_ref, sh_ref, sc_ref, rw_ref, rb_ref, h_ref, lg_ref)


def _odd_out(o, x, modt, w_o, g2, rw, rb):
    nq = C_HEADS * HEAD_DIM
    out_shape, out_specs = _mixer_out_shapes()
    return pl.pallas_call(
        _odd_out_kernel,
        out_shape=out_shape,
        grid=(N_TILES,),
        in_specs=[_tile_spec(nq), _tile_spec(D_MODEL), _mod_spec(2), _full_spec((nq, D_MODEL))]
        + _moe_input_specs(),
        out_specs=out_specs,
        compiler_params=_cparams("arbitrary"),
        name="odd_out",
    )(o, x, modt, w_o, g2, modt, modt, rw, rb)


def _pack_bf16_pair(lo, hi):
    lo_bits = lax.bitcast_convert_type(lo.astype(BF16).astype(F32), U32)
    hi_bits = lax.bitcast_convert_type(hi.astype(BF16).astype(F32), U32)
    return (hi_bits & jnp.uint32(0xFFFF0000)) | (lo_bits >> 16)


def _unpack_bf16_pair(packed):
    lo = lax.bitcast_convert_type(packed << 16, F32).astype(BF16)
    hi = lax.bitcast_convert_type(packed & jnp.uint32(0xFFFF0000), F32).astype(BF16)
    return lo, hi


def _moe_expert_kernel(layer, be_ref, nb_ref, first_ref, slot_ref, nxt_ref, src_cur_ref, src_nxt_ref,
                       h_ref, wgu_hbm, bgu_ref, wd_hbm, bd_ref, o_ref, xg0, xg1, wgu_buf, wd_buf, wsem):
    i = pl.program_id(0)
    nb = nb_ref[0]
    half = D_MODEL // 2
    xg = (xg0, xg1)

    def gather(idx_ref, u, dst):
        for r in range(MOE_TM):
            dst[pl.ds(r, 1), :] = h_ref[pl.ds(idx_ref[0, 0, u * MOE_TM + r], 1), :]

    def weight_copies(e, slot):
        return (pltpu.make_async_copy(wgu_hbm.at[layer, e], wgu_buf.at[slot], wsem.at[0, slot]),
                pltpu.make_async_copy(wd_hbm.at[layer, e], wd_buf.at[slot], wsem.at[1, slot]))

    @pl.when(i == 0)
    def _():
        for cp in weight_copies(be_ref[0], 0):
            cp.start()
        gather(src_cur_ref, 0, xg0)

    for u in range(MOE_STEP_BLOCKS):
        blk = i * MOE_STEP_BLOCKS + u
        rows = pl.ds(u * MOE_TM, MOE_TM)
        x_cur, x_nxt = xg[u % 2], xg[(u + 1) % 2]

        @pl.when(blk < nb)
        def _():
            ws = slot_ref[blk]
            e = be_ref[blk]

            @pl.when(first_ref[blk] == 1)
            def _():
                for cp in weight_copies(e, ws):
                    cp.wait()

                @pl.when(nxt_ref[blk] >= 0)
                def _():
                    for cp in weight_copies(nxt_ref[blk], 1 - ws):
                        cp.start()

            x_lo, x_hi = _unpack_bf16_pair(x_cur[...])
            gu = (_dot(x_lo, wgu_buf[ws, 0:half, :].astype(BF16)) + _dot(x_hi, wgu_buf[ws, half:, :].astype(BF16))
                  + bgu_ref[0, pl.ds(e, 1), :])
            if u + 1 < MOE_STEP_BLOCKS:
                gather(src_cur_ref, u + 1, x_nxt)
            else:
                gather(src_nxt_ref, 0, x_nxt)
            g = jnp.minimum(gu[:, 0:D_EXPERT], SWIGLU_LIMIT)
            lin = jnp.clip(gu[:, D_EXPERT:], -SWIGLU_LIMIT, SWIGLU_LIMIT)
            act = g * jax.nn.sigmoid(SWIGLU_ALPHA * g) * (lin + 1.0)
            o_ref[rows, :] = _dot(act.astype(BF16), wd_buf[ws].astype(BF16)) + bd_ref[0, pl.ds(e, 1), :]

        @pl.when(blk >= nb)
        def _():
            o_ref[rows, :] = jnp.zeros((MOE_TM, D_MODEL), F32)


def _moe_experts(layer, plan, h_packed, w_gu, b_gu, w_down, b_down):
    block_e, n_used, first, slot, nxt, row_tok = plan
    half = D_MODEL // 2
    steps = MOE_BLOCKS // MOE_STEP_BLOCKS
    step_rows = MOE_STEP_BLOCKS * MOE_TM
    idx_spec = lambda f: pl.BlockSpec((1, 1, step_rows), f, memory_space=pltpu.SMEM)
    hbm = pl.BlockSpec(memory_space=pl.ANY)
    grid_spec = pltpu.PrefetchScalarGridSpec(
        num_scalar_prefetch=5,
        grid=(steps,),
        in_specs=[
            idx_spec(lambda i, *_: (i, 0, 0)),
            idx_spec(lambda i, *_: (jnp.minimum(i + 1, steps - 1), 0, 0)),
            pl.BlockSpec((N_TOK, half), lambda i, *_: (0, 0), pipeline_mode=pl.Buffered(1)),
            hbm,
            pl.BlockSpec((1, N_EXPERTS, 2 * D_EXPERT), lambda i, *_: (layer, 0, 0)),
            hbm,
            pl.BlockSpec((1, N_EXPERTS, D_MODEL), lambda i, *_: (layer, 0, 0)),
        ],
        out_specs=pl.BlockSpec((step_rows, D_MODEL), lambda i, *_: (i, 0)),
        scratch_shapes=[
            pltpu.VMEM((MOE_TM, half), U32), pltpu.VMEM((MOE_TM, half), U32),
            pltpu.VMEM((2, D_MODEL, 2 * D_EXPERT), F32), pltpu.VMEM((2, D_EXPERT, D_MODEL), F32),
            pltpu.SemaphoreType.DMA((2, 2)),
        ],
    )
    rows = row_tok.reshape(steps, 1, step_rows)
    return pl.pallas_call(
        functools.partial(_moe_expert_kernel, layer),
        out_shape=jax.ShapeDtypeStruct((MOE_ROWS, D_MODEL), F32),
        grid_spec=grid_spec,
        compiler_params=pltpu.CompilerParams(dimension_semantics=("arbitrary",), vmem_limit_bytes=MOE_VMEM_LIMIT),
        name="moe_experts",
    )(block_e, n_used, first, slot, nxt, rows, rows, h_packed, w_gu, b_gu, w_down, b_down)


def _moe_combine_kernel(x_ref, *refs):
    refs[N_COMB][...] = _combined(x_ref, refs[:N_COMB])


def _moe_combine(x, comb):
    return pl.pallas_call(
        _moe_combine_kernel,
        out_shape=jax.ShapeDtypeStruct((N_TOK, D_MODEL), F32),
        grid=(N_TILES,),
        in_specs=[_tile_spec(D_MODEL)] + _comb_specs(),
        out_specs=_tile_spec(D_MODEL),
        compiler_params=_cparams("arbitrary"),
        name="moe_combine",
    )(x, *comb)


def _lut(idx, table):
    n = table.shape[0]
    hit = idx[:, None] == jnp.arange(n, dtype=idx.dtype)[None, :]
    return jnp.sum(jnp.where(hit, table[None, :], 0), axis=1)


def _moe_route(logits):
    i32 = jnp.int32
    top_val, top_idx = lax.top_k(logits[:, :N_EXPERTS], TOP_K)
    gates = jax.nn.softmax(top_val, axis=-1)
    flat_e = top_idx.reshape(N_ASG).astype(i32)
    asg = jnp.arange(N_ASG, dtype=i32)
    experts = jnp.arange(N_EXPERTS, dtype=i32)
    skey = jnp.sort(flat_e * ASG_STRIDE + asg)
    sorted_e = skey // ASG_STRIDE
    sorted_asg = skey % ASG_STRIDE
    counts = jnp.sum((flat_e[:, None] == experts[None, :]).astype(i32), axis=0)
    start = jnp.cumsum(counts) - counts
    pad_counts = (counts + MOE_TM - 1) // MOE_TM * MOE_TM
    pad_end = jnp.cumsum(pad_counts)
    pad_start = pad_end - pad_counts
    n_used = pad_end[-1] // MOE_TM
    pos = asg + _lut(sorted_e, pad_start - start)
    _, dest = lax.sort((sorted_asg, pos), num_keys=1)
    blk = jnp.arange(MOE_BLOCKS, dtype=i32)
    block_e = jnp.sum((blk[:, None] * MOE_TM >= pad_end[None, :]).astype(i32), axis=1)
    last_e = jnp.max(jnp.where(counts > 0, experts, 0))
    block_e = jnp.where(blk < n_used, jnp.minimum(block_e, N_EXPERTS - 1), last_e)
    first = jnp.concatenate([jnp.ones((1,), i32), (block_e[1:] != block_e[:-1]).astype(i32)])
    slot = (jnp.cumsum(first) - 1) % 2
    later = (experts[None, :] > experts[:, None]) & (counts[None, :] > 0)
    nxt_of = jnp.min(jnp.where(later, experts[None, :], N_EXPERTS), axis=1)
    nxt = _lut(block_e, jnp.where(nxt_of < N_EXPERTS, nxt_of, -1))
    row = jnp.arange(MOE_ROWS, dtype=i32)
    e_row = jnp.repeat(block_e, MOE_TM)
    off = row - _lut(e_row, pad_start)
    valid = (off < _lut(e_row, counts)) & (row < n_used * MOE_TM)
    src = jnp.clip(_lut(e_row, start) + off, 0, N_ASG - 1)
    row_tok = jnp.where(valid, sorted_asg[src] // TOP_K, 0)
    plan = (block_e, n_used.reshape(1).astype(i32), first, slot.astype(i32), nxt.astype(i32), row_tok)
    return gates, dest.reshape(N_TOK, TOP_K), plan


def _moe(layer, h_packed, logits, modt, w_gu, b_gu, w_down, b_down):
    gates, dest, plan = _moe_route(logits)
    y_rows = _moe_experts(layer, plan, h_packed, w_gu, b_gu, w_down, b_down)
    return (modt, gates) + tuple(y_rows[dest[:, k]] for k in range(TOP_K))


def kernel(x_prompt, x_sample, cache_nat_k, cache_nat_v, cache_swa_k, cache_swa_v, state_ssm, c, c_ctx,
           norm1_g, norm2_g, mod_w, mod_b,
           ab_w_in, ab_w_out, ssm_lam_re, ssm_lam_im, ssm_log_dt, ssm_b_re, ssm_b_im, ssm_c_re, ssm_c_im,
           ssm_d, ssm_glu_w, ssm_glu_b, nat_qn, nat_kn, nat_rpb,
           swa_w_qkv, swa_w_o, swa_qn, swa_kn, swa_sink,
           moe_router_w, moe_router_b, moe_w_gu, moe_b_gu, moe_w_down, moe_b_down):
    x = jnp.concatenate([x_prompt.reshape(N_CTX_TOK, D_MODEL), x_sample.reshape(N_LAT_TOK, D_MODEL)], axis=0)
    cond = jnp.zeros((SUBLANES, D_MODEL), F32).at[0].set(c_ctx).at[1:1 + DEC_BATCH].set(c)
    mod = _modulation(cond, mod_w, mod_b)
    tile_row = np.concatenate([np.zeros(N_CTX_TILES, np.int32),
                               1 + np.arange(N_TILES - N_CTX_TILES, dtype=np.int32) // LAT_TILES_PER_SEQ])
    head_gain = lambda gn: jnp.tile(gn, 2).reshape(1, LANES)
    bd = np.kron(np.eye(2, dtype=np.float32), np.full((HEAD_DIM, HEAD_DIM), 1.0 / HEAD_DIM, np.float32))
    bd2 = jnp.asarray(np.concatenate([bd, bd], axis=0), BF16)
    rope_cos, rope_sin = _rope_tables()
    router_w = jnp.pad(moe_router_w, ((0, 0), (0, 0), (0, ROUTER_PAD - N_EXPERTS)))
    router_b = jnp.pad(moe_router_b, ((0, 0), (0, ROUTER_PAD - N_EXPERTS))).reshape(DEPTH, 1, ROUTER_PAD)
    nkv = C_KV_HEADS * HEAD_DIM

    ssm_out = []
    nat_caches, swa_caches = (), ()
    comb = ()
    for l in range(DEPTH):
        modt = mod[l][tile_row].reshape(N_TILES, 1, N_MOD * D_MODEL)
        g1 = norm1_g[l].reshape(1, D_MODEL)
        moe_in = (norm2_g[l].reshape(1, D_MODEL), router_w[l], router_b[l])
        i = l // 2
        if l % 2 == 0:
            outs = _even_in(x, comb, nat_caches, i, g1, modt, ab_w_in[i].astype(BF16), bd2,
                            head_gain(nat_qn[i]), head_gain(nat_kn[i]))
            if comb:
                x, outs = outs[0], outs[1:]
            u_t, q, k, v = outs[:4]
            nat_caches = tuple(outs[4:])
            a_b, w_bd, c_bd = _s5_params(ssm_lam_re[i], ssm_lam_im[i], ssm_log_dt[i], ssm_b_re[i], ssm_b_im[i],
                                         ssm_c_re[i], ssm_c_im[i])
            st = state_ssm[:, i].reshape(DEC_BATCH, 2, 2, SSM_LANES).transpose(1, 2, 0, 3)
            s0 = jnp.zeros((2, 2, N_TILES, SSM_LANES), F32)
            first = N_CTX_TILES + LAT_TILES_PER_SEQ * np.arange(DEC_BATCH)
            s0 = s0.at[0, :, first].set(st[0].transpose(1, 0, 2))
            s0 = s0.at[1, :, first + LAT_TILES_PER_SEQ - 1].set(st[1].transpose(1, 0, 2))
            y_t, fin = _s5_scan(u_t.reshape(TILE, N_TILES, A_WIDTH), a_b, w_bd, c_bd, s0)
            o = _nat_attention(q, k, v,
                               cache_nat_k[:, i].reshape(DEC_BATCH, PAST_LEN, B_WIDTH),
                               cache_nat_v[:, i].reshape(DEC_BATCH, PAST_LEN, B_WIDTH),
                               _nat_bias(nat_rpb[i]))
            x, h_packed, logits = _even_out(y_t.reshape(TILE, N_TILES * A_WIDTH), u_t, o, x, modt,
                                            ssm_d[i].reshape(1, A_WIDTH), ssm_glu_w[i].astype(BF16),
                                            ssm_glu_b[i].reshape(1, A_WIDTH), ab_w_out[i].astype(BF16), *moe_in)
            ssm_out.append(fin[:, :, :N_CTX_TILES].transpose(2, 0, 1, 3).reshape(BATCH, 2, 2, A_GROUPS, A_STATE))
        else:
            outs = _odd_in(x, comb, swa_caches, i, g1, modt, swa_w_qkv[i].astype(BF16), bd2,
                           head_gain(swa_qn[i]), head_gain(swa_kn[i]), rope_cos, rope_sin)
            if comb:
                x, outs = outs[0], outs[1:]
            q, k_rot, v = outs[:3]
            swa_caches = tuple(outs[3:])
            o = _swa_attention(q, k_rot, v,
                               cache_swa_k[:, i].reshape(DEC_BATCH, PAST_LEN, nkv),
                               cache_swa_v[:, i].reshape(DEC_BATCH, PAST_LEN, nkv), swa_sink[i])
            x, h_packed, logits = _odd_out(o, x, modt, swa_w_o[i].astype(BF16), *moe_in)
        comb = _moe(l, h_packed, logits, modt, moe_w_gu, moe_b_gu, moe_w_down, moe_b_down)

    x = _moe_combine(x, comb)
    y_prompt = x[:N_CTX_TOK].reshape(BATCH, SEQ, D_MODEL)
    y_sample = x[N_CTX_TOK:].reshape(DEC_BATCH, DEC_SEQ, D_MODEL)
    nat_shape = (BATCH, N_EVEN, SEQ, B_HEADS, HEAD_DIM)
    swa_shape = (BATCH, N_ODD, SEQ, C_KV_HEADS, HEAD_DIM)
    return (y_prompt, y_sample,
            nat_caches[0].reshape(nat_shape), nat_caches[1].reshape(nat_shape),
            swa_caches[0].reshape(swa_shape), swa_caches[1].reshape(swa_shape),
            jnp.stack(ssm_out, axis=1))
```

```python
import functools

import jax
import jax.numpy as jnp
import numpy as np
from jax import lax
from jax.experimental import pallas as pl
from jax.experimental.pallas import tpu as pltpu

F32 = jnp.float32
BF16 = jnp.bfloat16
U32 = jnp.uint32

D_MODEL = 1024
BATCH = 32
SEQ = 256
DEPTH = 4
DEC_BATCH = 2
DEC_SEQ = 1024
PAST_LEN = 256
GRID_W = 64
HEAD_DIM = 64
N_MOD = 6
N_EVEN = (DEPTH + 1) // 2
N_ODD = DEPTH // 2
A_WIDTH = 512
A_GROUP_CH = 16
A_GROUPS = 32
A_STATE = 64
B_HEADS = 8
B_WIDTH = 512
WIN_R_MAX = 8
WIN_C = 16
C_HEADS = 16
C_KV_HEADS = 2
C_GROUP = 8
WINDOW = 128
ROPE_BASE = 10000.0
N_EXPERTS = 32
TOP_K = 4
D_EXPERT = 1024
SWIGLU_LIMIT = 7.0
SWIGLU_ALPHA = 1.702
EPS = 1e-6
NEG_INF = -1e30

LANES = 128
SUBLANES = 8
TILE = 256
N_CTX_TOK = BATCH * SEQ
N_LAT_TOK = DEC_BATCH * DEC_SEQ
N_TOK = N_CTX_TOK + N_LAT_TOK
N_TILES = N_TOK // TILE
N_CTX_TILES = N_CTX_TOK // TILE
LAT_TILES_PER_SEQ = DEC_SEQ // TILE
SSM_LANES = A_GROUPS * A_STATE
SSM_SLABS = A_WIDTH // LANES
SSM_SLAB_STATES = SSM_LANES // SSM_SLABS
SCAN_ROWS = 8
SCAN_GROUPS = N_TILES // SCAN_ROWS
SCAN_CHUNK = 64
SCAN_LANE_PARTS = 2
MOE_TM = 256
MOE_STEP_BLOCKS = 4
N_ASG = N_TOK * TOP_K
ASG_STRIDE = 1 << 16
MOE_BLOCKS = N_ASG // MOE_TM + N_EXPERTS
MOE_ROWS = MOE_BLOCKS * MOE_TM
ROUTER_PAD = LANES
VMEM_LIMIT = 56 * 1024 * 1024
MOE_VMEM_LIMIT = 60 * 1024 * 1024
ATT_SCALE = HEAD_DIM ** -0.5


def _cparams(*sem):
    return pltpu.CompilerParams(dimension_semantics=sem, vmem_limit_bytes=VMEM_LIMIT)


def _dot(a, b):
    return jnp.dot(a, b, preferred_element_type=F32)


def _dot_nt(a, b):
    return lax.dot_general(a, b, (((1,), (1,)), ((), ())), preferred_element_type=F32)


def _split(a):
    hi = a.astype(BF16)
    lo = (a - hi.astype(F32)).astype(BF16)
    return hi, lo


def _dot3(a, b):
    a_hi, a_lo = _split(a)
    b_hi, b_lo = _split(b)
    return _dot(a_hi, b_hi) + (_dot(a_hi, b_lo) + _dot(a_lo, b_hi))


def _silu(x):
    return x * jax.nn.sigmoid(x)


def _norm_mod(x, g, shift, scale):
    y = x * lax.rsqrt(jnp.mean(x * x, axis=-1, keepdims=True) + EPS)
    return (y * g) * (1.0 + scale) + shift


def _head_rms(x, bd2, gain):
    sq_hi, sq_lo = _split(x * x)
    ms = _dot(jnp.concatenate([sq_hi, sq_lo], axis=1), bd2)
    return x * lax.rsqrt(ms + EPS) * gain


def _combined(x_ref, comb_refs):
    x = x_ref[...]
    if not comb_refs:
        return x
    gmod_ref, gates_ref = comb_refs[0], comb_refs[1]
    gates = gates_ref[...]
    acc = gates[:, 0:1] * comb_refs[2][...]
    for k in range(1, TOP_K):
        acc = acc + gates[:, k:k + 1] * comb_refs[2 + k][...]
    return x + gmod_ref[0] * acc


def _comb_specs():
    return [_mod_spec(5), _tile_spec(TOP_K)] + [_tile_spec(D_MODEL)] * TOP_K


N_COMB = 2 + TOP_K


def _lane_half(shape, half):
    lane = lax.broadcasted_iota(jnp.int32, shape, len(shape) - 1)
    return (lane < HEAD_DIM) if half == 0 else (lane >= HEAD_DIM)


def _mod_kernel(cond_ref, w_ref, b_ref, o_ref):
    o_ref[0] = _dot3(_silu(cond_ref[...]), w_ref[0]) + b_ref[0]


def _modulation(cond, mod_w, mod_b):
    nc = N_MOD
    return pl.pallas_call(
        _mod_kernel,
        out_shape=jax.ShapeDtypeStruct((DEPTH, SUBLANES, N_MOD * D_MODEL), F32),
        grid=(DEPTH, nc),
        in_specs=[
            pl.BlockSpec((SUBLANES, D_MODEL), lambda l, c: (0, 0)),
            pl.BlockSpec((1, D_MODEL, D_MODEL), lambda l, c: (l, 0, c)),
            pl.BlockSpec((1, 1, D_MODEL), lambda l, c: (l, 0, c)),
        ],
        out_specs=pl.BlockSpec((1, SUBLANES, D_MODEL), lambda l, c: (l, 0, c)),
        compiler_params=_cparams("arbitrary", "arbitrary"),
        name="modulation",
    )(cond, mod_w, mod_b.reshape(DEPTH, 1, N_MOD * D_MODEL))


def _mod_spec(col):
    return pl.BlockSpec((1, 1, D_MODEL), lambda i: (i, 0, col))


def _tile_spec(width):
    return pl.BlockSpec((TILE, width), lambda i: (i, 0))


def _ctx_tile_spec(width):
    return pl.BlockSpec((TILE, width), lambda i: (jnp.minimum(i, N_CTX_TILES - 1), 0))


def _full_spec(shape):
    nd = len(shape)
    return pl.BlockSpec(shape, lambda i: (0,) * nd)


def _cache_store(slot, n_slots, ref, value):
    @pl.when(pl.program_id(0) < N_CTX_TILES)
    def _():
        if slot == 0:
            ref[0, 0] = value
            for other in range(1, n_slots):
                ref[0, other] = jnp.zeros_like(value)
        else:
            ref[0, 0] = value


def _cache_specs(slot, n_slots, width):
    shape = jax.ShapeDtypeStruct((BATCH, n_slots, SEQ, width), F32)
    seq = lambda i: jnp.minimum(i, N_CTX_TILES - 1)
    if slot == 0:
        return shape, pl.BlockSpec((1, n_slots, SEQ, width), lambda i: (seq(i), 0, 0, 0))
    return shape, pl.BlockSpec((1, 1, SEQ, width), lambda i: (seq(i), slot, 0, 0))


def _even_in_kernel(n_comb, slot, n_prev, x_ref, *refs):
    comb, refs = refs[:n_comb], refs[n_comb:]
    g_ref, sh_ref, sc_ref, w_ref, bd_ref, qn_ref, kn_ref = refs[:7]
    outs = refs[7 + n_prev:]
    x = _combined(x_ref, comb)
    if n_comb:
        outs[0][...] = x
        outs = outs[1:]
    u_ref, q_ref, k_ref, v_ref, kc_ref, vc_ref = outs
    h = _norm_mod(x, g_ref[...], sh_ref[0], sc_ref[0]).astype(BF16)
    bd2 = bd_ref[...]
    z = _dot(h, w_ref[...])
    u_ref[...] = z[:, 0:A_WIDTH]
    ks = []
    for s in range(B_WIDTH // LANES):
        lo = A_WIDTH + s * LANES
        q = _head_rms(z[:, lo:lo + LANES], bd2, qn_ref[...])
        q_ref[:, s * LANES:(s + 1) * LANES] = (q * ATT_SCALE).astype(BF16)
        lo = A_WIDTH + B_WIDTH + s * LANES
        ks.append(_head_rms(z[:, lo:lo + LANES], bd2, kn_ref[...]))
    k = jnp.concatenate(ks, axis=1)
    v = z[:, A_WIDTH + 2 * B_WIDTH:]
    k_ref[...] = k
    v_ref[...] = v
    _cache_store(slot, N_EVEN, kc_ref, k)
    _cache_store(slot, N_EVEN, vc_ref, v)


def _even_in(x, comb, prev_caches, slot, g, modt, w, bd2, qn, kn):
    n_out = A_WIDTH + 3 * B_WIDTH
    cache_shape, cache_spec = _cache_specs(slot, N_EVEN, B_WIDTH)
    out_shape = [
        jax.ShapeDtypeStruct((TILE, N_TILES * A_WIDTH), F32),
        jax.ShapeDtypeStruct((N_TOK, B_WIDTH), BF16),
        jax.ShapeDtypeStruct((N_TOK, B_WIDTH), F32),
        jax.ShapeDtypeStruct((N_TOK, B_WIDTH), F32),
        cache_shape, cache_shape,
    ]
    out_specs = [pl.BlockSpec((TILE, A_WIDTH), lambda i: (0, i)),
                 _tile_spec(B_WIDTH), _tile_spec(B_WIDTH), _tile_spec(B_WIDTH), cache_spec, cache_spec]
    if comb:
        out_shape.insert(0, jax.ShapeDtypeStruct((N_TOK, D_MODEL), F32))
        out_specs.insert(0, _tile_spec(D_MODEL))
    n_in = 1 + len(comb) + 7
    aliases = {n_in + j: len(out_shape) - 2 + j for j in range(len(prev_caches))}
    return pl.pallas_call(
        functools.partial(_even_in_kernel, len(comb), slot, len(prev_caches)),
        out_shape=tuple(out_shape),
        grid=(N_TILES,),
        in_specs=[_tile_spec(D_MODEL)] + (_comb_specs() if comb else []) + [
            _full_spec((1, D_MODEL)), _mod_spec(0), _mod_spec(1),
            _full_spec((D_MODEL, n_out)), _full_spec((2 * LANES, LANES)),
            _full_spec((1, LANES)), _full_spec((1, LANES)),
        ] + [pl.BlockSpec(memory_space=pl.ANY)] * len(prev_caches),
        out_specs=tuple(out_specs),
        input_output_aliases=aliases,
        compiler_params=_cparams("arbitrary"),
        name="even_in",
    )(x, *comb, g, modt, modt, w, bd2, qn, kn, *prev_caches)


def _cmul(ar, ai, br, bi):
    return ar * br - ai * bi, ar * bi + ai * br


def _s5_kernel(u_ref, a_ref, w_ref, c_ref, s0_ref, y_ref, fin_ref, xr, xi, st_r, st_i):
    grp = pl.program_id(0)
    drn = pl.program_id(1)
    n_chunks = TILE // SCAN_CHUNK
    rows = SCAN_CHUNK * SCAN_ROWS
    part = SSM_LANES // SCAN_LANE_PARTS

    def run(store):
        @pl.loop(0, n_chunks)
        def _(c):
            cc = jnp.where(drn == 0, c, n_chunks - 1 - c)
            t0 = pl.multiple_of(cc * SCAN_CHUNK, SCAN_CHUNK)
            uu = u_ref[pl.ds(t0, SCAN_CHUNK), :, :]
            for s in range(SSM_SLABS):
                us = uu[:, :, s * LANES:(s + 1) * LANES].reshape(rows, LANES).astype(BF16)
                cols = slice(s * SSM_SLAB_STATES, (s + 1) * SSM_SLAB_STATES)
                xr[:, cols] = _dot(us, w_ref[0, 0, s])
                xi[:, cols] = _dot(us, w_ref[0, 1, s])
            for p in range(SCAN_LANE_PARTS):
                ls = slice(p * part, (p + 1) * part)
                ar = a_ref[0, 0, :, ls]
                ai = a_ref[0, 1, :, ls]

                def step(j, carry):
                    sr, si = carry
                    tt = jnp.where(drn == 0, j, SCAN_CHUNK - 1 - j)
                    r0 = pl.multiple_of(tt * SCAN_ROWS, SCAN_ROWS)
                    nr = ar * sr - ai * si + xr[pl.ds(r0, SCAN_ROWS), ls]
                    ni = ar * si + ai * sr + xi[pl.ds(r0, SCAN_ROWS), ls]
                    if store:
                        xr[pl.ds(r0, SCAN_ROWS), ls] = nr
                        xi[pl.ds(r0, SCAN_ROWS), ls] = ni
                    return nr, ni

                sr, si = lax.fori_loop(0, SCAN_CHUNK, step, (st_r[:, ls], st_i[:, ls]), unroll=4)
                st_r[:, ls] = sr
                st_i[:, ls] = si
            if store:
                for s in range(SSM_SLABS):
                    cols = slice(s * SSM_SLAB_STATES, (s + 1) * SSM_SLAB_STATES)
                    ys = (_dot(xr[:, cols].astype(BF16), c_ref[0, 0, s])
                          - _dot(xi[:, cols].astype(BF16), c_ref[0, 1, s]))
                    ys = ys.reshape(SCAN_CHUNK, SCAN_ROWS, LANES)
                    lanes = slice(s * LANES, (s + 1) * LANES)

                    @pl.when(drn == 0)
                    def _():
                        y_ref[pl.ds(t0, SCAN_CHUNK), :, lanes] = ys

                    @pl.when(drn != 0)
                    def _():
                        y_ref[pl.ds(t0, SCAN_CHUNK), :, lanes] += ys

    st_r[...] = s0_ref[0, 0]
    st_i[...] = s0_ref[0, 1]

    @pl.when(grp == SCAN_GROUPS - 1)
    def _():
        st_r[...] = jnp.zeros_like(st_r)
        st_i[...] = jnp.zeros_like(st_i)
        run(False)
        pr, pi = a_ref[0, 0], a_ref[0, 1]
        for _ in range(8):
            pr, pi = _cmul(pr, pi, pr, pi)
        fr, fi = st_r[...], st_i[...]
        s0r, s0i = s0_ref[0, 0], s0_ref[0, 1]
        row = lax.broadcasted_iota(jnp.int32, (SCAN_ROWS, SSM_LANES), 0)
        quarter = row % LAT_TILES_PER_SEQ
        fwd = drn == 0
        keep = quarter != jnp.where(fwd, 0, LAT_TILES_PER_SEQ - 1)
        ir, ii = s0r, s0i
        for _ in range(LAT_TILES_PER_SEQ - 1):
            nr, ni = _cmul(pr, pi, ir, ii)
            nr, ni = nr + fr, ni + fi
            nr = jnp.where(fwd, pltpu.roll(nr, 1, 0), pltpu.roll(nr, SCAN_ROWS - 1, 0))
            ni = jnp.where(fwd, pltpu.roll(ni, 1, 0), pltpu.roll(ni, SCAN_ROWS - 1, 0))
            ir = s0r + jnp.where(keep, nr, 0.0)
            ii = s0i + jnp.where(keep, ni, 0.0)
        st_r[...] = ir
        st_i[...] = ii

    run(True)
    fin_ref[0, 0] = st_r[...]
    fin_ref[0, 1] = st_i[...]


def _s5_scan(u_tb, a_b, w_bd, c_bd, s0):
    rows = SCAN_CHUNK * SCAN_ROWS
    return pl.pallas_call(
        _s5_kernel,
        out_shape=(
            jax.ShapeDtypeStruct((TILE, N_TILES, A_WIDTH), F32),
            jax.ShapeDtypeStruct((2, 2, N_TILES, SSM_LANES), F32),
        ),
        grid=(SCAN_GROUPS, 2),
        in_specs=[
            pl.BlockSpec((TILE, SCAN_ROWS, A_WIDTH), lambda g, d: (0, g, 0)),
            pl.BlockSpec((1, 2, SCAN_ROWS, SSM_LANES), lambda g, d: (d, 0, 0, 0)),
            pl.BlockSpec((1, 2, SSM_SLABS, LANES, SSM_SLAB_STATES), lambda g, d: (d, 0, 0, 0, 0)),
            pl.BlockSpec((1, 2, SSM_SLABS, SSM_SLAB_STATES, LANES), lambda g, d: (d, 0, 0, 0, 0)),
            pl.BlockSpec((1, 2, SCAN_ROWS, SSM_LANES), lambda g, d: (d, 0, g, 0)),
        ],
        out_specs=(
            pl.BlockSpec((TILE, SCAN_ROWS, A_WIDTH), lambda g, d: (0, g, 0)),
            pl.BlockSpec((1, 2, SCAN_ROWS, SSM_LANES), lambda g, d: (d, 0, g, 0)),
        ),
        scratch_shapes=[
            pltpu.VMEM((rows, SSM_LANES), F32), pltpu.VMEM((rows, SSM_LANES), F32),
            pltpu.VMEM((SCAN_ROWS, SSM_LANES), F32), pltpu.VMEM((SCAN_ROWS, SSM_LANES), F32),
        ],
        compiler_params=_cparams("arbitrary", "arbitrary"),
        name="s5_scan",
    )(u_tb, a_b, w_bd, c_bd, s0)


def _s5_params(lam_re, lam_im, log_dt, b_re, b_im, c_re, c_im):
    dt = jnp.exp(log_dt)[..., None]
    mag = jnp.exp(lam_re * dt)
    ab_re, ab_im = mag * jnp.cos(lam_im * dt), mag * jnp.sin(lam_im * dt)
    den = lam_re * lam_re + lam_im * lam_im
    nr, ni = ab_re - 1.0, ab_im
    f_re = (nr * lam_re + ni * lam_im) / den
    f_im = (ni * lam_re - nr * lam_im) / den
    bb_re = f_re[..., None] * b_re - f_im[..., None] * b_im
    bb_im = f_re[..., None] * b_im + f_im[..., None] * b_re
    a_b = jnp.stack([ab_re, ab_im], axis=1).reshape(2, 2, 1, SSM_LANES)
    a_b = jnp.broadcast_to(a_b, (2, 2, SCAN_ROWS, SSM_LANES))
    gps = A_GROUPS // SSM_SLABS
    eye = jnp.eye(gps, dtype=F32)

    def in_bd(bb):
        bb = bb.reshape(2, SSM_SLABS, gps, A_STATE, A_GROUP_CH)
        m = jnp.einsum('dsgph,gk->dsghkp', bb, eye)
        return m.reshape(2, SSM_SLABS, LANES, SSM_SLAB_STATES)

    def out_bd(cc):
        cc = cc.reshape(2, SSM_SLABS, gps, A_GROUP_CH, A_STATE)
        m = jnp.einsum('dsghp,gk->dsgpkh', cc, eye)
        return m.reshape(2, SSM_SLABS, SSM_SLAB_STATES, LANES)

    w_bd = jnp.stack([in_bd(bb_re), in_bd(bb_im)], axis=1).astype(BF16)
    c_bd = jnp.stack([out_bd(c_re), out_bd(c_im)], axis=1).astype(BF16)
    return a_b, w_bd, c_bd


def _softmax_pv(scores, values, sink=None):
    m = functools.reduce(jnp.maximum, [jnp.max(s, axis=-1, keepdims=True) for s in scores])
    if sink is not None:
        m = jnp.maximum(m, sink)
    den = None
    acc = None
    for s, v in zip(scores, values):
        e = jnp.exp(s - m)
        d = jnp.sum(e, axis=-1, keepdims=True)
        o = _dot(e.astype(BF16), v)
        den = d if den is None else den + d
        acc = o if acc is None else acc + o
    if sink is not None:
        den = den + jnp.exp(sink - m)
    return acc / den


def _nat_ctx_kernel(q_ref, k_ref, v_ref, o_ref):
    @pl.when(pl.program_id(0) >= N_CTX_TILES)
    def _():
        o_ref[...] = jnp.zeros_like(o_ref)

    @pl.when(pl.program_id(0) < N_CTX_TILES)
    def _():
        for p in range(B_WIDTH // LANES):
            ls = slice(p * LANES, (p + 1) * LANES)
            qs = q_ref[:, ls]
            ks = k_ref[:, ls].astype(BF16)
            vs = v_ref[:, ls].astype(BF16)
            out = None
            for half in range(2):
                msk = _lane_half(qs.shape, half)
                qm = jnp.where(msk, qs, jnp.zeros_like(qs))
                o = _softmax_pv([_dot_nt(qm, ks)], [vs])
                out = o if out is None else jnp.where(msk, o, out)
            o_ref[:, ls] = out.astype(BF16)


def _nat_lat_kernel(o_all_ref, q_ref, k_ref, v_ref, kc_ref, vc_ref, bias_ref, o_ref):
    del o_all_ref
    r = pl.program_id(0)
    wr = WIN_R_MAX
    rs = jnp.clip(r - wr // 2, 0, DEC_SEQ // GRID_W - wr)
    k0 = pl.multiple_of(rs * GRID_W, GRID_W)
    n_nb = wr * GRID_W
    for p in range(B_WIDTH // LANES):
        ls = slice(p * LANES, (p + 1) * LANES)
        for b in range(DEC_BATCH):
            qs = q_ref[b, :, ls]
            ks = k_ref[b, pl.ds(k0, n_nb), ls].astype(BF16)
            vs = v_ref[b, pl.ds(k0, n_nb), ls].astype(BF16)
            kc = kc_ref[b, :, ls].astype(BF16)
            vc = vc_ref[b, :, ls].astype(BF16)
            out = None
            for half in range(2):
                msk = _lane_half(qs.shape, half)
                qm = jnp.where(msk, qs, jnp.zeros_like(qs))
                s_nb = _dot_nt(qm, ks) + bias_ref[2 * p + half, 0]
                s_ctx = _dot_nt(qm, kc)
                o = _softmax_pv([s_nb, s_ctx], [vs, vc])
                out = o if out is None else jnp.where(msk, o, out)
            o_ref[b, :, ls] = out.astype(BF16)


def _nat_attention(q, k, v, kc, vc, bias):
    o_ctx = pl.pallas_call(
        _nat_ctx_kernel,
        out_shape=jax.ShapeDtypeStruct((N_TOK, B_WIDTH), BF16),
        grid=(N_TILES,),
        in_specs=[_ctx_tile_spec(B_WIDTH)] * 3,
        out_specs=_tile_spec(B_WIDTH),
        compiler_params=_cparams("arbitrary"),
        name="nat_ctx",
    )(q, k, v)
    rows = DEC_SEQ // GRID_W
    n_seq = N_TOK // DEC_SEQ
    lat = N_CTX_TOK // DEC_SEQ // DEC_BATCH
    as_seq = lambda a: a.reshape(n_seq, DEC_SEQ, B_WIDTH)
    row_spec = pl.BlockSpec((DEC_BATCH, GRID_W, B_WIDTH), lambda r: (lat, r, 0))
    seq_spec = pl.BlockSpec((DEC_BATCH, DEC_SEQ, B_WIDTH), lambda r: (lat, 0, 0))
    ctx_spec = pl.BlockSpec((DEC_BATCH, PAST_LEN, B_WIDTH), lambda r: (0, 0, 0))

    def bias_idx(r):
        return (0, r - jnp.clip(r - WIN_R_MAX // 2, 0, rows - WIN_R_MAX), 0, 0)

    o = pl.pallas_call(
        _nat_lat_kernel,
        out_shape=jax.ShapeDtypeStruct((n_seq, DEC_SEQ, B_WIDTH), BF16),
        grid=(rows,),
        in_specs=[
            pl.BlockSpec(memory_space=pl.ANY), row_spec, seq_spec, seq_spec, ctx_spec, ctx_spec,
            pl.BlockSpec((B_HEADS, 1, GRID_W, WIN_R_MAX * GRID_W), bias_idx),
        ],
        out_specs=row_spec,
        input_output_aliases={0: 0},
        compiler_params=_cparams("arbitrary"),
        name="nat_lat",
    )(as_seq(o_ctx), as_seq(q), as_seq(k), as_seq(v), kc, vc, bias)
    return o.reshape(N_TOK, B_WIDTH)


def _nat_bias(rpb):
    qc = np.arange(GRID_W)
    kc = np.arange(GRID_W)
    cs = np.clip(qc - WIN_C // 2, 0, GRID_W - WIN_C)
    ok = (kc[None, :] >= cs[:, None]) & (kc[None, :] < cs[:, None] + WIN_C)
    dc = np.clip(kc[None, :] - qc[:, None] + (WIN_C - 1), 0, 2 * WIN_C - 2)
    pick = (dc[:, :, None] == np.arange(2 * WIN_C - 1)).astype(np.float32)
    t = jnp.einsum('hrc,qkc->hrqk', rpb.astype(F32), pick, precision=lax.Precision.HIGHEST)
    t = jnp.where(ok[None, None], t, NEG_INF)
    per_d = [t[:, WIN_R_MAX - 1 - d:2 * WIN_R_MAX - 1 - d].transpose(0, 2, 1, 3)
             .reshape(B_HEADS, GRID_W, WIN_R_MAX * GRID_W) for d in range(WIN_R_MAX)]
    return jnp.stack(per_d, axis=1)


def _gqa_heads(q_ref, sink_ref, k_segs, v_segs, masks, o_ref):
    k_rot = [pltpu.roll(k.astype(F32), HEAD_DIM, 1).astype(BF16) for k in k_segs]
    v_rot = [pltpu.roll(v.astype(F32), HEAD_DIM, 1).astype(BF16) for v in v_segs]
    for p in range(C_HEADS // 2):
        ls = slice(p * LANES, (p + 1) * LANES)
        qs = q_ref[:, ls]
        out = None
        for half in range(2):
            head = 2 * p + half
            kv = head // C_GROUP
            ks = k_segs if kv == half else k_rot
            vs = v_segs if kv == half else v_rot
            msk = _lane_half(qs.shape, half)
            qm = jnp.where(msk, qs, jnp.zeros_like(qs))
            scores = []
            for kseg, m in zip(ks, masks):
                s = _dot_nt(qm, kseg)
                scores.append(s if m is None else jnp.where(m, s, NEG_INF))
            o = _softmax_pv(scores, vs, sink=sink_ref[head])
            out = o if out is None else jnp.where(msk, o, out)
        o_ref[:, ls] = out.astype(BF16)


def _swa_ctx_kernel(sink_ref, q_ref, k_ref, v_ref, o_ref):
    @pl.when(pl.program_id(0) >= N_CTX_TILES)
    def _():
        o_ref[...] = jnp.zeros_like(o_ref)

    @pl.when(pl.program_id(0) < N_CTX_TILES)
    def _():
        _gqa_heads(q_ref, sink_ref, [k_ref[...].astype(BF16)], [v_ref[...].astype(BF16)], [None], o_ref)


def _swa_lat_kernel(sink_ref, o_all_ref, q_ref, k_ref, v_ref, kc_ref, vc_ref, o_ref):
    del o_all_ref
    n = pl.program_id(1)
    n_win = 3 * WINDOW
    start = jnp.clip((n - 1) * WINDOW, 0, DEC_SEQ - n_win)
    k0 = pl.multiple_of(start, WINDOW)
    qpos = n * WINDOW + lax.broadcasted_iota(jnp.int32, (WINDOW, n_win), 0)
    kpos = start + lax.broadcasted_iota(jnp.int32, (WINDOW, n_win), 1)
    ok = jnp.abs(qpos - kpos) <= WINDOW
    _gqa_heads(q_ref, sink_ref,
               [k_ref[pl.ds(k0, n_win), :].astype(BF16), kc_ref[0].astype(BF16)],
               [v_ref[pl.ds(k0, n_win), :].astype(BF16), vc_ref[0].astype(BF16)],
               [ok, None], o_ref)


def _swa_attention(q, k, v, kc, vc, sink):
    nq, nkv = C_HEADS * HEAD_DIM, C_KV_HEADS * HEAD_DIM
    smem = pl.BlockSpec(memory_space=pltpu.SMEM)
    o_ctx = pl.pallas_call(
        _swa_ctx_kernel,
        out_shape=jax.ShapeDtypeStruct((N_TOK, nq), BF16),
        grid=(N_TILES,),
        in_specs=[smem, _ctx_tile_spec(nq), _ctx_tile_spec(nkv), _ctx_tile_spec(nkv)],
        out_specs=_tile_spec(nq),
        compiler_params=_cparams("arbitrary"),
        name="swa_ctx",
    )(sink, q, k, v)
    nb = DEC_SEQ // WINDOW
    ctx_seqs = N_CTX_TOK // DEC_SEQ
    lat_row = lambda b, n: (N_CTX_TOK // WINDOW + b * nb + n, 0)
    return pl.pallas_call(
        _swa_lat_kernel,
        out_shape=jax.ShapeDtypeStruct((N_TOK, nq), BF16),
        grid=(DEC_BATCH, nb),
        in_specs=[
            smem,
            pl.BlockSpec(memory_space=pl.ANY),
            pl.BlockSpec((WINDOW, nq), lat_row),
            pl.BlockSpec((DEC_SEQ, nkv), lambda b, n: (ctx_seqs + b, 0)),
            pl.BlockSpec((DEC_SEQ, nkv), lambda b, n: (ctx_seqs + b, 0)),
            pl.BlockSpec((1, PAST_LEN, nkv), lambda b, n: (b, 0, 0)),
            pl.BlockSpec((1, PAST_LEN, nkv), lambda b, n: (b, 0, 0)),
        ],
        out_specs=pl.BlockSpec((WINDOW, nq), lat_row),
        input_output_aliases={1: 0},
        compiler_params=_cparams("arbitrary", "arbitrary"),
        name="swa_lat",
    )(sink, o_ctx, q, k, v, kc, vc)


def _moe_input(x, g_ref, sh_ref, sc_ref, rw_ref, rb_ref, h_ref, lg_ref):
    h = _norm_mod(x, g_ref[...], sh_ref[0], sc_ref[0])
    h_ref[...] = _pack_bf16_pair(h[:, 0:D_MODEL // 2], h[:, D_MODEL // 2:])
    lg_ref[...] = _dot3(h, rw_ref[...]) + rb_ref[...]


def _moe_input_specs():
    return [_full_spec((1, D_MODEL)), _mod_spec(3), _mod_spec(4),
            _full_spec((D_MODEL, ROUTER_PAD)), _full_spec((1, ROUTER_PAD))]


def _mixer_out_shapes():
    shapes = (jax.ShapeDtypeStruct((N_TOK, D_MODEL), F32),
              jax.ShapeDtypeStruct((N_TOK, D_MODEL // 2), U32),
              jax.ShapeDtypeStruct((N_TOK, ROUTER_PAD), F32))
    specs = (_tile_spec(D_MODEL), _tile_spec(D_MODEL // 2), _tile_spec(ROUTER_PAD))
    return shapes, specs


def _even_out_kernel(y_ref, u_ref, o_ref, x_ref, gate_ref, d_ref, gw_ref, gb_ref, w_ref,
                     g2_ref, sh_ref, sc_ref, rw_ref, rb_ref, xo_ref, h_ref, lg_ref):
    yy = y_ref[...] + d_ref[...] * u_ref[...]
    g = jax.nn.gelu(yy)
    a = g * jax.nn.sigmoid(_dot(g.astype(BF16), gw_ref[...]) + gb_ref[...])
    mix = _dot(a.astype(BF16), w_ref[0:A_WIDTH, :]) + _dot(o_ref[...], w_ref[A_WIDTH:, :])
    x = x_ref[...] + gate_ref[0] * mix
    xo_ref[...] = x
    _moe_input(x, g2_ref, sh_ref, sc_ref, rw_ref, rb_ref, h_ref, lg_ref)


def _even_out(y_t, u_t, o, x, modt, d_skip, glu_w, glu_b, w_out, g2, rw, rb):
    tm_spec = pl.BlockSpec((TILE, A_WIDTH), lambda i: (0, i))
    out_shape, out_specs = _mixer_out_shapes()
    return pl.pallas_call(
        _even_out_kernel,
        out_shape=out_shape,
        grid=(N_TILES,),
        in_specs=[
            tm_spec, tm_spec, _tile_spec(B_WIDTH), _tile_spec(D_MODEL), _mod_spec(2),
            _full_spec((1, A_WIDTH)), _full_spec((A_WIDTH, A_WIDTH)), _full_spec((1, A_WIDTH)),
            _full_spec((A_WIDTH + B_WIDTH, D_MODEL)),
        ] + _moe_input_specs(),
        out_specs=out_specs,
        compiler_params=_cparams("arbitrary"),
        name="even_out",
    )(y_t, u_t, o, x, modt, d_skip, glu_w, glu_b, w_out, g2, modt, modt, rw, rb)


def _rope(x, cos, sin):
    lane = lax.broadcasted_iota(jnp.int32, x.shape, 1)
    first = (lane % (HEAD_DIM // 2)) < (HEAD_DIM // 4)
    partner = jnp.where(first, pltpu.roll(x, LANES - HEAD_DIM // 4, 1), pltpu.roll(x, HEAD_DIM // 4, 1))
    return x * cos + partner * sin


def _odd_in_kernel(n_comb, slot, n_prev, x_ref, *refs):
    comb, refs = refs[:n_comb], refs[n_comb:]
    g_ref, sh_ref, sc_ref, w_ref, bd_ref, qn_ref, kn_ref, cos_ref, sin_ref = refs[:9]
    outs = refs[9 + n_prev:]
    x = _combined(x_ref, comb)
    if n_comb:
        outs[0][...] = x
        outs = outs[1:]
    q_ref, kr_ref, v_ref, kcache_ref, vcache_ref = outs
    i = pl.program_id(0)
    h = _norm_mod(x, g_ref[...], sh_ref[0], sc_ref[0]).astype(BF16)
    bd2 = bd_ref[...]
    lat = i >= N_CTX_TILES
    cos = jnp.where(lat, cos_ref[...], 1.0)
    sin = jnp.where(lat, sin_ref[...], 0.0)
    nq = C_HEADS * HEAD_DIM
    z = _dot(h, w_ref[...])
    for s in range(nq // LANES):
        q = _head_rms(z[:, s * LANES:(s + 1) * LANES], bd2, qn_ref[...])
        q_ref[:, s * LANES:(s + 1) * LANES] = (_rope(q, cos, sin) * ATT_SCALE).astype(BF16)
    k = _head_rms(z[:, nq:nq + LANES], bd2, kn_ref[...])
    v = z[:, nq + LANES:]
    kr_ref[...] = _rope(k, cos, sin)
    v_ref[...] = v
    _cache_store(slot, N_ODD, kcache_ref, k)
    _cache_store(slot, N_ODD, vcache_ref, v)


def _odd_in(x, comb, prev_caches, slot, g, modt, w, bd2, qn, kn, cos, sin):
    nq, nkv = C_HEADS * HEAD_DIM, C_KV_HEADS * HEAD_DIM
    lat_spec = pl.BlockSpec((TILE, LANES), lambda i: (jnp.maximum(i - N_CTX_TILES, 0) % LAT_TILES_PER_SEQ, 0))
    cache_shape, cache_spec = _cache_specs(slot, N_ODD, nkv)
    out_shape = [
        jax.ShapeDtypeStruct((N_TOK, nq), BF16),
        jax.ShapeDtypeStruct((N_TOK, nkv), F32),
        jax.ShapeDtypeStruct((N_TOK, nkv), F32),
        cache_shape, cache_shape,
    ]
    out_specs = [_tile_spec(nq), _tile_spec(nkv), _tile_spec(nkv), cache_spec, cache_spec]
    if comb:
        out_shape.insert(0, jax.ShapeDtypeStruct((N_TOK, D_MODEL), F32))
        out_specs.insert(0, _tile_spec(D_MODEL))
    n_in = 1 + len(comb) + 9
    aliases = {n_in + j: len(out_shape) - 2 + j for j in range(len(prev_caches))}
    return pl.pallas_call(
        functools.partial(_odd_in_kernel, len(comb), slot, len(prev_caches)),
        out_shape=tuple(out_shape),
        grid=(N_TILES,),
        in_specs=[_tile_spec(D_MODEL)] + (_comb_specs() if comb else []) + [
            _full_spec((1, D_MODEL)), _mod_spec(0), _mod_spec(1),
            _full_spec((D_MODEL, nq + 2 * nkv)), _full_spec((2 * LANES, LANES)),
            _full_spec((1, LANES)), _full_spec((1, LANES)), lat_spec, lat_spec,
        ] + [pl.BlockSpec(memory_space=pl.ANY)] * len(prev_caches),
        out_specs=tuple(out_specs),
        input_output_aliases=aliases,
        compiler_params=_cparams("arbitrary"),
        name="odd_in",
    )(x, *comb, g, modt, modt, w, bd2, qn, kn, cos, sin, *prev_caches)


def _rope_tables():
    nf = HEAD_DIM // 4
    inv = ROPE_BASE ** (-jnp.arange(nf, dtype=F32) / nf)
    t = jnp.arange(DEC_SEQ)
    pos = jnp.stack([t // GRID_W, t % GRID_W], axis=-1).astype(F32)
    ang = pos[:, :, None] * inv
    cos, sin = jnp.cos(ang), jnp.sin(ang)
    cos_h = jnp.stack([cos, cos], axis=2).reshape(DEC_SEQ, HEAD_DIM)
    sin_h = jnp.stack([-sin, sin], axis=2).reshape(DEC_SEQ, HEAD_DIM)
    return jnp.tile(cos_h, (1, 2)), jnp.tile(sin_h, (1, 2))


def _odd_out_kernel(o_ref, x_ref, gate_ref, w_ref, g2_ref, sh_ref, sc_ref, rw_ref, rb_ref,
                    xo_ref, h_ref, lg_ref):
    x = x_ref[...] + gate_ref[0] * _dot(o_ref[...], w_ref[...])
    xo_ref[...] = x
    _moe_input(x, g2_ref, sh_ref, sc_ref, rw_ref, rb_ref, h_ref, lg_ref)


def _odd_out(o, x, modt, w_o, g2, rw, rb):
    nq = C_HEADS * HEAD_DIM
    out_shape, out_specs = _mixer_out_shapes()
    return pl.pallas_call(
        _odd_out_kernel,
        out_shape=out_shape,
        grid=(N_TILES,),
        in_specs=[_tile_spec(nq), _tile_spec(D_MODEL), _mod_spec(2), _full_spec((nq, D_MODEL))]
        + _moe_input_specs(),
        out_specs=out_specs,
        compiler_params=_cparams("arbitrary"),
        name="odd_out",
    )(o, x, modt, w_o, g2, modt, modt, rw, rb)


def _pack_bf16_pair(lo, hi):
    lo_bits = lax.bitcast_convert_type(lo.astype(BF16).astype(F32), U32)
    hi_bits = lax.bitcast_convert_type(hi.astype(BF16).astype(F32), U32)
    return (hi_bits & jnp.uint32(0xFFFF0000)) | (lo_bits >> 16)


def _unpack_bf16_pair(packed):
    lo = lax.bitcast_convert_type(packed << 16, F32).astype(BF16)
    hi = lax.bitcast_convert_type(packed & jnp.uint32(0xFFFF0000), F32).astype(BF16)
    return lo, hi


def _moe_expert_kernel(layer, be_ref, nb_ref, first_ref, slot_ref, nxt_ref, src_cur_ref, src_nxt_ref,
                       h_ref, wgu_hbm, bgu_ref, wd_hbm, bd_ref, o_ref, xg0, xg1, wgu_buf, wd_buf, wsem):
    i = pl.program_id(0)
    nb = nb_ref[0]
    half = D_MODEL // 2
    xg = (xg0, xg1)

    def gather(idx_ref, u, dst):
        for r in range(MOE_TM):
            dst[pl.ds(r, 1), :] = h_ref[pl.ds(idx_ref[0, 0, u * MOE_TM + r], 1), :]

    def weight_copies(e, slot):
        return (pltpu.make_async_copy(wgu_hbm.at[layer, e], wgu_buf.at[slot], wsem.at[0, slot]),
                pltpu.make_async_copy(wd_hbm.at[layer, e], wd_buf.at[slot], wsem.at[1, slot]))

    @pl.when(i == 0)
    def _():
        for cp in weight_copies(be_ref[0], 0):
            cp.start()
        gather(src_cur_ref, 0, xg0)

    for u in range(MOE_STEP_BLOCKS):
        blk = i * MOE_STEP_BLOCKS + u
        rows = pl.ds(u * MOE_TM, MOE_TM)
        x_cur, x_nxt = xg[u % 2], xg[(u + 1) % 2]

        @pl.when(blk < nb)
        def _():
            ws = slot_ref[blk]
            e = be_ref[blk]

            @pl.when(first_ref[blk] == 1)
            def _():
                for cp in weight_copies(e, ws):
                    cp.wait()

                @pl.when(nxt_ref[blk] >= 0)
                def _():
                    for cp in weight_copies(nxt_ref[blk], 1 - ws):
                        cp.start()

            x_lo, x_hi = _unpack_bf16_pair(x_cur[...])
            gu = (_dot(x_lo, wgu_buf[ws, 0:half, :].astype(BF16)) + _dot(x_hi, wgu_buf[ws, half:, :].astype(BF16))
                  + bgu_ref[0, pl.ds(e, 1), :])
            if u + 1 < MOE_STEP_BLOCKS:
                gather(src_cur_ref, u + 1, x_nxt)
            else:
                gather(src_nxt_ref, 0, x_nxt)
            g = jnp.minimum(gu[:, 0:D_EXPERT], SWIGLU_LIMIT)
            lin = jnp.clip(gu[:, D_EXPERT:], -SWIGLU_LIMIT, SWIGLU_LIMIT)
            act = g * jax.nn.sigmoid(SWIGLU_ALPHA * g) * (lin + 1.0)
            o_ref[rows, :] = _dot(act.astype(BF16), wd_buf[ws].astype(BF16)) + bd_ref[0, pl.ds(e, 1), :]

        @pl.when(blk >= nb)
        def _():
            o_ref[rows, :] = jnp.zeros((MOE_TM, D_MODEL), F32)


def _moe_experts(layer, plan, h_packed, w_gu, b_gu, w_down, b_down):
    block_e, n_used, first, slot, nxt, row_tok = plan
    half = D_MODEL // 2
    steps = MOE_BLOCKS // MOE_STEP_BLOCKS
    step_rows = MOE_STEP_BLOCKS * MOE_TM
    idx_spec = lambda f: pl.BlockSpec((1, 1, step_rows), f, memory_space=pltpu.SMEM)
    hbm = pl.BlockSpec(memory_space=pl.ANY)
    grid_spec = pltpu.PrefetchScalarGridSpec(
        num_scalar_prefetch=5,
        grid=(steps,),
        in_specs=[
            idx_spec(lambda i, *_: (i, 0, 0)),
            idx_spec(lambda i, *_: (jnp.minimum(i + 1, steps - 1), 0, 0)),
            pl.BlockSpec((N_TOK, half), lambda i, *_: (0, 0), pipeline_mode=pl.Buffered(1)),
            hbm,
            pl.BlockSpec((1, N_EXPERTS, 2 * D_EXPERT), lambda i, *_: (layer, 0, 0)),
            hbm,
            pl.BlockSpec((1, N_EXPERTS, D_MODEL), lambda i, *_: (layer, 0, 0)),
        ],
        out_specs=pl.BlockSpec((step_rows, D_MODEL), lambda i, *_: (i, 0)),
        scratch_shapes=[
            pltpu.VMEM((MOE_TM, half), U32), pltpu.VMEM((MOE_TM, half), U32),
            pltpu.VMEM((2, D_MODEL, 2 * D_EXPERT), F32), pltpu.VMEM((2, D_EXPERT, D_MODEL), F32),
            pltpu.SemaphoreType.DMA((2, 2)),
        ],
    )
    rows = row_tok.reshape(steps, 1, step_rows)
    return pl.pallas_call(
        functools.partial(_moe_expert_kernel, layer),
        out_shape=jax.ShapeDtypeStruct((MOE_ROWS, D_MODEL), F32),
        grid_spec=grid_spec,
        compiler_params=pltpu.CompilerParams(dimension_semantics=("arbitrary",), vmem_limit_bytes=MOE_VMEM_LIMIT),
        name="moe_experts",
    )(block_e, n_used, first, slot, nxt, rows, rows, h_packed, w_gu, b_gu, w_down, b_down)


def _moe_combine_kernel(x_ref, *refs):
    ctx_ref, lat_ref = refs[N_COMB:]
    x = _combined(x_ref, refs[:N_COMB])

    @pl.when(pl.program_id(0) < N_CTX_TILES)
    def _():
        ctx_ref[...] = x

    @pl.when(pl.program_id(0) >= N_CTX_TILES)
    def _():
        lat_ref[...] = x


def _moe_combine(x, comb):
    return pl.pallas_call(
        _moe_combine_kernel,
        out_shape=(jax.ShapeDtypeStruct((N_CTX_TOK, D_MODEL), F32), jax.ShapeDtypeStruct((N_LAT_TOK, D_MODEL), F32)),
        grid=(N_TILES,),
        in_specs=[_tile_spec(D_MODEL)] + _comb_specs(),
        out_specs=(pl.BlockSpec((TILE, D_MODEL), lambda i: (jnp.minimum(i, N_CTX_TILES - 1), 0)),
                   pl.BlockSpec((TILE, D_MODEL), lambda i: (jnp.maximum(i - N_CTX_TILES, 0), 0))),
        compiler_params=_cparams("arbitrary"),
        name="moe_combine",
    )(x, *comb)


def _lut(idx, table):
    n = table.shape[0]
    hit = idx[:, None] == jnp.arange(n, dtype=idx.dtype)[None, :]
    return jnp.sum(jnp.where(hit, table[None, :], 0), axis=1)


def _moe_route(logits):
    i32 = jnp.int32
    top_val, top_idx = lax.top_k(logits[:, :N_EXPERTS], TOP_K)
    gates = jax.nn.softmax(top_val, axis=-1)
    flat_e = top_idx.reshape(N_ASG).astype(i32)
    asg = jnp.arange(N_ASG, dtype=i32)
    experts = jnp.arange(N_EXPERTS, dtype=i32)
    counts = jnp.sum((flat_e[:, None] == experts[None, :]).astype(i32), axis=0)
    pad_counts = (counts + MOE_TM - 1) // MOE_TM * MOE_TM
    pad_end = jnp.cumsum(pad_counts)
    n_used = pad_end[-1] // MOE_TM
    fill_end = jnp.cumsum(pad_counts - counts)
    filler = jnp.arange(MOE_ROWS - N_ASG, dtype=i32)
    fill_e = jnp.sum((filler[:, None] >= fill_end[None, :]).astype(i32), axis=1)
    keys = jnp.concatenate([flat_e * ASG_STRIDE + asg, fill_e * ASG_STRIDE + (ASG_STRIDE - 1)])
    low = jnp.sort(keys) % ASG_STRIDE
    is_real = low < N_ASG
    row_tok = jnp.where(is_real, low // TOP_K, 0)
    row = jnp.arange(MOE_ROWS, dtype=i32)
    _, by_asg = lax.sort((jnp.where(is_real, low, ASG_STRIDE), row), num_keys=1)
    dest = by_asg[:N_ASG]
    blk = jnp.arange(MOE_BLOCKS, dtype=i32)
    block_e = jnp.sum((blk[:, None] * MOE_TM >= pad_end[None, :]).astype(i32), axis=1)
    last_e = jnp.max(jnp.where(counts > 0, experts, 0))
    block_e = jnp.where(blk < n_used, jnp.minimum(block_e, N_EXPERTS - 1), last_e)
    first = jnp.concatenate([jnp.ones((1,), i32), (block_e[1:] != block_e[:-1]).astype(i32)])
    slot = (jnp.cumsum(first) - 1) % 2
    later = (experts[None, :] > experts[:, None]) & (counts[None, :] > 0)
    nxt_of = jnp.min(jnp.where(later, experts[None, :], N_EXPERTS), axis=1)
    nxt = _lut(block_e, jnp.where(nxt_of < N_EXPERTS, nxt_of, -1))
    plan = (block_e, n_used.reshape(1).astype(i32), first, slot.astype(i32), nxt.astype(i32), row_tok)
    return gates, dest.reshape(N_TOK, TOP_K), plan


def _moe(layer, h_packed, logits, modt, w_gu, b_gu, w_down, b_down):
    gates, dest, plan = _moe_route(logits)
    y_rows = _moe_experts(layer, plan, h_packed, w_gu, b_gu, w_down, b_down)
    return (modt, gates) + tuple(y_rows[dest[:, k]] for k in range(TOP_K))


def kernel(x_prompt, x_sample, cache_nat_k, cache_nat_v, cache_swa_k, cache_swa_v, state_ssm, c, c_ctx,
           norm1_g, norm2_g, mod_w, mod_b,
           ab_w_in, ab_w_out, ssm_lam_re, ssm_lam_im, ssm_log_dt, ssm_b_re, ssm_b_im, ssm_c_re, ssm_c_im,
           ssm_d, ssm_glu_w, ssm_glu_b, nat_qn, nat_kn, nat_rpb,
           swa_w_qkv, swa_w_o, swa_qn, swa_kn, swa_sink,
           moe_router_w, moe_router_b, moe_w_gu, moe_b_gu, moe_w_down, moe_b_down):
    x = jnp.concatenate([x_prompt.reshape(N_CTX_TOK, D_MODEL), x_sample.reshape(N_LAT_TOK, D_MODEL)], axis=0)
    cond = jnp.zeros((SUBLANES, D_MODEL), F32).at[0].set(c_ctx).at[1:1 + DEC_BATCH].set(c)
    mod = _modulation(cond, mod_w, mod_b)
    tile_row = np.concatenate([np.zeros(N_CTX_TILES, np.int32),
                               1 + np.arange(N_TILES - N_CTX_TILES, dtype=np.int32) // LAT_TILES_PER_SEQ])
    head_gain = lambda gn: jnp.tile(gn, 2).reshape(1, LANES)
    bd = np.kron(np.eye(2, dtype=np.float32), np.full((HEAD_DIM, HEAD_DIM), 1.0 / HEAD_DIM, np.float32))
    bd2 = jnp.asarray(np.concatenate([bd, bd], axis=0), BF16)
    rope_cos, rope_sin = _rope_tables()
    router_w = jnp.pad(moe_router_w, ((0, 0), (0, 0), (0, ROUTER_PAD - N_EXPERTS)))
    router_b = jnp.pad(moe_router_b, ((0, 0), (0, ROUTER_PAD - N_EXPERTS))).reshape(DEPTH, 1, ROUTER_PAD)
    nkv = C_KV_HEADS * HEAD_DIM

    ssm_out = []
    nat_caches, swa_caches = (), ()
    comb = ()
    for l in range(DEPTH):
        modt = mod[l][tile_row].reshape(N_TILES, 1, N_MOD * D_MODEL)
        g1 = norm1_g[l].reshape(1, D_MODEL)
        moe_in = (norm2_g[l].reshape(1, D_MODEL), router_w[l], router_b[l])
        i = l // 2
        if l % 2 == 0:
            outs = _even_in(x, comb, nat_caches, i, g1, modt, ab_w_in[i].astype(BF16), bd2,
                            head_gain(nat_qn[i]), head_gain(nat_kn[i]))
            if comb:
                x, outs = outs[0], outs[1:]
            u_t, q, k, v = outs[:4]
            nat_caches = tuple(outs[4:])
            a_b, w_bd, c_bd = _s5_params(ssm_lam_re[i], ssm_lam_im[i], ssm_log_dt[i], ssm_b_re[i], ssm_b_im[i],
                                         ssm_c_re[i], ssm_c_im[i])
            st = state_ssm[:, i].reshape(DEC_BATCH, 2, 2, SSM_LANES).transpose(1, 2, 0, 3)
            s0 = jnp.zeros((2, 2, N_TILES, SSM_LANES), F32)
            first = N_CTX_TILES + LAT_TILES_PER_SEQ * np.arange(DEC_BATCH)
            s0 = s0.at[0, :, first].set(st[0].transpose(1, 0, 2))
            s0 = s0.at[1, :, first + LAT_TILES_PER_SEQ - 1].set(st[1].transpose(1, 0, 2))
            y_t, fin = _s5_scan(u_t.reshape(TILE, N_TILES, A_WIDTH), a_b, w_bd, c_bd, s0)
            o = _nat_attention(q, k, v,
                               cache_nat_k[:, i].reshape(DEC_BATCH, PAST_LEN, B_WIDTH),
                               cache_nat_v[:, i].reshape(DEC_BATCH, PAST_LEN, B_WIDTH),
                               _nat_bias(nat_rpb[i]))
            x, h_packed, logits = _even_out(y_t.reshape(TILE, N_TILES * A_WIDTH), u_t, o, x, modt,
                                            ssm_d[i].reshape(1, A_WIDTH), ssm_glu_w[i].astype(BF16),
                                            ssm_glu_b[i].reshape(1, A_WIDTH), ab_w_out[i].astype(BF16), *moe_in)
            ssm_out.append(fin[:, :, :N_CTX_TILES].transpose(2, 0, 1, 3).reshape(BATCH, 2, 2, A_GROUPS, A_STATE))
        else:
            outs = _odd_in(x, comb, swa_caches, i, g1, modt, swa_w_qkv[i].astype(BF16), bd2,
                           head_gain(swa_qn[i]), head_gain(swa_kn[i]), rope_cos, rope_sin)
            if comb:
                x, outs = outs[0], outs[1:]
            q, k_rot, v = outs[:3]
            swa_caches = tuple(outs[3:])
            o = _swa_attention(q, k_rot, v,
                               cache_swa_k[:, i].reshape(DEC_BATCH, PAST_LEN, nkv),
                               cache_swa_v[:, i].reshape(DEC_BATCH, PAST_LEN, nkv), swa_sink[i])
            x, h_packed, logits = _odd_out(o, x, modt, swa_w_o[i].astype(BF16), *moe_in)
        comb = _moe(l, h_packed, logits, modt, moe_w_gu, moe_b_gu, moe_w_down, moe_b_down)

    y_ctx, y_lat = _moe_combine(x, comb)
    y_prompt = y_ctx.reshape(BATCH, SEQ, D_MODEL)
    y_sample = y_lat.reshape(DEC_BATCH, DEC_SEQ, D_MODEL)
    nat_shape = (BATCH, N_EVEN, SEQ, B_HEADS, HEAD_DIM)
    swa_shape = (BATCH, N_ODD, SEQ, C_KV_HEADS, HEAD_DIM)
    return (y_prompt, y_sample,
            nat_caches[0].reshape(nat_shape), nat_caches[1].reshape(nat_shape),
            swa_caches[0].reshape(swa_shape), swa_caches[1].reshape(swa_shape),
            jnp.stack(ssm_out, axis=1))
```

```python
import functools

import jax
import jax.numpy as jnp
import numpy as np
from jax import lax
from jax.experimental import pallas as pl
from jax.experimental.pallas import tpu as pltpu

F32 = jnp.float32
BF16 = jnp.bfloat16
U32 = jnp.uint32

D_MODEL = 1024
BATCH = 32
SEQ = 256
DEPTH = 4
DEC_BATCH = 2
DEC_SEQ = 1024
PAST_LEN = 256
GRID_W = 64
HEAD_DIM = 64
N_MOD = 6
N_EVEN = (DEPTH + 1) // 2
N_ODD = DEPTH // 2
A_WIDTH = 512
A_GROUP_CH = 16
A_GROUPS = 32
A_STATE = 64
B_HEADS = 8
B_WIDTH = 512
WIN_R_MAX = 8
WIN_C = 16
C_HEADS = 16
C_KV_HEADS = 2
C_GROUP = 8
WINDOW = 128
ROPE_BASE = 10000.0
N_EXPERTS = 32
TOP_K = 4
D_EXPERT = 1024
SWIGLU_LIMIT = 7.0
SWIGLU_ALPHA = 1.702
EPS = 1e-6
NEG_INF = -1e30

LANES = 128
SUBLANES = 8
TILE = 256
N_CTX_TOK = BATCH * SEQ
N_LAT_TOK = DEC_BATCH * DEC_SEQ
N_TOK = N_CTX_TOK + N_LAT_TOK
N_TILES = N_TOK // TILE
N_CTX_TILES = N_CTX_TOK // TILE
LAT_TILES_PER_SEQ = DEC_SEQ // TILE
SSM_LANES = A_GROUPS * A_STATE
SSM_SLABS = A_WIDTH // LANES
SSM_SLAB_STATES = SSM_LANES // SSM_SLABS
SCAN_ROWS = 8
SCAN_GROUPS = N_TILES // SCAN_ROWS
SCAN_CHUNK = 64
SCAN_LANE_PARTS = 2
MOE_TM = 256
MOE_STEP_BLOCKS = 4
N_ASG = N_TOK * TOP_K
ASG_STRIDE = 1 << 16
MOE_BLOCKS = N_ASG // MOE_TM + N_EXPERTS
MOE_ROWS = MOE_BLOCKS * MOE_TM
ROUTER_PAD = LANES
VMEM_LIMIT = 56 * 1024 * 1024
MOE_VMEM_LIMIT = 60 * 1024 * 1024
ATT_SCALE = HEAD_DIM ** -0.5


def _cparams(*sem):
    return pltpu.CompilerParams(dimension_semantics=sem, vmem_limit_bytes=VMEM_LIMIT)


def _dot(a, b):
    return jnp.dot(a, b, preferred_element_type=F32)


def _dot_nt(a, b):
    return lax.dot_general(a, b, (((1,), (1,)), ((), ())), preferred_element_type=F32)


def _split(a):
    hi = a.astype(BF16)
    lo = (a - hi.astype(F32)).astype(BF16)
    return hi, lo


def _dot3(a, b):
    a_hi, a_lo = _split(a)
    b_hi, b_lo = _split(b)
    return _dot(a_hi, b_hi) + (_dot(a_hi, b_lo) + _dot(a_lo, b_hi))


def _silu(x):
    return x * jax.nn.sigmoid(x)


def _norm_mod(x, g, shift, scale):
    y = x * lax.rsqrt(jnp.mean(x * x, axis=-1, keepdims=True) + EPS)
    return (y * g) * (1.0 + scale) + shift


def _head_rms(x, bd2, gain):
    sq_hi, sq_lo = _split(x * x)
    ms = _dot(jnp.concatenate([sq_hi, sq_lo], axis=1), bd2)
    return x * lax.rsqrt(ms + EPS) * gain


def _combined(x_ref, comb_refs):
    x = x_ref[...]
    if not comb_refs:
        return x
    gmod_ref, gates_ref = comb_refs[0], comb_refs[1]
    gates = gates_ref[...]
    acc = gates[:, 0:1] * comb_refs[2][...]
    for k in range(1, TOP_K):
        acc = acc + gates[:, k:k + 1] * comb_refs[2 + k][...]
    return x + gmod_ref[0] * acc


def _comb_specs():
    return [_mod_spec(5), _tile_spec(TOP_K)] + [_tile_spec(D_MODEL)] * TOP_K


N_COMB = 2 + TOP_K


def _lane_half(shape, half):
    lane = lax.broadcasted_iota(jnp.int32, shape, len(shape) - 1)
    return (lane < HEAD_DIM) if half == 0 else (lane >= HEAD_DIM)


def _mod_kernel(cond_ref, w_ref, b_ref, o_ref):
    o_ref[0] = _dot3(_silu(cond_ref[...]), w_ref[0]) + b_ref[0]


def _modulation(cond, mod_w, mod_b):
    nc = N_MOD
    return pl.pallas_call(
        _mod_kernel,
        out_shape=jax.ShapeDtypeStruct((DEPTH, SUBLANES, N_MOD * D_MODEL), F32),
        grid=(DEPTH, nc),
        in_specs=[
            pl.BlockSpec((SUBLANES, D_MODEL), lambda l, c: (0, 0)),
            pl.BlockSpec((1, D_MODEL, D_MODEL), lambda l, c: (l, 0, c)),
            pl.BlockSpec((1, 1, D_MODEL), lambda l, c: (l, 0, c)),
        ],
        out_specs=pl.BlockSpec((1, SUBLANES, D_MODEL), lambda l, c: (l, 0, c)),
        compiler_params=_cparams("arbitrary", "arbitrary"),
        name="modulation",
    )(cond, mod_w, mod_b.reshape(DEPTH, 1, N_MOD * D_MODEL))


def _mod_spec(col):
    return pl.BlockSpec((1, 1, D_MODEL), lambda i: (i, 0, col))


def _tile_spec(width):
    return pl.BlockSpec((TILE, width), lambda i: (i, 0))


def _ctx_tile_spec(width):
    return pl.BlockSpec((TILE, width), lambda i: (jnp.minimum(i, N_CTX_TILES - 1), 0))


def _full_spec(shape):
    nd = len(shape)
    return pl.BlockSpec(shape, lambda i: (0,) * nd)


def _cache_store(slot, n_slots, ref, value):
    @pl.when(pl.program_id(0) < N_CTX_TILES)
    def _():
        if slot == 0:
            ref[0, 0] = value
            for other in range(1, n_slots):
                ref[0, other] = jnp.zeros_like(value)
        else:
            ref[0, 0] = value


def _cache_specs(slot, n_slots, width):
    shape = jax.ShapeDtypeStruct((BATCH, n_slots, SEQ, width), F32)
    seq = lambda i: jnp.minimum(i, N_CTX_TILES - 1)
    if slot == 0:
        return shape, pl.BlockSpec((1, n_slots, SEQ, width), lambda i: (seq(i), 0, 0, 0))
    return shape, pl.BlockSpec((1, 1, SEQ, width), lambda i: (seq(i), slot, 0, 0))


def _even_in_kernel(n_comb, slot, n_prev, x_ref, *refs):
    comb, refs = refs[:n_comb], refs[n_comb:]
    g_ref, sh_ref, sc_ref, w_ref, bd_ref, qn_ref, kn_ref = refs[:7]
    outs = refs[7 + n_prev:]
    x = _combined(x_ref, comb)
    if n_comb:
        outs[0][...] = x
        outs = outs[1:]
    u_ref, q_ref, k_ref, v_ref, kc_ref, vc_ref = outs
    h = _norm_mod(x, g_ref[...], sh_ref[0], sc_ref[0]).astype(BF16)
    bd2 = bd_ref[...]
    z = _dot(h, w_ref[...])
    u_ref[...] = z[:, 0:A_WIDTH]
    ks = []
    for s in range(B_WIDTH // LANES):
        lo = A_WIDTH + s * LANES
        q = _head_rms(z[:, lo:lo + LANES], bd2, qn_ref[...])
        q_ref[:, s * LANES:(s + 1) * LANES] = (q * ATT_SCALE).astype(BF16)
        lo = A_WIDTH + B_WIDTH + s * LANES
        ks.append(_head_rms(z[:, lo:lo + LANES], bd2, kn_ref[...]))
    k = jnp.concatenate(ks, axis=1)
    v = z[:, A_WIDTH + 2 * B_WIDTH:]
    k_ref[...] = k
    v_ref[...] = v
    _cache_store(slot, N_EVEN, kc_ref, k)
    _cache_store(slot, N_EVEN, vc_ref, v)


def _even_in(x, comb, prev_caches, slot, g, modt, w, bd2, qn, kn):
    n_out = A_WIDTH + 3 * B_WIDTH
    cache_shape, cache_spec = _cache_specs(slot, N_EVEN, B_WIDTH)
    out_shape = [
        jax.ShapeDtypeStruct((TILE, N_TILES * A_WIDTH), F32),
        jax.ShapeDtypeStruct((N_TOK, B_WIDTH), BF16),
        jax.ShapeDtypeStruct((N_TOK, B_WIDTH), F32),
        jax.ShapeDtypeStruct((N_TOK, B_WIDTH), F32),
        cache_shape, cache_shape,
    ]
    out_specs = [pl.BlockSpec((TILE, A_WIDTH), lambda i: (0, i)),
                 _tile_spec(B_WIDTH), _tile_spec(B_WIDTH), _tile_spec(B_WIDTH), cache_spec, cache_spec]
    if comb:
        out_shape.insert(0, jax.ShapeDtypeStruct((N_TOK, D_MODEL), F32))
        out_specs.insert(0, _tile_spec(D_MODEL))
    n_in = 1 + len(comb) + 7
    aliases = {n_in + j: len(out_shape) - 2 + j for j in range(len(prev_caches))}
    return pl.pallas_call(
        functools.partial(_even_in_kernel, len(comb), slot, len(prev_caches)),
        out_shape=tuple(out_shape),
        grid=(N_TILES,),
        in_specs=[_tile_spec(D_MODEL)] + (_comb_specs() if comb else []) + [
            _full_spec((1, D_MODEL)), _mod_spec(0), _mod_spec(1),
            _full_spec((D_MODEL, n_out)), _full_spec((2 * LANES, LANES)),
            _full_spec((1, LANES)), _full_spec((1, LANES)),
        ] + [pl.BlockSpec(memory_space=pl.ANY)] * len(prev_caches),
        out_specs=tuple(out_specs),
        input_output_aliases=aliases,
        compiler_params=_cparams("arbitrary"),
        name="even_in",
    )(x, *comb, g, modt, modt, w, bd2, qn, kn, *prev_caches)


def _cmul(ar, ai, br, bi):
    return ar * br - ai * bi, ar * bi + ai * br


def _s5_kernel(u_ref, a_ref, w_ref, c_ref, s0_ref, y_ref, fin_ref, xr, xi, st_r, st_i):
    grp = pl.program_id(0)
    drn = pl.program_id(1)
    n_chunks = TILE // SCAN_CHUNK
    rows = SCAN_CHUNK * SCAN_ROWS
    part = SSM_LANES // SCAN_LANE_PARTS

    def run(store):
        @pl.loop(0, n_chunks)
        def _(c):
            cc = jnp.where(drn == 0, c, n_chunks - 1 - c)
            t0 = pl.multiple_of(cc * SCAN_CHUNK, SCAN_CHUNK)
            uu = u_ref[pl.ds(t0, SCAN_CHUNK), :, :]
            for s in range(SSM_SLABS):
                us = uu[:, :, s * LANES:(s + 1) * LANES].reshape(rows, LANES).astype(BF16)
                cols = slice(s * SSM_SLAB_STATES, (s + 1) * SSM_SLAB_STATES)
                xr[:, cols] = _dot(us, w_ref[0, 0, s])
                xi[:, cols] = _dot(us, w_ref[0, 1, s])
            for p in range(SCAN_LANE_PARTS):
                ls = slice(p * part, (p + 1) * part)
                ar = a_ref[0, 0, :, ls]
                ai = a_ref[0, 1, :, ls]

                def step(j, carry):
                    sr, si = carry
                    tt = jnp.where(drn == 0, j, SCAN_CHUNK - 1 - j)
                    r0 = pl.multiple_of(tt * SCAN_ROWS, SCAN_ROWS)
                    nr = ar * sr - ai * si + xr[pl.ds(r0, SCAN_ROWS), ls]
                    ni = ar * si + ai * sr + xi[pl.ds(r0, SCAN_ROWS), ls]
                    if store:
                        xr[pl.ds(r0, SCAN_ROWS), ls] = nr
                        xi[pl.ds(r0, SCAN_ROWS), ls] = ni
                    return nr, ni

                sr, si = lax.fori_loop(0, SCAN_CHUNK, step, (st_r[:, ls], st_i[:, ls]), unroll=4)
                st_r[:, ls] = sr
                st_i[:, ls] = si
            if store:
                for s in range(SSM_SLABS):
                    cols = slice(s * SSM_SLAB_STATES, (s + 1) * SSM_SLAB_STATES)
                    ys = (_dot(xr[:, cols].astype(BF16), c_ref[0, 0, s])
                          - _dot(xi[:, cols].astype(BF16), c_ref[0, 1, s]))
                    ys = ys.reshape(SCAN_CHUNK, SCAN_ROWS, LANES)
                    lanes = slice(s * LANES, (s + 1) * LANES)

                    @pl.when(drn == 0)
                    def _():
                        y_ref[pl.ds(t0, SCAN_CHUNK), :, lanes] = ys

                    @pl.when(drn != 0)
                    def _():
                        y_ref[pl.ds(t0, SCAN_CHUNK), :, lanes] += ys

    st_r[...] = s0_ref[0, 0]
    st_i[...] = s0_ref[0, 1]

    @pl.when(grp == SCAN_GROUPS - 1)
    def _():
        st_r[...] = jnp.zeros_like(st_r)
        st_i[...] = jnp.zeros_like(st_i)
        run(False)
        pr, pi = a_ref[0, 0], a_ref[0, 1]
        for _ in range(8):
            pr, pi = _cmul(pr, pi, pr, pi)
        fr, fi = st_r[...], st_i[...]
        s0r, s0i = s0_ref[0, 0], s0_ref[0, 1]
        row = lax.broadcasted_iota(jnp.int32, (SCAN_ROWS, SSM_LANES), 0)
        quarter = row % LAT_TILES_PER_SEQ
        fwd = drn == 0
        keep = quarter != jnp.where(fwd, 0, LAT_TILES_PER_SEQ - 1)
        ir, ii = s0r, s0i
        for _ in range(LAT_TILES_PER_SEQ - 1):
            nr, ni = _cmul(pr, pi, ir, ii)
            nr, ni = nr + fr, ni + fi
            nr = jnp.where(fwd, pltpu.roll(nr, 1, 0), pltpu.roll(nr, SCAN_ROWS - 1, 0))
            ni = jnp.where(fwd, pltpu.roll(ni, 1, 0), pltpu.roll(ni, SCAN_ROWS - 1, 0))
            ir = s0r + jnp.where(keep, nr, 0.0)
            ii = s0i + jnp.where(keep, ni, 0.0)
        st_r[...] = ir
        st_i[...] = ii

    run(True)
    fin_ref[0, 0] = st_r[...]
    fin_ref[0, 1] = st_i[...]


def _s5_scan(u_tb, a_b, w_bd, c_bd, s0):
    rows = SCAN_CHUNK * SCAN_ROWS
    return pl.pallas_call(
        _s5_kernel,
        out_shape=(
            jax.ShapeDtypeStruct((TILE, N_TILES, A_WIDTH), F32),
            jax.ShapeDtypeStruct((2, 2, N_TILES, SSM_LANES), F32),
        ),
        grid=(SCAN_GROUPS, 2),
        in_specs=[
            pl.BlockSpec((TILE, SCAN_ROWS, A_WIDTH), lambda g, d: (0, g, 0)),
            pl.BlockSpec((1, 2, SCAN_ROWS, SSM_LANES), lambda g, d: (d, 0, 0, 0)),
            pl.BlockSpec((1, 2, SSM_SLABS, LANES, SSM_SLAB_STATES), lambda g, d: (d, 0, 0, 0, 0)),
            pl.BlockSpec((1, 2, SSM_SLABS, SSM_SLAB_STATES, LANES), lambda g, d: (d, 0, 0, 0, 0)),
            pl.BlockSpec((1, 2, SCAN_ROWS, SSM_LANES), lambda g, d: (d, 0, g, 0)),
        ],
        out_specs=(
            pl.BlockSpec((TILE, SCAN_ROWS, A_WIDTH), lambda g, d: (0, g, 0)),
            pl.BlockSpec((1, 2, SCAN_ROWS, SSM_LANES), lambda g, d: (d, 0, g, 0)),
        ),
        scratch_shapes=[
            pltpu.VMEM((rows, SSM_LANES), F32), pltpu.VMEM((rows, SSM_LANES), F32),
            pltpu.VMEM((SCAN_ROWS, SSM_LANES), F32), pltpu.VMEM((SCAN_ROWS, SSM_LANES), F32),
        ],
        compiler_params=_cparams("arbitrary", "arbitrary"),
        name="s5_scan",
    )(u_tb, a_b, w_bd, c_bd, s0)


def _s5_params(lam_re, lam_im, log_dt, b_re, b_im, c_re, c_im):
    dt = jnp.exp(log_dt)[..., None]
    mag = jnp.exp(lam_re * dt)
    ab_re, ab_im = mag * jnp.cos(lam_im * dt), mag * jnp.sin(lam_im * dt)
    den = lam_re * lam_re + lam_im * lam_im
    nr, ni = ab_re - 1.0, ab_im
    f_re = (nr * lam_re + ni * lam_im) / den
    f_im = (ni * lam_re - nr * lam_im) / den
    bb_re = f_re[..., None] * b_re - f_im[..., None] * b_im
    bb_im = f_re[..., None] * b_im + f_im[..., None] * b_re
    a_b = jnp.stack([ab_re, ab_im], axis=1).reshape(2, 2, 1, SSM_LANES)
    a_b = jnp.broadcast_to(a_b, (2, 2, SCAN_ROWS, SSM_LANES))
    gps = A_GROUPS // SSM_SLABS
    eye = jnp.eye(gps, dtype=F32)

    def in_bd(bb):
        bb = bb.reshape(2, SSM_SLABS, gps, A_STATE, A_GROUP_CH)
        m = jnp.einsum('dsgph,gk->dsghkp', bb, eye)
        return m.reshape(2, SSM_SLABS, LANES, SSM_SLAB_STATES)

    def out_bd(cc):
        cc = cc.reshape(2, SSM_SLABS, gps, A_GROUP_CH, A_STATE)
        m = jnp.einsum('dsghp,gk->dsgpkh', cc, eye)
        return m.reshape(2, SSM_SLABS, SSM_SLAB_STATES, LANES)

    w_bd = jnp.stack([in_bd(bb_re), in_bd(bb_im)], axis=1).astype(BF16)
    c_bd = jnp.stack([out_bd(c_re), out_bd(c_im)], axis=1).astype(BF16)
    return a_b, w_bd, c_bd


def _softmax_pv(scores, values, sink=None):
    m = functools.reduce(jnp.maximum, [jnp.max(s, axis=-1, keepdims=True) for s in scores])
    if sink is not None:
        m = jnp.maximum(m, sink)
    den = None
    acc = None
    for s, v in zip(scores, values):
        e = jnp.exp(s - m)
        d = jnp.sum(e, axis=-1, keepdims=True)
        o = _dot(e.astype(BF16), v)
        den = d if den is None else den + d
        acc = o if acc is None else acc + o
    if sink is not None:
        den = den + jnp.exp(sink - m)
    return acc / den


def _nat_ctx_kernel(q_ref, k_ref, v_ref, o_ref):
    @pl.when(pl.program_id(0) >= N_CTX_TILES)
    def _():
        o_ref[...] = jnp.zeros_like(o_ref)

    @pl.when(pl.program_id(0) < N_CTX_TILES)
    def _():
        for p in range(B_WIDTH // LANES):
            ls = slice(p * LANES, (p + 1) * LANES)
            qs = q_ref[:, ls]
            ks = k_ref[:, ls].astype(BF16)
            vs = v_ref[:, ls].astype(BF16)
            out = None
            for half in range(2):
                msk = _lane_half(qs.shape, half)
                qm = jnp.where(msk, qs, jnp.zeros_like(qs))
                o = _softmax_pv([_dot_nt(qm, ks)], [vs])
                out = o if out is None else jnp.where(msk, o, out)
            o_ref[:, ls] = out.astype(BF16)


def _nat_lat_kernel(o_all_ref, q_ref, k_ref, v_ref, kc_ref, vc_ref, bias_ref, o_ref):
    del o_all_ref
    r = pl.program_id(0)
    wr = WIN_R_MAX
    rs = jnp.clip(r - wr // 2, 0, DEC_SEQ // GRID_W - wr)
    k0 = pl.multiple_of(rs * GRID_W, GRID_W)
    n_nb = wr * GRID_W
    for p in range(B_WIDTH // LANES):
        ls = slice(p * LANES, (p + 1) * LANES)
        for b in range(DEC_BATCH):
            qs = q_ref[b, :, ls]
            ks = jnp.concatenate([k_ref[b, pl.ds(k0, n_nb), ls], kc_ref[b, :, ls]], axis=0).astype(BF16)
            vs = jnp.concatenate([v_ref[b, pl.ds(k0, n_nb), ls], vc_ref[b, :, ls]], axis=0).astype(BF16)
            out = None
            for half in range(2):
                msk = _lane_half(qs.shape, half)
                qm = jnp.where(msk, qs, jnp.zeros_like(qs))
                o = _softmax_pv([_dot_nt(qm, ks) + bias_ref[2 * p + half, 0]], [vs])
                out = o if out is None else jnp.where(msk, o, out)
            o_ref[b, :, ls] = out.astype(BF16)


def _nat_attention(q, k, v, kc, vc, bias):
    o_ctx = pl.pallas_call(
        _nat_ctx_kernel,
        out_shape=jax.ShapeDtypeStruct((N_TOK, B_WIDTH), BF16),
        grid=(N_TILES,),
        in_specs=[_ctx_tile_spec(B_WIDTH)] * 3,
        out_specs=_tile_spec(B_WIDTH),
        compiler_params=_cparams("arbitrary"),
        name="nat_ctx",
    )(q, k, v)
    rows = DEC_SEQ // GRID_W
    n_seq = N_TOK // DEC_SEQ
    lat = N_CTX_TOK // DEC_SEQ // DEC_BATCH
    as_seq = lambda a: a.reshape(n_seq, DEC_SEQ, B_WIDTH)
    row_spec = pl.BlockSpec((DEC_BATCH, GRID_W, B_WIDTH), lambda r: (lat, r, 0))
    seq_spec = pl.BlockSpec((DEC_BATCH, DEC_SEQ, B_WIDTH), lambda r: (lat, 0, 0))
    ctx_spec = pl.BlockSpec((DEC_BATCH, PAST_LEN, B_WIDTH), lambda r: (0, 0, 0))

    def bias_idx(r):
        return (0, r - jnp.clip(r - WIN_R_MAX // 2, 0, rows - WIN_R_MAX), 0, 0)

    o = pl.pallas_call(
        _nat_lat_kernel,
        out_shape=jax.ShapeDtypeStruct((n_seq, DEC_SEQ, B_WIDTH), BF16),
        grid=(rows,),
        in_specs=[
            pl.BlockSpec(memory_space=pl.ANY), row_spec, seq_spec, seq_spec, ctx_spec, ctx_spec,
            pl.BlockSpec((B_HEADS, 1, GRID_W, WIN_R_MAX * GRID_W + PAST_LEN), bias_idx),
        ],
        out_specs=row_spec,
        input_output_aliases={0: 0},
        compiler_params=_cparams("arbitrary"),
        name="nat_lat",
    )(as_seq(o_ctx), as_seq(q), as_seq(k), as_seq(v), kc, vc, bias)
    return o.reshape(N_TOK, B_WIDTH)


def _nat_bias(rpb):
    qc = np.arange(GRID_W)
    kc = np.arange(GRID_W)
    cs = np.clip(qc - WIN_C // 2, 0, GRID_W - WIN_C)
    ok = (kc[None, :] >= cs[:, None]) & (kc[None, :] < cs[:, None] + WIN_C)
    dc = np.clip(kc[None, :] - qc[:, None] + (WIN_C - 1), 0, 2 * WIN_C - 2)
    pick = (dc[:, :, None] == np.arange(2 * WIN_C - 1)).astype(np.float32)
    t = jnp.einsum('hrc,qkc->hrqk', rpb.astype(F32), pick, precision=lax.Precision.HIGHEST)
    t = jnp.where(ok[None, None], t, NEG_INF)
    per_d = [t[:, WIN_R_MAX - 1 - d:2 * WIN_R_MAX - 1 - d].transpose(0, 2, 1, 3)
             .reshape(B_HEADS, GRID_W, WIN_R_MAX * GRID_W) for d in range(WIN_R_MAX)]
    nb = jnp.stack(per_d, axis=1)
    return jnp.concatenate([nb, jnp.zeros((B_HEADS, WIN_R_MAX, GRID_W, PAST_LEN), F32)], axis=-1)


def _gqa_heads(q_ref, sink_ref, k_segs, v_segs, masks, o_ref):
    k_rot = [pltpu.roll(k.astype(F32), HEAD_DIM, 1).astype(BF16) for k in k_segs]
    v_rot = [pltpu.roll(v.astype(F32), HEAD_DIM, 1).astype(BF16) for v in v_segs]
    for p in range(C_HEADS // 2):
        ls = slice(p * LANES, (p + 1) * LANES)
        qs = q_ref[:, ls]
        out = None
        for half in range(2):
            head = 2 * p + half
            kv = head // C_GROUP
            ks = k_segs if kv == half else k_rot
            vs = v_segs if kv == half else v_rot
            msk = _lane_half(qs.shape, half)
            qm = jnp.where(msk, qs, jnp.zeros_like(qs))
            scores = []
            for kseg, m in zip(ks, masks):
                s = _dot_nt(qm, kseg)
                scores.append(s if m is None else jnp.where(m, s, NEG_INF))
            o = _softmax_pv(scores, vs, sink=sink_ref[head])
            out = o if out is None else jnp.where(msk, o, out)
        o_ref[:, ls] = out.astype(BF16)


def _swa_ctx_kernel(sink_ref, q_ref, k_ref, v_ref, o_ref):
    @pl.when(pl.program_id(0) >= N_CTX_TILES)
    def _():
        o_ref[...] = jnp.zeros_like(o_ref)

    @pl.when(pl.program_id(0) < N_CTX_TILES)
    def _():
        _gqa_heads(q_ref, sink_ref, [k_ref[...].astype(BF16)], [v_ref[...].astype(BF16)], [None], o_ref)


def _swa_lat_kernel(sink_ref, o_all_ref, q_ref, k_ref, v_ref, kc_ref, vc_ref, o_ref):
    del o_all_ref
    n = pl.program_id(1)
    n_win = 3 * WINDOW
    start = jnp.clip((n - 1) * WINDOW, 0, DEC_SEQ - n_win)
    k0 = pl.multiple_of(start, WINDOW)
    n_keys = n_win + PAST_LEN
    col = lax.broadcasted_iota(jnp.int32, (WINDOW, n_keys), 1)
    qpos = n * WINDOW + lax.broadcasted_iota(jnp.int32, (WINDOW, n_keys), 0)
    ok = (col >= n_win) | (jnp.abs(qpos - (start + col)) <= WINDOW)
    ks = jnp.concatenate([k_ref[pl.ds(k0, n_win), :], kc_ref[0]], axis=0).astype(BF16)
    vs = jnp.concatenate([v_ref[pl.ds(k0, n_win), :], vc_ref[0]], axis=0).astype(BF16)
    _gqa_heads(q_ref, sink_ref, [ks], [vs], [ok], o_ref)


def _swa_attention(q, k, v, kc, vc, sink):
    nq, nkv = C_HEADS * HEAD_DIM, C_KV_HEADS * HEAD_DIM
    smem = pl.BlockSpec(memory_space=pltpu.SMEM)
    o_ctx = pl.pallas_call(
        _swa_ctx_kernel,
        out_shape=jax.ShapeDtypeStruct((N_TOK, nq), BF16),
        grid=(N_TILES,),
        in_specs=[smem, _ctx_tile_spec(nq), _ctx_tile_spec(nkv), _ctx_tile_spec(nkv)],
        out_specs=_tile_spec(nq),
        compiler_params=_cparams("arbitrary"),
        name="swa_ctx",
    )(sink, q, k, v)
    nb = DEC_SEQ // WINDOW
    ctx_seqs = N_CTX_TOK // DEC_SEQ
    lat_row = lambda b, n: (N_CTX_TOK // WINDOW + b * nb + n, 0)
    return pl.pallas_call(
        _swa_lat_kernel,
        out_shape=jax.ShapeDtypeStruct((N_TOK, nq), BF16),
        grid=(DEC_BATCH, nb),
        in_specs=[
            smem,
            pl.BlockSpec(memory_space=pl.ANY),
            pl.BlockSpec((WINDOW, nq), lat_row),
            pl.BlockSpec((DEC_SEQ, nkv), lambda b, n: (ctx_seqs + b, 0)),
            pl.BlockSpec((DEC_SEQ, nkv), lambda b, n: (ctx_seqs + b, 0)),
            pl.BlockSpec((1, PAST_LEN, nkv), lambda b, n: (b, 0, 0)),
            pl.BlockSpec((1, PAST_LEN, nkv), lambda b, n: (b, 0, 0)),
        ],
        out_specs=pl.BlockSpec((WINDOW, nq), lat_row),
        input_output_aliases={1: 0},
        compiler_params=_cparams("arbitrary", "arbitrary"),
        name="swa_lat",
    )(sink, o_ctx, q, k, v, kc, vc)


def _route_init(cnt_ref, lg_scr):
    @pl.when(pl.program_id(0) == 0)
    def _():
        cnt_ref[...] = jnp.zeros_like(cnt_ref)
        lg_scr[...] = jnp.zeros_like(lg_scr)


def _moe_input(x, g_ref, sh_ref, sc_ref, rw_ref, rb_ref, tri_ref, h_ref, route_ref, cnt_ref, lg_scr):
    h = _norm_mod(x, g_ref[...], sh_ref[0], sc_ref[0])
    h_ref[...] = _pack_bf16_pair(h[:, 0:D_MODEL // 2], h[:, D_MODEL // 2:])
    w_hi, w_lo = _split(rw_ref[...])
    h_hi, h_lo = _split(h)
    new_logits = (_dot_nt(w_hi, h_hi) + (_dot_nt(w_hi, h_lo) + _dot_nt(w_lo, h_hi)) + rb_ref[...])[0:N_EXPERTS]
    logits = lg_scr[...]
    live = jnp.where(pl.program_id(0) > 0, 1.0, 0.0)
    row = lax.broadcasted_iota(jnp.int32, logits.shape, 0)
    cur = logits
    picked, vals, idxs = [], [], []
    for _ in range(TOP_K):
        m = jnp.max(cur, axis=0, keepdims=True)
        idx = jnp.min(jnp.where(cur == m, row, N_EXPERTS), axis=0, keepdims=True)
        sel = row == idx
        picked.append(sel)
        vals.append(m)
        idxs.append(idx)
        cur = jnp.where(sel, -jnp.inf, cur)
    exps = [jnp.exp(v - vals[0]) for v in vals]
    den = functools.reduce(lambda a, b: a + b, exps)
    chosen = functools.reduce(lambda a, b: a + b, [jnp.where(s, 1.0, 0.0) for s in picked])
    cnt = cnt_ref[...]
    before = jnp.concatenate([cnt] * (TILE // LANES), axis=1) + _dot(chosen.astype(BF16), tri_ref[...])
    cnt_ref[...] = cnt + live * jnp.sum(chosen, axis=1, keepdims=True)
    ranks = [jnp.sum(jnp.where(s, before, 0.0), axis=0, keepdims=True) for s in picked]
    fields = [i.astype(F32) for i in idxs] + [e / den for e in exps] + ranks
    out_row = lax.broadcasted_iota(jnp.int32, route_ref.shape, 0)
    route = jnp.zeros(route_ref.shape, F32)
    for j, f in enumerate(fields):
        route = jnp.where(out_row == j, f, route)
    route_ref[...] = route
    lg_scr[...] = new_logits


ROUTE_ROWS = 16
OUT_STEPS = N_TILES + 1


def _last_tile(i):
    return jnp.minimum(i, N_TILES - 1)


def _out_tile_spec(width):
    return pl.BlockSpec((TILE, width), lambda i: (_last_tile(i), 0))


def _out_mod_spec(col):
    return pl.BlockSpec((1, 1, D_MODEL), lambda i: (_last_tile(i), 0, col))


def _moe_input_specs():
    return [_full_spec((1, D_MODEL)), _out_mod_spec(3), _out_mod_spec(4),
            _full_spec((ROUTER_PAD, D_MODEL)), _full_spec((ROUTER_PAD, TILE)), _full_spec((TILE, TILE))]


def _mixer_out_shapes():
    shapes = (jax.ShapeDtypeStruct((N_TOK, D_MODEL), F32),
              jax.ShapeDtypeStruct((N_TOK, D_MODEL // 2), U32),
              jax.ShapeDtypeStruct((ROUTE_ROWS, N_TOK), F32),
              jax.ShapeDtypeStruct((N_EXPERTS, LANES), F32))
    specs = (_out_tile_spec(D_MODEL), _out_tile_spec(D_MODEL // 2),
             pl.BlockSpec((ROUTE_ROWS, TILE), lambda i: (0, jnp.maximum(i - 1, 0))),
             _full_spec((N_EXPERTS, LANES)))
    return shapes, specs


def _even_out_kernel(y_ref, u_ref, o_ref, x_ref, gate_ref, d_ref, gw_ref, gb_ref, w_ref,
                     g2_ref, sh_ref, sc_ref, rw_ref, rb_ref, tri_ref, xo_ref, h_ref, route_ref, cnt_ref, lg_scr):
    _route_init(cnt_ref, lg_scr)
    yy = y_ref[...] + d_ref[...] * u_ref[...]
    g = jax.nn.gelu(yy)
    a = g * jax.nn.sigmoid(_dot(g.astype(BF16), gw_ref[...]) + gb_ref[...])
    mix = _dot(a.astype(BF16), w_ref[0:A_WIDTH, :]) + _dot(o_ref[...], w_ref[A_WIDTH:, :])
    x = x_ref[...] + gate_ref[0] * mix
    xo_ref[...] = x
    _moe_input(x, g2_ref, sh_ref, sc_ref, rw_ref, rb_ref, tri_ref, h_ref, route_ref, cnt_ref, lg_scr)


def _even_out(y_t, u_t, o, x, modt, d_skip, glu_w, glu_b, w_out, g2, rw, rb, tri):
    tm_spec = pl.BlockSpec((TILE, A_WIDTH), lambda i: (0, _last_tile(i)))
    out_shape, out_specs = _mixer_out_shapes()
    return pl.pallas_call(
        _even_out_kernel,
        out_shape=out_shape,
        grid=(OUT_STEPS,),
        in_specs=[
            tm_spec, tm_spec, _out_tile_spec(B_WIDTH), _out_tile_spec(D_MODEL), _out_mod_spec(2),
            _full_spec((1, A_WIDTH)), _full_spec((A_WIDTH, A_WIDTH)), _full_spec((1, A_WIDTH)),
            _full_spec((A_WIDTH + B_WIDTH, D_MODEL)),
        ] + _moe_input_specs(),
        out_specs=out_specs,
        scratch_shapes=[pltpu.VMEM((N_EXPERTS, TILE), F32)],
        compiler_params=_cparams("arbitrary"),
        name="even_out",
    )(y_t, u_t, o, x, modt, d_skip, glu_w, glu_b, w_out, g2, modt, modt, rw, rb, tri)


def _rope(x, cos, sin):
    lane = lax.broadcasted_iota(jnp.int32, x.shape, 1)
    first = (lane % (HEAD_DIM // 2)) < (HEAD_DIM // 4)
    partner = jnp.where(first, pltpu.roll(x, LANES - HEAD_DIM // 4, 1), pltpu.roll(x, HEAD_DIM // 4, 1))
    return x * cos + partner * sin


def _odd_in_kernel(n_comb, slot, n_prev, x_ref, *refs):
    comb, refs = refs[:n_comb], refs[n_comb:]
    g_ref, sh_ref, sc_ref, w_ref, bd_ref, qn_ref, kn_ref, cos_ref, sin_ref = refs[:9]
    outs = refs[9 + n_prev:]
    x = _combined(x_ref, comb)
    if n_comb:
        outs[0][...] = x
        outs = outs[1:]
    q_ref, kr_ref, v_ref, kcache_ref, vcache_ref = outs
    i = pl.program_id(0)
    h = _norm_mod(x, g_ref[...], sh_ref[0], sc_ref[0]).astype(BF16)
    bd2 = bd_ref[...]
    lat = i >= N_CTX_TILES
    cos = jnp.where(lat, cos_ref[...], 1.0)
    sin = jnp.where(lat, sin_ref[...], 0.0)
    nq = C_HEADS * HEAD_DIM
    z = _dot(h, w_ref[...])
    for s in range(nq // LANES):
        q = _head_rms(z[:, s * LANES:(s + 1) * LANES], bd2, qn_ref[...])
        q_ref[:, s * LANES:(s + 1) * LANES] = (_rope(q, cos, sin) * ATT_SCALE).astype(BF16)
    k = _head_rms(z[:, nq:nq + LANES], bd2, kn_ref[...])
    v = z[:, nq + LANES:]
    kr_ref[...] = _rope(k, cos, sin)
    v_ref[...] = v
    _cache_store(slot, N_ODD, kcache_ref, k)
    _cache_store(slot, N_ODD, vcache_ref, v)


def _odd_in(x, comb, prev_caches, slot, g, modt, w, bd2, qn, kn, cos, sin):
    nq, nkv = C_HEADS * HEAD_DIM, C_KV_HEADS * HEAD_DIM
    lat_spec = pl.BlockSpec((TILE, LANES), lambda i: (jnp.maximum(i - N_CTX_TILES, 0) % LAT_TILES_PER_SEQ, 0))
    cache_shape, cache_spec = _cache_specs(slot, N_ODD, nkv)
    out_shape = [
        jax.ShapeDtypeStruct((N_TOK, nq), BF16),
        jax.ShapeDtypeStruct((N_TOK, nkv), F32),
        jax.ShapeDtypeStruct((N_TOK, nkv), F32),
        cache_shape, cache_shape,
    ]
    out_specs = [_tile_spec(nq), _tile_spec(nkv), _tile_spec(nkv), cache_spec, cache_spec]
    if comb:
        out_shape.insert(0, jax.ShapeDtypeStruct((N_TOK, D_MODEL), F32))
        out_specs.insert(0, _tile_spec(D_MODEL))
    n_in = 1 + len(comb) + 9
    aliases = {n_in + j: len(out_shape) - 2 + j for j in range(len(prev_caches))}
    return pl.pallas_call(
        functools.partial(_odd_in_kernel, len(comb), slot, len(prev_caches)),
        out_shape=tuple(out_shape),
        grid=(N_TILES,),
        in_specs=[_tile_spec(D_MODEL)] + (_comb_specs() if comb else []) + [
            _full_spec((1, D_MODEL)), _mod_spec(0), _mod_spec(1),
            _full_spec((D_MODEL, nq + 2 * nkv)), _full_spec((2 * LANES, LANES)),
            _full_spec((1, LANES)), _full_spec((1, LANES)), lat_spec, lat_spec,
        ] + [pl.BlockSpec(memory_space=pl.ANY)] * len(prev_caches),
        out_specs=tuple(out_specs),
        input_output_aliases=aliases,
        compiler_params=_cparams("arbitrary"),
        name="odd_in",
    )(x, *comb, g, modt, modt, w, bd2, qn, kn, cos, sin, *prev_caches)


def _rope_tables():
    nf = HEAD_DIM // 4
    inv = ROPE_BASE ** (-jnp.arange(nf, dtype=F32) / nf)
    t = jnp.arange(DEC_SEQ)
    pos = jnp.stack([t // GRID_W, t % GRID_W], axis=-1).astype(F32)
    ang = pos[:, :, None] * inv
    cos, sin = jnp.cos(ang), jnp.sin(ang)
    cos_h = jnp.stack([cos, cos], axis=2).reshape(DEC_SEQ, HEAD_DIM)
    sin_h = jnp.stack([-sin, sin], axis=2).reshape(DEC_SEQ, HEAD_DIM)
    return jnp.tile(cos_h, (1, 2)), jnp.tile(sin_h, (1, 2))


def _odd_out_kernel(o_ref, x_ref, gate_ref, w_ref, g2_ref, sh_ref, sc_ref, rw_ref, rb_ref, tri_ref,
                    xo_ref, h_ref, route_ref, cnt_ref, lg_scr):
    _route_init(cnt_ref, lg_scr)
    x = x_ref[...] + gate_ref[0] * _dot(o_ref[...], w_ref[...])
    xo_ref[...] = x
    _moe_input(x, g2_ref, sh_ref, sc_ref, rw_ref, rb_ref, tri_ref, h_ref, route_ref, cnt_ref, lg_scr)


def _odd_out(o, x, modt, w_o, g2, rw, rb, tri):
    nq = C_HEADS * HEAD_DIM
    out_shape, out_specs = _mixer_out_shapes()
    return pl.pallas_call(
        _odd_out_kernel,
        out_shape=out_shape,
        grid=(OUT_STEPS,),
        in_specs=[_out_tile_spec(nq), _out_tile_spec(D_MODEL), _out_mod_spec(2), _full_spec((nq, D_MODEL))]
        + _moe_input_specs(),
        out_specs=out_specs,
        scratch_shapes=[pltpu.VMEM((N_EXPERTS, TILE), F32)],
        compiler_params=_cparams("arbitrary"),
        name="odd_out",
    )(o, x, modt, w_o, g2, modt, modt, rw, rb, tri)


def _pack_bf16_pair(lo, hi):
    lo_bits = lax.bitcast_convert_type(lo.astype(BF16).astype(F32), U32)
    hi_bits = lax.bitcast_convert_type(hi.astype(BF16).astype(F32), U32)
    return (hi_bits & jnp.uint32(0xFFFF0000)) | (lo_bits >> 16)


def _unpack_bf16_pair(packed):
    lo = lax.bitcast_convert_type(packed << 16, F32).astype(BF16)
    hi = lax.bitcast_convert_type(packed & jnp.uint32(0xFFFF0000), F32).astype(BF16)
    return lo, hi


def _moe_expert_kernel(layer, be_ref, nb_ref, first_ref, slot_ref, nxt_ref, src_cur_ref, src_nxt_ref,
                       h_ref, wgu_hbm, bgu_ref, wd_hbm, bd_ref, o_ref, xg0, xg1, wgu_buf, wd_buf, wsem):
    i = pl.program_id(0)
    nb = nb_ref[0]
    half = D_MODEL // 2
    xg = (xg0, xg1)

    def gather(idx_ref, u, dst):
        for r in range(MOE_TM):
            dst[pl.ds(r, 1), :] = h_ref[pl.ds(idx_ref[0, 0, u * MOE_TM + r], 1), :]

    def weight_copies(e, slot):
        return (pltpu.make_async_copy(wgu_hbm.at[layer, e], wgu_buf.at[slot], wsem.at[0, slot]),
                pltpu.make_async_copy(wd_hbm.at[layer, e], wd_buf.at[slot], wsem.at[1, slot]))

    @pl.when(i == 0)
    def _():
        for cp in weight_copies(be_ref[0], 0):
            cp.start()
        gather(src_cur_ref, 0, xg0)

    for u in range(MOE_STEP_BLOCKS):
        blk = i * MOE_STEP_BLOCKS + u
        rows = pl.ds(u * MOE_TM, MOE_TM)
        x_cur, x_nxt = xg[u % 2], xg[(u + 1) % 2]

        @pl.when(blk < nb)
        def _():
            ws = slot_ref[blk]
            e = be_ref[blk]

            @pl.when(first_ref[blk] == 1)
            def _():
                for cp in weight_copies(e, ws):
                    cp.wait()

                @pl.when(nxt_ref[blk] >= 0)
                def _():
                    for cp in weight_copies(nxt_ref[blk], 1 - ws):
                        cp.start()

            x_lo, x_hi = _unpack_bf16_pair(x_cur[...])
            gu = (_dot(x_lo, wgu_buf[ws, 0:half, :].astype(BF16)) + _dot(x_hi, wgu_buf[ws, half:, :].astype(BF16))
                  + bgu_ref[0, pl.ds(e, 1), :])
            if u + 1 < MOE_STEP_BLOCKS:
                gather(src_cur_ref, u + 1, x_nxt)
            else:
                gather(src_nxt_ref, 0, x_nxt)
            g = jnp.minimum(gu[:, 0:D_EXPERT], SWIGLU_LIMIT)
            lin = jnp.clip(gu[:, D_EXPERT:], -SWIGLU_LIMIT, SWIGLU_LIMIT)
            act = g * jax.nn.sigmoid(SWIGLU_ALPHA * g) * (lin + 1.0)
            o_ref[rows, :] = _dot(act.astype(BF16), wd_buf[ws].astype(BF16)) + bd_ref[0, pl.ds(e, 1), :]

        @pl.when(blk >= nb)
        def _():
            o_ref[rows, :] = jnp.zeros((MOE_TM, D_MODEL), F32)


def _moe_experts(layer, plan, h_packed, w_gu, b_gu, w_down, b_down):
    block_e, n_used, first, slot, nxt, row_tok = plan
    half = D_MODEL // 2
    steps = MOE_BLOCKS // MOE_STEP_BLOCKS
    step_rows = MOE_STEP_BLOCKS * MOE_TM
    idx_spec = lambda f: pl.BlockSpec((1, 1, step_rows), f, memory_space=pltpu.SMEM)
    hbm = pl.BlockSpec(memory_space=pl.ANY)
    grid_spec = pltpu.PrefetchScalarGridSpec(
        num_scalar_prefetch=5,
        grid=(steps,),
        in_specs=[
            idx_spec(lambda i, *_: (i, 0, 0)),
            idx_spec(lambda i, *_: (jnp.minimum(i + 1, steps - 1), 0, 0)),
            pl.BlockSpec((N_TOK, half), lambda i, *_: (0, 0), pipeline_mode=pl.Buffered(1)),
            hbm,
            pl.BlockSpec((1, N_EXPERTS, 2 * D_EXPERT), lambda i, *_: (layer, 0, 0)),
            hbm,
            pl.BlockSpec((1, N_EXPERTS, D_MODEL), lambda i, *_: (layer, 0, 0)),
        ],
        out_specs=pl.BlockSpec((step_rows, D_MODEL), lambda i, *_: (i, 0)),
        scratch_shapes=[
            pltpu.VMEM((MOE_TM, half), U32), pltpu.VMEM((MOE_TM, half), U32),
            pltpu.VMEM((2, D_MODEL, 2 * D_EXPERT), F32), pltpu.VMEM((2, D_EXPERT, D_MODEL), F32),
            pltpu.SemaphoreType.DMA((2, 2)),
        ],
    )
    rows = row_tok.reshape(steps, 1, step_rows)
    return pl.pallas_call(
        functools.partial(_moe_expert_kernel, layer),
        out_shape=jax.ShapeDtypeStruct((MOE_ROWS, D_MODEL), F32),
        grid_spec=grid_spec,
        compiler_params=pltpu.CompilerParams(dimension_semantics=("arbitrary",), vmem_limit_bytes=MOE_VMEM_LIMIT),
        name="moe_experts",
    )(block_e, n_used, first, slot, nxt, rows, rows, h_packed, w_gu, b_gu, w_down, b_down)


def _moe_combine_kernel(x_ref, *refs):
    ctx_ref, lat_ref = refs[N_COMB:]
    x = _combined(x_ref, refs[:N_COMB])

    @pl.when(pl.program_id(0) < N_CTX_TILES)
    def _():
        ctx_ref[...] = x

    @pl.when(pl.program_id(0) >= N_CTX_TILES)
    def _():
        lat_ref[...] = x


def _moe_combine(x, comb):
    return pl.pallas_call(
        _moe_combine_kernel,
        out_shape=(jax.ShapeDtypeStruct((N_CTX_TOK, D_MODEL), F32), jax.ShapeDtypeStruct((N_LAT_TOK, D_MODEL), F32)),
        grid=(N_TILES,),
        in_specs=[_tile_spec(D_MODEL)] + _comb_specs(),
        out_specs=(pl.BlockSpec((TILE, D_MODEL), lambda i: (jnp.minimum(i, N_CTX_TILES - 1), 0)),
                   pl.BlockSpec((TILE, D_MODEL), lambda i: (jnp.maximum(i - N_CTX_TILES, 0), 0))),
        compiler_params=_cparams("arbitrary"),
        name="moe_combine",
    )(x, *comb)


def _lut(idx, table):
    n = table.shape[0]
    hit = idx[:, None] == jnp.arange(n, dtype=idx.dtype)[None, :]
    return jnp.sum(jnp.where(hit, table[None, :], 0), axis=1)


def _moe_route(route, cnt):
    i32 = jnp.int32
    route = route.T
    gates = route[:, TOP_K:2 * TOP_K]
    flat_e = route[:, 0:TOP_K].astype(i32).reshape(N_ASG)
    rank = route[:, 2 * TOP_K:3 * TOP_K].astype(i32).reshape(N_ASG)
    asg = jnp.arange(N_ASG, dtype=i32)
    experts = jnp.arange(N_EXPERTS, dtype=i32)
    counts = cnt[:, 0].astype(i32)
    pad_counts = (counts + MOE_TM - 1) // MOE_TM * MOE_TM
    pad_end = jnp.cumsum(pad_counts)
    n_used = pad_end[-1] // MOE_TM
    dest = _lut(flat_e, pad_end - pad_counts) + rank
    fill_end = jnp.cumsum(pad_counts - counts)
    filler = jnp.arange(MOE_ROWS - N_ASG, dtype=i32)
    fill_e = jnp.sum((filler[:, None] >= fill_end[None, :]).astype(i32), axis=1)
    keys = jnp.concatenate([flat_e * ASG_STRIDE + asg, fill_e * ASG_STRIDE + (ASG_STRIDE - 1)])
    low = jnp.sort(keys) % ASG_STRIDE
    row_tok = jnp.where(low < N_ASG, low // TOP_K, 0)
    blk = jnp.arange(MOE_BLOCKS, dtype=i32)
    block_e = jnp.sum((blk[:, None] * MOE_TM >= pad_end[None, :]).astype(i32), axis=1)
    last_e = jnp.max(jnp.where(counts > 0, experts, 0))
    block_e = jnp.where(blk < n_used, jnp.minimum(block_e, N_EXPERTS - 1), last_e)
    first = jnp.concatenate([jnp.ones((1,), i32), (block_e[1:] != block_e[:-1]).astype(i32)])
    slot = (jnp.cumsum(first) - 1) % 2
    later = (experts[None, :] > experts[:, None]) & (counts[None, :] > 0)
    nxt_of = jnp.min(jnp.where(later, experts[None, :], N_EXPERTS), axis=1)
    nxt = _lut(block_e, jnp.where(nxt_of < N_EXPERTS, nxt_of, -1))
    plan = (block_e, n_used.reshape(1).astype(i32), first, slot.astype(i32), nxt.astype(i32), row_tok)
    return gates, dest.reshape(N_TOK, TOP_K), plan


def _moe(layer, h_packed, route, cnt, modt, w_gu, b_gu, w_down, b_down):
    gates, dest, plan = _moe_route(route, cnt)
    y_rows = _moe_experts(layer, plan, h_packed, w_gu, b_gu, w_down, b_down)
    return (modt, gates) + tuple(y_rows[dest[:, k]] for k in range(TOP_K))


def kernel(x_prompt, x_sample, cache_nat_k, cache_nat_v, cache_swa_k, cache_swa_v, state_ssm, c, c_ctx,
           norm1_g, norm2_g, mod_w, mod_b,
           ab_w_in, ab_w_out, ssm_lam_re, ssm_lam_im, ssm_log_dt, ssm_b_re, ssm_b_im, ssm_c_re, ssm_c_im,
           ssm_d, ssm_glu_w, ssm_glu_b, nat_qn, nat_kn, nat_rpb,
           swa_w_qkv, swa_w_o, swa_qn, swa_kn, swa_sink,
           moe_router_w, moe_router_b, moe_w_gu, moe_b_gu, moe_w_down, moe_b_down):
    x = jnp.concatenate([x_prompt.reshape(N_CTX_TOK, D_MODEL), x_sample.reshape(N_LAT_TOK, D_MODEL)], axis=0)
    cond = jnp.zeros((SUBLANES, D_MODEL), F32).at[0].set(c_ctx).at[1:1 + DEC_BATCH].set(c)
    mod = _modulation(cond, mod_w, mod_b)
    tile_row = np.concatenate([np.zeros(N_CTX_TILES, np.int32),
                               1 + np.arange(N_TILES - N_CTX_TILES, dtype=np.int32) // LAT_TILES_PER_SEQ])
    head_gain = lambda gn: jnp.tile(gn, 2).reshape(1, LANES)
    bd = np.kron(np.eye(2, dtype=np.float32), np.full((HEAD_DIM, HEAD_DIM), 1.0 / HEAD_DIM, np.float32))
    bd2 = jnp.asarray(np.concatenate([bd, bd], axis=0), BF16)
    rope_cos, rope_sin = _rope_tables()
    tri = jnp.asarray(np.triu(np.ones((TILE, TILE), np.float32), 1), BF16)
    router_w = jnp.pad(moe_router_w, ((0, 0), (0, 0), (0, ROUTER_PAD - N_EXPERTS))).transpose(0, 2, 1)
    router_b = jnp.broadcast_to(jnp.pad(moe_router_b, ((0, 0), (0, ROUTER_PAD - N_EXPERTS)))[:, :, None],
                                (DEPTH, ROUTER_PAD, TILE))
    nkv = C_KV_HEADS * HEAD_DIM

    ssm_out = []
    nat_caches, swa_caches = (), ()
    comb = ()
    for l in range(DEPTH):
        modt = mod[l][tile_row].reshape(N_TILES, 1, N_MOD * D_MODEL)
        g1 = norm1_g[l].reshape(1, D_MODEL)
        moe_in = (norm2_g[l].reshape(1, D_MODEL), router_w[l], router_b[l], tri)
        i = l // 2
        if l % 2 == 0:
            outs = _even_in(x, comb, nat_caches, i, g1, modt, ab_w_in[i].astype(BF16), bd2,
                            head_gain(nat_qn[i]), head_gain(nat_kn[i]))
            if comb:
                x, outs = outs[0], outs[1:]
            u_t, q, k, v = outs[:4]
            nat_caches = tuple(outs[4:])
            a_b, w_bd, c_bd = _s5_params(ssm_lam_re[i], ssm_lam_im[i], ssm_log_dt[i], ssm_b_re[i], ssm_b_im[i],
                                         ssm_c_re[i], ssm_c_im[i])
            st = state_ssm[:, i].reshape(DEC_BATCH, 2, 2, SSM_LANES).transpose(1, 2, 0, 3)
            s0 = jnp.zeros((2, 2, N_TILES, SSM_LANES), F32)
            first = N_CTX_TILES + LAT_TILES_PER_SEQ * np.arange(DEC_BATCH)
            s0 = s0.at[0, :, first].set(st[0].transpose(1, 0, 2))
            s0 = s0.at[1, :, first + LAT_TILES_PER_SEQ - 1].set(st[1].transpose(1, 0, 2))
            y_t, fin = _s5_scan(u_t.reshape(TILE, N_TILES, A_WIDTH), a_b, w_bd, c_bd, s0)
            o = _nat_attention(q, k, v,
                               cache_nat_k[:, i].reshape(DEC_BATCH, PAST_LEN, B_WIDTH),
                               cache_nat_v[:, i].reshape(DEC_BATCH, PAST_LEN, B_WIDTH),
                               _nat_bias(nat_rpb[i]))
            x, h_packed, route, cnt = _even_out(y_t.reshape(TILE, N_TILES * A_WIDTH), u_t, o, x, modt,
                                            ssm_d[i].reshape(1, A_WIDTH), ssm_glu_w[i].astype(BF16),
                                            ssm_glu_b[i].reshape(1, A_WIDTH), ab_w_out[i].astype(BF16), *moe_in)
            ssm_out.append(fin[:, :, :N_CTX_TILES].transpose(2, 0, 1, 3).reshape(BATCH, 2, 2, A_GROUPS, A_STATE))
        else:
            outs = _odd_in(x, comb, swa_caches, i, g1, modt, swa_w_qkv[i].astype(BF16), bd2,
                           head_gain(swa_qn[i]), head_gain(swa_kn[i]), rope_cos, rope_sin)
            if comb:
                x, outs = outs[0], outs[1:]
            q, k_rot, v = outs[:3]
            swa_caches = tuple(outs[3:])
            o = _swa_attention(q, k_rot, v,
                               cache_swa_k[:, i].reshape(DEC_BATCH, PAST_LEN, nkv),
                               cache_swa_v[:, i].reshape(DEC_BATCH, PAST_LEN, nkv), swa_sink[i])
            x, h_packed, route, cnt = _odd_out(o, x, modt, swa_w_o[i].astype(BF16), *moe_in)
        comb = _moe(l, h_packed, route, cnt, modt, moe_w_gu, moe_b_gu, moe_w_down, moe_b_down)

    y_ctx, y_lat = _moe_combine(x, comb)
    y_prompt = y_ctx.reshape(BATCH, SEQ, D_MODEL)
    y_sample = y_lat.reshape(DEC_BATCH, DEC_SEQ, D_MODEL)
    nat_shape = (BATCH, N_EVEN, SEQ, B_HEADS, HEAD_DIM)
    swa_shape = (BATCH, N_ODD, SEQ, C_KV_HEADS, HEAD_DIM)
    return (y_prompt, y_sample,
            nat_caches[0].reshape(nat_shape), nat_caches[1].reshape(nat_shape),
            swa_caches[0].reshape(swa_shape), swa_caches[1].reshape(swa_shape),
            jnp.stack(ssm_out, axis=1))
```

```python
import functools

import jax
import jax.numpy as jnp
import numpy as np
from jax import lax
from jax.experimental import pallas as pl
from jax.experimental.pallas import tpu as pltpu

F32 = jnp.float32
BF16 = jnp.bfloat16
U32 = jnp.uint32

D_MODEL = 1024
BATCH = 32
SEQ = 256
DEPTH = 4
DEC_BATCH = 2
DEC_SEQ = 1024
PAST_LEN = 256
GRID_W = 64
HEAD_DIM = 64
N_MOD = 6
N_EVEN = (DEPTH + 1) // 2
N_ODD = DEPTH // 2
A_WIDTH = 512
A_GROUP_CH = 16
A_GROUPS = 32
A_STATE = 64
B_HEADS = 8
B_WIDTH = 512
WIN_R_MAX = 8
WIN_C = 16
C_HEADS = 16
C_KV_HEADS = 2
C_GROUP = 8
WINDOW = 128
ROPE_BASE = 10000.0
N_EXPERTS = 32
TOP_K = 4
D_EXPERT = 1024
SWIGLU_LIMIT = 7.0
SWIGLU_ALPHA = 1.702
EPS = 1e-6
NEG_INF = -1e30

LANES = 128
SUBLANES = 8
TILE = 256
N_CTX_TOK = BATCH * SEQ
N_LAT_TOK = DEC_BATCH * DEC_SEQ
N_TOK = N_CTX_TOK + N_LAT_TOK
N_TILES = N_TOK // TILE
N_CTX_TILES = N_CTX_TOK // TILE
LAT_TILES_PER_SEQ = DEC_SEQ // TILE
SSM_LANES = A_GROUPS * A_STATE
SSM_SLABS = A_WIDTH // LANES
SSM_SLAB_STATES = SSM_LANES // SSM_SLABS
SCAN_ROWS = 8
SCAN_GROUPS = N_TILES // SCAN_ROWS
SCAN_CHUNK = 64
SCAN_LANE_PARTS = 2
MOE_TM = 256
MOE_STEP_BLOCKS = 4
N_ASG = N_TOK * TOP_K
ASG_STRIDE = 1 << 16
MOE_BLOCKS = N_ASG // MOE_TM + N_EXPERTS
MOE_ROWS = MOE_BLOCKS * MOE_TM
ROUTER_PAD = LANES
VMEM_LIMIT = 56 * 1024 * 1024
MOE_VMEM_LIMIT = 60 * 1024 * 1024
ATT_SCALE = HEAD_DIM ** -0.5


def _cparams(*sem):
    return pltpu.CompilerParams(dimension_semantics=sem, vmem_limit_bytes=VMEM_LIMIT)


def _dot(a, b):
    return jnp.dot(a, b, preferred_element_type=F32)


def _dot_nt(a, b):
    return lax.dot_general(a, b, (((1,), (1,)), ((), ())), preferred_element_type=F32)


def _split(a):
    hi = a.astype(BF16)
    lo = (a - hi.astype(F32)).astype(BF16)
    return hi, lo


def _dot3(a, b):
    a_hi, a_lo = _split(a)
    b_hi, b_lo = _split(b)
    return _dot(a_hi, b_hi) + (_dot(a_hi, b_lo) + _dot(a_lo, b_hi))


def _silu(x):
    return x * jax.nn.sigmoid(x)


def _norm_mod(x, g, shift, scale):
    y = x * lax.rsqrt(jnp.mean(x * x, axis=-1, keepdims=True) + EPS)
    return (y * g) * (1.0 + scale) + shift


def _head_rms(x, bd2, gain):
    sq_hi, sq_lo = _split(x * x)
    ms = _dot(jnp.concatenate([sq_hi, sq_lo], axis=1), bd2)
    return x * lax.rsqrt(ms + EPS) * gain


def _combined(x_ref, comb_refs):
    x = x_ref[...]
    if not comb_refs:
        return x
    gmod_ref, gates_ref = comb_refs[0], comb_refs[1]
    gates = gates_ref[...]
    acc = gates[:, 0:1] * comb_refs[2][...]
    for k in range(1, TOP_K):
        acc = acc + gates[:, k:k + 1] * comb_refs[2 + k][...]
    return x + gmod_ref[0] * acc


def _comb_specs():
    return [_mod_spec(5), _tile_spec(TOP_K)] + [_tile_spec(D_MODEL)] * TOP_K


N_COMB = 2 + TOP_K


def _lane_half(shape, half):
    lane = lax.broadcasted_iota(jnp.int32, shape, len(shape) - 1)
    return (lane < HEAD_DIM) if half == 0 else (lane >= HEAD_DIM)


def _mod_kernel(cond_ref, w_ref, b_ref, o_ref):
    o_ref[0] = _dot3(_silu(cond_ref[...]), w_ref[0]) + b_ref[0]


def _modulation(cond, mod_w, mod_b):
    nc = N_MOD
    return pl.pallas_call(
        _mod_kernel,
        out_shape=jax.ShapeDtypeStruct((DEPTH, SUBLANES, N_MOD * D_MODEL), F32),
        grid=(DEPTH, nc),
        in_specs=[
            pl.BlockSpec((SUBLANES, D_MODEL), lambda l, c: (0, 0)),
            pl.BlockSpec((1, D_MODEL, D_MODEL), lambda l, c: (l, 0, c)),
            pl.BlockSpec((1, 1, D_MODEL), lambda l, c: (l, 0, c)),
        ],
        out_specs=pl.BlockSpec((1, SUBLANES, D_MODEL), lambda l, c: (l, 0, c)),
        compiler_params=_cparams("arbitrary", "arbitrary"),
        name="modulation",
    )(cond, mod_w, mod_b.reshape(DEPTH, 1, N_MOD * D_MODEL))


def _mod_spec(col):
    return pl.BlockSpec((1, 1, D_MODEL), lambda i: (i, 0, col))


def _tile_spec(width):
    return pl.BlockSpec((TILE, width), lambda i: (i, 0))


def _ctx_tile_spec(width):
    return pl.BlockSpec((TILE, width), lambda i: (jnp.minimum(i, N_CTX_TILES - 1), 0))


def _full_spec(shape):
    nd = len(shape)
    return pl.BlockSpec(shape, lambda i: (0,) * nd)


def _cache_store(slot, n_slots, ref, value):
    @pl.when(pl.program_id(0) < N_CTX_TILES)
    def _():
        if slot == 0:
            ref[0, 0] = value
            for other in range(1, n_slots):
                ref[0, other] = jnp.zeros_like(value)
        else:
            ref[0, 0] = value


def _cache_specs(slot, n_slots, width):
    shape = jax.ShapeDtypeStruct((BATCH, n_slots, SEQ, width), F32)
    seq = lambda i: jnp.minimum(i, N_CTX_TILES - 1)
    if slot == 0:
        return shape, pl.BlockSpec((1, n_slots, SEQ, width), lambda i: (seq(i), 0, 0, 0))
    return shape, pl.BlockSpec((1, 1, SEQ, width), lambda i: (seq(i), slot, 0, 0))


def _even_in_kernel(n_comb, slot, n_prev, x_ref, *refs):
    comb, refs = refs[:n_comb], refs[n_comb:]
    g_ref, sh_ref, sc_ref, w_ref, bd_ref, qn_ref, kn_ref = refs[:7]
    outs = refs[7 + n_prev:]
    x = _combined(x_ref, comb)
    if n_comb:
        outs[0][...] = x
        outs = outs[1:]
    u_ref, q_ref, k_ref, v_ref, kc_ref, vc_ref = outs
    h = _norm_mod(x, g_ref[...], sh_ref[0], sc_ref[0]).astype(BF16)
    bd2 = bd_ref[...]
    z = _dot(h, w_ref[...])
    u_ref[...] = z[:, 0:A_WIDTH]
    ks = []
    for s in range(B_WIDTH // LANES):
        lo = A_WIDTH + s * LANES
        q = _head_rms(z[:, lo:lo + LANES], bd2, qn_ref[...])
        q_ref[:, s * LANES:(s + 1) * LANES] = (q * ATT_SCALE).astype(BF16)
        lo = A_WIDTH + B_WIDTH + s * LANES
        ks.append(_head_rms(z[:, lo:lo + LANES], bd2, kn_ref[...]))
    k = jnp.concatenate(ks, axis=1)
    v = z[:, A_WIDTH + 2 * B_WIDTH:]
    k_ref[...] = k
    v_ref[...] = v
    _cache_store(slot, N_EVEN, kc_ref, k)
    _cache_store(slot, N_EVEN, vc_ref, v)


def _even_in(x, comb, prev_caches, slot, g, modt, w, bd2, qn, kn):
    n_out = A_WIDTH + 3 * B_WIDTH
    cache_shape, cache_spec = _cache_specs(slot, N_EVEN, B_WIDTH)
    out_shape = [
        jax.ShapeDtypeStruct((TILE, N_TILES * A_WIDTH), F32),
        jax.ShapeDtypeStruct((N_TOK, B_WIDTH), BF16),
        jax.ShapeDtypeStruct((N_TOK, B_WIDTH), F32),
        jax.ShapeDtypeStruct((N_TOK, B_WIDTH), F32),
        cache_shape, cache_shape,
    ]
    out_specs = [pl.BlockSpec((TILE, A_WIDTH), lambda i: (0, i)),
                 _tile_spec(B_WIDTH), _tile_spec(B_WIDTH), _tile_spec(B_WIDTH), cache_spec, cache_spec]
    if comb:
        out_shape.insert(0, jax.ShapeDtypeStruct((N_TOK, D_MODEL), F32))
        out_specs.insert(0, _tile_spec(D_MODEL))
    n_in = 1 + len(comb) + 7
    aliases = {n_in + j: len(out_shape) - 2 + j for j in range(len(prev_caches))}
    return pl.pallas_call(
        functools.partial(_even_in_kernel, len(comb), slot, len(prev_caches)),
        out_shape=tuple(out_shape),
        grid=(N_TILES,),
        in_specs=[_tile_spec(D_MODEL)] + (_comb_specs() if comb else []) + [
            _full_spec((1, D_MODEL)), _mod_spec(0), _mod_spec(1),
            _full_spec((D_MODEL, n_out)), _full_spec((2 * LANES, LANES)),
            _full_spec((1, LANES)), _full_spec((1, LANES)),
        ] + [pl.BlockSpec(memory_space=pl.ANY)] * len(prev_caches),
        out_specs=tuple(out_specs),
        input_output_aliases=aliases,
        compiler_params=_cparams("arbitrary"),
        name="even_in",
    )(x, *comb, g, modt, modt, w, bd2, qn, kn, *prev_caches)


def _cmul(ar, ai, br, bi):
    return ar * br - ai * bi, ar * bi + ai * br


def _s5_kernel(u_ref, a_ref, w_ref, c_ref, s0_ref, y_ref, fin_ref, xr, xi, st_r, st_i):
    grp = pl.program_id(0)
    drn = pl.program_id(1)
    n_chunks = TILE // SCAN_CHUNK
    rows = SCAN_CHUNK * SCAN_ROWS
    part = SSM_LANES // SCAN_LANE_PARTS

    def run(store):
        @pl.loop(0, n_chunks)
        def _(c):
            cc = jnp.where(drn == 0, c, n_chunks - 1 - c)
            t0 = pl.multiple_of(cc * SCAN_CHUNK, SCAN_CHUNK)
            uu = u_ref[pl.ds(t0, SCAN_CHUNK), :, :]
            for s in range(SSM_SLABS):
                us = uu[:, :, s * LANES:(s + 1) * LANES].reshape(rows, LANES).astype(BF16)
                cols = slice(s * SSM_SLAB_STATES, (s + 1) * SSM_SLAB_STATES)
                xr[:, cols] = _dot(us, w_ref[0, 0, s])
                xi[:, cols] = _dot(us, w_ref[0, 1, s])
            for p in range(SCAN_LANE_PARTS):
                ls = slice(p * part, (p + 1) * part)
                ar = a_ref[0, 0, :, ls]
                ai = a_ref[0, 1, :, ls]

                def step(j, carry):
                    sr, si = carry
                    tt = jnp.where(drn == 0, j, SCAN_CHUNK - 1 - j)
                    r0 = pl.multiple_of(tt * SCAN_ROWS, SCAN_ROWS)
                    nr = ar * sr - ai * si + xr[pl.ds(r0, SCAN_ROWS), ls]
                    ni = ar * si + ai * sr + xi[pl.ds(r0, SCAN_ROWS), ls]
                    if store:
                        xr[pl.ds(r0, SCAN_ROWS), ls] = nr
                        xi[pl.ds(r0, SCAN_ROWS), ls] = ni
                    return nr, ni

                sr, si = lax.fori_loop(0, SCAN_CHUNK, step, (st_r[:, ls], st_i[:, ls]), unroll=4)
                st_r[:, ls] = sr
                st_i[:, ls] = si
            if store:
                for s in range(SSM_SLABS):
                    cols = slice(s * SSM_SLAB_STATES, (s + 1) * SSM_SLAB_STATES)
                    ys = (_dot(xr[:, cols].astype(BF16), c_ref[0, 0, s])
                          - _dot(xi[:, cols].astype(BF16), c_ref[0, 1, s]))
                    ys = ys.reshape(SCAN_CHUNK, SCAN_ROWS, LANES)
                    lanes = slice(s * LANES, (s + 1) * LANES)

                    @pl.when(drn == 0)
                    def _():
                        y_ref[pl.ds(t0, SCAN_CHUNK), :, lanes] = ys

                    @pl.when(drn != 0)
                    def _():
                        y_ref[pl.ds(t0, SCAN_CHUNK), :, lanes] += ys

    st_r[...] = s0_ref[0, 0]
    st_i[...] = s0_ref[0, 1]

    @pl.when(grp == SCAN_GROUPS - 1)
    def _():
        st_r[...] = jnp.zeros_like(st_r)
        st_i[...] = jnp.zeros_like(st_i)
        run(False)
        pr, pi = a_ref[0, 0], a_ref[0, 1]
        for _ in range(8):
            pr, pi = _cmul(pr, pi, pr, pi)
        fr, fi = st_r[...], st_i[...]
        s0r, s0i = s0_ref[0, 0], s0_ref[0, 1]
        row = lax.broadcasted_iota(jnp.int32, (SCAN_ROWS, SSM_LANES), 0)
        quarter = row % LAT_TILES_PER_SEQ
        fwd = drn == 0
        keep = quarter != jnp.where(fwd, 0, LAT_TILES_PER_SEQ - 1)
        ir, ii = s0r, s0i
        for _ in range(LAT_TILES_PER_SEQ - 1):
            nr, ni = _cmul(pr, pi, ir, ii)
            nr, ni = nr + fr, ni + fi
            nr = jnp.where(fwd, pltpu.roll(nr, 1, 0), pltpu.roll(nr, SCAN_ROWS - 1, 0))
            ni = jnp.where(fwd, pltpu.roll(ni, 1, 0), pltpu.roll(ni, SCAN_ROWS - 1, 0))
            ir = s0r + jnp.where(keep, nr, 0.0)
            ii = s0i + jnp.where(keep, ni, 0.0)
        st_r[...] = ir
        st_i[...] = ii

    run(True)
    fin_ref[0, 0] = st_r[...]
    fin_ref[0, 1] = st_i[...]


def _s5_scan(u_tb, a_b, w_bd, c_bd, s0):
    rows = SCAN_CHUNK * SCAN_ROWS
    return pl.pallas_call(
        _s5_kernel,
        out_shape=(
            jax.ShapeDtypeStruct((TILE, N_TILES, A_WIDTH), F32),
            jax.ShapeDtypeStruct((2, 2, N_TILES, SSM_LANES), F32),
        ),
        grid=(SCAN_GROUPS, 2),
        in_specs=[
            pl.BlockSpec((TILE, SCAN_ROWS, A_WIDTH), lambda g, d: (0, g, 0)),
            pl.BlockSpec((1, 2, SCAN_ROWS, SSM_LANES), lambda g, d: (d, 0, 0, 0)),
            pl.BlockSpec((1, 2, SSM_SLABS, LANES, SSM_SLAB_STATES), lambda g, d: (d, 0, 0, 0, 0)),
            pl.BlockSpec((1, 2, SSM_SLABS, SSM_SLAB_STATES, LANES), lambda g, d: (d, 0, 0, 0, 0)),
            pl.BlockSpec((1, 2, SCAN_ROWS, SSM_LANES), lambda g, d: (d, 0, g, 0)),
        ],
        out_specs=(
            pl.BlockSpec((TILE, SCAN_ROWS, A_WIDTH), lambda g, d: (0, g, 0)),
            pl.BlockSpec((1, 2, SCAN_ROWS, SSM_LANES), lambda g, d: (d, 0, g, 0)),
        ),
        scratch_shapes=[
            pltpu.VMEM((rows, SSM_LANES), F32), pltpu.VMEM((rows, SSM_LANES), F32),
            pltpu.VMEM((SCAN_ROWS, SSM_LANES), F32), pltpu.VMEM((SCAN_ROWS, SSM_LANES), F32),
        ],
        compiler_params=_cparams("arbitrary", "arbitrary"),
        name="s5_scan",
    )(u_tb, a_b, w_bd, c_bd, s0)


def _s5_params(lam_re, lam_im, log_dt, b_re, b_im, c_re, c_im):
    dt = jnp.exp(log_dt)[..., None]
    mag = jnp.exp(lam_re * dt)
    ab_re, ab_im = mag * jnp.cos(lam_im * dt), mag * jnp.sin(lam_im * dt)
    den = lam_re * lam_re + lam_im * lam_im
    nr, ni = ab_re - 1.0, ab_im
    f_re = (nr * lam_re + ni * lam_im) / den
    f_im = (ni * lam_re - nr * lam_im) / den
    bb_re = f_re[..., None] * b_re - f_im[..., None] * b_im
    bb_im = f_re[..., None] * b_im + f_im[..., None] * b_re
    a_b = jnp.stack([ab_re, ab_im], axis=1).reshape(2, 2, 1, SSM_LANES)
    a_b = jnp.broadcast_to(a_b, (2, 2, SCAN_ROWS, SSM_LANES))
    gps = A_GROUPS // SSM_SLABS
    eye = jnp.eye(gps, dtype=F32)

    def in_bd(bb):
        bb = bb.reshape(2, SSM_SLABS, gps, A_STATE, A_GROUP_CH)
        m = jnp.einsum('dsgph,gk->dsghkp', bb, eye)
        return m.reshape(2, SSM_SLABS, LANES, SSM_SLAB_STATES)

    def out_bd(cc):
        cc = cc.reshape(2, SSM_SLABS, gps, A_GROUP_CH, A_STATE)
        m = jnp.einsum('dsghp,gk->dsgpkh', cc, eye)
        return m.reshape(2, SSM_SLABS, SSM_SLAB_STATES, LANES)

    w_bd = jnp.stack([in_bd(bb_re), in_bd(bb_im)], axis=1).astype(BF16)
    c_bd = jnp.stack([out_bd(c_re), out_bd(c_im)], axis=1).astype(BF16)
    return a_b, w_bd, c_bd


def _softmax_pv(scores, values, sink=None):
    m = functools.reduce(jnp.maximum, [jnp.max(s, axis=-1, keepdims=True) for s in scores])
    if sink is not None:
        m = jnp.maximum(m, sink)
    den = None
    acc = None
    for s, v in zip(scores, values):
        e = jnp.exp(s - m)
        d = jnp.sum(e, axis=-1, keepdims=True)
        o = _dot(e.astype(BF16), v)
        den = d if den is None else den + d
        acc = o if acc is None else acc + o
    if sink is not None:
        den = den + jnp.exp(sink - m)
    return acc / den


def _head_pair(qs, ks, vs, bias=None, mask=None, sinks=None):
    m = qs.shape[0]
    low = _lane_half(qs.shape, 0)
    zero = jnp.zeros_like(qs)
    q2 = jnp.concatenate([jnp.where(low, qs, zero), jnp.where(low, zero, qs)], axis=0)
    s = _dot_nt(q2, ks)
    if bias is not None:
        s = s + bias
    if mask is not None:
        s = jnp.where(jnp.concatenate([mask, mask], axis=0), s, NEG_INF)
    sink = None
    if sinks is not None:
        row = lax.broadcasted_iota(jnp.int32, (2 * m, 1), 0)
        sink = jnp.where(row < m, sinks[0], sinks[1])
    o2 = _softmax_pv([s], [vs], sink=sink)
    return jnp.where(low, o2[0:m], o2[m:])


def _nat_ctx_kernel(q_ref, k_ref, v_ref, o_ref):
    @pl.when(pl.program_id(0) >= N_CTX_TILES)
    def _():
        o_ref[...] = jnp.zeros_like(o_ref)

    @pl.when(pl.program_id(0) < N_CTX_TILES)
    def _():
        for p in range(B_WIDTH // LANES):
            ls = slice(p * LANES, (p + 1) * LANES)
            out = _head_pair(q_ref[:, ls], k_ref[:, ls].astype(BF16), v_ref[:, ls].astype(BF16))
            o_ref[:, ls] = out.astype(BF16)


def _nat_lat_kernel(o_all_ref, q_ref, k_ref, v_ref, kc_ref, vc_ref, bias_ref, o_ref):
    del o_all_ref
    r = pl.program_id(0)
    wr = WIN_R_MAX
    rs = jnp.clip(r - wr // 2, 0, DEC_SEQ // GRID_W - wr)
    k0 = pl.multiple_of(rs * GRID_W, GRID_W)
    n_nb = wr * GRID_W
    for p in range(B_WIDTH // LANES):
        ls = slice(p * LANES, (p + 1) * LANES)
        for b in range(DEC_BATCH):
            ks = jnp.concatenate([k_ref[b, pl.ds(k0, n_nb), ls], kc_ref[b, :, ls]], axis=0).astype(BF16)
            vs = jnp.concatenate([v_ref[b, pl.ds(k0, n_nb), ls], vc_ref[b, :, ls]], axis=0).astype(BF16)
            bias = jnp.concatenate([bias_ref[2 * p, 0], bias_ref[2 * p + 1, 0]], axis=0)
            o_ref[b, :, ls] = _head_pair(q_ref[b, :, ls], ks, vs, bias=bias).astype(BF16)


def _nat_attention(q, k, v, kc, vc, bias):
    o_ctx = pl.pallas_call(
        _nat_ctx_kernel,
        out_shape=jax.ShapeDtypeStruct((N_TOK, B_WIDTH), BF16),
        grid=(N_TILES,),
        in_specs=[_ctx_tile_spec(B_WIDTH)] * 3,
        out_specs=_tile_spec(B_WIDTH),
        compiler_params=_cparams("arbitrary"),
        name="nat_ctx",
    )(q, k, v)
    rows = DEC_SEQ // GRID_W
    n_seq = N_TOK // DEC_SEQ
    lat = N_CTX_TOK // DEC_SEQ // DEC_BATCH
    as_seq = lambda a: a.reshape(n_seq, DEC_SEQ, B_WIDTH)
    row_spec = pl.BlockSpec((DEC_BATCH, GRID_W, B_WIDTH), lambda r: (lat, r, 0))
    seq_spec = pl.BlockSpec((DEC_BATCH, DEC_SEQ, B_WIDTH), lambda r: (lat, 0, 0))
    ctx_spec = pl.BlockSpec((DEC_BATCH, PAST_LEN, B_WIDTH), lambda r: (0, 0, 0))

    def bias_idx(r):
        return (0, r - jnp.clip(r - WIN_R_MAX // 2, 0, rows - WIN_R_MAX), 0, 0)

    o = pl.pallas_call(
        _nat_lat_kernel,
        out_shape=jax.ShapeDtypeStruct((n_seq, DEC_SEQ, B_WIDTH), BF16),
        grid=(rows,),
        in_specs=[
            pl.BlockSpec(memory_space=pl.ANY), row_spec, seq_spec, seq_spec, ctx_spec, ctx_spec,
            pl.BlockSpec((B_HEADS, 1, GRID_W, WIN_R_MAX * GRID_W + PAST_LEN), bias_idx),
        ],
        out_specs=row_spec,
        input_output_aliases={0: 0},
        compiler_params=_cparams("arbitrary"),
        name="nat_lat",
    )(as_seq(o_ctx), as_seq(q), as_seq(k), as_seq(v), kc, vc, bias)
    return o.reshape(N_TOK, B_WIDTH)


def _nat_bias(rpb):
    qc = np.arange(GRID_W)
    kc = np.arange(GRID_W)
    cs = np.clip(qc - WIN_C // 2, 0, GRID_W - WIN_C)
    ok = (kc[None, :] >= cs[:, None]) & (kc[None, :] < cs[:, None] + WIN_C)
    dc = np.clip(kc[None, :] - qc[:, None] + (WIN_C - 1), 0, 2 * WIN_C - 2)
    pick = (dc[:, :, None] == np.arange(2 * WIN_C - 1)).astype(np.float32)
    t = jnp.einsum('hrc,qkc->hrqk', rpb.astype(F32), pick, precision=lax.Precision.HIGHEST)
    t = jnp.where(ok[None, None], t, NEG_INF)
    per_d = [t[:, WIN_R_MAX - 1 - d:2 * WIN_R_MAX - 1 - d].transpose(0, 2, 1, 3)
             .reshape(B_HEADS, GRID_W, WIN_R_MAX * GRID_W) for d in range(WIN_R_MAX)]
    nb = jnp.stack(per_d, axis=1)
    return jnp.concatenate([nb, jnp.zeros((B_HEADS, WIN_R_MAX, GRID_W, PAST_LEN), F32)], axis=-1)


def _gqa_heads(q_ref, sink_ref, k_segs, v_segs, masks, o_ref):
    (k,), (v,), (mask,) = k_segs, v_segs, masks
    low = _lane_half(k.shape, 0)
    k_rot = pltpu.roll(k.astype(F32), HEAD_DIM, 1).astype(BF16)
    v_rot = pltpu.roll(v.astype(F32), HEAD_DIM, 1).astype(BF16)
    k_dup = (jnp.where(low, k, k_rot), jnp.where(low, k_rot, k))
    v_dup = (jnp.where(low, v, v_rot), jnp.where(low, v_rot, v))
    for p in range(C_HEADS // 2):
        ls = slice(p * LANES, (p + 1) * LANES)
        kv = (2 * p) // C_GROUP
        out = _head_pair(q_ref[:, ls], k_dup[kv], v_dup[kv], mask=mask,
                         sinks=(sink_ref[2 * p], sink_ref[2 * p + 1]))
        o_ref[:, ls] = out.astype(BF16)


def _swa_ctx_kernel(sink_ref, q_ref, k_ref, v_ref, o_ref):
    @pl.when(pl.program_id(0) >= N_CTX_TILES)
    def _():
        o_ref[...] = jnp.zeros_like(o_ref)

    @pl.when(pl.program_id(0) < N_CTX_TILES)
    def _():
        _gqa_heads(q_ref, sink_ref, [k_ref[...].astype(BF16)], [v_ref[...].astype(BF16)], [None], o_ref)


def _swa_lat_kernel(sink_ref, o_all_ref, q_ref, k_ref, v_ref, kc_ref, vc_ref, o_ref):
    del o_all_ref
    n = pl.program_id(1)
    n_win = 3 * WINDOW
    start = jnp.clip((n - 1) * WINDOW, 0, DEC_SEQ - n_win)
    k0 = pl.multiple_of(start, WINDOW)
    n_keys = n_win + PAST_LEN
    col = lax.broadcasted_iota(jnp.int32, (WINDOW, n_keys), 1)
    qpos = n * WINDOW + lax.broadcasted_iota(jnp.int32, (WINDOW, n_keys), 0)
    ok = (col >= n_win) | (jnp.abs(qpos - (start + col)) <= WINDOW)
    ks = jnp.concatenate([k_ref[pl.ds(k0, n_win), :], kc_ref[0]], axis=0).astype(BF16)
    vs = jnp.concatenate([v_ref[pl.ds(k0, n_win), :], vc_ref[0]], axis=0).astype(BF16)
    _gqa_heads(q_ref, sink_ref, [ks], [vs], [ok], o_ref)


def _swa_attention(q, k, v, kc, vc, sink):
    nq, nkv = C_HEADS * HEAD_DIM, C_KV_HEADS * HEAD_DIM
    smem = pl.BlockSpec(memory_space=pltpu.SMEM)
    o_ctx = pl.pallas_call(
        _swa_ctx_kernel,
        out_shape=jax.ShapeDtypeStruct((N_TOK, nq), BF16),
        grid=(N_TILES,),
        in_specs=[smem, _ctx_tile_spec(nq), _ctx_tile_spec(nkv), _ctx_tile_spec(nkv)],
        out_specs=_tile_spec(nq),
        compiler_params=_cparams("arbitrary"),
        name="swa_ctx",
    )(sink, q, k, v)
    nb = DEC_SEQ // WINDOW
    ctx_seqs = N_CTX_TOK // DEC_SEQ
    lat_row = lambda b, n: (N_CTX_TOK // WINDOW + b * nb + n, 0)
    return pl.pallas_call(
        _swa_lat_kernel,
        out_shape=jax.ShapeDtypeStruct((N_TOK, nq), BF16),
        grid=(DEC_BATCH, nb),
        in_specs=[
            smem,
            pl.BlockSpec(memory_space=pl.ANY),
            pl.BlockSpec((WINDOW, nq), lat_row),
            pl.BlockSpec((DEC_SEQ, nkv), lambda b, n: (ctx_seqs + b, 0)),
            pl.BlockSpec((DEC_SEQ, nkv), lambda b, n: (ctx_seqs + b, 0)),
            pl.BlockSpec((1, PAST_LEN, nkv), lambda b, n: (b, 0, 0)),
            pl.BlockSpec((1, PAST_LEN, nkv), lambda b, n: (b, 0, 0)),
        ],
        out_specs=pl.BlockSpec((WINDOW, nq), lat_row),
        input_output_aliases={1: 0},
        compiler_params=_cparams("arbitrary", "arbitrary"),
        name="swa_lat",
    )(sink, o_ctx, q, k, v, kc, vc)


def _route_init(cnt_ref, lg_scr):
    @pl.when(pl.program_id(0) == 0)
    def _():
        cnt_ref[...] = jnp.zeros_like(cnt_ref)
        lg_scr[...] = jnp.zeros_like(lg_scr)


def _moe_input(x, g_ref, sh_ref, sc_ref, rw_ref, rb_ref, tri_ref, h_ref, route_ref, cnt_ref, lg_scr):
    h = _norm_mod(x, g_ref[...], sh_ref[0], sc_ref[0])
    h_ref[...] = _pack_bf16_pair(h[:, 0:D_MODEL // 2], h[:, D_MODEL // 2:])
    w_hi, w_lo = _split(rw_ref[...])
    h_hi, h_lo = _split(h)
    new_logits = (_dot_nt(w_hi, h_hi) + (_dot_nt(w_hi, h_lo) + _dot_nt(w_lo, h_hi)) + rb_ref[...])[0:N_EXPERTS]
    logits = lg_scr[...]
    live = jnp.where(pl.program_id(0) > 0, 1.0, 0.0)
    row = lax.broadcasted_iota(jnp.int32, logits.shape, 0)
    cur = logits
    picked, vals, idxs = [], [], []
    for _ in range(TOP_K):
        m = jnp.max(cur, axis=0, keepdims=True)
        idx = jnp.min(jnp.where(cur == m, row, N_EXPERTS), axis=0, keepdims=True)
        sel = row == idx
        picked.append(sel)
        vals.append(m)
        idxs.append(idx)
        cur = jnp.where(sel, -jnp.inf, cur)
    exps = [jnp.exp(v - vals[0]) for v in vals]
    den = functools.reduce(lambda a, b: a + b, exps)
    chosen = functools.reduce(lambda a, b: a + b, [jnp.where(s, 1.0, 0.0) for s in picked])
    cnt = cnt_ref[...]
    before = jnp.concatenate([cnt] * (TILE // LANES), axis=1) + _dot(chosen.astype(BF16), tri_ref[...])
    cnt_ref[...] = cnt + live * jnp.sum(chosen, axis=1, keepdims=True)
    ranks = [jnp.sum(jnp.where(s, before, 0.0), axis=0, keepdims=True) for s in picked]
    fields = [i.astype(F32) for i in idxs] + [e / den for e in exps] + ranks
    out_row = lax.broadcasted_iota(jnp.int32, route_ref.shape, 0)
    route = jnp.zeros(route_ref.shape, F32)
    for j, f in enumerate(fields):
        route = jnp.where(out_row == j, f, route)
    route_ref[...] = route
    lg_scr[...] = new_logits


ROUTE_ROWS = 16
OUT_STEPS = N_TILES + 1


def _last_tile(i):
    return jnp.minimum(i, N_TILES - 1)


def _out_tile_spec(width):
    return pl.BlockSpec((TILE, width), lambda i: (_last_tile(i), 0))


def _out_mod_spec(col):
    return pl.BlockSpec((1, 1, D_MODEL), lambda i: (_last_tile(i), 0, col))


def _moe_input_specs():
    return [_full_spec((1, D_MODEL)), _out_mod_spec(3), _out_mod_spec(4),
            _full_spec((ROUTER_PAD, D_MODEL)), _full_spec((ROUTER_PAD, TILE)), _full_spec((TILE, TILE))]


def _mixer_out_shapes():
    shapes = (jax.ShapeDtypeStruct((N_TOK, D_MODEL), F32),
              jax.ShapeDtypeStruct((N_TOK, D_MODEL // 2), U32),
              jax.ShapeDtypeStruct((ROUTE_ROWS, N_TOK), F32),
              jax.ShapeDtypeStruct((N_EXPERTS, LANES), F32))
    specs = (_out_tile_spec(D_MODEL), _out_tile_spec(D_MODEL // 2),
             pl.BlockSpec((ROUTE_ROWS, TILE), lambda i: (0, jnp.maximum(i - 1, 0))),
             _full_spec((N_EXPERTS, LANES)))
    return shapes, specs


def _even_out_kernel(y_ref, u_ref, o_ref, x_ref, gate_ref, d_ref, gw_ref, gb_ref, w_ref,
                     g2_ref, sh_ref, sc_ref, rw_ref, rb_ref, tri_ref, xo_ref, h_ref, route_ref, cnt_ref, lg_scr):
    _route_init(cnt_ref, lg_scr)
    yy = y_ref[...] + d_ref[...] * u_ref[...]
    g = jax.nn.gelu(yy)
    a = g * jax.nn.sigmoid(_dot(g.astype(BF16), gw_ref[...]) + gb_ref[...])
    mix = _dot(a.astype(BF16), w_ref[0:A_WIDTH, :]) + _dot(o_ref[...], w_ref[A_WIDTH:, :])
    x = x_ref[...] + gate_ref[0] * mix
    xo_ref[...] = x
    _moe_input(x, g2_ref, sh_ref, sc_ref, rw_ref, rb_ref, tri_ref, h_ref, route_ref, cnt_ref, lg_scr)


def _even_out(y_t, u_t, o, x, modt, d_skip, glu_w, glu_b, w_out, g2, rw, rb, tri):
    tm_spec = pl.BlockSpec((TILE, A_WIDTH), lambda i: (0, _last_tile(i)))
    out_shape, out_specs = _mixer_out_shapes()
    return pl.pallas_call(
        _even_out_kernel,
        out_shape=out_shape,
        grid=(OUT_STEPS,),
        in_specs=[
            tm_spec, tm_spec, _out_tile_spec(B_WIDTH), _out_tile_spec(D_MODEL), _out_mod_spec(2),
            _full_spec((1, A_WIDTH)), _full_spec((A_WIDTH, A_WIDTH)), _full_spec((1, A_WIDTH)),
            _full_spec((A_WIDTH + B_WIDTH, D_MODEL)),
        ] + _moe_input_specs(),
        out_specs=out_specs,
        scratch_shapes=[pltpu.VMEM((N_EXPERTS, TILE), F32)],
        compiler_params=_cparams("arbitrary"),
        name="even_out",
    )(y_t, u_t, o, x, modt, d_skip, glu_w, glu_b, w_out, g2, modt, modt, rw, rb, tri)


def _rope(x, cos, sin):
    lane = lax.broadcasted_iota(jnp.int32, x.shape, 1)
    first = (lane % (HEAD_DIM // 2)) < (HEAD_DIM // 4)
    partner = jnp.where(first, pltpu.roll(x, LANES - HEAD_DIM // 4, 1), pltpu.roll(x, HEAD_DIM // 4, 1))
    return x * cos + partner * sin


def _odd_in_kernel(n_comb, slot, n_prev, x_ref, *refs):
    comb, refs = refs[:n_comb], refs[n_comb:]
    g_ref, sh_ref, sc_ref, w_ref, bd_ref, qn_ref, kn_ref, cos_ref, sin_ref = refs[:9]
    outs = refs[9 + n_prev:]
    x = _combined(x_ref, comb)
    if n_comb:
        outs[0][...] = x
        outs = outs[1:]
    q_ref, kr_ref, v_ref, kcache_ref, vcache_ref = outs
    i = pl.program_id(0)
    h = _norm_mod(x, g_ref[...], sh_ref[0], sc_ref[0]).astype(BF16)
    bd2 = bd_ref[...]
    lat = i >= N_CTX_TILES
    cos = jnp.where(lat, cos_ref[...], 1.0)
    sin = jnp.where(lat, sin_ref[...], 0.0)
    nq = C_HEADS * HEAD_DIM
    z = _dot(h, w_ref[...])
    for s in range(nq // LANES):
        q = _head_rms(z[:, s * LANES:(s + 1) * LANES], bd2, qn_ref[...])
        q_ref[:, s * LANES:(s + 1) * LANES] = (_rope(q, cos, sin) * ATT_SCALE).astype(BF16)
    k = _head_rms(z[:, nq:nq + LANES], bd2, kn_ref[...])
    v = z[:, nq + LANES:]
    kr_ref[...] = _rope(k, cos, sin)
    v_ref[...] = v
    _cache_store(slot, N_ODD, kcache_ref, k)
    _cache_store(slot, N_ODD, vcache_ref, v)


def _odd_in(x, comb, prev_caches, slot, g, modt, w, bd2, qn, kn, cos, sin):
    nq, nkv = C_HEADS * HEAD_DIM, C_KV_HEADS * HEAD_DIM
    lat_spec = pl.BlockSpec((TILE, LANES), lambda i: (jnp.maximum(i - N_CTX_TILES, 0) % LAT_TILES_PER_SEQ, 0))
    cache_shape, cache_spec = _cache_specs(slot, N_ODD, nkv)
    out_shape = [
        jax.ShapeDtypeStruct((N_TOK, nq), BF16),
        jax.ShapeDtypeStruct((N_TOK, nkv), F32),
        jax.ShapeDtypeStruct((N_TOK, nkv), F32),
        cache_shape, cache_shape,
    ]
    out_specs = [_tile_spec(nq), _tile_spec(nkv), _tile_spec(nkv), cache_spec, cache_spec]
    if comb:
        out_shape.insert(0, jax.ShapeDtypeStruct((N_TOK, D_MODEL), F32))
        out_specs.insert(0, _tile_spec(D_MODEL))
    n_in = 1 + len(comb) + 9
    aliases = {n_in + j: len(out_shape) - 2 + j for j in range(len(prev_caches))}
    return pl.pallas_call(
        functools.partial(_odd_in_kernel, len(comb), slot, len(prev_caches)),
        out_shape=tuple(out_shape),
        grid=(N_TILES,),
        in_specs=[_tile_spec(D_MODEL)] + (_comb_specs() if comb else []) + [
            _full_spec((1, D_MODEL)), _mod_spec(0), _mod_spec(1),
            _full_spec((D_MODEL, nq + 2 * nkv)), _full_spec((2 * LANES, LANES)),
            _full_spec((1, LANES)), _full_spec((1, LANES)), lat_spec, lat_spec,
        ] + [pl.BlockSpec(memory_space=pl.ANY)] * len(prev_caches),
        out_specs=tuple(out_specs),
        input_output_aliases=aliases,
        compiler_params=_cparams("arbitrary"),
        name="odd_in",
    )(x, *comb, g, modt, modt, w, bd2, qn, kn, cos, sin, *prev_caches)


def _rope_tables():
    nf = HEAD_DIM // 4
    inv = ROPE_BASE ** (-jnp.arange(nf, dtype=F32) / nf)
    t = jnp.arange(DEC_SEQ)
    pos = jnp.stack([t // GRID_W, t % GRID_W], axis=-1).astype(F32)
    ang = pos[:, :, None] * inv
    cos, sin = jnp.cos(ang), jnp.sin(ang)
    cos_h = jnp.stack([cos, cos], axis=2).reshape(DEC_SEQ, HEAD_DIM)
    sin_h = jnp.stack([-sin, sin], axis=2).reshape(DEC_SEQ, HEAD_DIM)
    return jnp.tile(cos_h, (1, 2)), jnp.tile(sin_h, (1, 2))


def _odd_out_kernel(o_ref, x_ref, gate_ref, w_ref, g2_ref, sh_ref, sc_ref, rw_ref, rb_ref, tri_ref,
                    xo_ref, h_ref, route_ref, cnt_ref, lg_scr):
    _route_init(cnt_ref, lg_scr)
    x = x_ref[...] + gate_ref[0] * _dot(o_ref[...], w_ref[...])
    xo_ref[...] = x
    _moe_input(x, g2_ref, sh_ref, sc_ref, rw_ref, rb_ref, tri_ref, h_ref, route_ref, cnt_ref, lg_scr)


def _odd_out(o, x, modt, w_o, g2, rw, rb, tri):
    nq = C_HEADS * HEAD_DIM
    out_shape, out_specs = _mixer_out_shapes()
    return pl.pallas_call(
        _odd_out_kernel,
        out_shape=out_shape,
        grid=(OUT_STEPS,),
        in_specs=[_out_tile_spec(nq), _out_tile_spec(D_MODEL), _out_mod_spec(2), _full_spec((nq, D_MODEL))]
        + _moe_input_specs(),
        out_specs=out_specs,
        scratch_shapes=[pltpu.VMEM((N_EXPERTS, TILE), F32)],
        compiler_params=_cparams("arbitrary"),
        name="odd_out",
    )(o, x, modt, w_o, g2, modt, modt, rw, rb, tri)


def _pack_bf16_pair(lo, hi):
    lo_bits = lax.bitcast_convert_type(lo.astype(BF16).astype(F32), U32)
    hi_bits = lax.bitcast_convert_type(hi.astype(BF16).astype(F32), U32)
    return (hi_bits & jnp.uint32(0xFFFF0000)) | (lo_bits >> 16)


def _unpack_bf16_pair(packed):
    lo = lax.bitcast_convert_type(packed << 16, F32).astype(BF16)
    hi = lax.bitcast_convert_type(packed & jnp.uint32(0xFFFF0000), F32).astype(BF16)
    return lo, hi


def _moe_expert_kernel(layer, be_ref, nb_ref, first_ref, slot_ref, nxt_ref, src_cur_ref, src_nxt_ref,
                       h_ref, wgu_hbm, bgu_ref, wd_hbm, bd_ref, o_ref, xg0, xg1, wgu_buf, wd_buf, wsem):
    i = pl.program_id(0)
    nb = nb_ref[0]
    half = D_MODEL // 2
    xg = (xg0, xg1)

    def gather(idx_ref, u, dst):
        for r in range(MOE_TM):
            dst[pl.ds(r, 1), :] = h_ref[pl.ds(idx_ref[0, 0, u * MOE_TM + r], 1), :]

    def weight_copies(e, slot):
        return (pltpu.make_async_copy(wgu_hbm.at[layer, e], wgu_buf.at[slot], wsem.at[0, slot]),
                pltpu.make_async_copy(wd_hbm.at[layer, e], wd_buf.at[slot], wsem.at[1, slot]))

    @pl.when(i == 0)
    def _():
        for cp in weight_copies(be_ref[0], 0):
            cp.start()
        gather(src_cur_ref, 0, xg0)

    for u in range(MOE_STEP_BLOCKS):
        blk = i * MOE_STEP_BLOCKS + u
        rows = pl.ds(u * MOE_TM, MOE_TM)
        x_cur, x_nxt = xg[u % 2], xg[(u + 1) % 2]

        @pl.when(blk < nb)
        def _():
            ws = slot_ref[blk]
            e = be_ref[blk]

            @pl.when(first_ref[blk] == 1)
            def _():
                for cp in weight_copies(e, ws):
                    cp.wait()

                @pl.when(nxt_ref[blk] >= 0)
                def _():
                    for cp in weight_copies(nxt_ref[blk], 1 - ws):
                        cp.start()

            x_lo, x_hi = _unpack_bf16_pair(x_cur[...])
            gu = (_dot(x_lo, wgu_buf[ws, 0:half, :].astype(BF16)) + _dot(x_hi, wgu_buf[ws, half:, :].astype(BF16))
                  + bgu_ref[0, pl.ds(e, 1), :])
            if u + 1 < MOE_STEP_BLOCKS:
                gather(src_cur_ref, u + 1, x_nxt)
            else:
                gather(src_nxt_ref, 0, x_nxt)
            g = jnp.minimum(gu[:, 0:D_EXPERT], SWIGLU_LIMIT)
            lin = jnp.clip(gu[:, D_EXPERT:], -SWIGLU_LIMIT, SWIGLU_LIMIT)
            act = g * jax.nn.sigmoid(SWIGLU_ALPHA * g) * (lin + 1.0)
            o_ref[rows, :] = _dot(act.astype(BF16), wd_buf[ws].astype(BF16)) + bd_ref[0, pl.ds(e, 1), :]

        @pl.when(blk >= nb)
        def _():
            o_ref[rows, :] = jnp.zeros((MOE_TM, D_MODEL), F32)


def _moe_experts(layer, plan, h_packed, w_gu, b_gu, w_down, b_down):
    block_e, n_used, first, slot, nxt, row_tok = plan
    half = D_MODEL // 2
    steps = MOE_BLOCKS // MOE_STEP_BLOCKS
    step_rows = MOE_STEP_BLOCKS * MOE_TM
    idx_spec = lambda f: pl.BlockSpec((1, 1, step_rows), f, memory_space=pltpu.SMEM)
    hbm = pl.BlockSpec(memory_space=pl.ANY)
    grid_spec = pltpu.PrefetchScalarGridSpec(
        num_scalar_prefetch=5,
        grid=(steps,),
        in_specs=[
            idx_spec(lambda i, *_: (i, 0, 0)),
            idx_spec(lambda i, *_: (jnp.minimum(i + 1, steps - 1), 0, 0)),
            pl.BlockSpec((N_TOK, half), lambda i, *_: (0, 0), pipeline_mode=pl.Buffered(1)),
            hbm,
            pl.BlockSpec((1, N_EXPERTS, 2 * D_EXPERT), lambda i, *_: (layer, 0, 0)),
            hbm,
            pl.BlockSpec((1, N_EXPERTS, D_MODEL), lambda i, *_: (layer, 0, 0)),
        ],
        out_specs=pl.BlockSpec((step_rows, D_MODEL), lambda i, *_: (i, 0)),
        scratch_shapes=[
            pltpu.VMEM((MOE_TM, half), U32), pltpu.VMEM((MOE_TM, half), U32),
            pltpu.VMEM((2, D_MODEL, 2 * D_EXPERT), F32), pltpu.VMEM((2, D_EXPERT, D_MODEL), F32),
            pltpu.SemaphoreType.DMA((2, 2)),
        ],
    )
    rows = row_tok.reshape(steps, 1, step_rows)
    return pl.pallas_call(
        functools.partial(_moe_expert_kernel, layer),
        out_shape=jax.ShapeDtypeStruct((MOE_ROWS, D_MODEL), F32),
        grid_spec=grid_spec,
        compiler_params=pltpu.CompilerParams(dimension_semantics=("arbitrary",), vmem_limit_bytes=MOE_VMEM_LIMIT),
        name="moe_experts",
    )(block_e, n_used, first, slot, nxt, rows, rows, h_packed, w_gu, b_gu, w_down, b_down)


def _moe_combine_kernel(x_ref, *refs):
    ctx_ref, lat_ref = refs[N_COMB:]
    x = _combined(x_ref, refs[:N_COMB])

    @pl.when(pl.program_id(0) < N_CTX_TILES)
    def _():
        ctx_ref[...] = x

    @pl.when(pl.program_id(0) >= N_CTX_TILES)
    def _():
        lat_ref[...] = x


def _moe_combine(x, comb):
    return pl.pallas_call(
        _moe_combine_kernel,
        out_shape=(jax.ShapeDtypeStruct((N_CTX_TOK, D_MODEL), F32), jax.ShapeDtypeStruct((N_LAT_TOK, D_MODEL), F32)),
        grid=(N_TILES,),
        in_specs=[_tile_spec(D_MODEL)] + _comb_specs(),
        out_specs=(pl.BlockSpec((TILE, D_MODEL), lambda i: (jnp.minimum(i, N_CTX_TILES - 1), 0)),
                   pl.BlockSpec((TILE, D_MODEL), lambda i: (jnp.maximum(i - N_CTX_TILES, 0), 0))),
        compiler_params=_cparams("arbitrary"),
        name="moe_combine",
    )(x, *comb)


def _lut(idx, table):
    n = table.shape[0]
    hit = idx[:, None] == jnp.arange(n, dtype=idx.dtype)[None, :]
    return jnp.sum(jnp.where(hit, table[None, :], 0), axis=1)


def _moe_route(route, cnt):
    i32 = jnp.int32
    route = route.T
    gates = route[:, TOP_K:2 * TOP_K]
    flat_e = route[:, 0:TOP_K].astype(i32).reshape(N_ASG)
    rank = route[:, 2 * TOP_K:3 * TOP_K].astype(i32).reshape(N_ASG)
    asg = jnp.arange(N_ASG, dtype=i32)
    experts = jnp.arange(N_EXPERTS, dtype=i32)
    counts = cnt[:, 0].astype(i32)
    pad_counts = (counts + MOE_TM - 1) // MOE_TM * MOE_TM
    pad_end = jnp.cumsum(pad_counts)
    n_used = pad_end[-1] // MOE_TM
    dest = _lut(flat_e, pad_end - pad_counts) + rank
    fill_end = jnp.cumsum(pad_counts - counts)
    filler = jnp.arange(MOE_ROWS - N_ASG, dtype=i32)
    fill_e = jnp.sum((filler[:, None] >= fill_end[None, :]).astype(i32), axis=1)
    keys = jnp.concatenate([flat_e * ASG_STRIDE + asg, fill_e * ASG_STRIDE + (ASG_STRIDE - 1)])
    low = jnp.sort(keys) % ASG_STRIDE
    row_tok = jnp.where(low < N_ASG, low // TOP_K, 0)
    blk = jnp.arange(MOE_BLOCKS, dtype=i32)
    block_e = jnp.sum((blk[:, None] * MOE_TM >= pad_end[None, :]).astype(i32), axis=1)
    last_e = jnp.max(jnp.where(counts > 0, experts, 0))
    block_e = jnp.where(blk < n_used, jnp.minimum(block_e, N_EXPERTS - 1), last_e)
    first = jnp.concatenate([jnp.ones((1,), i32), (block_e[1:] != block_e[:-1]).astype(i32)])
    slot = (jnp.cumsum(first) - 1) % 2
    later = (experts[None, :] > experts[:, None]) & (counts[None, :] > 0)
    nxt_of = jnp.min(jnp.where(later, experts[None, :], N_EXPERTS), axis=1)
    nxt = _lut(block_e, jnp.where(nxt_of < N_EXPERTS, nxt_of, -1))
    plan = (block_e, n_used.reshape(1).astype(i32), first, slot.astype(i32), nxt.astype(i32), row_tok)
    return gates, dest.reshape(N_TOK, TOP_K), plan


def _moe(layer, h_packed, route, cnt, modt, w_gu, b_gu, w_down, b_down):
    gates, dest, plan = _moe_route(route, cnt)
    y_rows = _moe_experts(layer, plan, h_packed, w_gu, b_gu, w_down, b_down)
    return (modt, gates) + tuple(y_rows[dest[:, k]] for k in range(TOP_K))


def kernel(x_prompt, x_sample, cache_nat_k, cache_nat_v, cache_swa_k, cache_swa_v, state_ssm, c, c_ctx,
           norm1_g, norm2_g, mod_w, mod_b,
           ab_w_in, ab_w_out, ssm_lam_re, ssm_lam_im, ssm_log_dt, ssm_b_re, ssm_b_im, ssm_c_re, ssm_c_im,
           ssm_d, ssm_glu_w, ssm_glu_b, nat_qn, nat_kn, nat_rpb,
           swa_w_qkv, swa_w_o, swa_qn, swa_kn, swa_sink,
           moe_router_w, moe_router_b, moe_w_gu, moe_b_gu, moe_w_down, moe_b_down):
    x = jnp.concatenate([x_prompt.reshape(N_CTX_TOK, D_MODEL), x_sample.reshape(N_LAT_TOK, D_MODEL)], axis=0)
    cond = jnp.zeros((SUBLANES, D_MODEL), F32).at[0].set(c_ctx).at[1:1 + DEC_BATCH].set(c)
    mod = _modulation(cond, mod_w, mod_b)
    tile_row = np.concatenate([np.zeros(N_CTX_TILES, np.int32),
                               1 + np.arange(N_TILES - N_CTX_TILES, dtype=np.int32) // LAT_TILES_PER_SEQ])
    head_gain = lambda gn: jnp.tile(gn, 2).reshape(1, LANES)
    bd = np.kron(np.eye(2, dtype=np.float32), np.full((HEAD_DIM, HEAD_DIM), 1.0 / HEAD_DIM, np.float32))
    bd2 = jnp.asarray(np.concatenate([bd, bd], axis=0), BF16)
    rope_cos, rope_sin = _rope_tables()
    tri = jnp.asarray(np.triu(np.ones((TILE, TILE), np.float32), 1), BF16)
    router_w = jnp.pad(moe_router_w, ((0, 0), (0, 0), (0, ROUTER_PAD - N_EXPERTS))).transpose(0, 2, 1)
    router_b = jnp.broadcast_to(jnp.pad(moe_router_b, ((0, 0), (0, ROUTER_PAD - N_EXPERTS)))[:, :, None],
                                (DEPTH, ROUTER_PAD, TILE))
    nkv = C_KV_HEADS * HEAD_DIM

    ssm_out = []
    nat_caches, swa_caches = (), ()
    comb = ()
    for l in range(DEPTH):
        modt = mod[l][tile_row].reshape(N_TILES, 1, N_MOD * D_MODEL)
        g1 = norm1_g[l].reshape(1, D_MODEL)
        moe_in = (norm2_g[l].reshape(1, D_MODEL), router_w[l], router_b[l], tri)
        i = l // 2
        if l % 2 == 0:
            outs = _even_in(x, comb, nat_caches, i, g1, modt, ab_w_in[i].astype(BF16), bd2,
                            head_gain(nat_qn[i]), head_gain(nat_kn[i]))
            if comb:
                x, outs = outs[0], outs[1:]
            u_t, q, k, v = outs[:4]
            nat_caches = tuple(outs[4:])
            a_b, w_bd, c_bd = _s5_params(ssm_lam_re[i], ssm_lam_im[i], ssm_log_dt[i], ssm_b_re[i], ssm_b_im[i],
                                         ssm_c_re[i], ssm_c_im[i])
            st = state_ssm[:, i].reshape(DEC_BATCH, 2, 2, SSM_LANES).transpose(1, 2, 0, 3)
            s0 = jnp.zeros((2, 2, N_TILES, SSM_LANES), F32)
            first = N_CTX_TILES + LAT_TILES_PER_SEQ * np.arange(DEC_BATCH)
            s0 = s0.at[0, :, first].set(st[0].transpose(1, 0, 2))
            s0 = s0.at[1, :, first + LAT_TILES_PER_SEQ - 1].set(st[1].transpose(1, 0, 2))
            y_t, fin = _s5_scan(u_t.reshape(TILE, N_TILES, A_WIDTH), a_b, w_bd, c_bd, s0)
            o = _nat_attention(q, k, v,
                               cache_nat_k[:, i].reshape(DEC_BATCH, PAST_LEN, B_WIDTH),
                               cache_nat_v[:, i].reshape(DEC_BATCH, PAST_LEN, B_WIDTH),
                               _nat_bias(nat_rpb[i]))
            x, h_packed, route, cnt = _even_out(y_t.reshape(TILE, N_TILES * A_WIDTH), u_t, o, x, modt,
                                            ssm_d[i].reshape(1, A_WIDTH), ssm_glu_w[i].astype(BF16),
                                            ssm_glu_b[i].reshape(1, A_WIDTH), ab_w_out[i].astype(BF16), *moe_in)
            ssm_out.append(fin[:, :, :N_CTX_TILES].transpose(2, 0, 1, 3).reshape(BATCH, 2, 2, A_GROUPS, A_STATE))
        else:
            outs = _odd_in(x, comb, swa_caches, i, g1, modt, swa_w_qkv[i].astype(BF16), bd2,
                           head_gain(swa_qn[i]), head_gain(swa_kn[i]), rope_cos, rope_sin)
            if comb:
                x, outs = outs[0], outs[1:]
            q, k_rot, v = outs[:3]
            swa_caches = tuple(outs[3:])
            o = _swa_attention(q, k_rot, v,
                               cache_swa_k[:, i].reshape(DEC_BATCH, PAST_LEN, nkv),
                               cache_swa_v[:, i].reshape(DEC_BATCH, PAST_LEN, nkv), swa_sink[i])
            x, h_packed, route, cnt = _odd_out(o, x, modt, swa_w_o[i].astype(BF16), *moe_in)
        comb = _moe(l, h_packed, route, cnt, modt, moe_w_gu, moe_b_gu, moe_w_down, moe_b_down)

    y_ctx, y_lat = _moe_combine(x, comb)
    y_prompt = y_ctx.reshape(BATCH, SEQ, D_MODEL)
    y_sample = y_lat.reshape(DEC_BATCH, DEC_SEQ, D_MODEL)
    nat_shape = (BATCH, N_EVEN, SEQ, B_HEADS, HEAD_DIM)
    swa_shape = (BATCH, N_ODD, SEQ, C_KV_HEADS, HEAD_DIM)
    return (y_prompt, y_sample,
            nat_caches[0].reshape(nat_shape), nat_caches[1].reshape(nat_shape),
            swa_caches[0].reshape(swa_shape), swa_caches[1].reshape(swa_shape),
            jnp.stack(ssm_out, axis=1))
```

```python
import functools

import jax
import jax.numpy as jnp
import numpy as np
from jax import lax
from jax.experimental import pallas as pl
from jax.experimental.pallas import tpu as pltpu

F32 = jnp.float32
BF16 = jnp.bfloat16
U32 = jnp.uint32

D_MODEL = 1024
BATCH = 32
SEQ = 256
DEPTH = 4
DEC_BATCH = 2
DEC_SEQ = 1024
PAST_LEN = 256
GRID_W = 64
HEAD_DIM = 64
N_MOD = 6
N_EVEN = (DEPTH + 1) // 2
N_ODD = DEPTH // 2
A_WIDTH = 512
A_GROUP_CH = 16
A_GROUPS = 32
A_STATE = 64
B_HEADS = 8
B_WIDTH = 512
WIN_R_MAX = 8
WIN_C = 16
C_HEADS = 16
C_KV_HEADS = 2
C_GROUP = 8
WINDOW = 128
ROPE_BASE = 10000.0
N_EXPERTS = 32
TOP_K = 4
D_EXPERT = 1024
SWIGLU_LIMIT = 7.0
SWIGLU_ALPHA = 1.702
EPS = 1e-6
NEG_INF = -1e30

LANES = 128
SUBLANES = 8
TILE = 256
N_CTX_TOK = BATCH * SEQ
N_LAT_TOK = DEC_BATCH * DEC_SEQ
N_TOK = N_CTX_TOK + N_LAT_TOK
N_TILES = N_TOK // TILE
N_CTX_TILES = N_CTX_TOK // TILE
LAT_TILES_PER_SEQ = DEC_SEQ // TILE
SSM_LANES = A_GROUPS * A_STATE
SSM_SLABS = A_WIDTH // LANES
SSM_SLAB_STATES = SSM_LANES // SSM_SLABS
SCAN_ROWS = 8
SCAN_GROUPS = N_TILES // SCAN_ROWS
SCAN_CHUNK = 64
SCAN_LANE_PARTS = 2
MOE_TM = 256
MOE_STEP_BLOCKS = 4
N_ASG = N_TOK * TOP_K
ASG_STRIDE = 1 << 16
MOE_BLOCKS = N_ASG // MOE_TM + N_EXPERTS
MOE_ROWS = MOE_BLOCKS * MOE_TM
ROUTER_PAD = LANES
VMEM_LIMIT = 56 * 1024 * 1024
MOE_VMEM_LIMIT = 60 * 1024 * 1024
ATT_SCALE = HEAD_DIM ** -0.5
LAT_GROUP_W = 4 * HEAD_DIM
CTX_GROUP_W = 2 * HEAD_DIM


def _cparams(*sem):
    return pltpu.CompilerParams(dimension_semantics=sem, vmem_limit_bytes=VMEM_LIMIT)


def _dot(a, b):
    return jnp.dot(a, b, preferred_element_type=F32)


def _dot_nt(a, b):
    return lax.dot_general(a, b, (((1,), (1,)), ((), ())), preferred_element_type=F32)


def _split(a):
    hi = a.astype(BF16)
    lo = (a - hi.astype(F32)).astype(BF16)
    return hi, lo


def _dot3(a, b):
    a_hi, a_lo = _split(a)
    b_hi, b_lo = _split(b)
    return _dot(a_hi, b_hi) + (_dot(a_hi, b_lo) + _dot(a_lo, b_hi))


def _silu(x):
    return x * jax.nn.sigmoid(x)


def _norm_mod(x, g, shift, scale):
    y = x * lax.rsqrt(jnp.mean(x * x, axis=-1, keepdims=True) + EPS)
    return (y * g) * (1.0 + scale) + shift


def _head_rms(x, bd2, gain):
    sq_hi, sq_lo = _split(x * x)
    ms = _dot(jnp.concatenate([sq_hi, sq_lo], axis=1), bd2)
    return x * lax.rsqrt(ms + EPS) * gain


def _combined(x_ref, comb_refs):
    x = x_ref[...]
    if not comb_refs:
        return x
    gmod_ref, gates_ref = comb_refs[0], comb_refs[1]
    gates = gates_ref[...]
    acc = gates[:, 0:1] * comb_refs[2][...]
    for k in range(1, TOP_K):
        acc = acc + gates[:, k:k + 1] * comb_refs[2 + k][...]
    return x + gmod_ref[0] * acc


def _comb_specs():
    return [_mod_spec(5), _tile_spec(TOP_K)] + [_tile_spec(D_MODEL)] * TOP_K


N_COMB = 2 + TOP_K


def _lane_half(shape, half):
    lane = lax.broadcasted_iota(jnp.int32, shape, len(shape) - 1)
    return (lane < HEAD_DIM) if half == 0 else (lane >= HEAD_DIM)


def _mod_kernel(cond_ref, w_ref, b_ref, o_ref):
    o_ref[0] = _dot3(_silu(cond_ref[...]), w_ref[0]) + b_ref[0]


def _modulation(cond, mod_w, mod_b):
    nc = N_MOD
    return pl.pallas_call(
        _mod_kernel,
        out_shape=jax.ShapeDtypeStruct((DEPTH, SUBLANES, N_MOD * D_MODEL), F32),
        grid=(DEPTH, nc),
        in_specs=[
            pl.BlockSpec((SUBLANES, D_MODEL), lambda l, c: (0, 0)),
            pl.BlockSpec((1, D_MODEL, D_MODEL), lambda l, c: (l, 0, c)),
            pl.BlockSpec((1, 1, D_MODEL), lambda l, c: (l, 0, c)),
        ],
        out_specs=pl.BlockSpec((1, SUBLANES, D_MODEL), lambda l, c: (l, 0, c)),
        compiler_params=_cparams("arbitrary", "arbitrary"),
        name="modulation",
    )(cond, mod_w, mod_b.reshape(DEPTH, 1, N_MOD * D_MODEL))


def _mod_spec(col):
    return pl.BlockSpec((1, 1, D_MODEL), lambda i: (i, 0, col))


def _tile_spec(width):
    return pl.BlockSpec((TILE, width), lambda i: (i, 0))


def _ctx_tile_spec(width):
    return pl.BlockSpec((TILE, width), lambda i: (jnp.minimum(i, N_CTX_TILES - 1), 0))


def _full_spec(shape):
    nd = len(shape)
    return pl.BlockSpec(shape, lambda i: (0,) * nd)


def _cache_store(slot, n_slots, ref, value):
    @pl.when(pl.program_id(0) < N_CTX_TILES)
    def _():
        if slot == 0:
            ref[0, 0] = value
            for other in range(1, n_slots):
                ref[0, other] = jnp.zeros_like(value)
        else:
            ref[0, 0] = value


def _cache_specs(slot, n_slots, width):
    shape = jax.ShapeDtypeStruct((BATCH, n_slots, SEQ, width), F32)
    seq = lambda i: jnp.minimum(i, N_CTX_TILES - 1)
    if slot == 0:
        return shape, pl.BlockSpec((1, n_slots, SEQ, width), lambda i: (seq(i), 0, 0, 0))
    return shape, pl.BlockSpec((1, 1, SEQ, width), lambda i: (seq(i), slot, 0, 0))


def _even_in_kernel(n_comb, slot, n_prev, x_ref, *refs):
    comb, refs = refs[:n_comb], refs[n_comb:]
    g_ref, sh_ref, sc_ref, w_ref, bd_ref, qn_ref, kn_ref = refs[:7]
    outs = refs[7 + n_prev:]
    x = _combined(x_ref, comb)
    if n_comb:
        outs[0][...] = x
        outs = outs[1:]
    u_ref, q_ref, k_ref, v_ref, kc_ref, vc_ref = outs
    h = _norm_mod(x, g_ref[...], sh_ref[0], sc_ref[0]).astype(BF16)
    bd2 = bd_ref[...]
    z = _dot(h, w_ref[...])
    u_ref[...] = z[:, 0:A_WIDTH]
    ks = []
    for s in range(B_WIDTH // LANES):
        lo = A_WIDTH + s * LANES
        q = _head_rms(z[:, lo:lo + LANES], bd2, qn_ref[...])
        q_ref[:, s * LANES:(s + 1) * LANES] = (q * ATT_SCALE).astype(BF16)
        lo = A_WIDTH + B_WIDTH + s * LANES
        ks.append(_head_rms(z[:, lo:lo + LANES], bd2, kn_ref[...]))
    k = jnp.concatenate(ks, axis=1)
    v = z[:, A_WIDTH + 2 * B_WIDTH:]
    k_ref[...] = k
    v_ref[...] = v
    _cache_store(slot, N_EVEN, kc_ref, k)
    _cache_store(slot, N_EVEN, vc_ref, v)


def _even_in(x, comb, prev_caches, slot, g, modt, w, bd2, qn, kn):
    n_out = A_WIDTH + 3 * B_WIDTH
    cache_shape, cache_spec = _cache_specs(slot, N_EVEN, B_WIDTH)
    out_shape = [
        jax.ShapeDtypeStruct((TILE, N_TILES * A_WIDTH), F32),
        jax.ShapeDtypeStruct((N_TOK, B_WIDTH), BF16),
        jax.ShapeDtypeStruct((N_TOK, B_WIDTH), F32),
        jax.ShapeDtypeStruct((N_TOK, B_WIDTH), F32),
        cache_shape, cache_shape,
    ]
    out_specs = [pl.BlockSpec((TILE, A_WIDTH), lambda i: (0, i)),
                 _tile_spec(B_WIDTH), _tile_spec(B_WIDTH), _tile_spec(B_WIDTH), cache_spec, cache_spec]
    if comb:
        out_shape.insert(0, jax.ShapeDtypeStruct((N_TOK, D_MODEL), F32))
        out_specs.insert(0, _tile_spec(D_MODEL))
    n_in = 1 + len(comb) + 7
    aliases = {n_in + j: len(out_shape) - 2 + j for j in range(len(prev_caches))}
    return pl.pallas_call(
        functools.partial(_even_in_kernel, len(comb), slot, len(prev_caches)),
        out_shape=tuple(out_shape),
        grid=(N_TILES,),
        in_specs=[_tile_spec(D_MODEL)] + (_comb_specs() if comb else []) + [
            _full_spec((1, D_MODEL)), _mod_spec(0), _mod_spec(1),
            _full_spec((D_MODEL, n_out)), _full_spec((2 * LANES, LANES)),
            _full_spec((1, LANES)), _full_spec((1, LANES)),
        ] + [pl.BlockSpec(memory_space=pl.ANY)] * len(prev_caches),
        out_specs=tuple(out_specs),
        input_output_aliases=aliases,
        compiler_params=_cparams("arbitrary"),
        name="even_in",
    )(x, *comb, g, modt, modt, w, bd2, qn, kn, *prev_caches)


def _cmul(ar, ai, br, bi):
    return ar * br - ai * bi, ar * bi + ai * br


def _s5_kernel(u_ref, a_ref, w_ref, c_ref, s0_ref, y_ref, fin_ref, xr, xi, st_r, st_i):
    grp = pl.program_id(0)
    drn = pl.program_id(1)
    n_chunks = TILE // SCAN_CHUNK
    rows = SCAN_CHUNK * SCAN_ROWS
    part = SSM_LANES // SCAN_LANE_PARTS

    def run(store):
        @pl.loop(0, n_chunks)
        def _(c):
            cc = jnp.where(drn == 0, c, n_chunks - 1 - c)
            t0 = pl.multiple_of(cc * SCAN_CHUNK, SCAN_CHUNK)
            uu = u_ref[pl.ds(t0, SCAN_CHUNK), :, :]
            for s in range(SSM_SLABS):
                us = uu[:, :, s * LANES:(s + 1) * LANES].reshape(rows, LANES).astype(BF16)
                cols = slice(s * SSM_SLAB_STATES, (s + 1) * SSM_SLAB_STATES)
                xr[:, cols] = _dot(us, w_ref[0, 0, s])
                xi[:, cols] = _dot(us, w_ref[0, 1, s])
            for p in range(SCAN_LANE_PARTS):
                ls = slice(p * part, (p + 1) * part)
                ar = a_ref[0, 0, :, ls]
                ai = a_ref[0, 1, :, ls]

                def step(j, carry):
                    sr, si = carry
                    tt = jnp.where(drn == 0, j, SCAN_CHUNK - 1 - j)
                    r0 = pl.multiple_of(tt * SCAN_ROWS, SCAN_ROWS)
                    nr = ar * sr - ai * si + xr[pl.ds(r0, SCAN_ROWS), ls]
                    ni = ar * si + ai * sr + xi[pl.ds(r0, SCAN_ROWS), ls]
                    if store:
                        xr[pl.ds(r0, SCAN_ROWS), ls] = nr
                        xi[pl.ds(r0, SCAN_ROWS), ls] = ni
                    return nr, ni

                sr, si = lax.fori_loop(0, SCAN_CHUNK, step, (st_r[:, ls], st_i[:, ls]), unroll=4)
                st_r[:, ls] = sr
                st_i[:, ls] = si
            if store:
                for s in range(SSM_SLABS):
                    cols = slice(s * SSM_SLAB_STATES, (s + 1) * SSM_SLAB_STATES)
                    ys = (_dot(xr[:, cols].astype(BF16), c_ref[0, 0, s])
                          - _dot(xi[:, cols].astype(BF16), c_ref[0, 1, s]))
                    ys = ys.reshape(SCAN_CHUNK, SCAN_ROWS, LANES)
                    lanes = slice(s * LANES, (s + 1) * LANES)

                    @pl.when(drn == 0)
                    def _():
                        y_ref[pl.ds(t0, SCAN_CHUNK), :, lanes] = ys

                    @pl.when(drn != 0)
                    def _():
                        y_ref[pl.ds(t0, SCAN_CHUNK), :, lanes] += ys

    st_r[...] = s0_ref[0, 0]
    st_i[...] = s0_ref[0, 1]

    @pl.when(grp == SCAN_GROUPS - 1)
    def _():
        st_r[...] = jnp.zeros_like(st_r)
        st_i[...] = jnp.zeros_like(st_i)
        run(False)
        pr, pi = a_ref[0, 0], a_ref[0, 1]
        for _ in range(8):
            pr, pi = _cmul(pr, pi, pr, pi)
        fr, fi = st_r[...], st_i[...]
        s0r, s0i = s0_ref[0, 0], s0_ref[0, 1]
        row = lax.broadcasted_iota(jnp.int32, (SCAN_ROWS, SSM_LANES), 0)
        quarter = row % LAT_TILES_PER_SEQ
        fwd = drn == 0
        keep = quarter != jnp.where(fwd, 0, LAT_TILES_PER_SEQ - 1)
        ir, ii = s0r, s0i
        for _ in range(LAT_TILES_PER_SEQ - 1):
            nr, ni = _cmul(pr, pi, ir, ii)
            nr, ni = nr + fr, ni + fi
            nr = jnp.where(fwd, pltpu.roll(nr, 1, 0), pltpu.roll(nr, SCAN_ROWS - 1, 0))
            ni = jnp.where(fwd, pltpu.roll(ni, 1, 0), pltpu.roll(ni, SCAN_ROWS - 1, 0))
            ir = s0r + jnp.where(keep, nr, 0.0)
            ii = s0i + jnp.where(keep, ni, 0.0)
        st_r[...] = ir
        st_i[...] = ii

    run(True)
    fin_ref[0, 0] = st_r[...]
    fin_ref[0, 1] = st_i[...]


def _s5_scan(u_tb, a_b, w_bd, c_bd, s0):
    rows = SCAN_CHUNK * SCAN_ROWS
    return pl.pallas_call(
        _s5_kernel,
        out_shape=(
            jax.ShapeDtypeStruct((TILE, N_TILES, A_WIDTH), F32),
            jax.ShapeDtypeStruct((2, 2, N_TILES, SSM_LANES), F32),
        ),
        grid=(SCAN_GROUPS, 2),
        in_specs=[
            pl.BlockSpec((TILE, SCAN_ROWS, A_WIDTH), lambda g, d: (0, g, 0)),
            pl.BlockSpec((1, 2, SCAN_ROWS, SSM_LANES), lambda g, d: (d, 0, 0, 0)),
            pl.BlockSpec((1, 2, SSM_SLABS, LANES, SSM_SLAB_STATES), lambda g, d: (d, 0, 0, 0, 0)),
            pl.BlockSpec((1, 2, SSM_SLABS, SSM_SLAB_STATES, LANES), lambda g, d: (d, 0, 0, 0, 0)),
            pl.BlockSpec((1, 2, SCAN_ROWS, SSM_LANES), lambda g, d: (d, 0, g, 0)),
        ],
        out_specs=(
            pl.BlockSpec((TILE, SCAN_ROWS, A_WIDTH), lambda g, d: (0, g, 0)),
            pl.BlockSpec((1, 2, SCAN_ROWS, SSM_LANES), lambda g, d: (d, 0, g, 0)),
        ),
        scratch_shapes=[
            pltpu.VMEM((rows, SSM_LANES), F32), pltpu.VMEM((rows, SSM_LANES), F32),
            pltpu.VMEM((SCAN_ROWS, SSM_LANES), F32), pltpu.VMEM((SCAN_ROWS, SSM_LANES), F32),
        ],
        compiler_params=_cparams("arbitrary", "arbitrary"),
        name="s5_scan",
    )(u_tb, a_b, w_bd, c_bd, s0)


def _s5_params(lam_re, lam_im, log_dt, b_re, b_im, c_re, c_im):
    dt = jnp.exp(log_dt)[..., None]
    mag = jnp.exp(lam_re * dt)
    ab_re, ab_im = mag * jnp.cos(lam_im * dt), mag * jnp.sin(lam_im * dt)
    den = lam_re * lam_re + lam_im * lam_im
    nr, ni = ab_re - 1.0, ab_im
    f_re = (nr * lam_re + ni * lam_im) / den
    f_im = (ni * lam_re - nr * lam_im) / den
    bb_re = f_re[..., None] * b_re - f_im[..., None] * b_im
    bb_im = f_re[..., None] * b_im + f_im[..., None] * b_re
    a_b = jnp.stack([ab_re, ab_im], axis=1).reshape(2, 2, 1, SSM_LANES)
    a_b = jnp.broadcast_to(a_b, (2, 2, SCAN_ROWS, SSM_LANES))
    gps = A_GROUPS // SSM_SLABS
    eye = jnp.eye(gps, dtype=F32)

    def in_bd(bb):
        bb = bb.reshape(2, SSM_SLABS, gps, A_STATE, A_GROUP_CH)
        m = jnp.einsum('dsgph,gk->dsghkp', bb, eye)
        return m.reshape(2, SSM_SLABS, LANES, SSM_SLAB_STATES)

    def out_bd(cc):
        cc = cc.reshape(2, SSM_SLABS, gps, A_GROUP_CH, A_STATE)
        m = jnp.einsum('dsghp,gk->dsgpkh', cc, eye)
        return m.reshape(2, SSM_SLABS, SSM_SLAB_STATES, LANES)

    w_bd = jnp.stack([in_bd(bb_re), in_bd(bb_im)], axis=1).astype(BF16)
    c_bd = jnp.stack([out_bd(c_re), out_bd(c_im)], axis=1).astype(BF16)
    return a_b, w_bd, c_bd


def _softmax_pv(scores, values, sink=None):
    m = functools.reduce(jnp.maximum, [jnp.max(s, axis=-1, keepdims=True) for s in scores])
    if sink is not None:
        m = jnp.maximum(m, sink)
    den = None
    acc = None
    for s, v in zip(scores, values):
        e = jnp.exp(s - m)
        d = jnp.sum(e, axis=-1, keepdims=True)
        o = _dot(e.astype(BF16), v)
        den = d if den is None else den + d
        acc = o if acc is None else acc + o
    if sink is not None:
        den = den + jnp.exp(sink - m)
    return acc / den


def _head_group(qs, ks, vs, bias=None, mask=None, sinks=None):
    m, g = qs.shape[0], qs.shape[1] // HEAD_DIM
    head = lax.broadcasted_iota(jnp.int32, qs.shape, 1) // HEAD_DIM
    zero = jnp.zeros_like(qs)
    q_all = jnp.concatenate([jnp.where(head == j, qs, zero) for j in range(g)], axis=0)
    s = _dot_nt(q_all, ks)
    if bias is not None:
        s = s + bias
    if mask is not None:
        s = jnp.where(jnp.concatenate([mask] * g, axis=0), s, NEG_INF)
    sink = None
    if sinks is not None:
        row = lax.broadcasted_iota(jnp.int32, (g * m, 1), 0)
        sink = sinks[0]
        for j in range(1, g):
            sink = jnp.where(row >= j * m, sinks[j], sink)
    o_all = _softmax_pv([s], [vs], sink=sink)
    out = o_all[0:m]
    for j in range(1, g):
        out = jnp.where(head == j, o_all[j * m:(j + 1) * m], out)
    return out


def _nat_ctx_kernel(q_ref, k_ref, v_ref, o_ref):
    @pl.when(pl.program_id(0) >= N_CTX_TILES)
    def _():
        o_ref[...] = jnp.zeros_like(o_ref)

    @pl.when(pl.program_id(0) < N_CTX_TILES)
    def _():
        for p in range(B_WIDTH // CTX_GROUP_W):
            ls = slice(p * CTX_GROUP_W, (p + 1) * CTX_GROUP_W)
            out = _head_group(q_ref[:, ls], k_ref[:, ls].astype(BF16), v_ref[:, ls].astype(BF16))
            o_ref[:, ls] = out.astype(BF16)


def _nat_lat_kernel(o_all_ref, q_ref, k_ref, v_ref, kc_ref, vc_ref, bias_ref, o_ref):
    del o_all_ref
    r = pl.program_id(0)
    wr = WIN_R_MAX
    rs = jnp.clip(r - wr // 2, 0, DEC_SEQ // GRID_W - wr)
    k0 = pl.multiple_of(rs * GRID_W, GRID_W)
    n_nb = wr * GRID_W
    g = LAT_GROUP_W // HEAD_DIM
    for p in range(B_WIDTH // LAT_GROUP_W):
        ls = slice(p * LAT_GROUP_W, (p + 1) * LAT_GROUP_W)
        for b in range(DEC_BATCH):
            ks = jnp.concatenate([k_ref[b, pl.ds(k0, n_nb), ls], kc_ref[b, :, ls]], axis=0).astype(BF16)
            vs = jnp.concatenate([v_ref[b, pl.ds(k0, n_nb), ls], vc_ref[b, :, ls]], axis=0).astype(BF16)
            bias = jnp.concatenate([bias_ref[g * p + j, 0] for j in range(g)], axis=0)
            o_ref[b, :, ls] = _head_group(q_ref[b, :, ls], ks, vs, bias=bias).astype(BF16)


def _nat_attention(q, k, v, kc, vc, bias):
    o_ctx = pl.pallas_call(
        _nat_ctx_kernel,
        out_shape=jax.ShapeDtypeStruct((N_TOK, B_WIDTH), BF16),
        grid=(N_TILES,),
        in_specs=[_ctx_tile_spec(B_WIDTH)] * 3,
        out_specs=_tile_spec(B_WIDTH),
        compiler_params=_cparams("arbitrary"),
        name="nat_ctx",
    )(q, k, v)
    rows = DEC_SEQ // GRID_W
    n_seq = N_TOK // DEC_SEQ
    lat = N_CTX_TOK // DEC_SEQ // DEC_BATCH
    as_seq = lambda a: a.reshape(n_seq, DEC_SEQ, B_WIDTH)
    row_spec = pl.BlockSpec((DEC_BATCH, GRID_W, B_WIDTH), lambda r: (lat, r, 0))
    seq_spec = pl.BlockSpec((DEC_BATCH, DEC_SEQ, B_WIDTH), lambda r: (lat, 0, 0))
    ctx_spec = pl.BlockSpec((DEC_BATCH, PAST_LEN, B_WIDTH), lambda r: (0, 0, 0))

    def bias_idx(r):
        return (0, r - jnp.clip(r - WIN_R_MAX // 2, 0, rows - WIN_R_MAX), 0, 0)

    o = pl.pallas_call(
        _nat_lat_kernel,
        out_shape=jax.ShapeDtypeStruct((n_seq, DEC_SEQ, B_WIDTH), BF16),
        grid=(rows,),
        in_specs=[
            pl.BlockSpec(memory_space=pl.ANY), row_spec, seq_spec, seq_spec, ctx_spec, ctx_spec,
            pl.BlockSpec((B_HEADS, 1, GRID_W, WIN_R_MAX * GRID_W + PAST_LEN), bias_idx),
        ],
        out_specs=row_spec,
        input_output_aliases={0: 0},
        compiler_params=_cparams("arbitrary"),
        name="nat_lat",
    )(as_seq(o_ctx), as_seq(q), as_seq(k), as_seq(v), kc, vc, bias)
    return o.reshape(N_TOK, B_WIDTH)


def _nat_bias(rpb):
    qc = np.arange(GRID_W)
    kc = np.arange(GRID_W)
    cs = np.clip(qc - WIN_C // 2, 0, GRID_W - WIN_C)
    ok = (kc[None, :] >= cs[:, None]) & (kc[None, :] < cs[:, None] + WIN_C)
    dc = np.clip(kc[None, :] - qc[:, None] + (WIN_C - 1), 0, 2 * WIN_C - 2)
    pick = (dc[:, :, None] == np.arange(2 * WIN_C - 1)).astype(np.float32)
    t = jnp.einsum('hrc,qkc->hrqk', rpb.astype(F32), pick, precision=lax.Precision.HIGHEST)
    t = jnp.where(ok[None, None], t, NEG_INF)
    per_d = [t[:, WIN_R_MAX - 1 - d:2 * WIN_R_MAX - 1 - d].transpose(0, 2, 1, 3)
             .reshape(B_HEADS, GRID_W, WIN_R_MAX * GRID_W) for d in range(WIN_R_MAX)]
    nb = jnp.stack(per_d, axis=1)
    return jnp.concatenate([nb, jnp.zeros((B_HEADS, WIN_R_MAX, GRID_W, PAST_LEN), F32)], axis=-1)


def _gqa_heads(q_ref, sink_ref, k_segs, v_segs, masks, o_ref, group_w):
    (k,), (v,), (mask,) = k_segs, v_segs, masks
    g = group_w // HEAD_DIM
    low = _lane_half(k.shape, 0)
    k_rot = pltpu.roll(k.astype(F32), HEAD_DIM, 1).astype(BF16)
    v_rot = pltpu.roll(v.astype(F32), HEAD_DIM, 1).astype(BF16)
    wide = lambda a: jnp.concatenate([a] * (group_w // LANES), axis=1)
    k_dup = (wide(jnp.where(low, k, k_rot)), wide(jnp.where(low, k_rot, k)))
    v_dup = (wide(jnp.where(low, v, v_rot)), wide(jnp.where(low, v_rot, v)))
    for p in range(C_HEADS // g):
        ls = slice(p * group_w, (p + 1) * group_w)
        kv = (g * p) // C_GROUP
        out = _head_group(q_ref[:, ls], k_dup[kv], v_dup[kv], mask=mask,
                          sinks=[sink_ref[g * p + j] for j in range(g)])
        o_ref[:, ls] = out.astype(BF16)


def _swa_ctx_kernel(sink_ref, q_ref, k_ref, v_ref, o_ref):
    @pl.when(pl.program_id(0) >= N_CTX_TILES)
    def _():
        o_ref[...] = jnp.zeros_like(o_ref)

    @pl.when(pl.program_id(0) < N_CTX_TILES)
    def _():
        _gqa_heads(q_ref, sink_ref, [k_ref[...].astype(BF16)], [v_ref[...].astype(BF16)], [None], o_ref,
                   CTX_GROUP_W)


def _swa_lat_kernel(sink_ref, o_all_ref, q_ref, k_ref, v_ref, kc_ref, vc_ref, o_ref):
    del o_all_ref
    n = pl.program_id(1)
    n_win = 3 * WINDOW
    start = jnp.clip((n - 1) * WINDOW, 0, DEC_SEQ - n_win)
    k0 = pl.multiple_of(start, WINDOW)
    n_keys = n_win + PAST_LEN
    col = lax.broadcasted_iota(jnp.int32, (WINDOW, n_keys), 1)
    qpos = n * WINDOW + lax.broadcasted_iota(jnp.int32, (WINDOW, n_keys), 0)
    ok = (col >= n_win) | (jnp.abs(qpos - (start + col)) <= WINDOW)
    ks = jnp.concatenate([k_ref[pl.ds(k0, n_win), :], kc_ref[0]], axis=0).astype(BF16)
    vs = jnp.concatenate([v_ref[pl.ds(k0, n_win), :], vc_ref[0]], axis=0).astype(BF16)
    _gqa_heads(q_ref, sink_ref, [ks], [vs], [ok], o_ref, LAT_GROUP_W)


def _swa_attention(q, k, v, kc, vc, sink):
    nq, nkv = C_HEADS * HEAD_DIM, C_KV_HEADS * HEAD_DIM
    smem = pl.BlockSpec(memory_space=pltpu.SMEM)
    o_ctx = pl.pallas_call(
        _swa_ctx_kernel,
        out_shape=jax.ShapeDtypeStruct((N_TOK, nq), BF16),
        grid=(N_TILES,),
        in_specs=[smem, _ctx_tile_spec(nq), _ctx_tile_spec(nkv), _ctx_tile_spec(nkv)],
        out_specs=_tile_spec(nq),
        compiler_params=_cparams("arbitrary"),
        name="swa_ctx",
    )(sink, q, k, v)
    nb = DEC_SEQ // WINDOW
    ctx_seqs = N_CTX_TOK // DEC_SEQ
    lat_row = lambda b, n: (N_CTX_TOK // WINDOW + b * nb + n, 0)
    return pl.pallas_call(
        _swa_lat_kernel,
        out_shape=jax.ShapeDtypeStruct((N_TOK, nq), BF16),
        grid=(DEC_BATCH, nb),
        in_specs=[
            smem,
            pl.BlockSpec(memory_space=pl.ANY),
            pl.BlockSpec((WINDOW, nq), lat_row),
            pl.BlockSpec((DEC_SEQ, nkv), lambda b, n: (ctx_seqs + b, 0)),
            pl.BlockSpec((DEC_SEQ, nkv), lambda b, n: (ctx_seqs + b, 0)),
            pl.BlockSpec((1, PAST_LEN, nkv), lambda b, n: (b, 0, 0)),
            pl.BlockSpec((1, PAST_LEN, nkv), lambda b, n: (b, 0, 0)),
        ],
        out_specs=pl.BlockSpec((WINDOW, nq), lat_row),
        input_output_aliases={1: 0},
        compiler_params=_cparams("arbitrary", "arbitrary"),
        name="swa_lat",
    )(sink, o_ctx, q, k, v, kc, vc)


def _route_init(cnt_ref, lg_scr):
    @pl.when(pl.program_id(0) == 0)
    def _():
        cnt_ref[...] = jnp.zeros_like(cnt_ref)
        lg_scr[...] = jnp.zeros_like(lg_scr)


def _moe_input(x, g_ref, sh_ref, sc_ref, rw_ref, rb_ref, tri_ref, h_ref, route_ref, cnt_ref, lg_scr):
    h = _norm_mod(x, g_ref[...], sh_ref[0], sc_ref[0])
    h_ref[...] = _pack_bf16_pair(h[:, 0:D_MODEL // 2], h[:, D_MODEL // 2:])
    w_hi, w_lo = _split(rw_ref[...])
    h_hi, h_lo = _split(h)
    new_logits = (_dot_nt(w_hi, h_hi) + (_dot_nt(w_hi, h_lo) + _dot_nt(w_lo, h_hi)) + rb_ref[...])[0:N_EXPERTS]
    logits = lg_scr[...]
    live = jnp.where(pl.program_id(0) > 0, 1.0, 0.0)
    row = lax.broadcasted_iota(jnp.int32, logits.shape, 0)
    cur = logits
    picked, vals, idxs = [], [], []
    for _ in range(TOP_K):
        m = jnp.max(cur, axis=0, keepdims=True)
        idx = jnp.min(jnp.where(cur == m, row, N_EXPERTS), axis=0, keepdims=True)
        sel = row == idx
        picked.append(sel)
        vals.append(m)
        idxs.append(idx)
        cur = jnp.where(sel, -jnp.inf, cur)
    exps = [jnp.exp(v - vals[0]) for v in vals]
    den = functools.reduce(lambda a, b: a + b, exps)
    chosen = functools.reduce(lambda a, b: a + b, [jnp.where(s, 1.0, 0.0) for s in picked])
    cnt = cnt_ref[...]
    before = jnp.concatenate([cnt] * (TILE // LANES), axis=1) + _dot(chosen.astype(BF16), tri_ref[...])
    cnt_ref[...] = cnt + live * jnp.sum(chosen, axis=1, keepdims=True)
    ranks = [jnp.sum(jnp.where(s, before, 0.0), axis=0, keepdims=True) for s in picked]
    fields = [i.astype(F32) for i in idxs] + [e / den for e in exps] + ranks
    out_row = lax.broadcasted_iota(jnp.int32, route_ref.shape, 0)
    route = jnp.zeros(route_ref.shape, F32)
    for j, f in enumerate(fields):
        route = jnp.where(out_row == j, f, route)
    route_ref[...] = route
    lg_scr[...] = new_logits


ROUTE_ROWS = 16
OUT_STEPS = N_TILES + 1


def _last_tile(i):
    return jnp.minimum(i, N_TILES - 1)


def _out_tile_spec(width):
    return pl.BlockSpec((TILE, width), lambda i: (_last_tile(i), 0))


def _out_mod_spec(col):
    return pl.BlockSpec((1, 1, D_MODEL), lambda i: (_last_tile(i), 0, col))


def _moe_input_specs():
    return [_full_spec((1, D_MODEL)), _out_mod_spec(3), _out_mod_spec(4),
            _full_spec((ROUTER_PAD, D_MODEL)), _full_spec((ROUTER_PAD, TILE)), _full_spec((TILE, TILE))]


def _mixer_out_shapes():
    shapes = (jax.ShapeDtypeStruct((N_TOK, D_MODEL), F32),
              jax.ShapeDtypeStruct((N_TOK, D_MODEL // 2), U32),
              jax.ShapeDtypeStruct((ROUTE_ROWS, N_TOK), F32),
              jax.ShapeDtypeStruct((N_EXPERTS, LANES), F32))
    specs = (_out_tile_spec(D_MODEL), _out_tile_spec(D_MODEL // 2),
             pl.BlockSpec((ROUTE_ROWS, TILE), lambda i: (0, jnp.maximum(i - 1, 0))),
             _full_spec((N_EXPERTS, LANES)))
    return shapes, specs


def _even_out_kernel(y_ref, u_ref, o_ref, x_ref, gate_ref, d_ref, gw_ref, gb_ref, w_ref,
                     g2_ref, sh_ref, sc_ref, rw_ref, rb_ref, tri_ref, xo_ref, h_ref, route_ref, cnt_ref, lg_scr):
    _route_init(cnt_ref, lg_scr)
    yy = y_ref[...] + d_ref[...] * u_ref[...]
    g = jax.nn.gelu(yy)
    a = g * jax.nn.sigmoid(_dot(g.astype(BF16), gw_ref[...]) + gb_ref[...])
    mix = _dot(a.astype(BF16), w_ref[0:A_WIDTH, :]) + _dot(o_ref[...], w_ref[A_WIDTH:, :])
    x = x_ref[...] + gate_ref[0] * mix
    xo_ref[...] = x
    _moe_input(x, g2_ref, sh_ref, sc_ref, rw_ref, rb_ref, tri_ref, h_ref, route_ref, cnt_ref, lg_scr)


def _even_out(y_t, u_t, o, x, modt, d_skip, glu_w, glu_b, w_out, g2, rw, rb, tri):
    tm_spec = pl.BlockSpec((TILE, A_WIDTH), lambda i: (0, _last_tile(i)))
    out_shape, out_specs = _mixer_out_shapes()
    return pl.pallas_call(
        _even_out_kernel,
        out_shape=out_shape,
        grid=(OUT_STEPS,),
        in_specs=[
            tm_spec, tm_spec, _out_tile_spec(B_WIDTH), _out_tile_spec(D_MODEL), _out_mod_spec(2),
            _full_spec((1, A_WIDTH)), _full_spec((A_WIDTH, A_WIDTH)), _full_spec((1, A_WIDTH)),
            _full_spec((A_WIDTH + B_WIDTH, D_MODEL)),
        ] + _moe_input_specs(),
        out_specs=out_specs,
        scratch_shapes=[pltpu.VMEM((N_EXPERTS, TILE), F32)],
        compiler_params=_cparams("arbitrary"),
        name="even_out",
    )(y_t, u_t, o, x, modt, d_skip, glu_w, glu_b, w_out, g2, modt, modt, rw, rb, tri)


def _rope(x, cos, sin):
    lane = lax.broadcasted_iota(jnp.int32, x.shape, 1)
    first = (lane % (HEAD_DIM // 2)) < (HEAD_DIM // 4)
    partner = jnp.where(first, pltpu.roll(x, LANES - HEAD_DIM // 4, 1), pltpu.roll(x, HEAD_DIM // 4, 1))
    return x * cos + partner * sin


def _odd_in_kernel(n_comb, slot, n_prev, x_ref, *refs):
    comb, refs = refs[:n_comb], refs[n_comb:]
    g_ref, sh_ref, sc_ref, w_ref, bd_ref, qn_ref, kn_ref, cos_ref, sin_ref = refs[:9]
    outs = refs[9 + n_prev:]
    x = _combined(x_ref, comb)
    if n_comb:
        outs[0][...] = x
        outs = outs[1:]
    q_ref, kr_ref, v_ref, kcache_ref, vcache_ref = outs
    i = pl.program_id(0)
    h = _norm_mod(x, g_ref[...], sh_ref[0], sc_ref[0]).astype(BF16)
    bd2 = bd_ref[...]
    lat = i >= N_CTX_TILES
    cos = jnp.where(lat, cos_ref[...], 1.0)
    sin = jnp.where(lat, sin_ref[...], 0.0)
    nq = C_HEADS * HEAD_DIM
    z = _dot(h, w_ref[...])
    for s in range(nq // LANES):
        q = _head_rms(z[:, s * LANES:(s + 1) * LANES], bd2, qn_ref[...])
        q_ref[:, s * LANES:(s + 1) * LANES] = (_rope(q, cos, sin) * ATT_SCALE).astype(BF16)
    k = _head_rms(z[:, nq:nq + LANES], bd2, kn_ref[...])
    v = z[:, nq + LANES:]
    kr_ref[...] = _rope(k, cos, sin)
    v_ref[...] = v
    _cache_store(slot, N_ODD, kcache_ref, k)
    _cache_store(slot, N_ODD, vcache_ref, v)


def _odd_in(x, comb, prev_caches, slot, g, modt, w, bd2, qn, kn, cos, sin):
    nq, nkv = C_HEADS * HEAD_DIM, C_KV_HEADS * HEAD_DIM
    lat_spec = pl.BlockSpec((TILE, LANES), lambda i: (jnp.maximum(i - N_CTX_TILES, 0) % LAT_TILES_PER_SEQ, 0))
    cache_shape, cache_spec = _cache_specs(slot, N_ODD, nkv)
    out_shape = [
        jax.ShapeDtypeStruct((N_TOK, nq), BF16),
        jax.ShapeDtypeStruct((N_TOK, nkv), F32),
        jax.ShapeDtypeStruct((N_TOK, nkv), F32),
        cache_shape, cache_shape,
    ]
    out_specs = [_tile_spec(nq), _tile_spec(nkv), _tile_spec(nkv), cache_spec, cache_spec]
    if comb:
        out_shape.insert(0, jax.ShapeDtypeStruct((N_TOK, D_MODEL), F32))
        out_specs.insert(0, _tile_spec(D_MODEL))
    n_in = 1 + len(comb) + 9
    aliases = {n_in + j: len(out_shape) - 2 + j for j in range(len(prev_caches))}
    return pl.pallas_call(
        functools.partial(_odd_in_kernel, len(comb), slot, len(prev_caches)),
        out_shape=tuple(out_shape),
        grid=(N_TILES,),
        in_specs=[_tile_spec(D_MODEL)] + (_comb_specs() if comb else []) + [
            _full_spec((1, D_MODEL)), _mod_spec(0), _mod_spec(1),
            _full_spec((D_MODEL, nq + 2 * nkv)), _full_spec((2 * LANES, LANES)),
            _full_spec((1, LANES)), _full_spec((1, LANES)), lat_spec, lat_spec,
        ] + [pl.BlockSpec(memory_space=pl.ANY)] * len(prev_caches),
        out_specs=tuple(out_specs),
        input_output_aliases=aliases,
        compiler_params=_cparams("arbitrary"),
        name="odd_in",
    )(x, *comb, g, modt, modt, w, bd2, qn, kn, cos, sin, *prev_caches)


def _rope_tables():
    nf = HEAD_DIM // 4
    inv = ROPE_BASE ** (-jnp.arange(nf, dtype=F32) / nf)
    t = jnp.arange(DEC_SEQ)
    pos = jnp.stack([t // GRID_W, t % GRID_W], axis=-1).astype(F32)
    ang = pos[:, :, None] * inv
    cos, sin = jnp.cos(ang), jnp.sin(ang)
    cos_h = jnp.stack([cos, cos], axis=2).reshape(DEC_SEQ, HEAD_DIM)
    sin_h = jnp.stack([-sin, sin], axis=2).reshape(DEC_SEQ, HEAD_DIM)
    return jnp.tile(cos_h, (1, 2)), jnp.tile(sin_h, (1, 2))


def _odd_out_kernel(o_ref, x_ref, gate_ref, w_ref, g2_ref, sh_ref, sc_ref, rw_ref, rb_ref, tri_ref,
                    xo_ref, h_ref, route_ref, cnt_ref, lg_scr):
    _route_init(cnt_ref, lg_scr)
    x = x_ref[...] + gate_ref[0] * _dot(o_ref[...], w_ref[...])
    xo_ref[...] = x
    _moe_input(x, g2_ref, sh_ref, sc_ref, rw_ref, rb_ref, tri_ref, h_ref, route_ref, cnt_ref, lg_scr)


def _odd_out(o, x, modt, w_o, g2, rw, rb, tri):
    nq = C_HEADS * HEAD_DIM
    out_shape, out_specs = _mixer_out_shapes()
    return pl.pallas_call(
        _odd_out_kernel,
        out_shape=out_shape,
        grid=(OUT_STEPS,),
        in_specs=[_out_tile_spec(nq), _out_tile_spec(D_MODEL), _out_mod_spec(2), _full_spec((nq, D_MODEL))]
        + _moe_input_specs(),
        out_specs=out_specs,
        scratch_shapes=[pltpu.VMEM((N_EXPERTS, TILE), F32)],
        compiler_params=_cparams("arbitrary"),
        name="odd_out",
    )(o, x, modt, w_o, g2, modt, modt, rw, rb, tri)


def _pack_bf16_pair(lo, hi):
    lo_bits = lax.bitcast_convert_type(lo.astype(BF16).astype(F32), U32)
    hi_bits = lax.bitcast_convert_type(hi.astype(BF16).astype(F32), U32)
    return (hi_bits & jnp.uint32(0xFFFF0000)) | (lo_bits >> 16)


def _unpack_bf16_pair(packed):
    lo = lax.bitcast_convert_type(packed << 16, F32).astype(BF16)
    hi = lax.bitcast_convert_type(packed & jnp.uint32(0xFFFF0000), F32).astype(BF16)
    return lo, hi


def _moe_expert_kernel(layer, be_ref, nb_ref, first_ref, slot_ref, nxt_ref, src_cur_ref, src_nxt_ref,
                       h_ref, wgu_hbm, bgu_ref, wd_hbm, bd_ref, o_ref, xg0, xg1, wgu_buf, wd_buf, wsem):
    i = pl.program_id(0)
    nb = nb_ref[0]
    half = D_MODEL // 2
    xg = (xg0, xg1)

    def gather(idx_ref, u, dst):
        for r in range(MOE_TM):
            dst[pl.ds(r, 1), :] = h_ref[pl.ds(idx_ref[0, 0, u * MOE_TM + r], 1), :]

    def weight_copies(e, slot):
        return (pltpu.make_async_copy(wgu_hbm.at[layer, e], wgu_buf.at[slot], wsem.at[0, slot]),
                pltpu.make_async_copy(wd_hbm.at[layer, e], wd_buf.at[slot], wsem.at[1, slot]))

    @pl.when(i == 0)
    def _():
        for cp in weight_copies(be_ref[0], 0):
            cp.start()
        gather(src_cur_ref, 0, xg0)

    for u in range(MOE_STEP_BLOCKS):
        blk = i * MOE_STEP_BLOCKS + u
        rows = pl.ds(u * MOE_TM, MOE_TM)
        x_cur, x_nxt = xg[u % 2], xg[(u + 1) % 2]

        @pl.when(blk < nb)
        def _():
            ws = slot_ref[blk]
            e = be_ref[blk]

            @pl.when(first_ref[blk] == 1)
            def _():
                for cp in weight_copies(e, ws):
                    cp.wait()

                @pl.when(nxt_ref[blk] >= 0)
                def _():
                    for cp in weight_copies(nxt_ref[blk], 1 - ws):
                        cp.start()

            x_lo, x_hi = _unpack_bf16_pair(x_cur[...])
            gu = (_dot(x_lo, wgu_buf[ws, 0:half, :].astype(BF16)) + _dot(x_hi, wgu_buf[ws, half:, :].astype(BF16))
                  + bgu_ref[0, pl.ds(e, 1), :])
            if u + 1 < MOE_STEP_BLOCKS:
                gather(src_cur_ref, u + 1, x_nxt)
            else:
                gather(src_nxt_ref, 0, x_nxt)
            g = jnp.minimum(gu[:, 0:D_EXPERT], SWIGLU_LIMIT)
            lin = jnp.clip(gu[:, D_EXPERT:], -SWIGLU_LIMIT, SWIGLU_LIMIT)
            act = g * jax.nn.sigmoid(SWIGLU_ALPHA * g) * (lin + 1.0)
            o_ref[rows, :] = _dot(act.astype(BF16), wd_buf[ws].astype(BF16)) + bd_ref[0, pl.ds(e, 1), :]

        @pl.when(blk >= nb)
        def _():
            o_ref[rows, :] = jnp.zeros((MOE_TM, D_MODEL), F32)


def _moe_experts(layer, plan, h_packed, w_gu, b_gu, w_down, b_down):
    block_e, n_used, first, slot, nxt, row_tok = plan
    half = D_MODEL // 2
    steps = MOE_BLOCKS // MOE_STEP_BLOCKS
    step_rows = MOE_STEP_BLOCKS * MOE_TM
    idx_spec = lambda f: pl.BlockSpec((1, 1, step_rows), f, memory_space=pltpu.SMEM)
    hbm = pl.BlockSpec(memory_space=pl.ANY)
    grid_spec = pltpu.PrefetchScalarGridSpec(
        num_scalar_prefetch=5,
        grid=(steps,),
        in_specs=[
            idx_spec(lambda i, *_: (i, 0, 0)),
            idx_spec(lambda i, *_: (jnp.minimum(i + 1, steps - 1), 0, 0)),
            pl.BlockSpec((N_TOK, half), lambda i, *_: (0, 0), pipeline_mode=pl.Buffered(1)),
            hbm,
            pl.BlockSpec((1, N_EXPERTS, 2 * D_EXPERT), lambda i, *_: (layer, 0, 0)),
            hbm,
            pl.BlockSpec((1, N_EXPERTS, D_MODEL), lambda i, *_: (layer, 0, 0)),
        ],
        out_specs=pl.BlockSpec((step_rows, D_MODEL), lambda i, *_: (i, 0)),
        scratch_shapes=[
            pltpu.VMEM((MOE_TM, half), U32), pltpu.VMEM((MOE_TM, half), U32),
            pltpu.VMEM((2, D_MODEL, 2 * D_EXPERT), F32), pltpu.VMEM((2, D_EXPERT, D_MODEL), F32),
            pltpu.SemaphoreType.DMA((2, 2)),
        ],
    )
    rows = row_tok.reshape(steps, 1, step_rows)
    return pl.pallas_call(
        functools.partial(_moe_expert_kernel, layer),
        out_shape=jax.ShapeDtypeStruct((MOE_ROWS, D_MODEL), F32),
        grid_spec=grid_spec,
        compiler_params=pltpu.CompilerParams(dimension_semantics=("arbitrary",), vmem_limit_bytes=MOE_VMEM_LIMIT),
        name="moe_experts",
    )(block_e, n_used, first, slot, nxt, rows, rows, h_packed, w_gu, b_gu, w_down, b_down)


def _moe_combine_kernel(x_ref, *refs):
    ctx_ref, lat_ref = refs[N_COMB:]
    x = _combined(x_ref, refs[:N_COMB])

    @pl.when(pl.program_id(0) < N_CTX_TILES)
    def _():
        ctx_ref[...] = x

    @pl.when(pl.program_id(0) >= N_CTX_TILES)
    def _():
        lat_ref[...] = x


def _moe_combine(x, comb):
    return pl.pallas_call(
        _moe_combine_kernel,
        out_shape=(jax.ShapeDtypeStruct((N_CTX_TOK, D_MODEL), F32), jax.ShapeDtypeStruct((N_LAT_TOK, D_MODEL), F32)),
        grid=(N_TILES,),
        in_specs=[_tile_spec(D_MODEL)] + _comb_specs(),
        out_specs=(pl.BlockSpec((TILE, D_MODEL), lambda i: (jnp.minimum(i, N_CTX_TILES - 1), 0)),
                   pl.BlockSpec((TILE, D_MODEL), lambda i: (jnp.maximum(i - N_CTX_TILES, 0), 0))),
        compiler_params=_cparams("arbitrary"),
        name="moe_combine",
    )(x, *comb)


def _lut(idx, table):
    n = table.shape[0]
    hit = idx[:, None] == jnp.arange(n, dtype=idx.dtype)[None, :]
    return jnp.sum(jnp.where(hit, table[None, :], 0), axis=1)


def _moe_route(route, cnt):
    i32 = jnp.int32
    route = route.T
    gates = route[:, TOP_K:2 * TOP_K]
    flat_e = route[:, 0:TOP_K].astype(i32).reshape(N_ASG)
    rank = route[:, 2 * TOP_K:3 * TOP_K].astype(i32).reshape(N_ASG)
    asg = jnp.arange(N_ASG, dtype=i32)
    experts = jnp.arange(N_EXPERTS, dtype=i32)
    counts = cnt[:, 0].astype(i32)
    pad_counts = (counts + MOE_TM - 1) // MOE_TM * MOE_TM
    pad_end = jnp.cumsum(pad_counts)
    n_used = pad_end[-1] // MOE_TM
    dest = _lut(flat_e, pad_end - pad_counts) + rank
    fill_end = jnp.cumsum(pad_counts - counts)
    filler = jnp.arange(MOE_ROWS - N_ASG, dtype=i32)
    fill_e = jnp.sum((filler[:, None] >= fill_end[None, :]).astype(i32), axis=1)
    keys = jnp.concatenate([flat_e * ASG_STRIDE + asg, fill_e * ASG_STRIDE + (ASG_STRIDE - 1)])
    low = jnp.sort(keys) % ASG_STRIDE
    row_tok = jnp.where(low < N_ASG, low // TOP_K, 0)
    blk = jnp.arange(MOE_BLOCKS, dtype=i32)
    block_e = jnp.sum((blk[:, None] * MOE_TM >= pad_end[None, :]).astype(i32), axis=1)
    last_e = jnp.max(jnp.where(counts > 0, experts, 0))
    block_e = jnp.where(blk < n_used, jnp.minimum(block_e, N_EXPERTS - 1), last_e)
    first = jnp.concatenate([jnp.ones((1,), i32), (block_e[1:] != block_e[:-1]).astype(i32)])
    slot = (jnp.cumsum(first) - 1) % 2
    later = (experts[None, :] > experts[:, None]) & (counts[None, :] > 0)
    nxt_of = jnp.min(jnp.where(later, experts[None, :], N_EXPERTS), axis=1)
    nxt = _lut(block_e, jnp.where(nxt_of < N_EXPERTS, nxt_of, -1))
    plan = (block_e, n_used.reshape(1).astype(i32), first, slot.astype(i32), nxt.astype(i32), row_tok)
    return gates, dest.reshape(N_TOK, TOP_K), plan


def _moe(layer, h_packed, route, cnt, modt, w_gu, b_gu, w_down, b_down):
    gates, dest, plan = _moe_route(route, cnt)
    y_rows = _moe_experts(layer, plan, h_packed, w_gu, b_gu, w_down, b_down)
    return (modt, gates) + tuple(y_rows[dest[:, k]] for k in range(TOP_K))


def kernel(x_prompt, x_sample, cache_nat_k, cache_nat_v, cache_swa_k, cache_swa_v, state_ssm, c, c_ctx,
           norm1_g, norm2_g, mod_w, mod_b,
           ab_w_in, ab_w_out, ssm_lam_re, ssm_lam_im, ssm_log_dt, ssm_b_re, ssm_b_im, ssm_c_re, ssm_c_im,
           ssm_d, ssm_glu_w, ssm_glu_b, nat_qn, nat_kn, nat_rpb,
           swa_w_qkv, swa_w_o, swa_qn, swa_kn, swa_sink,
           moe_router_w, moe_router_b, moe_w_gu, moe_b_gu, moe_w_down, moe_b_down):
    x = jnp.concatenate([x_prompt.reshape(N_CTX_TOK, D_MODEL), x_sample.reshape(N_LAT_TOK, D_MODEL)], axis=0)
    cond = jnp.zeros((SUBLANES, D_MODEL), F32).at[0].set(c_ctx).at[1:1 + DEC_BATCH].set(c)
    mod = _modulation(cond, mod_w, mod_b)
    tile_row = np.concatenate([np.zeros(N_CTX_TILES, np.int32),
                               1 + np.arange(N_TILES - N_CTX_TILES, dtype=np.int32) // LAT_TILES_PER_SEQ])
    head_gain = lambda gn: jnp.tile(gn, 2).reshape(1, LANES)
    bd = np.kron(np.eye(2, dtype=np.float32), np.full((HEAD_DIM, HEAD_DIM), 1.0 / HEAD_DIM, np.float32))
    bd2 = jnp.asarray(np.concatenate([bd, bd], axis=0), BF16)
    rope_cos, rope_sin = _rope_tables()
    tri = jnp.asarray(np.triu(np.ones((TILE, TILE), np.float32), 1), BF16)
    router_w = jnp.pad(moe_router_w, ((0, 0), (0, 0), (0, ROUTER_PAD - N_EXPERTS))).transpose(0, 2, 1)
    router_b = jnp.broadcast_to(jnp.pad(moe_router_b, ((0, 0), (0, ROUTER_PAD - N_EXPERTS)))[:, :, None],
                                (DEPTH, ROUTER_PAD, TILE))
    nkv = C_KV_HEADS * HEAD_DIM

    ssm_out = []
    nat_caches, swa_caches = (), ()
    comb = ()
    for l in range(DEPTH):
        modt = mod[l][tile_row].reshape(N_TILES, 1, N_MOD * D_MODEL)
        g1 = norm1_g[l].reshape(1, D_MODEL)
        moe_in = (norm2_g[l].reshape(1, D_MODEL), router_w[l], router_b[l], tri)
        i = l // 2
        if l % 2 == 0:
            outs = _even_in(x, comb, nat_caches, i, g1, modt, ab_w_in[i].astype(BF16), bd2,
                            head_gain(nat_qn[i]), head_gain(nat_kn[i]))
            if comb:
                x, outs = outs[0], outs[1:]
            u_t, q, k, v = outs[:4]
            nat_caches = tuple(outs[4:])
            a_b, w_bd, c_bd = _s5_params(ssm_lam_re[i], ssm_lam_im[i], ssm_log_dt[i], ssm_b_re[i], ssm_b_im[i],
                                         ssm_c_re[i], ssm_c_im[i])
            st = state_ssm[:, i].reshape(DEC_BATCH, 2, 2, SSM_LANES).transpose(1, 2, 0, 3)
            s0 = jnp.zeros((2, 2, N_TILES, SSM_LANES), F32)
            first = N_CTX_TILES + LAT_TILES_PER_SEQ * np.arange(DEC_BATCH)
            s0 = s0.at[0, :, first].set(st[0].transpose(1, 0, 2))
            s0 = s0.at[1, :, first + LAT_TILES_PER_SEQ - 1].set(st[1].transpose(1, 0, 2))
            y_t, fin = _s5_scan(u_t.reshape(TILE, N_TILES, A_WIDTH), a_b, w_bd, c_bd, s0)
            o = _nat_attention(q, k, v,
                               cache_nat_k[:, i].reshape(DEC_BATCH, PAST_LEN, B_WIDTH),
                               cache_nat_v[:, i].reshape(DEC_BATCH, PAST_LEN, B_WIDTH),
                               _nat_bias(nat_rpb[i]))
            x, h_packed, route, cnt = _even_out(y_t.reshape(TILE, N_TILES * A_WIDTH), u_t, o, x, modt,
                                            ssm_d[i].reshape(1, A_WIDTH), ssm_glu_w[i].astype(BF16),
                                            ssm_glu_b[i].reshape(1, A_WIDTH), ab_w_out[i].astype(BF16), *moe_in)
            ssm_out.append(fin[:, :, :N_CTX_TILES].transpose(2, 0, 1, 3).reshape(BATCH, 2, 2, A_GROUPS, A_STATE))
        else:
            outs = _odd_in(x, comb, swa_caches, i, g1, modt, swa_w_qkv[i].astype(BF16), bd2,
                           head_gain(swa_qn[i]), head_gain(swa_kn[i]), rope_cos, rope_sin)
            if comb:
                x, outs = outs[0], outs[1:]
            q, k_rot, v = outs[:3]
            swa_caches = tuple(outs[3:])
            o = _swa_attention(q, k_rot, v,
                               cache_swa_k[:, i].reshape(DEC_BATCH, PAST_LEN, nkv),
                               cache_swa_v[:, i].reshape(DEC_BATCH, PAST_LEN, nkv), swa_sink[i])
            x, h_packed, route, cnt = _odd_out(o, x, modt, swa_w_o[i].astype(BF16), *moe_in)
        comb = _moe(l, h_packed, route, cnt, modt, moe_w_gu, moe_b_gu, moe_w_down, moe_b_down)

    y_ctx, y_lat = _moe_combine(x, comb)
    y_prompt = y_ctx.reshape(BATCH, SEQ, D_MODEL)
    y_sample = y_lat.reshape(DEC_BATCH, DEC_SEQ, D_MODEL)
    nat_shape = (BATCH, N_EVEN, SEQ, B_HEADS, HEAD_DIM)
    swa_shape = (BATCH, N_ODD, SEQ, C_KV_HEADS, HEAD_DIM)
    return (y_prompt, y_sample,
            nat_caches[0].reshape(nat_shape), nat_caches[1].reshape(nat_shape),
            swa_caches[0].reshape(swa_shape), swa_caches[1].reshape(swa_shape),
            jnp.stack(ssm_out, axis=1))
```

```python
import functools

import jax
import jax.numpy as jnp
import numpy as np
from jax import lax
from jax.experimental import pallas as pl
from jax.experimental.pallas import tpu as pltpu

F32 = jnp.float32
BF16 = jnp.bfloat16
U32 = jnp.uint32

D_MODEL = 1024
BATCH = 32
SEQ = 256
DEPTH = 4
DEC_BATCH = 2
DEC_SEQ = 1024
PAST_LEN = 256
GRID_W = 64
HEAD_DIM = 64
N_MOD = 6
N_EVEN = (DEPTH + 1) // 2
N_ODD = DEPTH // 2
A_WIDTH = 512
A_GROUP_CH = 16
A_GROUPS = 32
A_STATE = 64
B_HEADS = 8
B_WIDTH = 512
WIN_R_MAX = 8
WIN_C = 16
C_HEADS = 16
C_KV_HEADS = 2
C_GROUP = 8
WINDOW = 128
ROPE_BASE = 10000.0
N_EXPERTS = 32
TOP_K = 4
D_EXPERT = 1024
SWIGLU_LIMIT = 7.0
SWIGLU_ALPHA = 1.702
EPS = 1e-6
NEG_INF = -1e30

LANES = 128
SUBLANES = 8
TILE = 256
N_CTX_TOK = BATCH * SEQ
N_LAT_TOK = DEC_BATCH * DEC_SEQ
N_TOK = N_CTX_TOK + N_LAT_TOK
N_TILES = N_TOK // TILE
N_CTX_TILES = N_CTX_TOK // TILE
LAT_TILES_PER_SEQ = DEC_SEQ // TILE
SSM_LANES = A_GROUPS * A_STATE
SSM_SLABS = A_WIDTH // LANES
SSM_SLAB_STATES = SSM_LANES // SSM_SLABS
SCAN_ROWS = 8
SCAN_GROUPS = N_TILES // SCAN_ROWS
SCAN_CHUNK = 128
SCAN_LANE_PARTS = 2
MOE_TM = 256
MOE_STEP_BLOCKS = 4
N_ASG = N_TOK * TOP_K
ASG_STRIDE = 1 << 16
MOE_BLOCKS = N_ASG // MOE_TM + N_EXPERTS
MOE_ROWS = MOE_BLOCKS * MOE_TM
ROUTER_PAD = LANES
VMEM_LIMIT = 56 * 1024 * 1024
MOE_VMEM_LIMIT = 60 * 1024 * 1024
ATT_SCALE = HEAD_DIM ** -0.5
LAT_GROUP_W = 4 * HEAD_DIM
CTX_GROUP_W = 2 * HEAD_DIM


def _cparams(*sem):
    return pltpu.CompilerParams(dimension_semantics=sem, vmem_limit_bytes=VMEM_LIMIT)


def _dot(a, b):
    return jnp.dot(a, b, preferred_element_type=F32)


def _dot_nt(a, b):
    return lax.dot_general(a, b, (((1,), (1,)), ((), ())), preferred_element_type=F32)


def _split(a):
    hi = a.astype(BF16)
    lo = (a - hi.astype(F32)).astype(BF16)
    return hi, lo


def _dot3(a, b):
    a_hi, a_lo = _split(a)
    b_hi, b_lo = _split(b)
    return _dot(a_hi, b_hi) + (_dot(a_hi, b_lo) + _dot(a_lo, b_hi))


def _silu(x):
    return x * jax.nn.sigmoid(x)


def _norm_mod(x, g, shift, scale):
    y = x * lax.rsqrt(jnp.mean(x * x, axis=-1, keepdims=True) + EPS)
    return (y * g) * (1.0 + scale) + shift


def _head_rms(x, bd2, gain):
    sq_hi, sq_lo = _split(x * x)
    ms = _dot(jnp.concatenate([sq_hi, sq_lo], axis=1), bd2)
    return x * lax.rsqrt(ms + EPS) * gain


def _combined(x_ref, comb_refs):
    x = x_ref[...]
    if not comb_refs:
        return x
    gmod_ref, gates_ref = comb_refs[0], comb_refs[1]
    gates = gates_ref[...]
    acc = gates[:, 0:1] * comb_refs[2][...]
    for k in range(1, TOP_K):
        acc = acc + gates[:, k:k + 1] * comb_refs[2 + k][...]
    return x + gmod_ref[0] * acc


def _comb_specs():
    return [_mod_spec(5), _tile_spec(TOP_K)] + [_tile_spec(D_MODEL)] * TOP_K


N_COMB = 2 + TOP_K


def _lane_half(shape, half):
    lane = lax.broadcasted_iota(jnp.int32, shape, len(shape) - 1)
    return (lane < HEAD_DIM) if half == 0 else (lane >= HEAD_DIM)


def _mod_kernel(cond_ref, w_ref, b_ref, o_ref):
    o_ref[0] = _dot3(_silu(cond_ref[...]), w_ref[0]) + b_ref[0]


def _modulation(cond, mod_w, mod_b):
    nc = 2
    cw = N_MOD * D_MODEL // nc
    return pl.pallas_call(
        _mod_kernel,
        out_shape=jax.ShapeDtypeStruct((DEPTH, SUBLANES, N_MOD * D_MODEL), F32),
        grid=(DEPTH, nc),
        in_specs=[
            pl.BlockSpec((SUBLANES, D_MODEL), lambda l, c: (0, 0)),
            pl.BlockSpec((1, D_MODEL, cw), lambda l, c: (l, 0, c)),
            pl.BlockSpec((1, 1, cw), lambda l, c: (l, 0, c)),
        ],
        out_specs=pl.BlockSpec((1, SUBLANES, cw), lambda l, c: (l, 0, c)),
        compiler_params=_cparams("arbitrary", "arbitrary"),
        name="modulation",
    )(cond, mod_w, mod_b.reshape(DEPTH, 1, N_MOD * D_MODEL))


def _mod_spec(col):
    return pl.BlockSpec((1, 1, D_MODEL), lambda i: (i, 0, col))


def _tile_spec(width):
    return pl.BlockSpec((TILE, width), lambda i: (i, 0))


def _ctx_tile_spec(width):
    return pl.BlockSpec((TILE, width), lambda i: (jnp.minimum(i, N_CTX_TILES - 1), 0))


def _full_spec(shape):
    nd = len(shape)
    return pl.BlockSpec(shape, lambda i: (0,) * nd)


def _cache_store(slot, n_slots, ref, value):
    @pl.when(pl.program_id(0) < N_CTX_TILES)
    def _():
        if slot == 0:
            ref[0, 0] = value
            for other in range(1, n_slots):
                ref[0, other] = jnp.zeros_like(value)
        else:
            ref[0, 0] = value


def _cache_specs(slot, n_slots, width):
    shape = jax.ShapeDtypeStruct((BATCH, n_slots, SEQ, width), F32)
    seq = lambda i: jnp.minimum(i, N_CTX_TILES - 1)
    if slot == 0:
        return shape, pl.BlockSpec((1, n_slots, SEQ, width), lambda i: (seq(i), 0, 0, 0))
    return shape, pl.BlockSpec((1, 1, SEQ, width), lambda i: (seq(i), slot, 0, 0))


def _even_in_kernel(n_comb, slot, n_prev, x_ref, *refs):
    comb, refs = refs[:n_comb], refs[n_comb:]
    g_ref, sh_ref, sc_ref, w_ref, bd_ref, qn_ref, kn_ref = refs[:7]
    outs = refs[7 + n_prev:]
    x = _combined(x_ref, comb)
    if n_comb:
        outs[0][...] = x
        outs = outs[1:]
    u_ref, q_ref, k_ref, v_ref, kc_ref, vc_ref = outs
    h = _norm_mod(x, g_ref[...], sh_ref[0], sc_ref[0]).astype(BF16)
    bd2 = bd_ref[...]
    z = _dot(h, w_ref[...])
    u_ref[...] = z[:, 0:A_WIDTH]
    ks = []
    for s in range(B_WIDTH // LANES):
        lo = A_WIDTH + s * LANES
        q = _head_rms(z[:, lo:lo + LANES], bd2, qn_ref[...])
        q_ref[:, s * LANES:(s + 1) * LANES] = (q * ATT_SCALE).astype(BF16)
        lo = A_WIDTH + B_WIDTH + s * LANES
        ks.append(_head_rms(z[:, lo:lo + LANES], bd2, kn_ref[...]))
    k = jnp.concatenate(ks, axis=1)
    v = z[:, A_WIDTH + 2 * B_WIDTH:]
    k_ref[...] = k
    v_ref[...] = v
    _cache_store(slot, N_EVEN, kc_ref, k)
    _cache_store(slot, N_EVEN, vc_ref, v)


def _even_in(x, comb, prev_caches, slot, g, modt, w, bd2, qn, kn):
    n_out = A_WIDTH + 3 * B_WIDTH
    cache_shape, cache_spec = _cache_specs(slot, N_EVEN, B_WIDTH)
    out_shape = [
        jax.ShapeDtypeStruct((TILE, N_TILES * A_WIDTH), F32),
        jax.ShapeDtypeStruct((N_TOK, B_WIDTH), BF16),
        jax.ShapeDtypeStruct((N_TOK, B_WIDTH), F32),
        jax.ShapeDtypeStruct((N_TOK, B_WIDTH), F32),
        cache_shape, cache_shape,
    ]
    out_specs = [pl.BlockSpec((TILE, A_WIDTH), lambda i: (0, i)),
                 _tile_spec(B_WIDTH), _tile_spec(B_WIDTH), _tile_spec(B_WIDTH), cache_spec, cache_spec]
    if comb:
        out_shape.insert(0, jax.ShapeDtypeStruct((N_TOK, D_MODEL), F32))
        out_specs.insert(0, _tile_spec(D_MODEL))
    n_in = 1 + len(comb) + 7
    aliases = {n_in + j: len(out_shape) - 2 + j for j in range(len(prev_caches))}
    return pl.pallas_call(
        functools.partial(_even_in_kernel, len(comb), slot, len(prev_caches)),
        out_shape=tuple(out_shape),
        grid=(N_TILES,),
        in_specs=[_tile_spec(D_MODEL)] + (_comb_specs() if comb else []) + [
            _full_spec((1, D_MODEL)), _mod_spec(0), _mod_spec(1),
            _full_spec((D_MODEL, n_out)), _full_spec((2 * LANES, LANES)),
            _full_spec((1, LANES)), _full_spec((1, LANES)),
        ] + [pl.BlockSpec(memory_space=pl.ANY)] * len(prev_caches),
        out_specs=tuple(out_specs),
        input_output_aliases=aliases,
        compiler_params=_cparams("arbitrary"),
        name="even_in",
    )(x, *comb, g, modt, modt, w, bd2, qn, kn, *prev_caches)


def _cmul(ar, ai, br, bi):
    return ar * br - ai * bi, ar * bi + ai * br


def _s5_kernel(u_ref, a_ref, w_ref, c_ref, s0_ref, y_ref, fin_ref, xr, xi, st_r, st_i):
    grp = pl.program_id(0)
    drn = pl.program_id(1)
    n_chunks = TILE // SCAN_CHUNK
    rows = SCAN_CHUNK * SCAN_ROWS
    part = SSM_LANES // SCAN_LANE_PARTS

    def run(store):
        @pl.loop(0, n_chunks)
        def _(c):
            cc = jnp.where(drn == 0, c, n_chunks - 1 - c)
            t0 = pl.multiple_of(cc * SCAN_CHUNK, SCAN_CHUNK)
            uu = u_ref[pl.ds(t0, SCAN_CHUNK), :, :]
            for s in range(SSM_SLABS):
                us = uu[:, :, s * LANES:(s + 1) * LANES].reshape(rows, LANES).astype(BF16)
                cols = slice(s * SSM_SLAB_STATES, (s + 1) * SSM_SLAB_STATES)
                xr[:, cols] = _dot(us, w_ref[0, 0, s])
                xi[:, cols] = _dot(us, w_ref[0, 1, s])
            for p in range(SCAN_LANE_PARTS):
                ls = slice(p * part, (p + 1) * part)
                ar = a_ref[0, 0, :, ls]
                ai = a_ref[0, 1, :, ls]

                def step(j, carry):
                    sr, si = carry
                    tt = jnp.where(drn == 0, j, SCAN_CHUNK - 1 - j)
                    r0 = pl.multiple_of(tt * SCAN_ROWS, SCAN_ROWS)
                    nr = ar * sr - ai * si + xr[pl.ds(r0, SCAN_ROWS), ls]
                    ni = ar * si + ai * sr + xi[pl.ds(r0, SCAN_ROWS), ls]
                    if store:
                        xr[pl.ds(r0, SCAN_ROWS), ls] = nr
                        xi[pl.ds(r0, SCAN_ROWS), ls] = ni
                    return nr, ni

                sr, si = lax.fori_loop(0, SCAN_CHUNK, step, (st_r[:, ls], st_i[:, ls]), unroll=4)
                st_r[:, ls] = sr
                st_i[:, ls] = si
            if store:
                for s in range(SSM_SLABS):
                    cols = slice(s * SSM_SLAB_STATES, (s + 1) * SSM_SLAB_STATES)
                    ys = (_dot(xr[:, cols].astype(BF16), c_ref[0, 0, s])
                          - _dot(xi[:, cols].astype(BF16), c_ref[0, 1, s]))
                    ys = ys.reshape(SCAN_CHUNK, SCAN_ROWS, LANES)
                    lanes = slice(s * LANES, (s + 1) * LANES)

                    @pl.when(drn == 0)
                    def _():
                        y_ref[pl.ds(t0, SCAN_CHUNK), :, lanes] = ys

                    @pl.when(drn != 0)
                    def _():
                        y_ref[pl.ds(t0, SCAN_CHUNK), :, lanes] += ys

    st_r[...] = s0_ref[0, 0]
    st_i[...] = s0_ref[0, 1]

    @pl.when(grp == SCAN_GROUPS - 1)
    def _():
        st_r[...] = jnp.zeros_like(st_r)
        st_i[...] = jnp.zeros_like(st_i)
        run(False)
        pr, pi = a_ref[0, 0], a_ref[0, 1]
        for _ in range(8):
            pr, pi = _cmul(pr, pi, pr, pi)
        fr, fi = st_r[...], st_i[...]
        s0r, s0i = s0_ref[0, 0], s0_ref[0, 1]
        row = lax.broadcasted_iota(jnp.int32, (SCAN_ROWS, SSM_LANES), 0)
        quarter = row % LAT_TILES_PER_SEQ
        fwd = drn == 0
        keep = quarter != jnp.where(fwd, 0, LAT_TILES_PER_SEQ - 1)
        ir, ii = s0r, s0i
        for _ in range(LAT_TILES_PER_SEQ - 1):
            nr, ni = _cmul(pr, pi, ir, ii)
            nr, ni = nr + fr, ni + fi
            nr = jnp.where(fwd, pltpu.roll(nr, 1, 0), pltpu.roll(nr, SCAN_ROWS - 1, 0))
            ni = jnp.where(fwd, pltpu.roll(ni, 1, 0), pltpu.roll(ni, SCAN_ROWS - 1, 0))
            ir = s0r + jnp.where(keep, nr, 0.0)
            ii = s0i + jnp.where(keep, ni, 0.0)
        st_r[...] = ir
        st_i[...] = ii

    run(True)
    fin_ref[0, 0] = st_r[...]
    fin_ref[0, 1] = st_i[...]


def _s5_scan(u_tb, a_b, w_bd, c_bd, s0):
    rows = SCAN_CHUNK * SCAN_ROWS
    return pl.pallas_call(
        _s5_kernel,
        out_shape=(
            jax.ShapeDtypeStruct((TILE, N_TILES, A_WIDTH), F32),
            jax.ShapeDtypeStruct((2, 2, N_TILES, SSM_LANES), F32),
        ),
        grid=(SCAN_GROUPS, 2),
        in_specs=[
            pl.BlockSpec((TILE, SCAN_ROWS, A_WIDTH), lambda g, d: (0, g, 0)),
            pl.BlockSpec((1, 2, SCAN_ROWS, SSM_LANES), lambda g, d: (d, 0, 0, 0)),
            pl.BlockSpec((1, 2, SSM_SLABS, LANES, SSM_SLAB_STATES), lambda g, d: (d, 0, 0, 0, 0)),
            pl.BlockSpec((1, 2, SSM_SLABS, SSM_SLAB_STATES, LANES), lambda g, d: (d, 0, 0, 0, 0)),
            pl.BlockSpec((1, 2, SCAN_ROWS, SSM_LANES), lambda g, d: (d, 0, g, 0)),
        ],
        out_specs=(
            pl.BlockSpec((TILE, SCAN_ROWS, A_WIDTH), lambda g, d: (0, g, 0)),
            pl.BlockSpec((1, 2, SCAN_ROWS, SSM_LANES), lambda g, d: (d, 0, g, 0)),
        ),
        scratch_shapes=[
            pltpu.VMEM((rows, SSM_LANES), F32), pltpu.VMEM((rows, SSM_LANES), F32),
            pltpu.VMEM((SCAN_ROWS, SSM_LANES), F32), pltpu.VMEM((SCAN_ROWS, SSM_LANES), F32),
        ],
        compiler_params=_cparams("arbitrary", "arbitrary"),
        name="s5_scan",
    )(u_tb, a_b, w_bd, c_bd, s0)


def _s5_params(lam_re, lam_im, log_dt, b_re, b_im, c_re, c_im):
    dt = jnp.exp(log_dt)[..., None]
    mag = jnp.exp(lam_re * dt)
    ab_re, ab_im = mag * jnp.cos(lam_im * dt), mag * jnp.sin(lam_im * dt)
    den = lam_re * lam_re + lam_im * lam_im
    nr, ni = ab_re - 1.0, ab_im
    f_re = (nr * lam_re + ni * lam_im) / den
    f_im = (ni * lam_re - nr * lam_im) / den
    bb_re = f_re[..., None] * b_re - f_im[..., None] * b_im
    bb_im = f_re[..., None] * b_im + f_im[..., None] * b_re
    a_b = jnp.stack([ab_re, ab_im], axis=1).reshape(2, 2, 1, SSM_LANES)
    a_b = jnp.broadcast_to(a_b, (2, 2, SCAN_ROWS, SSM_LANES))
    gps = A_GROUPS // SSM_SLABS
    eye = jnp.eye(gps, dtype=F32)

    def in_bd(bb):
        bb = bb.reshape(2, SSM_SLABS, gps, A_STATE, A_GROUP_CH)
        m = jnp.einsum('dsgph,gk->dsghkp', bb, eye)
        return m.reshape(2, SSM_SLABS, LANES, SSM_SLAB_STATES)

    def out_bd(cc):
        cc = cc.reshape(2, SSM_SLABS, gps, A_GROUP_CH, A_STATE)
        m = jnp.einsum('dsghp,gk->dsgpkh', cc, eye)
        return m.reshape(2, SSM_SLABS, SSM_SLAB_STATES, LANES)

    w_bd = jnp.stack([in_bd(bb_re), in_bd(bb_im)], axis=1).astype(BF16)
    c_bd = jnp.stack([out_bd(c_re), out_bd(c_im)], axis=1).astype(BF16)
    return a_b, w_bd, c_bd


def _softmax_pv(scores, values, sink=None):
    m = functools.reduce(jnp.maximum, [jnp.max(s, axis=-1, keepdims=True) for s in scores])
    if sink is not None:
        m = jnp.maximum(m, sink)
    den = None
    acc = None
    for s, v in zip(scores, values):
        e = jnp.exp(s - m)
        d = jnp.sum(e, axis=-1, keepdims=True)
        o = _dot(e.astype(BF16), v)
        den = d if den is None else den + d
        acc = o if acc is None else acc + o
    if sink is not None:
        den = den + jnp.exp(sink - m)
    return acc / den


def _head_group(qs, ks, vs, bias=None, mask=None, sinks=None):
    m, g = qs.shape[0], qs.shape[1] // HEAD_DIM
    head = lax.broadcasted_iota(jnp.int32, qs.shape, 1) // HEAD_DIM
    zero = jnp.zeros_like(qs)
    q_all = jnp.concatenate([jnp.where(head == j, qs, zero) for j in range(g)], axis=0)
    s = _dot_nt(q_all, ks)
    if bias is not None:
        s = s + bias
    if mask is not None:
        s = jnp.where(jnp.concatenate([mask] * g, axis=0), s, NEG_INF)
    sink = None
    if sinks is not None:
        row = lax.broadcasted_iota(jnp.int32, (g * m, 1), 0)
        sink = sinks[0]
        for j in range(1, g):
            sink = jnp.where(row >= j * m, sinks[j], sink)
    o_all = _softmax_pv([s], [vs], sink=sink)
    out = o_all[0:m]
    for j in range(1, g):
        out = jnp.where(head == j, o_all[j * m:(j + 1) * m], out)
    return out


def _nat_ctx_kernel(q_ref, k_ref, v_ref, o_ref):
    @pl.when(pl.program_id(0) >= N_CTX_TILES)
    def _():
        o_ref[...] = jnp.zeros_like(o_ref)

    @pl.when(pl.program_id(0) < N_CTX_TILES)
    def _():
        for p in range(B_WIDTH // CTX_GROUP_W):
            ls = slice(p * CTX_GROUP_W, (p + 1) * CTX_GROUP_W)
            out = _head_group(q_ref[:, ls], k_ref[:, ls].astype(BF16), v_ref[:, ls].astype(BF16))
            o_ref[:, ls] = out.astype(BF16)


def _nat_lat_kernel(o_all_ref, q_ref, k_ref, v_ref, kc_ref, vc_ref, bias_ref, o_ref):
    del o_all_ref
    r = pl.program_id(0)
    wr = WIN_R_MAX
    rs = jnp.clip(r - wr // 2, 0, DEC_SEQ // GRID_W - wr)
    k0 = pl.multiple_of(rs * GRID_W, GRID_W)
    n_nb = wr * GRID_W
    g = LAT_GROUP_W // HEAD_DIM
    for p in range(B_WIDTH // LAT_GROUP_W):
        ls = slice(p * LAT_GROUP_W, (p + 1) * LAT_GROUP_W)
        for b in range(DEC_BATCH):
            ks = jnp.concatenate([k_ref[b, pl.ds(k0, n_nb), ls], kc_ref[b, :, ls]], axis=0).astype(BF16)
            vs = jnp.concatenate([v_ref[b, pl.ds(k0, n_nb), ls], vc_ref[b, :, ls]], axis=0).astype(BF16)
            bias = jnp.concatenate([bias_ref[g * p + j, 0] for j in range(g)], axis=0)
            o_ref[b, :, ls] = _head_group(q_ref[b, :, ls], ks, vs, bias=bias).astype(BF16)


def _nat_attention(q, k, v, kc, vc, bias):
    o_ctx = pl.pallas_call(
        _nat_ctx_kernel,
        out_shape=jax.ShapeDtypeStruct((N_TOK, B_WIDTH), BF16),
        grid=(N_TILES,),
        in_specs=[_ctx_tile_spec(B_WIDTH)] * 3,
        out_specs=_tile_spec(B_WIDTH),
        compiler_params=_cparams("arbitrary"),
        name="nat_ctx",
    )(q, k, v)
    rows = DEC_SEQ // GRID_W
    n_seq = N_TOK // DEC_SEQ
    lat = N_CTX_TOK // DEC_SEQ // DEC_BATCH
    as_seq = lambda a: a.reshape(n_seq, DEC_SEQ, B_WIDTH)
    row_spec = pl.BlockSpec((DEC_BATCH, GRID_W, B_WIDTH), lambda r: (lat, r, 0))
    seq_spec = pl.BlockSpec((DEC_BATCH, DEC_SEQ, B_WIDTH), lambda r: (lat, 0, 0))
    ctx_spec = pl.BlockSpec((DEC_BATCH, PAST_LEN, B_WIDTH), lambda r: (0, 0, 0))

    def bias_idx(r):
        return (0, r - jnp.clip(r - WIN_R_MAX // 2, 0, rows - WIN_R_MAX), 0, 0)

    o = pl.pallas_call(
        _nat_lat_kernel,
        out_shape=jax.ShapeDtypeStruct((n_seq, DEC_SEQ, B_WIDTH), BF16),
        grid=(rows,),
        in_specs=[
            pl.BlockSpec(memory_space=pl.ANY), row_spec, seq_spec, seq_spec, ctx_spec, ctx_spec,
            pl.BlockSpec((B_HEADS, 1, GRID_W, WIN_R_MAX * GRID_W + PAST_LEN), bias_idx),
        ],
        out_specs=row_spec,
        input_output_aliases={0: 0},
        compiler_params=_cparams("arbitrary"),
        name="nat_lat",
    )(as_seq(o_ctx), as_seq(q), as_seq(k), as_seq(v), kc, vc, bias)
    return o.reshape(N_TOK, B_WIDTH)


def _nat_bias(rpb):
    qc = np.arange(GRID_W)
    kc = np.arange(GRID_W)
    cs = np.clip(qc - WIN_C // 2, 0, GRID_W - WIN_C)
    ok = (kc[None, :] >= cs[:, None]) & (kc[None, :] < cs[:, None] + WIN_C)
    dc = np.clip(kc[None, :] - qc[:, None] + (WIN_C - 1), 0, 2 * WIN_C - 2)
    pick = (dc[:, :, None] == np.arange(2 * WIN_C - 1)).astype(np.float32)
    t = jnp.einsum('hrc,qkc->hrqk', rpb.astype(F32), pick, precision=lax.Precision.HIGHEST)
    t = jnp.where(ok[None, None], t, NEG_INF)
    per_d = [t[:, WIN_R_MAX - 1 - d:2 * WIN_R_MAX - 1 - d].transpose(0, 2, 1, 3)
             .reshape(B_HEADS, GRID_W, WIN_R_MAX * GRID_W) for d in range(WIN_R_MAX)]
    nb = jnp.stack(per_d, axis=1)
    return jnp.concatenate([nb, jnp.zeros((B_HEADS, WIN_R_MAX, GRID_W, PAST_LEN), F32)], axis=-1)


def _gqa_heads(q_ref, sink_ref, k_segs, v_segs, masks, o_ref, group_w):
    (k,), (v,), (mask,) = k_segs, v_segs, masks
    g = group_w // HEAD_DIM
    low = _lane_half(k.shape, 0)
    k_rot = pltpu.roll(k.astype(F32), HEAD_DIM, 1).astype(BF16)
    v_rot = pltpu.roll(v.astype(F32), HEAD_DIM, 1).astype(BF16)
    wide = lambda a: jnp.concatenate([a] * (group_w // LANES), axis=1)
    k_dup = (wide(jnp.where(low, k, k_rot)), wide(jnp.where(low, k_rot, k)))
    v_dup = (wide(jnp.where(low, v, v_rot)), wide(jnp.where(low, v_rot, v)))
    for p in range(C_HEADS // g):
        ls = slice(p * group_w, (p + 1) * group_w)
        kv = (g * p) // C_GROUP
        out = _head_group(q_ref[:, ls], k_dup[kv], v_dup[kv], mask=mask,
                          sinks=[sink_ref[g * p + j] for j in range(g)])
        o_ref[:, ls] = out.astype(BF16)


def _swa_ctx_kernel(sink_ref, q_ref, k_ref, v_ref, o_ref):
    @pl.when(pl.program_id(0) >= N_CTX_TILES)
    def _():
        o_ref[...] = jnp.zeros_like(o_ref)

    @pl.when(pl.program_id(0) < N_CTX_TILES)
    def _():
        _gqa_heads(q_ref, sink_ref, [k_ref[...].astype(BF16)], [v_ref[...].astype(BF16)], [None], o_ref,
                   CTX_GROUP_W)


def _swa_lat_kernel(sink_ref, o_all_ref, q_ref, k_ref, v_ref, kc_ref, vc_ref, o_ref):
    del o_all_ref
    n = pl.program_id(1)
    n_win = 3 * WINDOW
    start = jnp.clip((n - 1) * WINDOW, 0, DEC_SEQ - n_win)
    k0 = pl.multiple_of(start, WINDOW)
    n_keys = n_win + PAST_LEN
    col = lax.broadcasted_iota(jnp.int32, (WINDOW, n_keys), 1)
    qpos = n * WINDOW + lax.broadcasted_iota(jnp.int32, (WINDOW, n_keys), 0)
    ok = (col >= n_win) | (jnp.abs(qpos - (start + col)) <= WINDOW)
    ks = jnp.concatenate([k_ref[pl.ds(k0, n_win), :], kc_ref[0]], axis=0).astype(BF16)
    vs = jnp.concatenate([v_ref[pl.ds(k0, n_win), :], vc_ref[0]], axis=0).astype(BF16)
    _gqa_heads(q_ref, sink_ref, [ks], [vs], [ok], o_ref, LAT_GROUP_W)


def _swa_attention(q, k, v, kc, vc, sink):
    nq, nkv = C_HEADS * HEAD_DIM, C_KV_HEADS * HEAD_DIM
    smem = pl.BlockSpec(memory_space=pltpu.SMEM)
    o_ctx = pl.pallas_call(
        _swa_ctx_kernel,
        out_shape=jax.ShapeDtypeStruct((N_TOK, nq), BF16),
        grid=(N_TILES,),
        in_specs=[smem, _ctx_tile_spec(nq), _ctx_tile_spec(nkv), _ctx_tile_spec(nkv)],
        out_specs=_tile_spec(nq),
        compiler_params=_cparams("arbitrary"),
        name="swa_ctx",
    )(sink, q, k, v)
    nb = DEC_SEQ // WINDOW
    ctx_seqs = N_CTX_TOK // DEC_SEQ
    lat_row = lambda b, n: (N_CTX_TOK // WINDOW + b * nb + n, 0)
    return pl.pallas_call(
        _swa_lat_kernel,
        out_shape=jax.ShapeDtypeStruct((N_TOK, nq), BF16),
        grid=(DEC_BATCH, nb),
        in_specs=[
            smem,
            pl.BlockSpec(memory_space=pl.ANY),
            pl.BlockSpec((WINDOW, nq), lat_row),
            pl.BlockSpec((DEC_SEQ, nkv), lambda b, n: (ctx_seqs + b, 0)),
            pl.BlockSpec((DEC_SEQ, nkv), lambda b, n: (ctx_seqs + b, 0)),
            pl.BlockSpec((1, PAST_LEN, nkv), lambda b, n: (b, 0, 0)),
            pl.BlockSpec((1, PAST_LEN, nkv), lambda b, n: (b, 0, 0)),
        ],
        out_specs=pl.BlockSpec((WINDOW, nq), lat_row),
        input_output_aliases={1: 0},
        compiler_params=_cparams("arbitrary", "arbitrary"),
        name="swa_lat",
    )(sink, o_ctx, q, k, v, kc, vc)


def _route_init(cnt_ref, lg_scr):
    @pl.when(pl.program_id(0) == 0)
    def _():
        cnt_ref[...] = jnp.zeros_like(cnt_ref)
        lg_scr[...] = jnp.zeros_like(lg_scr)


def _moe_input(x, g_ref, sh_ref, sc_ref, rw_ref, rb_ref, tri_ref, h_ref, route_ref, cnt_ref, lg_scr):
    h = _norm_mod(x, g_ref[...], sh_ref[0], sc_ref[0])
    h_ref[...] = _pack_bf16_pair(h[:, 0:D_MODEL // 2], h[:, D_MODEL // 2:])
    w_hi, w_lo = rw_ref[0], rw_ref[1]
    h_hi, h_lo = _split(h)
    new_logits = (_dot_nt(w_hi, h_hi) + (_dot_nt(w_hi, h_lo) + _dot_nt(w_lo, h_hi)) + rb_ref[...])[0:N_EXPERTS]
    logits = lg_scr[...]
    live = jnp.where(pl.program_id(0) > 0, 1.0, 0.0)
    row = lax.broadcasted_iota(jnp.int32, logits.shape, 0)
    cur = logits
    picked, vals, idxs = [], [], []
    for _ in range(TOP_K):
        m = jnp.max(cur, axis=0, keepdims=True)
        idx = jnp.min(jnp.where(cur == m, row, N_EXPERTS), axis=0, keepdims=True)
        sel = row == idx
        picked.append(sel)
        vals.append(m)
        idxs.append(idx)
        cur = jnp.where(sel, -jnp.inf, cur)
    exps = [jnp.exp(v - vals[0]) for v in vals]
    den = functools.reduce(lambda a, b: a + b, exps)
    chosen = functools.reduce(lambda a, b: a + b, [jnp.where(s, 1.0, 0.0) for s in picked])
    cnt = cnt_ref[...]
    before = jnp.concatenate([cnt] * (TILE // LANES), axis=1) + _dot(chosen.astype(BF16), tri_ref[...])
    cnt_ref[...] = cnt + live * jnp.sum(chosen, axis=1, keepdims=True)
    ranks = [jnp.sum(jnp.where(s, before, 0.0), axis=0, keepdims=True) for s in picked]
    fields = [i.astype(F32) for i in idxs] + [e / den for e in exps] + ranks
    out_row = lax.broadcasted_iota(jnp.int32, route_ref.shape, 0)
    route = jnp.zeros(route_ref.shape, F32)
    for j, f in enumerate(fields):
        route = jnp.where(out_row == j, f, route)
    route_ref[...] = route
    lg_scr[...] = new_logits


ROUTE_ROWS = 16
OUT_STEPS = N_TILES + 1


def _last_tile(i):
    return jnp.minimum(i, N_TILES - 1)


def _out_tile_spec(width):
    return pl.BlockSpec((TILE, width), lambda i: (_last_tile(i), 0))


def _out_mod_spec(col):
    return pl.BlockSpec((1, 1, D_MODEL), lambda i: (_last_tile(i), 0, col))


def _moe_input_specs():
    return [_full_spec((1, D_MODEL)), _out_mod_spec(3), _out_mod_spec(4),
            _full_spec((2, ROUTER_PAD, D_MODEL)), _full_spec((ROUTER_PAD, TILE)), _full_spec((TILE, TILE))]


def _mixer_out_shapes():
    shapes = (jax.ShapeDtypeStruct((N_TOK, D_MODEL), F32),
              jax.ShapeDtypeStruct((N_TOK, D_MODEL // 2), U32),
              jax.ShapeDtypeStruct((ROUTE_ROWS, N_TOK), F32),
              jax.ShapeDtypeStruct((N_EXPERTS, LANES), F32))
    specs = (_out_tile_spec(D_MODEL), _out_tile_spec(D_MODEL // 2),
             pl.BlockSpec((ROUTE_ROWS, TILE), lambda i: (0, jnp.maximum(i - 1, 0))),
             _full_spec((N_EXPERTS, LANES)))
    return shapes, specs


def _even_out_kernel(y_ref, u_ref, o_ref, x_ref, gate_ref, d_ref, gw_ref, gb_ref, w_ref,
                     g2_ref, sh_ref, sc_ref, rw_ref, rb_ref, tri_ref, xo_ref, h_ref, route_ref, cnt_ref, lg_scr):
    _route_init(cnt_ref, lg_scr)
    yy = y_ref[...] + d_ref[...] * u_ref[...]
    g = jax.nn.gelu(yy)
    a = g * jax.nn.sigmoid(_dot(g.astype(BF16), gw_ref[...]) + gb_ref[...])
    mix = _dot(a.astype(BF16), w_ref[0:A_WIDTH, :]) + _dot(o_ref[...], w_ref[A_WIDTH:, :])
    x = x_ref[...] + gate_ref[0] * mix
    xo_ref[...] = x
    _moe_input(x, g2_ref, sh_ref, sc_ref, rw_ref, rb_ref, tri_ref, h_ref, route_ref, cnt_ref, lg_scr)


def _even_out(y_t, u_t, o, x, modt, d_skip, glu_w, glu_b, w_out, g2, rw, rb, tri):
    tm_spec = pl.BlockSpec((TILE, A_WIDTH), lambda i: (0, _last_tile(i)))
    out_shape, out_specs = _mixer_out_shapes()
    return pl.pallas_call(
        _even_out_kernel,
        out_shape=out_shape,
        grid=(OUT_STEPS,),
        in_specs=[
            tm_spec, tm_spec, _out_tile_spec(B_WIDTH), _out_tile_spec(D_MODEL), _out_mod_spec(2),
            _full_spec((1, A_WIDTH)), _full_spec((A_WIDTH, A_WIDTH)), _full_spec((1, A_WIDTH)),
            _full_spec((A_WIDTH + B_WIDTH, D_MODEL)),
        ] + _moe_input_specs(),
        out_specs=out_specs,
        scratch_shapes=[pltpu.VMEM((N_EXPERTS, TILE), F32)],
        compiler_params=_cparams("arbitrary"),
        name="even_out",
    )(y_t, u_t, o, x, modt, d_skip, glu_w, glu_b, w_out, g2, modt, modt, rw, rb, tri)


def _rope(x, cos, sin):
    lane = lax.broadcasted_iota(jnp.int32, x.shape, 1)
    first = (lane % (HEAD_DIM // 2)) < (HEAD_DIM // 4)
    partner = jnp.where(first, pltpu.roll(x, LANES - HEAD_DIM // 4, 1), pltpu.roll(x, HEAD_DIM // 4, 1))
    return x * cos + partner * sin


def _odd_in_kernel(n_comb, slot, n_prev, x_ref, *refs):
    comb, refs = refs[:n_comb], refs[n_comb:]
    g_ref, sh_ref, sc_ref, w_ref, bd_ref, qn_ref, kn_ref, cos_ref, sin_ref = refs[:9]
    outs = refs[9 + n_prev:]
    x = _combined(x_ref, comb)
    if n_comb:
        outs[0][...] = x
        outs = outs[1:]
    q_ref, kr_ref, v_ref, kcache_ref, vcache_ref = outs
    i = pl.program_id(0)
    h = _norm_mod(x, g_ref[...], sh_ref[0], sc_ref[0]).astype(BF16)
    bd2 = bd_ref[...]
    lat = i >= N_CTX_TILES
    cos = jnp.where(lat, cos_ref[...], 1.0)
    sin = jnp.where(lat, sin_ref[...], 0.0)
    nq = C_HEADS * HEAD_DIM
    z = _dot(h, w_ref[...])
    for s in range(nq // LANES):
        q = _head_rms(z[:, s * LANES:(s + 1) * LANES], bd2, qn_ref[...])
        q_ref[:, s * LANES:(s + 1) * LANES] = (_rope(q, cos, sin) * ATT_SCALE).astype(BF16)
    k = _head_rms(z[:, nq:nq + LANES], bd2, kn_ref[...])
    v = z[:, nq + LANES:]
    kr_ref[...] = _rope(k, cos, sin)
    v_ref[...] = v
    _cache_store(slot, N_ODD, kcache_ref, k)
    _cache_store(slot, N_ODD, vcache_ref, v)


def _odd_in(x, comb, prev_caches, slot, g, modt, w, bd2, qn, kn, cos, sin):
    nq, nkv = C_HEADS * HEAD_DIM, C_KV_HEADS * HEAD_DIM
    lat_spec = pl.BlockSpec((TILE, LANES), lambda i: (jnp.maximum(i - N_CTX_TILES, 0) % LAT_TILES_PER_SEQ, 0))
    cache_shape, cache_spec = _cache_specs(slot, N_ODD, nkv)
    out_shape = [
        jax.ShapeDtypeStruct((N_TOK, nq), BF16),
        jax.ShapeDtypeStruct((N_TOK, nkv), F32),
        jax.ShapeDtypeStruct((N_TOK, nkv), F32),
        cache_shape, cache_shape,
    ]
    out_specs = [_tile_spec(nq), _tile_spec(nkv), _tile_spec(nkv), cache_spec, cache_spec]
    if comb:
        out_shape.insert(0, jax.ShapeDtypeStruct((N_TOK, D_MODEL), F32))
        out_specs.insert(0, _tile_spec(D_MODEL))
    n_in = 1 + len(comb) + 9
    aliases = {n_in + j: len(out_shape) - 2 + j for j in range(len(prev_caches))}
    return pl.pallas_call(
        functools.partial(_odd_in_kernel, len(comb), slot, len(prev_caches)),
        out_shape=tuple(out_shape),
        grid=(N_TILES,),
        in_specs=[_tile_spec(D_MODEL)] + (_comb_specs() if comb else []) + [
            _full_spec((1, D_MODEL)), _mod_spec(0), _mod_spec(1),
            _full_spec((D_MODEL, nq + 2 * nkv)), _full_spec((2 * LANES, LANES)),
            _full_spec((1, LANES)), _full_spec((1, LANES)), lat_spec, lat_spec,
        ] + [pl.BlockSpec(memory_space=pl.ANY)] * len(prev_caches),
        out_specs=tuple(out_specs),
        input_output_aliases=aliases,
        compiler_params=_cparams("arbitrary"),
        name="odd_in",
    )(x, *comb, g, modt, modt, w, bd2, qn, kn, cos, sin, *prev_caches)


def _rope_tables():
    nf = HEAD_DIM // 4
    inv = ROPE_BASE ** (-jnp.arange(nf, dtype=F32) / nf)
    t = jnp.arange(DEC_SEQ)
    pos = jnp.stack([t // GRID_W, t % GRID_W], axis=-1).astype(F32)
    ang = pos[:, :, None] * inv
    cos, sin = jnp.cos(ang), jnp.sin(ang)
    cos_h = jnp.stack([cos, cos], axis=2).reshape(DEC_SEQ, HEAD_DIM)
    sin_h = jnp.stack([-sin, sin], axis=2).reshape(DEC_SEQ, HEAD_DIM)
    return jnp.tile(cos_h, (1, 2)), jnp.tile(sin_h, (1, 2))


def _odd_out_kernel(o_ref, x_ref, gate_ref, w_ref, g2_ref, sh_ref, sc_ref, rw_ref, rb_ref, tri_ref,
                    xo_ref, h_ref, route_ref, cnt_ref, lg_scr):
    _route_init(cnt_ref, lg_scr)
    x = x_ref[...] + gate_ref[0] * _dot(o_ref[...], w_ref[...])
    xo_ref[...] = x
    _moe_input(x, g2_ref, sh_ref, sc_ref, rw_ref, rb_ref, tri_ref, h_ref, route_ref, cnt_ref, lg_scr)


def _odd_out(o, x, modt, w_o, g2, rw, rb, tri):
    nq = C_HEADS * HEAD_DIM
    out_shape, out_specs = _mixer_out_shapes()
    return pl.pallas_call(
        _odd_out_kernel,
        out_shape=out_shape,
        grid=(OUT_STEPS,),
        in_specs=[_out_tile_spec(nq), _out_tile_spec(D_MODEL), _out_mod_spec(2), _full_spec((nq, D_MODEL))]
        + _moe_input_specs(),
        out_specs=out_specs,
        scratch_shapes=[pltpu.VMEM((N_EXPERTS, TILE), F32)],
        compiler_params=_cparams("arbitrary"),
        name="odd_out",
    )(o, x, modt, w_o, g2, modt, modt, rw, rb, tri)


def _pack_bf16_pair(lo, hi):
    lo_bits = lax.bitcast_convert_type(lo.astype(BF16).astype(F32), U32)
    hi_bits = lax.bitcast_convert_type(hi.astype(BF16).astype(F32), U32)
    return (hi_bits & jnp.uint32(0xFFFF0000)) | (lo_bits >> 16)


def _unpack_bf16_pair(packed):
    lo = lax.bitcast_convert_type(packed << 16, F32).astype(BF16)
    hi = lax.bitcast_convert_type(packed & jnp.uint32(0xFFFF0000), F32).astype(BF16)
    return lo, hi


def _moe_expert_kernel(layer, be_ref, nb_ref, first_ref, slot_ref, nxt_ref, src_cur_ref, src_nxt_ref,
                       h_ref, wgu_hbm, bgu_ref, wd_hbm, bd_ref, o_ref, xg0, xg1, wgu_buf, wd_buf, wsem):
    i = pl.program_id(0)
    nb = nb_ref[0]
    half = D_MODEL // 2
    xg = (xg0, xg1)

    def gather(idx_ref, u, dst):
        for r in range(MOE_TM):
            dst[pl.ds(r, 1), :] = h_ref[pl.ds(idx_ref[0, 0, u * MOE_TM + r], 1), :]

    def weight_copies(e, slot):
        return (pltpu.make_async_copy(wgu_hbm.at[layer, e], wgu_buf.at[slot], wsem.at[0, slot]),
                pltpu.make_async_copy(wd_hbm.at[layer, e], wd_buf.at[slot], wsem.at[1, slot]))

    @pl.when(i == 0)
    def _():
        for cp in weight_copies(be_ref[0], 0):
            cp.start()
        gather(src_cur_ref, 0, xg0)

    for u in range(MOE_STEP_BLOCKS):
        blk = i * MOE_STEP_BLOCKS + u
        rows = pl.ds(u * MOE_TM, MOE_TM)
        x_cur, x_nxt = xg[u % 2], xg[(u + 1) % 2]

        @pl.when(blk < nb)
        def _():
            ws = slot_ref[blk]
            e = be_ref[blk]

            @pl.when(first_ref[blk] == 1)
            def _():
                for cp in weight_copies(e, ws):
                    cp.wait()

                @pl.when(nxt_ref[blk] >= 0)
                def _():
                    for cp in weight_copies(nxt_ref[blk], 1 - ws):
                        cp.start()

            x_lo, x_hi = _unpack_bf16_pair(x_cur[...])
            gu = (_dot(x_lo, wgu_buf[ws, 0:half, :].astype(BF16)) + _dot(x_hi, wgu_buf[ws, half:, :].astype(BF16))
                  + bgu_ref[0, pl.ds(e, 1), :])
            if u + 1 < MOE_STEP_BLOCKS:
                gather(src_cur_ref, u + 1, x_nxt)
            else:
                gather(src_nxt_ref, 0, x_nxt)
            g = jnp.minimum(gu[:, 0:D_EXPERT], SWIGLU_LIMIT)
            lin = jnp.clip(gu[:, D_EXPERT:], -SWIGLU_LIMIT, SWIGLU_LIMIT)
            act = g * jax.nn.sigmoid(SWIGLU_ALPHA * g) * (lin + 1.0)
            o_ref[rows, :] = _dot(act.astype(BF16), wd_buf[ws].astype(BF16)) + bd_ref[0, pl.ds(e, 1), :]

        @pl.when(blk >= nb)
        def _():
            o_ref[rows, :] = jnp.zeros((MOE_TM, D_MODEL), F32)


def _moe_experts(layer, plan, h_packed, w_gu, b_gu, w_down, b_down):
    block_e, n_used, first, slot, nxt, row_tok = plan
    half = D_MODEL // 2
    steps = MOE_BLOCKS // MOE_STEP_BLOCKS
    step_rows = MOE_STEP_BLOCKS * MOE_TM
    idx_spec = lambda f: pl.BlockSpec((1, 1, step_rows), f, memory_space=pltpu.SMEM)
    hbm = pl.BlockSpec(memory_space=pl.ANY)
    grid_spec = pltpu.PrefetchScalarGridSpec(
        num_scalar_prefetch=5,
        grid=(steps,),
        in_specs=[
            idx_spec(lambda i, *_: (i, 0, 0)),
            idx_spec(lambda i, *_: (jnp.minimum(i + 1, steps - 1), 0, 0)),
            pl.BlockSpec((N_TOK, half), lambda i, *_: (0, 0), pipeline_mode=pl.Buffered(1)),
            hbm,
            pl.BlockSpec((1, N_EXPERTS, 2 * D_EXPERT), lambda i, *_: (layer, 0, 0)),
            hbm,
            pl.BlockSpec((1, N_EXPERTS, D_MODEL), lambda i, *_: (layer, 0, 0)),
        ],
        out_specs=pl.BlockSpec((step_rows, D_MODEL), lambda i, *_: (i, 0)),
        scratch_shapes=[
            pltpu.VMEM((MOE_TM, half), U32), pltpu.VMEM((MOE_TM, half), U32),
            pltpu.VMEM((2, D_MODEL, 2 * D_EXPERT), F32), pltpu.VMEM((2, D_EXPERT, D_MODEL), F32),
            pltpu.SemaphoreType.DMA((2, 2)),
        ],
    )
    rows = row_tok.reshape(steps, 1, step_rows)
    return pl.pallas_call(
        functools.partial(_moe_expert_kernel, layer),
        out_shape=jax.ShapeDtypeStruct((MOE_ROWS, D_MODEL), F32),
        grid_spec=grid_spec,
        compiler_params=pltpu.CompilerParams(dimension_semantics=("arbitrary",), vmem_limit_bytes=MOE_VMEM_LIMIT),
        name="moe_experts",
    )(block_e, n_used, first, slot, nxt, rows, rows, h_packed, w_gu, b_gu, w_down, b_down)


def _moe_combine_kernel(x_ref, *refs):
    ctx_ref, lat_ref = refs[N_COMB:]
    x = _combined(x_ref, refs[:N_COMB])

    @pl.when(pl.program_id(0) < N_CTX_TILES)
    def _():
        ctx_ref[...] = x

    @pl.when(pl.program_id(0) >= N_CTX_TILES)
    def _():
        lat_ref[...] = x


def _moe_combine(x, comb):
    return pl.pallas_call(
        _moe_combine_kernel,
        out_shape=(jax.ShapeDtypeStruct((N_CTX_TOK, D_MODEL), F32), jax.ShapeDtypeStruct((N_LAT_TOK, D_MODEL), F32)),
        grid=(N_TILES,),
        in_specs=[_tile_spec(D_MODEL)] + _comb_specs(),
        out_specs=(pl.BlockSpec((TILE, D_MODEL), lambda i: (jnp.minimum(i, N_CTX_TILES - 1), 0)),
                   pl.BlockSpec((TILE, D_MODEL), lambda i: (jnp.maximum(i - N_CTX_TILES, 0), 0))),
        compiler_params=_cparams("arbitrary"),
        name="moe_combine",
    )(x, *comb)


def _lut(idx, table):
    n = table.shape[0]
    hit = idx[:, None] == jnp.arange(n, dtype=idx.dtype)[None, :]
    return jnp.sum(jnp.where(hit, table[None, :], 0), axis=1)


def _moe_route(route, cnt):
    i32 = jnp.int32
    route = route.T
    gates = route[:, TOP_K:2 * TOP_K]
    flat_e = route[:, 0:TOP_K].astype(i32).reshape(N_ASG)
    rank = route[:, 2 * TOP_K:3 * TOP_K].astype(i32).reshape(N_ASG)
    asg = jnp.arange(N_ASG, dtype=i32)
    experts = jnp.arange(N_EXPERTS, dtype=i32)
    counts = cnt[:, 0].astype(i32)
    pad_counts = (counts + MOE_TM - 1) // MOE_TM * MOE_TM
    pad_end = jnp.cumsum(pad_counts)
    n_used = pad_end[-1] // MOE_TM
    dest = _lut(flat_e, pad_end - pad_counts) + rank
    fill_end = jnp.cumsum(pad_counts - counts)
    filler = jnp.arange(MOE_ROWS - N_ASG, dtype=i32)
    fill_e = jnp.sum((filler[:, None] >= fill_end[None, :]).astype(i32), axis=1)
    keys = jnp.concatenate([flat_e * ASG_STRIDE + asg, fill_e * ASG_STRIDE + (ASG_STRIDE - 1)])
    low = jnp.sort(keys) % ASG_STRIDE
    row_tok = jnp.where(low < N_ASG, low // TOP_K, 0)
    blk = jnp.arange(MOE_BLOCKS, dtype=i32)
    block_e = jnp.sum((blk[:, None] * MOE_TM >= pad_end[None, :]).astype(i32), axis=1)
    last_e = jnp.max(jnp.where(counts > 0, experts, 0))
    block_e = jnp.where(blk < n_used, jnp.minimum(block_e, N_EXPERTS - 1), last_e)
    first = jnp.concatenate([jnp.ones((1,), i32), (block_e[1:] != block_e[:-1]).astype(i32)])
    slot = (jnp.cumsum(first) - 1) % 2
    later = (experts[None, :] > experts[:, None]) & (counts[None, :] > 0)
    nxt_of = jnp.min(jnp.where(later, experts[None, :], N_EXPERTS), axis=1)
    nxt = _lut(block_e, jnp.where(nxt_of < N_EXPERTS, nxt_of, -1))
    plan = (block_e, n_used.reshape(1).astype(i32), first, slot.astype(i32), nxt.astype(i32), row_tok)
    return gates, dest.reshape(N_TOK, TOP_K), plan


def _moe(layer, h_packed, route, cnt, modt, w_gu, b_gu, w_down, b_down):
    gates, dest, plan = _moe_route(route, cnt)
    y_rows = _moe_experts(layer, plan, h_packed, w_gu, b_gu, w_down, b_down)
    return (modt, gates) + tuple(y_rows[dest[:, k]] for k in range(TOP_K))


def kernel(x_prompt, x_sample, cache_nat_k, cache_nat_v, cache_swa_k, cache_swa_v, state_ssm, c, c_ctx,
           norm1_g, norm2_g, mod_w, mod_b,
           ab_w_in, ab_w_out, ssm_lam_re, ssm_lam_im, ssm_log_dt, ssm_b_re, ssm_b_im, ssm_c_re, ssm_c_im,
           ssm_d, ssm_glu_w, ssm_glu_b, nat_qn, nat_kn, nat_rpb,
           swa_w_qkv, swa_w_o, swa_qn, swa_kn, swa_sink,
           moe_router_w, moe_router_b, moe_w_gu, moe_b_gu, moe_w_down, moe_b_down):
    x = jnp.concatenate([x_prompt.reshape(N_CTX_TOK, D_MODEL), x_sample.reshape(N_LAT_TOK, D_MODEL)], axis=0)
    cond = jnp.zeros((SUBLANES, D_MODEL), F32).at[0].set(c_ctx).at[1:1 + DEC_BATCH].set(c)
    mod = _modulation(cond, mod_w, mod_b)
    tile_row = np.concatenate([np.zeros(N_CTX_TILES, np.int32),
                               1 + np.arange(N_TILES - N_CTX_TILES, dtype=np.int32) // LAT_TILES_PER_SEQ])
    head_gain = lambda gn: jnp.tile(gn, 2).reshape(1, LANES)
    bd = np.kron(np.eye(2, dtype=np.float32), np.full((HEAD_DIM, HEAD_DIM), 1.0 / HEAD_DIM, np.float32))
    bd2 = jnp.asarray(np.concatenate([bd, bd], axis=0), BF16)
    rope_cos, rope_sin = _rope_tables()
    tri = jnp.asarray(np.triu(np.ones((TILE, TILE), np.float32), 1), BF16)
    router_w = jnp.pad(moe_router_w, ((0, 0), (0, 0), (0, ROUTER_PAD - N_EXPERTS))).transpose(0, 2, 1)
    router_w = jnp.stack(_split(router_w), axis=1)
    router_b = jnp.broadcast_to(jnp.pad(moe_router_b, ((0, 0), (0, ROUTER_PAD - N_EXPERTS)))[:, :, None],
                                (DEPTH, ROUTER_PAD, TILE))
    nkv = C_KV_HEADS * HEAD_DIM

    ssm_out = []
    nat_caches, swa_caches = (), ()
    comb = ()
    for l in range(DEPTH):
        modt = mod[l][tile_row].reshape(N_TILES, 1, N_MOD * D_MODEL)
        g1 = norm1_g[l].reshape(1, D_MODEL)
        moe_in = (norm2_g[l].reshape(1, D_MODEL), router_w[l], router_b[l], tri)
        i = l // 2
        if l % 2 == 0:
            outs = _even_in(x, comb, nat_caches, i, g1, modt, ab_w_in[i].astype(BF16), bd2,
                            head_gain(nat_qn[i]), head_gain(nat_kn[i]))
            if comb:
                x, outs = outs[0], outs[1:]
            u_t, q, k, v = outs[:4]
            nat_caches = tuple(outs[4:])
            a_b, w_bd, c_bd = _s5_params(ssm_lam_re[i], ssm_lam_im[i], ssm_log_dt[i], ssm_b_re[i], ssm_b_im[i],
                                         ssm_c_re[i], ssm_c_im[i])
            st = state_ssm[:, i].reshape(DEC_BATCH, 2, 2, SSM_LANES).transpose(1, 2, 0, 3)
            s0 = jnp.zeros((2, 2, N_TILES, SSM_LANES), F32)
            first = N_CTX_TILES + LAT_TILES_PER_SEQ * np.arange(DEC_BATCH)
            s0 = s0.at[0, :, first].set(st[0].transpose(1, 0, 2))
            s0 = s0.at[1, :, first + LAT_TILES_PER_SEQ - 1].set(st[1].transpose(1, 0, 2))
            y_t, fin = _s5_scan(u_t.reshape(TILE, N_TILES, A_WIDTH), a_b, w_bd, c_bd, s0)
            o = _nat_attention(q, k, v,
                               cache_nat_k[:, i].reshape(DEC_BATCH, PAST_LEN, B_WIDTH),
                               cache_nat_v[:, i].reshape(DEC_BATCH, PAST_LEN, B_WIDTH),
                               _nat_bias(nat_rpb[i]))
            x, h_packed, route, cnt = _even_out(y_t.reshape(TILE, N_TILES * A_WIDTH), u_t, o, x, modt,
                                            ssm_d[i].reshape(1, A_WIDTH), ssm_glu_w[i].astype(BF16),
                                            ssm_glu_b[i].reshape(1, A_WIDTH), ab_w_out[i].astype(BF16), *moe_in)
            ssm_out.append(fin[:, :, :N_CTX_TILES].transpose(2, 0, 1, 3).reshape(BATCH, 2, 2, A_GROUPS, A_STATE))
        else:
            outs = _odd_in(x, comb, swa_caches, i, g1, modt, swa_w_qkv[i].astype(BF16), bd2,
                           head_gain(swa_qn[i]), head_gain(swa_kn[i]), rope_cos, rope_sin)
            if comb:
                x, outs = outs[0], outs[1:]
            q, k_rot, v = outs[:3]
            swa_caches = tuple(outs[3:])
            o = _swa_attention(q, k_rot, v,
                               cache_swa_k[:, i].reshape(DEC_BATCH, PAST_LEN, nkv),
                               cache_swa_v[:, i].reshape(DEC_BATCH, PAST_LEN, nkv), swa_sink[i])
            x, h_packed, route, cnt = _odd_out(o, x, modt, swa_w_o[i].astype(BF16), *moe_in)
        comb = _moe(l, h_packed, route, cnt, modt, moe_w_gu, moe_b_gu, moe_w_down, moe_b_down)

    y_ctx, y_lat = _moe_combine(x, comb)
    y_prompt = y_ctx.reshape(BATCH, SEQ, D_MODEL)
    y_sample = y_lat.reshape(DEC_BATCH, DEC_SEQ, D_MODEL)
    nat_shape = (BATCH, N_EVEN, SEQ, B_HEADS, HEAD_DIM)
    swa_shape = (BATCH, N_ODD, SEQ, C_KV_HEADS, HEAD_DIM)
    return (y_prompt, y_sample,
            nat_caches[0].reshape(nat_shape), nat_caches[1].reshape(nat_shape),
            swa_caches[0].reshape(swa_shape), swa_caches[1].reshape(swa_shape),
            jnp.stack(ssm_out, axis=1))
```

```python
import functools

import jax
import jax.numpy as jnp
import numpy as np
from jax import lax
from jax.experimental import pallas as pl
from jax.experimental.pallas import tpu as pltpu

F32 = jnp.float32
BF16 = jnp.bfloat16
U32 = jnp.uint32

D_MODEL = 1024
BATCH = 32
SEQ = 256
DEPTH = 4
DEC_BATCH = 2
DEC_SEQ = 1024
PAST_LEN = 256
GRID_W = 64
HEAD_DIM = 64
N_MOD = 6
N_EVEN = (DEPTH + 1) // 2
N_ODD = DEPTH // 2
A_WIDTH = 512
A_GROUP_CH = 16
A_GROUPS = 32
A_STATE = 64
B_HEADS = 8
B_WIDTH = 512
WIN_R_MAX = 8
WIN_C = 16
C_HEADS = 16
C_KV_HEADS = 2
C_GROUP = 8
WINDOW = 128
ROPE_BASE = 10000.0
N_EXPERTS = 32
TOP_K = 4
D_EXPERT = 1024
SWIGLU_LIMIT = 7.0
SWIGLU_ALPHA = 1.702
EPS = 1e-6
NEG_INF = -1e30

LANES = 128
SUBLANES = 8
TILE = 256
N_CTX_TOK = BATCH * SEQ
N_LAT_TOK = DEC_BATCH * DEC_SEQ
N_TOK = N_CTX_TOK + N_LAT_TOK
N_TILES = N_TOK // TILE
N_CTX_TILES = N_CTX_TOK // TILE
LAT_TILES_PER_SEQ = DEC_SEQ // TILE
SSM_LANES = A_GROUPS * A_STATE
SSM_SLABS = A_WIDTH // LANES
SSM_SLAB_STATES = SSM_LANES // SSM_SLABS
SCAN_ROWS = 8
SCAN_GROUPS = N_TILES // SCAN_ROWS
SCAN_CHUNK = 128
SCAN_LANE_PARTS = 2
MOE_TM = 256
MOE_STEP_BLOCKS = 4
N_ASG = N_TOK * TOP_K
ASG_STRIDE = 1 << 16
MOE_BLOCKS = N_ASG // MOE_TM + N_EXPERTS
MOE_ROWS = MOE_BLOCKS * MOE_TM
ROUTER_PAD = LANES
VMEM_LIMIT = 56 * 1024 * 1024
MOE_VMEM_LIMIT = 60 * 1024 * 1024
ATT_SCALE = HEAD_DIM ** -0.5
LAT_GROUP_W = 4 * HEAD_DIM
CTX_GROUP_W = 2 * HEAD_DIM


def _cparams(*sem):
    return pltpu.CompilerParams(dimension_semantics=sem, vmem_limit_bytes=VMEM_LIMIT)


def _dot(a, b):
    return jnp.dot(a, b, preferred_element_type=F32)


def _dot_nt(a, b):
    return lax.dot_general(a, b, (((1,), (1,)), ((), ())), preferred_element_type=F32)


def _split(a):
    hi = a.astype(BF16)
    lo = (a - hi.astype(F32)).astype(BF16)
    return hi, lo


def _dot3(a, b):
    a_hi, a_lo = _split(a)
    b_hi, b_lo = _split(b)
    return _dot(a_hi, b_hi) + (_dot(a_hi, b_lo) + _dot(a_lo, b_hi))


def _silu(x):
    return x * jax.nn.sigmoid(x)


def _norm_mod(x, g, shift, scale):
    y = x * lax.rsqrt(jnp.mean(x * x, axis=-1, keepdims=True) + EPS)
    return (y * g) * (1.0 + scale) + shift


def _head_rms(x, bd2, gain):
    sq_hi, sq_lo = _split(x * x)
    ms = _dot(jnp.concatenate([sq_hi, sq_lo], axis=1), bd2)
    return x * lax.rsqrt(ms + EPS) * gain


def _combined(x_ref, comb_refs):
    x = x_ref[...]
    if not comb_refs:
        return x
    gmod_ref, gates_ref = comb_refs[0], comb_refs[1]
    gates = gates_ref[...]
    acc = gates[:, 0:1] * comb_refs[2][...]
    for k in range(1, TOP_K):
        acc = acc + gates[:, k:k + 1] * comb_refs[2 + k][...]
    return x + gmod_ref[0] * acc


def _comb_specs():
    return [_mod_spec(5), _tile_spec(TOP_K)] + [_tile_spec(D_MODEL)] * TOP_K


N_COMB = 2 + TOP_K


def _lane_half(shape, half):
    lane = lax.broadcasted_iota(jnp.int32, shape, len(shape) - 1)
    return (lane < HEAD_DIM) if half == 0 else (lane >= HEAD_DIM)


def _mod_kernel(cond_ref, w_ref, b_ref, o_ref):
    o_ref[0] = _dot3(_silu(cond_ref[...]), w_ref[0]) + b_ref[0]


def _modulation(cond, mod_w, mod_b):
    nc = 2
    cw = N_MOD * D_MODEL // nc
    return pl.pallas_call(
        _mod_kernel,
        out_shape=jax.ShapeDtypeStruct((DEPTH, SUBLANES, N_MOD * D_MODEL), F32),
        grid=(DEPTH, nc),
        in_specs=[
            pl.BlockSpec((SUBLANES, D_MODEL), lambda l, c: (0, 0)),
            pl.BlockSpec((1, D_MODEL, cw), lambda l, c: (l, 0, c)),
            pl.BlockSpec((1, 1, cw), lambda l, c: (l, 0, c)),
        ],
        out_specs=pl.BlockSpec((1, SUBLANES, cw), lambda l, c: (l, 0, c)),
        compiler_params=_cparams("arbitrary", "arbitrary"),
        name="modulation",
    )(cond, mod_w, mod_b.reshape(DEPTH, 1, N_MOD * D_MODEL))


def _mod_spec(col):
    return pl.BlockSpec((1, 1, D_MODEL), lambda i: (i, 0, col))


def _tile_spec(width):
    return pl.BlockSpec((TILE, width), lambda i: (i, 0))


def _ctx_tile_spec(width):
    return pl.BlockSpec((TILE, width), lambda i: (jnp.minimum(i, N_CTX_TILES - 1), 0))


def _full_spec(shape):
    nd = len(shape)
    return pl.BlockSpec(shape, lambda i: (0,) * nd)


def _cache_store(slot, n_slots, ref, value):
    @pl.when(pl.program_id(0) < N_CTX_TILES)
    def _():
        if slot == 0:
            ref[0, 0] = value
            for other in range(1, n_slots):
                ref[0, other] = jnp.zeros_like(value)
        else:
            ref[0, 0] = value


def _cache_specs(slot, n_slots, width):
    shape = jax.ShapeDtypeStruct((BATCH, n_slots, SEQ, width), F32)
    seq = lambda i: jnp.minimum(i, N_CTX_TILES - 1)
    if slot == 0:
        return shape, pl.BlockSpec((1, n_slots, SEQ, width), lambda i: (seq(i), 0, 0, 0))
    return shape, pl.BlockSpec((1, 1, SEQ, width), lambda i: (seq(i), slot, 0, 0))


def _even_in_kernel(n_comb, slot, n_prev, x_ref, *refs):
    comb, refs = refs[:n_comb], refs[n_comb:]
    g_ref, sh_ref, sc_ref, w_ref, bd_ref, qn_ref, kn_ref = refs[:7]
    outs = refs[7 + n_prev:]
    x = _combined(x_ref, comb)
    if n_comb:
        outs[0][...] = x
        outs = outs[1:]
    u_ref, q_ref, k_ref, v_ref, kc_ref, vc_ref = outs
    h = _norm_mod(x, g_ref[...], sh_ref[0], sc_ref[0]).astype(BF16)
    bd2 = bd_ref[...]
    z = _dot(h, w_ref[...])
    u_ref[...] = z[:, 0:A_WIDTH]
    ks = []
    for s in range(B_WIDTH // LANES):
        lo = A_WIDTH + s * LANES
        q = _head_rms(z[:, lo:lo + LANES], bd2, qn_ref[...])
        q_ref[:, s * LANES:(s + 1) * LANES] = (q * ATT_SCALE).astype(BF16)
        lo = A_WIDTH + B_WIDTH + s * LANES
        ks.append(_head_rms(z[:, lo:lo + LANES], bd2, kn_ref[...]))
    k = jnp.concatenate(ks, axis=1)
    v = z[:, A_WIDTH + 2 * B_WIDTH:]
    k_ref[...] = k
    v_ref[...] = v
    _cache_store(slot, N_EVEN, kc_ref, k)
    _cache_store(slot, N_EVEN, vc_ref, v)


def _even_in(x, comb, prev_caches, slot, g, modt, w, bd2, qn, kn):
    n_out = A_WIDTH + 3 * B_WIDTH
    cache_shape, cache_spec = _cache_specs(slot, N_EVEN, B_WIDTH)
    out_shape = [
        jax.ShapeDtypeStruct((TILE, N_TILES * A_WIDTH), F32),
        jax.ShapeDtypeStruct((N_TOK, B_WIDTH), BF16),
        jax.ShapeDtypeStruct((N_TOK, B_WIDTH), F32),
        jax.ShapeDtypeStruct((N_TOK, B_WIDTH), F32),
        cache_shape, cache_shape,
    ]
    out_specs = [pl.BlockSpec((TILE, A_WIDTH), lambda i: (0, i)),
                 _tile_spec(B_WIDTH), _tile_spec(B_WIDTH), _tile_spec(B_WIDTH), cache_spec, cache_spec]
    if comb:
        out_shape.insert(0, jax.ShapeDtypeStruct((N_TOK, D_MODEL), F32))
        out_specs.insert(0, _tile_spec(D_MODEL))
    n_in = 1 + len(comb) + 7
    aliases = {n_in + j: len(out_shape) - 2 + j for j in range(len(prev_caches))}
    return pl.pallas_call(
        functools.partial(_even_in_kernel, len(comb), slot, len(prev_caches)),
        out_shape=tuple(out_shape),
        grid=(N_TILES,),
        in_specs=[_tile_spec(D_MODEL)] + (_comb_specs() if comb else []) + [
            _full_spec((1, D_MODEL)), _mod_spec(0), _mod_spec(1),
            _full_spec((D_MODEL, n_out)), _full_spec((2 * LANES, LANES)),
            _full_spec((1, LANES)), _full_spec((1, LANES)),
        ] + [pl.BlockSpec(memory_space=pl.ANY)] * len(prev_caches),
        out_specs=tuple(out_specs),
        input_output_aliases=aliases,
        compiler_params=_cparams("arbitrary"),
        name="even_in",
    )(x, *comb, g, modt, modt, w, bd2, qn, kn, *prev_caches)


def _cmul(ar, ai, br, bi):
    return ar * br - ai * bi, ar * bi + ai * br


def _s5_kernel(u_ref, a_ref, w_ref, c_ref, s0_ref, y_ref, fin_ref, xr, xi, st_r, st_i):
    grp = pl.program_id(0)
    drn = pl.program_id(1)
    n_chunks = TILE // SCAN_CHUNK
    rows = SCAN_CHUNK * SCAN_ROWS
    part = SSM_LANES // SCAN_LANE_PARTS

    def run(store):
        @pl.loop(0, n_chunks)
        def _(c):
            cc = jnp.where(drn == 0, c, n_chunks - 1 - c)
            t0 = pl.multiple_of(cc * SCAN_CHUNK, SCAN_CHUNK)
            uu = u_ref[pl.ds(t0, SCAN_CHUNK), :, :]
            for s in range(SSM_SLABS):
                us = uu[:, :, s * LANES:(s + 1) * LANES].reshape(rows, LANES).astype(BF16)
                cols = slice(s * SSM_SLAB_STATES, (s + 1) * SSM_SLAB_STATES)
                xr[:, cols] = _dot(us, w_ref[0, 0, s])
                xi[:, cols] = _dot(us, w_ref[0, 1, s])
            for p in range(SCAN_LANE_PARTS):
                ls = slice(p * part, (p + 1) * part)
                ar = a_ref[0, 0, :, ls]
                ai = a_ref[0, 1, :, ls]

                def step(j, carry):
                    sr, si = carry
                    tt = jnp.where(drn == 0, j, SCAN_CHUNK - 1 - j)
                    r0 = pl.multiple_of(tt * SCAN_ROWS, SCAN_ROWS)
                    nr = ar * sr - ai * si + xr[pl.ds(r0, SCAN_ROWS), ls]
                    ni = ar * si + ai * sr + xi[pl.ds(r0, SCAN_ROWS), ls]
                    if store:
                        xr[pl.ds(r0, SCAN_ROWS), ls] = nr
                        xi[pl.ds(r0, SCAN_ROWS), ls] = ni
                    return nr, ni

                sr, si = lax.fori_loop(0, SCAN_CHUNK, step, (st_r[:, ls], st_i[:, ls]), unroll=4)
                st_r[:, ls] = sr
                st_i[:, ls] = si
            if store:
                for s in range(SSM_SLABS):
                    cols = slice(s * SSM_SLAB_STATES, (s + 1) * SSM_SLAB_STATES)
                    ys = (_dot(xr[:, cols].astype(BF16), c_ref[0, 0, s])
                          - _dot(xi[:, cols].astype(BF16), c_ref[0, 1, s]))
                    ys = ys.reshape(SCAN_CHUNK, SCAN_ROWS, LANES)
                    lanes = slice(s * LANES, (s + 1) * LANES)

                    @pl.when(drn == 0)
                    def _():
                        y_ref[pl.ds(t0, SCAN_CHUNK), :, lanes] = ys

                    @pl.when(drn != 0)
                    def _():
                        y_ref[pl.ds(t0, SCAN_CHUNK), :, lanes] += ys

    st_r[...] = s0_ref[0, 0]
    st_i[...] = s0_ref[0, 1]

    @pl.when(grp == SCAN_GROUPS - 1)
    def _():
        st_r[...] = jnp.zeros_like(st_r)
        st_i[...] = jnp.zeros_like(st_i)
        run(False)
        pr, pi = a_ref[0, 0], a_ref[0, 1]
        for _ in range(8):
            pr, pi = _cmul(pr, pi, pr, pi)
        fr, fi = st_r[...], st_i[...]
        s0r, s0i = s0_ref[0, 0], s0_ref[0, 1]
        row = lax.broadcasted_iota(jnp.int32, (SCAN_ROWS, SSM_LANES), 0)
        quarter = row % LAT_TILES_PER_SEQ
        fwd = drn == 0
        keep = quarter != jnp.where(fwd, 0, LAT_TILES_PER_SEQ - 1)
        ir, ii = s0r, s0i
        for _ in range(LAT_TILES_PER_SEQ - 1):
            nr, ni = _cmul(pr, pi, ir, ii)
            nr, ni = nr + fr, ni + fi
            nr = jnp.where(fwd, pltpu.roll(nr, 1, 0), pltpu.roll(nr, SCAN_ROWS - 1, 0))
            ni = jnp.where(fwd, pltpu.roll(ni, 1, 0), pltpu.roll(ni, SCAN_ROWS - 1, 0))
            ir = s0r + jnp.where(keep, nr, 0.0)
            ii = s0i + jnp.where(keep, ni, 0.0)
        st_r[...] = ir
        st_i[...] = ii

    run(True)
    fin_ref[0, 0] = st_r[...]
    fin_ref[0, 1] = st_i[...]


def _s5_scan(u_tb, a_b, w_bd, c_bd, s0):
    rows = SCAN_CHUNK * SCAN_ROWS
    return pl.pallas_call(
        _s5_kernel,
        out_shape=(
            jax.ShapeDtypeStruct((TILE, N_TILES, A_WIDTH), F32),
            jax.ShapeDtypeStruct((2, 2, N_TILES, SSM_LANES), F32),
        ),
        grid=(SCAN_GROUPS, 2),
        in_specs=[
            pl.BlockSpec((TILE, SCAN_ROWS, A_WIDTH), lambda g, d: (0, g, 0)),
            pl.BlockSpec((1, 2, SCAN_ROWS, SSM_LANES), lambda g, d: (d, 0, 0, 0)),
            pl.BlockSpec((1, 2, SSM_SLABS, LANES, SSM_SLAB_STATES), lambda g, d: (d, 0, 0, 0, 0)),
            pl.BlockSpec((1, 2, SSM_SLABS, SSM_SLAB_STATES, LANES), lambda g, d: (d, 0, 0, 0, 0)),
            pl.BlockSpec((1, 2, SCAN_ROWS, SSM_LANES), lambda g, d: (d, 0, g, 0)),
        ],
        out_specs=(
            pl.BlockSpec((TILE, SCAN_ROWS, A_WIDTH), lambda g, d: (0, g, 0)),
            pl.BlockSpec((1, 2, SCAN_ROWS, SSM_LANES), lambda g, d: (d, 0, g, 0)),
        ),
        scratch_shapes=[
            pltpu.VMEM((rows, SSM_LANES), F32), pltpu.VMEM((rows, SSM_LANES), F32),
            pltpu.VMEM((SCAN_ROWS, SSM_LANES), F32), pltpu.VMEM((SCAN_ROWS, SSM_LANES), F32),
        ],
        compiler_params=_cparams("arbitrary", "arbitrary"),
        name="s5_scan",
    )(u_tb, a_b, w_bd, c_bd, s0)


def _s5_params(lam_re, lam_im, log_dt, b_re, b_im, c_re, c_im):
    dt = jnp.exp(log_dt)[..., None]
    mag = jnp.exp(lam_re * dt)
    ab_re, ab_im = mag * jnp.cos(lam_im * dt), mag * jnp.sin(lam_im * dt)
    den = lam_re * lam_re + lam_im * lam_im
    nr, ni = ab_re - 1.0, ab_im
    f_re = (nr * lam_re + ni * lam_im) / den
    f_im = (ni * lam_re - nr * lam_im) / den
    bb_re = f_re[..., None] * b_re - f_im[..., None] * b_im
    bb_im = f_re[..., None] * b_im + f_im[..., None] * b_re
    a_b = jnp.stack([ab_re, ab_im], axis=1).reshape(2, 2, 1, SSM_LANES)
    a_b = jnp.broadcast_to(a_b, (2, 2, SCAN_ROWS, SSM_LANES))
    gps = A_GROUPS // SSM_SLABS
    eye = jnp.eye(gps, dtype=F32)

    def in_bd(bb):
        bb = bb.reshape(2, SSM_SLABS, gps, A_STATE, A_GROUP_CH)
        m = jnp.einsum('dsgph,gk->dsghkp', bb, eye)
        return m.reshape(2, SSM_SLABS, LANES, SSM_SLAB_STATES)

    def out_bd(cc):
        cc = cc.reshape(2, SSM_SLABS, gps, A_GROUP_CH, A_STATE)
        m = jnp.einsum('dsghp,gk->dsgpkh', cc, eye)
        return m.reshape(2, SSM_SLABS, SSM_SLAB_STATES, LANES)

    w_bd = jnp.stack([in_bd(bb_re), in_bd(bb_im)], axis=1).astype(BF16)
    c_bd = jnp.stack([out_bd(c_re), out_bd(c_im)], axis=1).astype(BF16)
    return a_b, w_bd, c_bd


def _softmax_pv(scores, values, sink=None):
    m = functools.reduce(jnp.maximum, [jnp.max(s, axis=-1, keepdims=True) for s in scores])
    if sink is not None:
        m = jnp.maximum(m, sink)
    den = None
    acc = None
    for s, v in zip(scores, values):
        e = jnp.exp(s - m)
        d = jnp.sum(e, axis=-1, keepdims=True)
        o = _dot(e.astype(BF16), v)
        den = d if den is None else den + d
        acc = o if acc is None else acc + o
    if sink is not None:
        den = den + jnp.exp(sink - m)
    return acc / den


def _head_group(qs, ks, vs, bias=None, mask=None, sinks=None):
    m, g = qs.shape[0], qs.shape[1] // HEAD_DIM
    head = lax.broadcasted_iota(jnp.int32, qs.shape, 1) // HEAD_DIM
    zero = jnp.zeros_like(qs)
    q_all = jnp.concatenate([jnp.where(head == j, qs, zero) for j in range(g)], axis=0)
    s = _dot_nt(q_all, ks)
    if bias is not None:
        s = s + bias
    if mask is not None:
        s = jnp.where(jnp.concatenate([mask] * g, axis=0), s, NEG_INF)
    sink = None
    if sinks is not None:
        row = lax.broadcasted_iota(jnp.int32, (g * m, 1), 0)
        sink = sinks[0]
        for j in range(1, g):
            sink = jnp.where(row >= j * m, sinks[j], sink)
    o_all = _softmax_pv([s], [vs], sink=sink)
    out = o_all[0:m]
    for j in range(1, g):
        out = jnp.where(head == j, o_all[j * m:(j + 1) * m], out)
    return out


def _nat_ctx_kernel(q_ref, k_ref, v_ref, o_ref):
    @pl.when(pl.program_id(0) >= N_CTX_TILES)
    def _():
        o_ref[...] = jnp.zeros_like(o_ref)

    @pl.when(pl.program_id(0) < N_CTX_TILES)
    def _():
        for p in range(B_WIDTH // CTX_GROUP_W):
            ls = slice(p * CTX_GROUP_W, (p + 1) * CTX_GROUP_W)
            out = _head_group(q_ref[:, ls], k_ref[:, ls].astype(BF16), v_ref[:, ls].astype(BF16))
            o_ref[:, ls] = out.astype(BF16)


def _nat_lat_kernel(o_all_ref, q_ref, k_ref, v_ref, kc_ref, vc_ref, bias_ref, o_ref):
    del o_all_ref
    r = pl.program_id(0)
    wr = WIN_R_MAX
    rs = jnp.clip(r - wr // 2, 0, DEC_SEQ // GRID_W - wr)
    k0 = pl.multiple_of(rs * GRID_W, GRID_W)
    n_nb = wr * GRID_W
    g = LAT_GROUP_W // HEAD_DIM
    for p in range(B_WIDTH // LAT_GROUP_W):
        ls = slice(p * LAT_GROUP_W, (p + 1) * LAT_GROUP_W)
        for b in range(DEC_BATCH):
            ks = jnp.concatenate([k_ref[b, pl.ds(k0, n_nb), ls], kc_ref[b, :, ls]], axis=0).astype(BF16)
            vs = jnp.concatenate([v_ref[b, pl.ds(k0, n_nb), ls], vc_ref[b, :, ls]], axis=0).astype(BF16)
            bias = jnp.concatenate([bias_ref[g * p + j, 0] for j in range(g)], axis=0)
            o_ref[b, :, ls] = _head_group(q_ref[b, :, ls], ks, vs, bias=bias).astype(BF16)


def _nat_attention(q, k, v, kc, vc, bias):
    o_ctx = pl.pallas_call(
        _nat_ctx_kernel,
        out_shape=jax.ShapeDtypeStruct((N_TOK, B_WIDTH), BF16),
        grid=(N_TILES,),
        in_specs=[_ctx_tile_spec(B_WIDTH)] * 3,
        out_specs=_tile_spec(B_WIDTH),
        compiler_params=_cparams("arbitrary"),
        name="nat_ctx",
    )(q, k, v)
    rows = DEC_SEQ // GRID_W
    n_seq = N_TOK // DEC_SEQ
    lat = N_CTX_TOK // DEC_SEQ // DEC_BATCH
    as_seq = lambda a: a.reshape(n_seq, DEC_SEQ, B_WIDTH)
    row_spec = pl.BlockSpec((DEC_BATCH, GRID_W, B_WIDTH), lambda r: (lat, r, 0))
    seq_spec = pl.BlockSpec((DEC_BATCH, DEC_SEQ, B_WIDTH), lambda r: (lat, 0, 0))
    ctx_spec = pl.BlockSpec((DEC_BATCH, PAST_LEN, B_WIDTH), lambda r: (0, 0, 0))

    def bias_idx(r):
        return (0, r - jnp.clip(r - WIN_R_MAX // 2, 0, rows - WIN_R_MAX), 0, 0)

    o = pl.pallas_call(
        _nat_lat_kernel,
        out_shape=jax.ShapeDtypeStruct((n_seq, DEC_SEQ, B_WIDTH), BF16),
        grid=(rows,),
        in_specs=[
            pl.BlockSpec(memory_space=pl.ANY), row_spec, seq_spec, seq_spec, ctx_spec, ctx_spec,
            pl.BlockSpec((B_HEADS, 1, GRID_W, WIN_R_MAX * GRID_W + PAST_LEN), bias_idx),
        ],
        out_specs=row_spec,
        input_output_aliases={0: 0},
        compiler_params=_cparams("arbitrary"),
        name="nat_lat",
    )(as_seq(o_ctx), as_seq(q), as_seq(k), as_seq(v), kc, vc, bias)
    return o.reshape(N_TOK, B_WIDTH)


def _nat_bias(rpb):
    qc = np.arange(GRID_W)
    kc = np.arange(GRID_W)
    cs = np.clip(qc - WIN_C // 2, 0, GRID_W - WIN_C)
    ok = (kc[None, :] >= cs[:, None]) & (kc[None, :] < cs[:, None] + WIN_C)
    dc = np.clip(kc[None, :] - qc[:, None] + (WIN_C - 1), 0, 2 * WIN_C - 2)
    pick = (dc[:, :, None] == np.arange(2 * WIN_C - 1)).astype(np.float32)
    t = jnp.einsum('hrc,qkc->hrqk', rpb.astype(F32), pick, precision=lax.Precision.HIGHEST)
    t = jnp.where(ok[None, None], t, NEG_INF)
    per_d = [t[:, WIN_R_MAX - 1 - d:2 * WIN_R_MAX - 1 - d].transpose(0, 2, 1, 3)
             .reshape(B_HEADS, GRID_W, WIN_R_MAX * GRID_W) for d in range(WIN_R_MAX)]
    nb = jnp.stack(per_d, axis=1)
    return jnp.concatenate([nb, jnp.zeros((B_HEADS, WIN_R_MAX, GRID_W, PAST_LEN), F32)], axis=-1)


def _gqa_heads(q_ref, sink_ref, k_segs, v_segs, masks, o_ref, group_w):
    (k,), (v,), (mask,) = k_segs, v_segs, masks
    g = group_w // HEAD_DIM
    low = _lane_half(k.shape, 0)
    k_rot = pltpu.roll(k.astype(F32), HEAD_DIM, 1).astype(BF16)
    v_rot = pltpu.roll(v.astype(F32), HEAD_DIM, 1).astype(BF16)
    wide = lambda a: jnp.concatenate([a] * (group_w // LANES), axis=1)
    k_dup = (wide(jnp.where(low, k, k_rot)), wide(jnp.where(low, k_rot, k)))
    v_dup = (wide(jnp.where(low, v, v_rot)), wide(jnp.where(low, v_rot, v)))
    for p in range(C_HEADS // g):
        ls = slice(p * group_w, (p + 1) * group_w)
        kv = (g * p) // C_GROUP
        out = _head_group(q_ref[:, ls], k_dup[kv], v_dup[kv], mask=mask,
                          sinks=[sink_ref[g * p + j] for j in range(g)])
        o_ref[:, ls] = out.astype(BF16)


def _swa_ctx_kernel(sink_ref, q_ref, k_ref, v_ref, o_ref):
    @pl.when(pl.program_id(0) >= N_CTX_TILES)
    def _():
        o_ref[...] = jnp.zeros_like(o_ref)

    @pl.when(pl.program_id(0) < N_CTX_TILES)
    def _():
        _gqa_heads(q_ref, sink_ref, [k_ref[...].astype(BF16)], [v_ref[...].astype(BF16)], [None], o_ref,
                   CTX_GROUP_W)


def _swa_lat_kernel(sink_ref, o_all_ref, q_ref, k_ref, v_ref, kc_ref, vc_ref, o_ref):
    del o_all_ref
    n = pl.program_id(1)
    n_win = 3 * WINDOW
    start = jnp.clip((n - 1) * WINDOW, 0, DEC_SEQ - n_win)
    k0 = pl.multiple_of(start, WINDOW)
    n_keys = n_win + PAST_LEN
    col = lax.broadcasted_iota(jnp.int32, (WINDOW, n_keys), 1)
    qpos = n * WINDOW + lax.broadcasted_iota(jnp.int32, (WINDOW, n_keys), 0)
    ok = (col >= n_win) | (jnp.abs(qpos - (start + col)) <= WINDOW)
    ks = jnp.concatenate([k_ref[pl.ds(k0, n_win), :], kc_ref[0]], axis=0).astype(BF16)
    vs = jnp.concatenate([v_ref[pl.ds(k0, n_win), :], vc_ref[0]], axis=0).astype(BF16)
    _gqa_heads(q_ref, sink_ref, [ks], [vs], [ok], o_ref, LAT_GROUP_W)


def _swa_attention(q, k, v, kc, vc, sink):
    nq, nkv = C_HEADS * HEAD_DIM, C_KV_HEADS * HEAD_DIM
    smem = pl.BlockSpec(memory_space=pltpu.SMEM)
    o_ctx = pl.pallas_call(
        _swa_ctx_kernel,
        out_shape=jax.ShapeDtypeStruct((N_TOK, nq), BF16),
        grid=(N_TILES,),
        in_specs=[smem, _ctx_tile_spec(nq), _ctx_tile_spec(nkv), _ctx_tile_spec(nkv)],
        out_specs=_tile_spec(nq),
        compiler_params=_cparams("arbitrary"),
        name="swa_ctx",
    )(sink, q, k, v)
    nb = DEC_SEQ // WINDOW
    ctx_seqs = N_CTX_TOK // DEC_SEQ
    lat_row = lambda b, n: (N_CTX_TOK // WINDOW + b * nb + n, 0)
    return pl.pallas_call(
        _swa_lat_kernel,
        out_shape=jax.ShapeDtypeStruct((N_TOK, nq), BF16),
        grid=(DEC_BATCH, nb),
        in_specs=[
            smem,
            pl.BlockSpec(memory_space=pl.ANY),
            pl.BlockSpec((WINDOW, nq), lat_row),
            pl.BlockSpec((DEC_SEQ, nkv), lambda b, n: (ctx_seqs + b, 0)),
            pl.BlockSpec((DEC_SEQ, nkv), lambda b, n: (ctx_seqs + b, 0)),
            pl.BlockSpec((1, PAST_LEN, nkv), lambda b, n: (b, 0, 0)),
            pl.BlockSpec((1, PAST_LEN, nkv), lambda b, n: (b, 0, 0)),
        ],
        out_specs=pl.BlockSpec((WINDOW, nq), lat_row),
        input_output_aliases={1: 0},
        compiler_params=_cparams("arbitrary", "arbitrary"),
        name="swa_lat",
    )(sink, o_ctx, q, k, v, kc, vc)


def _route_init(cnt_ref, lg_scr):
    @pl.when(pl.program_id(0) == 0)
    def _():
        cnt_ref[...] = jnp.zeros_like(cnt_ref)
        lg_scr[...] = jnp.zeros_like(lg_scr)


def _moe_input(x, g_ref, sh_ref, sc_ref, rw_ref, rb_ref, tri_ref, h_ref, route_ref, cnt_ref, lg_scr):
    h = _norm_mod(x, g_ref[...], sh_ref[0], sc_ref[0])
    h_ref[...] = _pack_bf16_pair(h[:, 0:D_MODEL // 2], h[:, D_MODEL // 2:])
    w_hi, w_lo = rw_ref[0], rw_ref[1]
    h_hi, h_lo = _split(h)
    new_logits = (_dot_nt(w_hi, h_hi) + (_dot_nt(w_hi, h_lo) + _dot_nt(w_lo, h_hi)) + rb_ref[...])[0:N_EXPERTS]
    logits = lg_scr[...]
    live = jnp.where(pl.program_id(0) > 0, 1.0, 0.0)
    row = lax.broadcasted_iota(jnp.int32, logits.shape, 0)
    cur = logits
    picked, vals, idxs = [], [], []
    for _ in range(TOP_K):
        m = jnp.max(cur, axis=0, keepdims=True)
        idx = jnp.min(jnp.where(cur == m, row, N_EXPERTS), axis=0, keepdims=True)
        sel = row == idx
        picked.append(sel)
        vals.append(m)
        idxs.append(idx)
        cur = jnp.where(sel, -jnp.inf, cur)
    exps = [jnp.exp(v - vals[0]) for v in vals]
    den = functools.reduce(lambda a, b: a + b, exps)
    chosen = functools.reduce(lambda a, b: a + b, [jnp.where(s, 1.0, 0.0) for s in picked])
    cnt = cnt_ref[...]
    before = jnp.concatenate([cnt] * (TILE // LANES), axis=1) + _dot(chosen.astype(BF16), tri_ref[...])
    cnt_ref[...] = cnt + live * jnp.sum(chosen, axis=1, keepdims=True)
    ranks = [jnp.sum(jnp.where(s, before, 0.0), axis=0, keepdims=True) for s in picked]
    fields = [i.astype(F32) for i in idxs] + [e / den for e in exps] + ranks
    out_row = lax.broadcasted_iota(jnp.int32, route_ref.shape, 0)
    route = jnp.zeros(route_ref.shape, F32)
    for j, f in enumerate(fields):
        route = jnp.where(out_row == j, f, route)
    route_ref[...] = route
    lg_scr[...] = new_logits


ROUTE_ROWS = 16
OUT_STEPS = N_TILES + 1


def _last_tile(i):
    return jnp.minimum(i, N_TILES - 1)


def _out_tile_spec(width):
    return pl.BlockSpec((TILE, width), lambda i: (_last_tile(i), 0))


def _out_mod_spec(col):
    return pl.BlockSpec((1, 1, D_MODEL), lambda i: (_last_tile(i), 0, col))


def _moe_input_specs():
    return [_full_spec((1, D_MODEL)), _out_mod_spec(3), _out_mod_spec(4),
            _full_spec((2, ROUTER_PAD, D_MODEL)), _full_spec((ROUTER_PAD, TILE)), _full_spec((TILE, TILE))]


def _mixer_out_shapes():
    shapes = (jax.ShapeDtypeStruct((N_TOK, D_MODEL), F32),
              jax.ShapeDtypeStruct((N_TOK, D_MODEL // 2), U32),
              jax.ShapeDtypeStruct((ROUTE_ROWS, N_TOK), F32),
              jax.ShapeDtypeStruct((N_EXPERTS, LANES), F32))
    specs = (_out_tile_spec(D_MODEL), _out_tile_spec(D_MODEL // 2),
             pl.BlockSpec((ROUTE_ROWS, TILE), lambda i: (0, jnp.maximum(i - 1, 0))),
             _full_spec((N_EXPERTS, LANES)))
    return shapes, specs


def _even_out_kernel(y_ref, u_ref, o_ref, x_ref, gate_ref, d_ref, gw_ref, gb_ref, w_ref,
                     g2_ref, sh_ref, sc_ref, rw_ref, rb_ref, tri_ref, xo_ref, h_ref, route_ref, cnt_ref, lg_scr):
    _route_init(cnt_ref, lg_scr)
    yy = y_ref[...] + d_ref[...] * u_ref[...]
    g = jax.nn.gelu(yy)
    a = g * jax.nn.sigmoid(_dot(g.astype(BF16), gw_ref[...]) + gb_ref[...])
    mix = _dot(a.astype(BF16), w_ref[0:A_WIDTH, :]) + _dot(o_ref[...], w_ref[A_WIDTH:, :])
    x = x_ref[...] + gate_ref[0] * mix
    xo_ref[...] = x
    _moe_input(x, g2_ref, sh_ref, sc_ref, rw_ref, rb_ref, tri_ref, h_ref, route_ref, cnt_ref, lg_scr)


def _even_out(y_t, u_t, o, x, modt, d_skip, glu_w, glu_b, w_out, g2, rw, rb, tri):
    tm_spec = pl.BlockSpec((TILE, A_WIDTH), lambda i: (0, _last_tile(i)))
    out_shape, out_specs = _mixer_out_shapes()
    return pl.pallas_call(
        _even_out_kernel,
        out_shape=out_shape,
        grid=(OUT_STEPS,),
        in_specs=[
            tm_spec, tm_spec, _out_tile_spec(B_WIDTH), _out_tile_spec(D_MODEL), _out_mod_spec(2),
            _full_spec((1, A_WIDTH)), _full_spec((A_WIDTH, A_WIDTH)), _full_spec((1, A_WIDTH)),
            _full_spec((A_WIDTH + B_WIDTH, D_MODEL)),
        ] + _moe_input_specs(),
        out_specs=out_specs,
        scratch_shapes=[pltpu.VMEM((N_EXPERTS, TILE), F32)],
        compiler_params=_cparams("arbitrary"),
        name="even_out",
    )(y_t, u_t, o, x, modt, d_skip, glu_w, glu_b, w_out, g2, modt, modt, rw, rb, tri)


def _rope(x, cos, sin):
    lane = lax.broadcasted_iota(jnp.int32, x.shape, 1)
    first = (lane % (HEAD_DIM // 2)) < (HEAD_DIM // 4)
    partner = jnp.where(first, pltpu.roll(x, LANES - HEAD_DIM // 4, 1), pltpu.roll(x, HEAD_DIM // 4, 1))
    return x * cos + partner * sin


def _odd_in_kernel(n_comb, slot, n_prev, x_ref, *refs):
    comb, refs = refs[:n_comb], refs[n_comb:]
    g_ref, sh_ref, sc_ref, w_ref, bd_ref, qn_ref, kn_ref, cos_ref, sin_ref = refs[:9]
    outs = refs[9 + n_prev:]
    x = _combined(x_ref, comb)
    if n_comb:
        outs[0][...] = x
        outs = outs[1:]
    q_ref, kr_ref, v_ref, kcache_ref, vcache_ref = outs
    i = pl.program_id(0)
    h = _norm_mod(x, g_ref[...], sh_ref[0], sc_ref[0]).astype(BF16)
    bd2 = bd_ref[...]
    lat = i >= N_CTX_TILES
    cos = jnp.where(lat, cos_ref[...], 1.0)
    sin = jnp.where(lat, sin_ref[...], 0.0)
    nq = C_HEADS * HEAD_DIM
    z = _dot(h, w_ref[...])
    for s in range(nq // LANES):
        q = _head_rms(z[:, s * LANES:(s + 1) * LANES], bd2, qn_ref[...])
        q_ref[:, s * LANES:(s + 1) * LANES] = (_rope(q, cos, sin) * ATT_SCALE).astype(BF16)
    k = _head_rms(z[:, nq:nq + LANES], bd2, kn_ref[...])
    v = z[:, nq + LANES:]
    kr_ref[...] = _rope(k, cos, sin)
    v_ref[...] = v
    _cache_store(slot, N_ODD, kcache_ref, k)
    _cache_store(slot, N_ODD, vcache_ref, v)


def _odd_in(x, comb, prev_caches, slot, g, modt, w, bd2, qn, kn, cos, sin):
    nq, nkv = C_HEADS * HEAD_DIM, C_KV_HEADS * HEAD_DIM
    lat_spec = pl.BlockSpec((TILE, LANES), lambda i: (jnp.maximum(i - N_CTX_TILES, 0) % LAT_TILES_PER_SEQ, 0))
    cache_shape, cache_spec = _cache_specs(slot, N_ODD, nkv)
    out_shape = [
        jax.ShapeDtypeStruct((N_TOK, nq), BF16),
        jax.ShapeDtypeStruct((N_TOK, nkv), F32),
        jax.ShapeDtypeStruct((N_TOK, nkv), F32),
        cache_shape, cache_shape,
    ]
    out_specs = [_tile_spec(nq), _tile_spec(nkv), _tile_spec(nkv), cache_spec, cache_spec]
    if comb:
        out_shape.insert(0, jax.ShapeDtypeStruct((N_TOK, D_MODEL), F32))
        out_specs.insert(0, _tile_spec(D_MODEL))
    n_in = 1 + len(comb) + 9
    aliases = {n_in + j: len(out_shape) - 2 + j for j in range(len(prev_caches))}
    return pl.pallas_call(
        functools.partial(_odd_in_kernel, len(comb), slot, len(prev_caches)),
        out_shape=tuple(out_shape),
        grid=(N_TILES,),
        in_specs=[_tile_spec(D_MODEL)] + (_comb_specs() if comb else []) + [
            _full_spec((1, D_MODEL)), _mod_spec(0), _mod_spec(1),
            _full_spec((D_MODEL, nq + 2 * nkv)), _full_spec((2 * LANES, LANES)),
            _full_spec((1, LANES)), _full_spec((1, LANES)), lat_spec, lat_spec,
        ] + [pl.BlockSpec(memory_space=pl.ANY)] * len(prev_caches),
        out_specs=tuple(out_specs),
        input_output_aliases=aliases,
        compiler_params=_cparams("arbitrary"),
        name="odd_in",
    )(x, *comb, g, modt, modt, w, bd2, qn, kn, cos, sin, *prev_caches)


def _rope_tables():
    nf = HEAD_DIM // 4
    inv = ROPE_BASE ** (-jnp.arange(nf, dtype=F32) / nf)
    t = jnp.arange(DEC_SEQ)
    pos = jnp.stack([t // GRID_W, t % GRID_W], axis=-1).astype(F32)
    ang = pos[:, :, None] * inv
    cos, sin = jnp.cos(ang), jnp.sin(ang)
    cos_h = jnp.stack([cos, cos], axis=2).reshape(DEC_SEQ, HEAD_DIM)
    sin_h = jnp.stack([-sin, sin], axis=2).reshape(DEC_SEQ, HEAD_DIM)
    return jnp.tile(cos_h, (1, 2)), jnp.tile(sin_h, (1, 2))


def _odd_out_kernel(o_ref, x_ref, gate_ref, w_ref, g2_ref, sh_ref, sc_ref, rw_ref, rb_ref, tri_ref,
                    xo_ref, h_ref, route_ref, cnt_ref, lg_scr):
    _route_init(cnt_ref, lg_scr)
    x = x_ref[...] + gate_ref[0] * _dot(o_ref[...], w_ref[...])
    xo_ref[...] = x
    _moe_input(x, g2_ref, sh_ref, sc_ref, rw_ref, rb_ref, tri_ref, h_ref, route_ref, cnt_ref, lg_scr)


def _odd_out(o, x, modt, w_o, g2, rw, rb, tri):
    nq = C_HEADS * HEAD_DIM
    out_shape, out_specs = _mixer_out_shapes()
    return pl.pallas_call(
        _odd_out_kernel,
        out_shape=out_shape,
        grid=(OUT_STEPS,),
        in_specs=[_out_tile_spec(nq), _out_tile_spec(D_MODEL), _out_mod_spec(2), _full_spec((nq, D_MODEL))]
        + _moe_input_specs(),
        out_specs=out_specs,
        scratch_shapes=[pltpu.VMEM((N_EXPERTS, TILE), F32)],
        compiler_params=_cparams("arbitrary"),
        name="odd_out",
    )(o, x, modt, w_o, g2, modt, modt, rw, rb, tri)


def _pack_bf16_pair(lo, hi):
    lo_bits = lax.bitcast_convert_type(lo.astype(BF16).astype(F32), U32)
    hi_bits = lax.bitcast_convert_type(hi.astype(BF16).astype(F32), U32)
    return (hi_bits & jnp.uint32(0xFFFF0000)) | (lo_bits >> 16)


def _unpack_bf16_pair(packed):
    lo = lax.bitcast_convert_type(packed << 16, F32).astype(BF16)
    hi = lax.bitcast_convert_type(packed & jnp.uint32(0xFFFF0000), F32).astype(BF16)
    return lo, hi


def _moe_expert_kernel(layer, be_ref, nb_ref, first_ref, slot_ref, nxt_ref, src_cur_ref, src_nxt_ref,
                       h_ref, wgu_hbm, bgu_ref, wd_hbm, bd_ref, o_ref, xg0, xg1, wgu_buf, wd_buf, wsem):
    i = pl.program_id(0)
    nb = nb_ref[0]
    half = D_MODEL // 2
    xg = (xg0, xg1)

    def gather(idx_ref, u, dst):
        for r in range(MOE_TM):
            dst[pl.ds(r, 1), :] = h_ref[pl.ds(idx_ref[0, 0, u * MOE_TM + r], 1), :]

    def weight_copies(e, slot):
        return (pltpu.make_async_copy(wgu_hbm.at[layer, e], wgu_buf.at[slot], wsem.at[0, slot]),
                pltpu.make_async_copy(wd_hbm.at[layer, e], wd_buf.at[slot], wsem.at[1, slot]))

    @pl.when(i == 0)
    def _():
        for cp in weight_copies(be_ref[0], 0):
            cp.start()
        gather(src_cur_ref, 0, xg0)

    for u in range(MOE_STEP_BLOCKS):
        blk = i * MOE_STEP_BLOCKS + u
        rows = pl.ds(u * MOE_TM, MOE_TM)
        x_cur, x_nxt = xg[u % 2], xg[(u + 1) % 2]

        @pl.when(blk < nb)
        def _():
            ws = slot_ref[blk]
            e = be_ref[blk]

            @pl.when(first_ref[blk] == 1)
            def _():
                for cp in weight_copies(e, ws):
                    cp.wait()

                @pl.when(nxt_ref[blk] >= 0)
                def _():
                    for cp in weight_copies(nxt_ref[blk], 1 - ws):
                        cp.start()

            x_lo, x_hi = _unpack_bf16_pair(x_cur[...])
            gu = (_dot(x_lo, wgu_buf[ws, 0:half, :].astype(BF16)) + _dot(x_hi, wgu_buf[ws, half:, :].astype(BF16))
                  + bgu_ref[0, pl.ds(e, 1), :])
            if u + 1 < MOE_STEP_BLOCKS:
                gather(src_cur_ref, u + 1, x_nxt)
            else:
                gather(src_nxt_ref, 0, x_nxt)
            g = jnp.minimum(gu[:, 0:D_EXPERT], SWIGLU_LIMIT)
            lin = jnp.clip(gu[:, D_EXPERT:], -SWIGLU_LIMIT, SWIGLU_LIMIT)
            act = g * jax.nn.sigmoid(SWIGLU_ALPHA * g) * (lin + 1.0)
            o_ref[rows, :] = _dot(act.astype(BF16), wd_buf[ws].astype(BF16)) + bd_ref[0, pl.ds(e, 1), :]

        @pl.when(blk >= nb)
        def _():
            o_ref[rows, :] = jnp.zeros((MOE_TM, D_MODEL), F32)


def _moe_experts(layer, plan, h_packed, w_gu, b_gu, w_down, b_down):
    block_e, n_used, first, slot, nxt, row_tok = plan
    half = D_MODEL // 2
    steps = MOE_BLOCKS // MOE_STEP_BLOCKS
    step_rows = MOE_STEP_BLOCKS * MOE_TM
    idx_spec = lambda f: pl.BlockSpec((1, 1, step_rows), f, memory_space=pltpu.SMEM)
    hbm = pl.BlockSpec(memory_space=pl.ANY)
    grid_spec = pltpu.PrefetchScalarGridSpec(
        num_scalar_prefetch=5,
        grid=(steps,),
        in_specs=[
            idx_spec(lambda i, *_: (i, 0, 0)),
            idx_spec(lambda i, *_: (jnp.minimum(i + 1, steps - 1), 0, 0)),
            pl.BlockSpec((N_TOK, half), lambda i, *_: (0, 0), pipeline_mode=pl.Buffered(1)),
            hbm,
            pl.BlockSpec((1, N_EXPERTS, 2 * D_EXPERT), lambda i, *_: (layer, 0, 0)),
            hbm,
            pl.BlockSpec((1, N_EXPERTS, D_MODEL), lambda i, *_: (layer, 0, 0)),
        ],
        out_specs=pl.BlockSpec((step_rows, D_MODEL), lambda i, *_: (i, 0)),
        scratch_shapes=[
            pltpu.VMEM((MOE_TM, half), U32), pltpu.VMEM((MOE_TM, half), U32),
            pltpu.VMEM((2, D_MODEL, 2 * D_EXPERT), F32), pltpu.VMEM((2, D_EXPERT, D_MODEL), F32),
            pltpu.SemaphoreType.DMA((2, 2)),
        ],
    )
    rows = row_tok.reshape(steps, 1, step_rows)
    return pl.pallas_call(
        functools.partial(_moe_expert_kernel, layer),
        out_shape=jax.ShapeDtypeStruct((MOE_ROWS, D_MODEL), F32),
        grid_spec=grid_spec,
        compiler_params=pltpu.CompilerParams(dimension_semantics=("arbitrary",), vmem_limit_bytes=MOE_VMEM_LIMIT),
        name="moe_experts",
    )(block_e, n_used, first, slot, nxt, rows, rows, h_packed, w_gu, b_gu, w_down, b_down)


def _moe_combine_kernel(x_ref, *refs):
    ctx_ref, lat_ref = refs[N_COMB:]
    x = _combined(x_ref, refs[:N_COMB])

    @pl.when(pl.program_id(0) < N_CTX_TILES // 2)
    def _():
        ctx_ref[...] = x

    @pl.when(pl.program_id(0) >= N_CTX_TILES // 2)
    def _():
        lat_ref[...] = x


def _moe_combine(x, comb):
    rows = 2 * TILE
    pair = lambda width: pl.BlockSpec((rows, width), lambda i: (i, 0))
    n_ctx = N_CTX_TILES // 2
    return pl.pallas_call(
        _moe_combine_kernel,
        out_shape=(jax.ShapeDtypeStruct((N_CTX_TOK, D_MODEL), F32), jax.ShapeDtypeStruct((N_LAT_TOK, D_MODEL), F32)),
        grid=(N_TILES // 2,),
        in_specs=[pair(D_MODEL), pl.BlockSpec((1, 1, D_MODEL), lambda i: (2 * i, 0, 5)), pair(TOP_K)]
        + [pair(D_MODEL)] * TOP_K,
        out_specs=(pl.BlockSpec((rows, D_MODEL), lambda i: (jnp.minimum(i, n_ctx - 1), 0)),
                   pl.BlockSpec((rows, D_MODEL), lambda i: (jnp.maximum(i - n_ctx, 0), 0))),
        compiler_params=_cparams("arbitrary"),
        name="moe_combine",
    )(x, *comb)


def _lut(idx, table):
    n = table.shape[0]
    hit = idx[:, None] == jnp.arange(n, dtype=idx.dtype)[None, :]
    return jnp.sum(jnp.where(hit, table[None, :], 0), axis=1)


def _moe_route(route, cnt):
    i32 = jnp.int32
    route = route.T
    gates = route[:, TOP_K:2 * TOP_K]
    flat_e = route[:, 0:TOP_K].astype(i32).reshape(N_ASG)
    rank = route[:, 2 * TOP_K:3 * TOP_K].astype(i32).reshape(N_ASG)
    asg = jnp.arange(N_ASG, dtype=i32)
    experts = jnp.arange(N_EXPERTS, dtype=i32)
    counts = cnt[:, 0].astype(i32)
    pad_counts = (counts + MOE_TM - 1) // MOE_TM * MOE_TM
    pad_end = jnp.cumsum(pad_counts)
    n_used = pad_end[-1] // MOE_TM
    dest = _lut(flat_e, pad_end - pad_counts) + rank
    fill_end = jnp.cumsum(pad_counts - counts)
    filler = jnp.arange(MOE_ROWS - N_ASG, dtype=i32)
    fill_e = jnp.sum((filler[:, None] >= fill_end[None, :]).astype(i32), axis=1)
    keys = jnp.concatenate([flat_e * ASG_STRIDE + asg, fill_e * ASG_STRIDE + (ASG_STRIDE - 1)])
    low = jnp.sort(keys) % ASG_STRIDE
    row_tok = jnp.where(low < N_ASG, low // TOP_K, 0)
    blk = jnp.arange(MOE_BLOCKS, dtype=i32)
    block_e = jnp.sum((blk[:, None] * MOE_TM >= pad_end[None, :]).astype(i32), axis=1)
    last_e = jnp.max(jnp.where(counts > 0, experts, 0))
    block_e = jnp.where(blk < n_used, jnp.minimum(block_e, N_EXPERTS - 1), last_e)
    first = jnp.concatenate([jnp.ones((1,), i32), (block_e[1:] != block_e[:-1]).astype(i32)])
    slot = (jnp.cumsum(first) - 1) % 2
    later = (experts[None, :] > experts[:, None]) & (counts[None, :] > 0)
    nxt_of = jnp.min(jnp.where(later, experts[None, :], N_EXPERTS), axis=1)
    nxt = _lut(block_e, jnp.where(nxt_of < N_EXPERTS, nxt_of, -1))
    plan = (block_e, n_used.reshape(1).astype(i32), first, slot.astype(i32), nxt.astype(i32), row_tok)
    return gates, dest.reshape(N_TOK, TOP_K), plan


def _moe(layer, h_packed, route, cnt, modt, w_gu, b_gu, w_down, b_down):
    gates, dest, plan = _moe_route(route, cnt)
    y_rows = _moe_experts(layer, plan, h_packed, w_gu, b_gu, w_down, b_down)
    return (modt, gates) + tuple(y_rows[dest[:, k]] for k in range(TOP_K))


def kernel(x_prompt, x_sample, cache_nat_k, cache_nat_v, cache_swa_k, cache_swa_v, state_ssm, c, c_ctx,
           norm1_g, norm2_g, mod_w, mod_b,
           ab_w_in, ab_w_out, ssm_lam_re, ssm_lam_im, ssm_log_dt, ssm_b_re, ssm_b_im, ssm_c_re, ssm_c_im,
           ssm_d, ssm_glu_w, ssm_glu_b, nat_qn, nat_kn, nat_rpb,
           swa_w_qkv, swa_w_o, swa_qn, swa_kn, swa_sink,
           moe_router_w, moe_router_b, moe_w_gu, moe_b_gu, moe_w_down, moe_b_down):
    x = jnp.concatenate([x_prompt.reshape(N_CTX_TOK, D_MODEL), x_sample.reshape(N_LAT_TOK, D_MODEL)], axis=0)
    cond = jnp.zeros((SUBLANES, D_MODEL), F32).at[0].set(c_ctx).at[1:1 + DEC_BATCH].set(c)
    mod = _modulation(cond, mod_w, mod_b)
    tile_row = np.concatenate([np.zeros(N_CTX_TILES, np.int32),
                               1 + np.arange(N_TILES - N_CTX_TILES, dtype=np.int32) // LAT_TILES_PER_SEQ])
    head_gain = lambda gn: jnp.tile(gn, 2).reshape(1, LANES)
    bd = np.kron(np.eye(2, dtype=np.float32), np.full((HEAD_DIM, HEAD_DIM), 1.0 / HEAD_DIM, np.float32))
    bd2 = jnp.asarray(np.concatenate([bd, bd], axis=0), BF16)
    rope_cos, rope_sin = _rope_tables()
    tri = jnp.asarray(np.triu(np.ones((TILE, TILE), np.float32), 1), BF16)
    router_w = jnp.pad(moe_router_w, ((0, 0), (0, 0), (0, ROUTER_PAD - N_EXPERTS))).transpose(0, 2, 1)
    router_w = jnp.stack(_split(router_w), axis=1)
    router_b = jnp.broadcast_to(jnp.pad(moe_router_b, ((0, 0), (0, ROUTER_PAD - N_EXPERTS)))[:, :, None],
                                (DEPTH, ROUTER_PAD, TILE))
    nkv = C_KV_HEADS * HEAD_DIM

    ssm_out = []
    nat_caches, swa_caches = (), ()
    comb = ()
    for l in range(DEPTH):
        modt = mod[l][tile_row].reshape(N_TILES, 1, N_MOD * D_MODEL)
        g1 = norm1_g[l].reshape(1, D_MODEL)
        moe_in = (norm2_g[l].reshape(1, D_MODEL), router_w[l], router_b[l], tri)
        i = l // 2
        if l % 2 == 0:
            outs = _even_in(x, comb, nat_caches, i, g1, modt, ab_w_in[i].astype(BF16), bd2,
                            head_gain(nat_qn[i]), head_gain(nat_kn[i]))
            if comb:
                x, outs = outs[0], outs[1:]
            u_t, q, k, v = outs[:4]
            nat_caches = tuple(outs[4:])
            a_b, w_bd, c_bd = _s5_params(ssm_lam_re[i], ssm_lam_im[i], ssm_log_dt[i], ssm_b_re[i], ssm_b_im[i],
                                         ssm_c_re[i], ssm_c_im[i])
            st = state_ssm[:, i].reshape(DEC_BATCH, 2, 2, SSM_LANES).transpose(1, 2, 0, 3)
            s0 = jnp.zeros((2, 2, N_TILES, SSM_LANES), F32)
            first = N_CTX_TILES + LAT_TILES_PER_SEQ * np.arange(DEC_BATCH)
            s0 = s0.at[0, :, first].set(st[0].transpose(1, 0, 2))
            s0 = s0.at[1, :, first + LAT_TILES_PER_SEQ - 1].set(st[1].transpose(1, 0, 2))
            y_t, fin = _s5_scan(u_t.reshape(TILE, N_TILES, A_WIDTH), a_b, w_bd, c_bd, s0)
            o = _nat_attention(q, k, v,
                               cache_nat_k[:, i].reshape(DEC_BATCH, PAST_LEN, B_WIDTH),
                               cache_nat_v[:, i].reshape(DEC_BATCH, PAST_LEN, B_WIDTH),
                               _nat_bias(nat_rpb[i]))
            x, h_packed, route, cnt = _even_out(y_t.reshape(TILE, N_TILES * A_WIDTH), u_t, o, x, modt,
                                            ssm_d[i].reshape(1, A_WIDTH), ssm_glu_w[i].astype(BF16),
                                            ssm_glu_b[i].reshape(1, A_WIDTH), ab_w_out[i].astype(BF16), *moe_in)
            ssm_out.append(fin[:, :, :N_CTX_TILES].transpose(2, 0, 1, 3).reshape(BATCH, 2, 2, A_GROUPS, A_STATE))
        else:
            outs = _odd_in(x, comb, swa_caches, i, g1, modt, swa_w_qkv[i].astype(BF16), bd2,
                           head_gain(swa_qn[i]), head_gain(swa_kn[i]), rope_cos, rope_sin)
            if comb:
                x, outs = outs[0], outs[1:]
            q, k_rot, v = outs[:3]
            swa_caches = tuple(outs[3:])
            o = _swa_attention(q, k_rot, v,
                               cache_swa_k[:, i].reshape(DEC_BATCH, PAST_LEN, nkv),
                               cache_swa_v[:, i].reshape(DEC_BATCH, PAST_LEN, nkv), swa_sink[i])
            x, h_packed, route, cnt = _odd_out(o, x, modt, swa_w_o[i].astype(BF16), *moe_in)
        comb = _moe(l, h_packed, route, cnt, modt, moe_w_gu, moe_b_gu, moe_w_down, moe_b_down)

    y_ctx, y_lat = _moe_combine(x, comb)
    y_prompt = y_ctx.reshape(BATCH, SEQ, D_MODEL)
    y_sample = y_lat.reshape(DEC_BATCH, DEC_SEQ, D_MODEL)
    nat_shape = (BATCH, N_EVEN, SEQ, B_HEADS, HEAD_DIM)
    swa_shape = (BATCH, N_ODD, SEQ, C_KV_HEADS, HEAD_DIM)
    return (y_prompt, y_sample,
            nat_caches[0].reshape(nat_shape), nat_caches[1].reshape(nat_shape),
            swa_caches[0].reshape(swa_shape), swa_caches[1].reshape(swa_shape),
            jnp.stack(ssm_out, axis=1))
```

```python
import functools

import jax
import jax.numpy as jnp
import numpy as np
from jax import lax
from jax.experimental import pallas as pl
from jax.experimental.pallas import tpu as pltpu

F32 = jnp.float32
BF16 = jnp.bfloat16
U32 = jnp.uint32

D_MODEL = 1024
BATCH = 32
SEQ = 256
DEPTH = 4
DEC_BATCH = 2
DEC_SEQ = 1024
PAST_LEN = 256
GRID_W = 64
HEAD_DIM = 64
N_MOD = 6
N_EVEN = (DEPTH + 1) // 2
N_ODD = DEPTH // 2
A_WIDTH = 512
A_GROUP_CH = 16
A_GROUPS = 32
A_STATE = 64
B_HEADS = 8
B_WIDTH = 512
WIN_R_MAX = 8
WIN_C = 16
C_HEADS = 16
C_KV_HEADS = 2
C_GROUP = 8
WINDOW = 128
ROPE_BASE = 10000.0
N_EXPERTS = 32
TOP_K = 4
D_EXPERT = 1024
SWIGLU_LIMIT = 7.0
SWIGLU_ALPHA = 1.702
EPS = 1e-6
NEG_INF = -1e30

LANES = 128
SUBLANES = 8
TILE = 256
N_CTX_TOK = BATCH * SEQ
N_LAT_TOK = DEC_BATCH * DEC_SEQ
N_TOK = N_CTX_TOK + N_LAT_TOK
N_TILES = N_TOK // TILE
N_CTX_TILES = N_CTX_TOK // TILE
LAT_TILES_PER_SEQ = DEC_SEQ // TILE
SSM_LANES = A_GROUPS * A_STATE
SSM_SLABS = A_WIDTH // LANES
SSM_SLAB_STATES = SSM_LANES // SSM_SLABS
SCAN_ROWS = 8
SCAN_GROUPS = N_TILES // SCAN_ROWS
SCAN_CHUNK = 128
SCAN_LANE_PARTS = 2
MOE_TM = 256
MOE_STEP_BLOCKS = 4
N_ASG = N_TOK * TOP_K
ASG_STRIDE = 1 << 16
MOE_BLOCKS = N_ASG // MOE_TM + N_EXPERTS
MOE_ROWS = MOE_BLOCKS * MOE_TM
ROUTER_PAD = LANES
VMEM_LIMIT = 56 * 1024 * 1024
MOE_VMEM_LIMIT = 60 * 1024 * 1024
ATT_SCALE = HEAD_DIM ** -0.5
LAT_GROUP_W = 4 * HEAD_DIM
CTX_GROUP_W = 2 * HEAD_DIM


def _cparams(*sem):
    return pltpu.CompilerParams(dimension_semantics=sem, vmem_limit_bytes=VMEM_LIMIT)


def _dot(a, b):
    return jnp.dot(a, b, preferred_element_type=F32)


def _dot_nt(a, b):
    return lax.dot_general(a, b, (((1,), (1,)), ((), ())), preferred_element_type=F32)


def _split(a):
    hi = a.astype(BF16)
    lo = (a - hi.astype(F32)).astype(BF16)
    return hi, lo


def _dot3(a, b):
    a_hi, a_lo = _split(a)
    b_hi, b_lo = _split(b)
    return _dot(a_hi, b_hi) + (_dot(a_hi, b_lo) + _dot(a_lo, b_hi))


def _silu(x):
    return x * jax.nn.sigmoid(x)


def _norm_mod(x, g, shift, scale):
    y = x * lax.rsqrt(jnp.mean(x * x, axis=-1, keepdims=True) + EPS)
    return (y * g) * (1.0 + scale) + shift


def _head_rms(x, bd2, gain):
    sq_hi, sq_lo = _split(x * x)
    ms = _dot(jnp.concatenate([sq_hi, sq_lo], axis=1), bd2)
    return x * lax.rsqrt(ms + EPS) * gain


def _combined(x_ref, comb_refs):
    x = x_ref[...]
    if not comb_refs:
        return x
    gmod_ref, gates_ref = comb_refs[0], comb_refs[1]
    gates = gates_ref[...]
    acc = gates[:, 0:1] * comb_refs[2][...]
    for k in range(1, TOP_K):
        acc = acc + gates[:, k:k + 1] * comb_refs[2 + k][...]
    return x + gmod_ref[0] * acc


def _comb_specs():
    return [_mod_spec(5), _tile_spec(TOP_K)] + [_tile_spec(D_MODEL)] * TOP_K


N_COMB = 2 + TOP_K


def _lane_half(shape, half):
    lane = lax.broadcasted_iota(jnp.int32, shape, len(shape) - 1)
    return (lane < HEAD_DIM) if half == 0 else (lane >= HEAD_DIM)


def _mod_kernel(cond_ref, w_ref, b_ref, o_ref):
    o_ref[0] = _dot3(_silu(cond_ref[...]), w_ref[0]) + b_ref[0]


def _modulation(cond, mod_w, mod_b):
    nc = 2
    cw = N_MOD * D_MODEL // nc
    return pl.pallas_call(
        _mod_kernel,
        out_shape=jax.ShapeDtypeStruct((DEPTH, SUBLANES, N_MOD * D_MODEL), F32),
        grid=(DEPTH, nc),
        in_specs=[
            pl.BlockSpec((SUBLANES, D_MODEL), lambda l, c: (0, 0)),
            pl.BlockSpec((1, D_MODEL, cw), lambda l, c: (l, 0, c)),
            pl.BlockSpec((1, 1, cw), lambda l, c: (l, 0, c)),
        ],
        out_specs=pl.BlockSpec((1, SUBLANES, cw), lambda l, c: (l, 0, c)),
        compiler_params=_cparams("arbitrary", "arbitrary"),
        name="modulation",
    )(cond, mod_w, mod_b.reshape(DEPTH, 1, N_MOD * D_MODEL))


def _mod_spec(col):
    return pl.BlockSpec((1, 1, D_MODEL), lambda i: (i, 0, col))


def _tile_spec(width):
    return pl.BlockSpec((TILE, width), lambda i: (i, 0))


def _ctx_tile_spec(width):
    return pl.BlockSpec((TILE, width), lambda i: (jnp.minimum(i, N_CTX_TILES - 1), 0))


def _full_spec(shape):
    nd = len(shape)
    return pl.BlockSpec(shape, lambda i: (0,) * nd)


def _cache_store(slot, n_slots, ref, value):
    @pl.when(pl.program_id(0) < N_CTX_TILES)
    def _():
        if slot == 0:
            ref[0, 0] = value
            for other in range(1, n_slots):
                ref[0, other] = jnp.zeros_like(value)
        else:
            ref[0, 0] = value


def _cache_specs(slot, n_slots, width):
    shape = jax.ShapeDtypeStruct((BATCH, n_slots, SEQ, width), F32)
    seq = lambda i: jnp.minimum(i, N_CTX_TILES - 1)
    if slot == 0:
        return shape, pl.BlockSpec((1, n_slots, SEQ, width), lambda i: (seq(i), 0, 0, 0))
    return shape, pl.BlockSpec((1, 1, SEQ, width), lambda i: (seq(i), slot, 0, 0))


def _even_in_kernel(n_comb, slot, n_prev, x_ref, *refs):
    comb, refs = refs[:n_comb], refs[n_comb:]
    g_ref, sh_ref, sc_ref, w_ref, bd_ref, qn_ref, kn_ref = refs[:7]
    outs = refs[7 + n_prev:]
    x = _combined(x_ref, comb)
    if n_comb:
        outs[0][...] = x
        outs = outs[1:]
    u_ref, q_ref, k_ref, v_ref, kc_ref, vc_ref = outs
    h = _norm_mod(x, g_ref[...], sh_ref[0], sc_ref[0]).astype(BF16)
    bd2 = bd_ref[...]
    z = _dot(h, w_ref[...])
    u_ref[...] = z[:, 0:A_WIDTH]
    ks = []
    for s in range(B_WIDTH // LANES):
        lo = A_WIDTH + s * LANES
        q = _head_rms(z[:, lo:lo + LANES], bd2, qn_ref[...])
        q_ref[:, s * LANES:(s + 1) * LANES] = (q * ATT_SCALE).astype(BF16)
        lo = A_WIDTH + B_WIDTH + s * LANES
        ks.append(_head_rms(z[:, lo:lo + LANES], bd2, kn_ref[...]))
    k = jnp.concatenate(ks, axis=1)
    v = z[:, A_WIDTH + 2 * B_WIDTH:]
    k_ref[...] = k
    v_ref[...] = v
    _cache_store(slot, N_EVEN, kc_ref, k)
    _cache_store(slot, N_EVEN, vc_ref, v)


def _even_in(x, comb, prev_caches, slot, g, modt, w, bd2, qn, kn):
    n_out = A_WIDTH + 3 * B_WIDTH
    cache_shape, cache_spec = _cache_specs(slot, N_EVEN, B_WIDTH)
    out_shape = [
        jax.ShapeDtypeStruct((TILE, N_TILES * A_WIDTH), F32),
        jax.ShapeDtypeStruct((N_TOK, B_WIDTH), BF16),
        jax.ShapeDtypeStruct((N_TOK, B_WIDTH), F32),
        jax.ShapeDtypeStruct((N_TOK, B_WIDTH), F32),
        cache_shape, cache_shape,
    ]
    out_specs = [pl.BlockSpec((TILE, A_WIDTH), lambda i: (0, i)),
                 _tile_spec(B_WIDTH), _tile_spec(B_WIDTH), _tile_spec(B_WIDTH), cache_spec, cache_spec]
    if comb:
        out_shape.insert(0, jax.ShapeDtypeStruct((N_TOK, D_MODEL), F32))
        out_specs.insert(0, _tile_spec(D_MODEL))
    n_in = 1 + len(comb) + 7
    aliases = {n_in + j: len(out_shape) - 2 + j for j in range(len(prev_caches))}
    return pl.pallas_call(
        functools.partial(_even_in_kernel, len(comb), slot, len(prev_caches)),
        out_shape=tuple(out_shape),
        grid=(N_TILES,),
        in_specs=[_tile_spec(D_MODEL)] + (_comb_specs() if comb else []) + [
            _full_spec((1, D_MODEL)), _mod_spec(0), _mod_spec(1),
            _full_spec((D_MODEL, n_out)), _full_spec((2 * LANES, LANES)),
            _full_spec((1, LANES)), _full_spec((1, LANES)),
        ] + [pl.BlockSpec(memory_space=pl.ANY)] * len(prev_caches),
        out_specs=tuple(out_specs),
        input_output_aliases=aliases,
        compiler_params=_cparams("arbitrary"),
        name="even_in",
    )(x, *comb, g, modt, modt, w, bd2, qn, kn, *prev_caches)


def _cmul(ar, ai, br, bi):
    return ar * br - ai * bi, ar * bi + ai * br


def _s5_kernel(u_ref, a_ref, w_ref, c_ref, s0_ref, y_ref, fin_ref, xr, xi, st_r, st_i):
    grp = pl.program_id(0)
    drn = pl.program_id(1)
    n_chunks = TILE // SCAN_CHUNK
    rows = SCAN_CHUNK * SCAN_ROWS
    part = SSM_LANES // SCAN_LANE_PARTS

    def run(store):
        @pl.loop(0, n_chunks)
        def _(c):
            cc = jnp.where(drn == 0, c, n_chunks - 1 - c)
            t0 = pl.multiple_of(cc * SCAN_CHUNK, SCAN_CHUNK)
            uu = u_ref[pl.ds(t0, SCAN_CHUNK), :, :]
            for s in range(SSM_SLABS):
                us = uu[:, :, s * LANES:(s + 1) * LANES].reshape(rows, LANES).astype(BF16)
                cols = slice(s * SSM_SLAB_STATES, (s + 1) * SSM_SLAB_STATES)
                xr[:, cols] = _dot(us, w_ref[0, 0, s])
                xi[:, cols] = _dot(us, w_ref[0, 1, s])
            for p in range(SCAN_LANE_PARTS):
                ls = slice(p * part, (p + 1) * part)
                ar = a_ref[0, 0, :, ls]
                ai = a_ref[0, 1, :, ls]

                def step(j, carry):
                    sr, si = carry
                    tt = jnp.where(drn == 0, j, SCAN_CHUNK - 1 - j)
                    r0 = pl.multiple_of(tt * SCAN_ROWS, SCAN_ROWS)
                    nr = ar * sr - ai * si + xr[pl.ds(r0, SCAN_ROWS), ls]
                    ni = ar * si + ai * sr + xi[pl.ds(r0, SCAN_ROWS), ls]
                    if store:
                        xr[pl.ds(r0, SCAN_ROWS), ls] = nr
                        xi[pl.ds(r0, SCAN_ROWS), ls] = ni
                    return nr, ni

                sr, si = lax.fori_loop(0, SCAN_CHUNK, step, (st_r[:, ls], st_i[:, ls]), unroll=4)
                st_r[:, ls] = sr
                st_i[:, ls] = si
            if store:
                for s in range(SSM_SLABS):
                    cols = slice(s * SSM_SLAB_STATES, (s + 1) * SSM_SLAB_STATES)
                    ys = (_dot(xr[:, cols].astype(BF16), c_ref[0, 0, s])
                          - _dot(xi[:, cols].astype(BF16), c_ref[0, 1, s]))
                    ys = ys.reshape(SCAN_CHUNK, SCAN_ROWS, LANES)
                    lanes = slice(s * LANES, (s + 1) * LANES)

                    @pl.when(drn == 0)
                    def _():
                        y_ref[pl.ds(t0, SCAN_CHUNK), :, lanes] = ys

                    @pl.when(drn != 0)
                    def _():
                        y_ref[pl.ds(t0, SCAN_CHUNK), :, lanes] += ys

    st_r[...] = s0_ref[0, 0]
    st_i[...] = s0_ref[0, 1]

    @pl.when(grp == SCAN_GROUPS - 1)
    def _():
        st_r[...] = jnp.zeros_like(st_r)
        st_i[...] = jnp.zeros_like(st_i)
        run(False)
        pr, pi = a_ref[0, 0], a_ref[0, 1]
        for _ in range(8):
            pr, pi = _cmul(pr, pi, pr, pi)
        fr, fi = st_r[...], st_i[...]
        s0r, s0i = s0_ref[0, 0], s0_ref[0, 1]
        row = lax.broadcasted_iota(jnp.int32, (SCAN_ROWS, SSM_LANES), 0)
        quarter = row % LAT_TILES_PER_SEQ
        fwd = drn == 0
        keep = quarter != jnp.where(fwd, 0, LAT_TILES_PER_SEQ - 1)
        ir, ii = s0r, s0i
        for _ in range(LAT_TILES_PER_SEQ - 1):
            nr, ni = _cmul(pr, pi, ir, ii)
            nr, ni = nr + fr, ni + fi
            nr = jnp.where(fwd, pltpu.roll(nr, 1, 0), pltpu.roll(nr, SCAN_ROWS - 1, 0))
            ni = jnp.where(fwd, pltpu.roll(ni, 1, 0), pltpu.roll(ni, SCAN_ROWS - 1, 0))
            ir = s0r + jnp.where(keep, nr, 0.0)
            ii = s0i + jnp.where(keep, ni, 0.0)
        st_r[...] = ir
        st_i[...] = ii

    run(True)
    fin_ref[0, 0] = st_r[...]
    fin_ref[0, 1] = st_i[...]


def _s5_scan(u_tb, a_b, w_bd, c_bd, s0):
    rows = SCAN_CHUNK * SCAN_ROWS
    return pl.pallas_call(
        _s5_kernel,
        out_shape=(
            jax.ShapeDtypeStruct((TILE, N_TILES, A_WIDTH), F32),
            jax.ShapeDtypeStruct((2, 2, N_TILES, SSM_LANES), F32),
        ),
        grid=(SCAN_GROUPS, 2),
        in_specs=[
            pl.BlockSpec((TILE, SCAN_ROWS, A_WIDTH), lambda g, d: (0, g, 0)),
            pl.BlockSpec((1, 2, SCAN_ROWS, SSM_LANES), lambda g, d: (d, 0, 0, 0)),
            pl.BlockSpec((1, 2, SSM_SLABS, LANES, SSM_SLAB_STATES), lambda g, d: (d, 0, 0, 0, 0)),
            pl.BlockSpec((1, 2, SSM_SLABS, SSM_SLAB_STATES, LANES), lambda g, d: (d, 0, 0, 0, 0)),
            pl.BlockSpec((1, 2, SCAN_ROWS, SSM_LANES), lambda g, d: (d, 0, g, 0)),
        ],
        out_specs=(
            pl.BlockSpec((TILE, SCAN_ROWS, A_WIDTH), lambda g, d: (0, g, 0)),
            pl.BlockSpec((1, 2, SCAN_ROWS, SSM_LANES), lambda g, d: (d, 0, g, 0)),
        ),
        scratch_shapes=[
            pltpu.VMEM((rows, SSM_LANES), F32), pltpu.VMEM((rows, SSM_LANES), F32),
            pltpu.VMEM((SCAN_ROWS, SSM_LANES), F32), pltpu.VMEM((SCAN_ROWS, SSM_LANES), F32),
        ],
        compiler_params=_cparams("arbitrary", "arbitrary"),
        name="s5_scan",
    )(u_tb, a_b, w_bd, c_bd, s0)


def _s5_params(lam_re, lam_im, log_dt, b_re, b_im, c_re, c_im):
    dt = jnp.exp(log_dt)[..., None]
    mag = jnp.exp(lam_re * dt)
    ab_re, ab_im = mag * jnp.cos(lam_im * dt), mag * jnp.sin(lam_im * dt)
    den = lam_re * lam_re + lam_im * lam_im
    nr, ni = ab_re - 1.0, ab_im
    f_re = (nr * lam_re + ni * lam_im) / den
    f_im = (ni * lam_re - nr * lam_im) / den
    bb_re = f_re[..., None] * b_re - f_im[..., None] * b_im
    bb_im = f_re[..., None] * b_im + f_im[..., None] * b_re
    a_b = jnp.stack([ab_re, ab_im], axis=1).reshape(2, 2, 1, SSM_LANES)
    a_b = jnp.broadcast_to(a_b, (2, 2, SCAN_ROWS, SSM_LANES))
    gps = A_GROUPS // SSM_SLABS
    eye = jnp.eye(gps, dtype=F32)

    def in_bd(bb):
        bb = bb.reshape(2, SSM_SLABS, gps, A_STATE, A_GROUP_CH)
        m = jnp.einsum('dsgph,gk->dsghkp', bb, eye)
        return m.reshape(2, SSM_SLABS, LANES, SSM_SLAB_STATES)

    def out_bd(cc):
        cc = cc.reshape(2, SSM_SLABS, gps, A_GROUP_CH, A_STATE)
        m = jnp.einsum('dsghp,gk->dsgpkh', cc, eye)
        return m.reshape(2, SSM_SLABS, SSM_SLAB_STATES, LANES)

    w_bd = jnp.stack([in_bd(bb_re), in_bd(bb_im)], axis=1).astype(BF16)
    c_bd = jnp.stack([out_bd(c_re), out_bd(c_im)], axis=1).astype(BF16)
    return a_b, w_bd, c_bd


def _softmax_pv(scores, values, sink=None):
    m = functools.reduce(jnp.maximum, [jnp.max(s, axis=-1, keepdims=True) for s in scores])
    if sink is not None:
        m = jnp.maximum(m, sink)
    den = None
    acc = None
    for s, v in zip(scores, values):
        e = jnp.exp(s - m)
        d = jnp.sum(e, axis=-1, keepdims=True)
        o = _dot(e.astype(BF16), v)
        den = d if den is None else den + d
        acc = o if acc is None else acc + o
    if sink is not None:
        den = den + jnp.exp(sink - m)
    return acc / den


def _head_group(qs, ks, vs, bias=None, mask=None, sinks=None):
    m, g = qs.shape[0], qs.shape[1] // HEAD_DIM
    head = lax.broadcasted_iota(jnp.int32, qs.shape, 1) // HEAD_DIM
    zero = jnp.zeros_like(qs)
    q_all = jnp.concatenate([jnp.where(head == j, qs, zero) for j in range(g)], axis=0)
    s = _dot_nt(q_all, ks)
    if bias is not None:
        s = s + bias
    if mask is not None:
        s = jnp.where(jnp.concatenate([mask] * g, axis=0), s, NEG_INF)
    sink = None
    if sinks is not None:
        row = lax.broadcasted_iota(jnp.int32, (g * m, 1), 0)
        sink = sinks[0]
        for j in range(1, g):
            sink = jnp.where(row >= j * m, sinks[j], sink)
    o_all = _softmax_pv([s], [vs], sink=sink)
    out = o_all[0:m]
    for j in range(1, g):
        out = jnp.where(head == j, o_all[j * m:(j + 1) * m], out)
    return out


def _nat_ctx_kernel(q_ref, k_ref, v_ref, o_ref):
    @pl.when(pl.program_id(0) >= N_CTX_TILES)
    def _():
        o_ref[...] = jnp.zeros_like(o_ref)

    @pl.when(pl.program_id(0) < N_CTX_TILES)
    def _():
        for p in range(B_WIDTH // CTX_GROUP_W):
            ls = slice(p * CTX_GROUP_W, (p + 1) * CTX_GROUP_W)
            out = _head_group(q_ref[:, ls], k_ref[:, ls].astype(BF16), v_ref[:, ls].astype(BF16))
            o_ref[:, ls] = out.astype(BF16)


def _nat_lat_kernel(o_all_ref, q_ref, k_ref, v_ref, kc_ref, vc_ref, bias_ref, o_ref):
    del o_all_ref
    r = pl.program_id(0)
    wr = WIN_R_MAX
    rs = jnp.clip(r - wr // 2, 0, DEC_SEQ // GRID_W - wr)
    k0 = pl.multiple_of(rs * GRID_W, GRID_W)
    n_nb = wr * GRID_W
    g = LAT_GROUP_W // HEAD_DIM
    for p in range(B_WIDTH // LAT_GROUP_W):
        ls = slice(p * LAT_GROUP_W, (p + 1) * LAT_GROUP_W)
        for b in range(DEC_BATCH):
            ks = jnp.concatenate([k_ref[b, pl.ds(k0, n_nb), ls], kc_ref[b, :, ls]], axis=0).astype(BF16)
            vs = jnp.concatenate([v_ref[b, pl.ds(k0, n_nb), ls], vc_ref[b, :, ls]], axis=0).astype(BF16)
            bias = jnp.concatenate([bias_ref[g * p + j, 0] for j in range(g)], axis=0)
            o_ref[b, :, ls] = _head_group(q_ref[b, :, ls], ks, vs, bias=bias).astype(BF16)


def _nat_attention(q, k, v, kc, vc, bias):
    o_ctx = pl.pallas_call(
        _nat_ctx_kernel,
        out_shape=jax.ShapeDtypeStruct((N_TOK, B_WIDTH), BF16),
        grid=(N_TILES,),
        in_specs=[_ctx_tile_spec(B_WIDTH)] * 3,
        out_specs=_tile_spec(B_WIDTH),
        compiler_params=_cparams("arbitrary"),
        name="nat_ctx",
    )(q, k, v)
    rows = DEC_SEQ // GRID_W
    n_seq = N_TOK // DEC_SEQ
    lat = N_CTX_TOK // DEC_SEQ // DEC_BATCH
    as_seq = lambda a: a.reshape(n_seq, DEC_SEQ, B_WIDTH)
    row_spec = pl.BlockSpec((DEC_BATCH, GRID_W, B_WIDTH), lambda r: (lat, r, 0))
    seq_spec = pl.BlockSpec((DEC_BATCH, DEC_SEQ, B_WIDTH), lambda r: (lat, 0, 0))
    ctx_spec = pl.BlockSpec((DEC_BATCH, PAST_LEN, B_WIDTH), lambda r: (0, 0, 0))

    def bias_idx(r):
        return (0, r - jnp.clip(r - WIN_R_MAX // 2, 0, rows - WIN_R_MAX), 0, 0)

    o = pl.pallas_call(
        _nat_lat_kernel,
        out_shape=jax.ShapeDtypeStruct((n_seq, DEC_SEQ, B_WIDTH), BF16),
        grid=(rows,),
        in_specs=[
            pl.BlockSpec(memory_space=pl.ANY), row_spec, seq_spec, seq_spec, ctx_spec, ctx_spec,
            pl.BlockSpec((B_HEADS, 1, GRID_W, WIN_R_MAX * GRID_W + PAST_LEN), bias_idx),
        ],
        out_specs=row_spec,
        input_output_aliases={0: 0},
        compiler_params=_cparams("arbitrary"),
        name="nat_lat",
    )(as_seq(o_ctx), as_seq(q), as_seq(k), as_seq(v), kc, vc, bias)
    return o.reshape(N_TOK, B_WIDTH)


def _nat_bias(rpb):
    qc = np.arange(GRID_W)
    kc = np.arange(GRID_W)
    cs = np.clip(qc - WIN_C // 2, 0, GRID_W - WIN_C)
    ok = (kc[None, :] >= cs[:, None]) & (kc[None, :] < cs[:, None] + WIN_C)
    dc = np.clip(kc[None, :] - qc[:, None] + (WIN_C - 1), 0, 2 * WIN_C - 2)
    pick = (dc[:, :, None] == np.arange(2 * WIN_C - 1)).astype(np.float32)
    t = jnp.einsum('hrc,qkc->hrqk', rpb.astype(F32), pick, precision=lax.Precision.HIGHEST)
    t = jnp.where(ok[None, None], t, NEG_INF)
    per_d = [t[:, WIN_R_MAX - 1 - d:2 * WIN_R_MAX - 1 - d].transpose(0, 2, 1, 3)
             .reshape(B_HEADS, GRID_W, WIN_R_MAX * GRID_W) for d in range(WIN_R_MAX)]
    nb = jnp.stack(per_d, axis=1)
    return jnp.concatenate([nb, jnp.zeros((B_HEADS, WIN_R_MAX, GRID_W, PAST_LEN), F32)], axis=-1)


def _gqa_heads(q_ref, sink_ref, k_segs, v_segs, masks, o_ref, group_w):
    (k,), (v,), (mask,) = k_segs, v_segs, masks
    g = group_w // HEAD_DIM
    low = _lane_half(k.shape, 0)
    k_rot = pltpu.roll(k.astype(F32), HEAD_DIM, 1).astype(BF16)
    v_rot = pltpu.roll(v.astype(F32), HEAD_DIM, 1).astype(BF16)
    wide = lambda a: jnp.concatenate([a] * (group_w // LANES), axis=1)
    k_dup = (wide(jnp.where(low, k, k_rot)), wide(jnp.where(low, k_rot, k)))
    v_dup = (wide(jnp.where(low, v, v_rot)), wide(jnp.where(low, v_rot, v)))
    for p in range(C_HEADS // g):
        ls = slice(p * group_w, (p + 1) * group_w)
        kv = (g * p) // C_GROUP
        out = _head_group(q_ref[:, ls], k_dup[kv], v_dup[kv], mask=mask,
                          sinks=[sink_ref[g * p + j] for j in range(g)])
        o_ref[:, ls] = out.astype(BF16)


def _swa_ctx_kernel(sink_ref, q_ref, k_ref, v_ref, o_ref):
    @pl.when(pl.program_id(0) >= N_CTX_TILES)
    def _():
        o_ref[...] = jnp.zeros_like(o_ref)

    @pl.when(pl.program_id(0) < N_CTX_TILES)
    def _():
        _gqa_heads(q_ref, sink_ref, [k_ref[...].astype(BF16)], [v_ref[...].astype(BF16)], [None], o_ref,
                   CTX_GROUP_W)


def _swa_lat_kernel(sink_ref, o_all_ref, q_ref, k_ref, v_ref, kc_ref, vc_ref, o_ref):
    del o_all_ref
    n = pl.program_id(1)
    n_win = 3 * WINDOW
    start = jnp.clip((n - 1) * WINDOW, 0, DEC_SEQ - n_win)
    k0 = pl.multiple_of(start, WINDOW)
    n_keys = n_win + PAST_LEN
    col = lax.broadcasted_iota(jnp.int32, (WINDOW, n_keys), 1)
    qpos = n * WINDOW + lax.broadcasted_iota(jnp.int32, (WINDOW, n_keys), 0)
    ok = (col >= n_win) | (jnp.abs(qpos - (start + col)) <= WINDOW)
    ks = jnp.concatenate([k_ref[pl.ds(k0, n_win), :], kc_ref[0]], axis=0).astype(BF16)
    vs = jnp.concatenate([v_ref[pl.ds(k0, n_win), :], vc_ref[0]], axis=0).astype(BF16)
    _gqa_heads(q_ref, sink_ref, [ks], [vs], [ok], o_ref, LAT_GROUP_W)


def _swa_attention(q, k, v, kc, vc, sink):
    nq, nkv = C_HEADS * HEAD_DIM, C_KV_HEADS * HEAD_DIM
    smem = pl.BlockSpec(memory_space=pltpu.SMEM)
    o_ctx = pl.pallas_call(
        _swa_ctx_kernel,
        out_shape=jax.ShapeDtypeStruct((N_TOK, nq), BF16),
        grid=(N_TILES,),
        in_specs=[smem, _ctx_tile_spec(nq), _ctx_tile_spec(nkv), _ctx_tile_spec(nkv)],
        out_specs=_tile_spec(nq),
        compiler_params=_cparams("arbitrary"),
        name="swa_ctx",
    )(sink, q, k, v)
    nb = DEC_SEQ // WINDOW
    ctx_seqs = N_CTX_TOK // DEC_SEQ
    lat_row = lambda b, n: (N_CTX_TOK // WINDOW + b * nb + n, 0)
    return pl.pallas_call(
        _swa_lat_kernel,
        out_shape=jax.ShapeDtypeStruct((N_TOK, nq), BF16),
        grid=(DEC_BATCH, nb),
        in_specs=[
            smem,
            pl.BlockSpec(memory_space=pl.ANY),
            pl.BlockSpec((WINDOW, nq), lat_row),
            pl.BlockSpec((DEC_SEQ, nkv), lambda b, n: (ctx_seqs + b, 0)),
            pl.BlockSpec((DEC_SEQ, nkv), lambda b, n: (ctx_seqs + b, 0)),
            pl.BlockSpec((1, PAST_LEN, nkv), lambda b, n: (b, 0, 0)),
            pl.BlockSpec((1, PAST_LEN, nkv), lambda b, n: (b, 0, 0)),
        ],
        out_specs=pl.BlockSpec((WINDOW, nq), lat_row),
        input_output_aliases={1: 0},
        compiler_params=_cparams("arbitrary", "arbitrary"),
        name="swa_lat",
    )(sink, o_ctx, q, k, v, kc, vc)


def _route_init(cnt_ref, lg_scr):
    @pl.when(pl.program_id(0) == 0)
    def _():
        cnt_ref[...] = jnp.zeros_like(cnt_ref)
        lg_scr[...] = jnp.zeros_like(lg_scr)


def _moe_input(x, g_ref, sh_ref, sc_ref, rw_ref, rb_ref, tri_ref, h_ref, route_ref, cnt_ref, lg_scr):
    h = _norm_mod(x, g_ref[...], sh_ref[0], sc_ref[0])
    h_ref[...] = _pack_bf16_pair(h[:, 0:D_MODEL // 2], h[:, D_MODEL // 2:])
    w_hi, w_lo = rw_ref[0], rw_ref[1]
    h_hi, h_lo = _split(h)
    new_logits = (_dot_nt(w_hi, h_hi) + (_dot_nt(w_hi, h_lo) + _dot_nt(w_lo, h_hi)) + rb_ref[...])[0:N_EXPERTS]
    logits = lg_scr[...]
    live = jnp.where(pl.program_id(0) > 0, 1.0, 0.0)
    row = lax.broadcasted_iota(jnp.int32, logits.shape, 0)
    cur = logits
    picked, vals, idxs = [], [], []
    for _ in range(TOP_K):
        m = jnp.max(cur, axis=0, keepdims=True)
        idx = jnp.min(jnp.where(cur == m, row, N_EXPERTS), axis=0, keepdims=True)
        sel = row == idx
        picked.append(sel)
        vals.append(m)
        idxs.append(idx)
        cur = jnp.where(sel, -jnp.inf, cur)
    exps = [jnp.exp(v - vals[0]) for v in vals]
    den = functools.reduce(lambda a, b: a + b, exps)
    chosen = functools.reduce(lambda a, b: a + b, [jnp.where(s, 1.0, 0.0) for s in picked])
    cnt = cnt_ref[...]
    before = jnp.concatenate([cnt] * (TILE // LANES), axis=1) + _dot(chosen.astype(BF16), tri_ref[...])
    cnt_ref[...] = cnt + live * jnp.sum(chosen, axis=1, keepdims=True)
    ranks = [jnp.sum(jnp.where(s, before, 0.0), axis=0, keepdims=True) for s in picked]
    fields = [i.astype(F32) for i in idxs] + [e / den for e in exps] + ranks
    out_row = lax.broadcasted_iota(jnp.int32, route_ref.shape, 0)
    route = jnp.zeros(route_ref.shape, F32)
    for j, f in enumerate(fields):
        route = jnp.where(out_row == j, f, route)
    route_ref[...] = route
    lg_scr[...] = new_logits


ROUTE_ROWS = 16
OUT_STEPS = N_TILES + 1


def _last_tile(i):
    return jnp.minimum(i, N_TILES - 1)


def _out_tile_spec(width):
    return pl.BlockSpec((TILE, width), lambda i: (_last_tile(i), 0))


def _out_mod_spec(col):
    return pl.BlockSpec((1, 1, D_MODEL), lambda i: (_last_tile(i), 0, col))


def _moe_input_specs():
    return [_full_spec((1, D_MODEL)), _out_mod_spec(3), _out_mod_spec(4),
            _full_spec((2, ROUTER_PAD, D_MODEL)), _full_spec((ROUTER_PAD, TILE)), _full_spec((TILE, TILE))]


def _mixer_out_shapes():
    shapes = (jax.ShapeDtypeStruct((N_TOK, D_MODEL), F32),
              jax.ShapeDtypeStruct((N_TOK, D_MODEL // 2), U32),
              jax.ShapeDtypeStruct((ROUTE_ROWS, N_TOK), F32),
              jax.ShapeDtypeStruct((N_EXPERTS, LANES), F32))
    specs = (_out_tile_spec(D_MODEL), _out_tile_spec(D_MODEL // 2),
             pl.BlockSpec((ROUTE_ROWS, TILE), lambda i: (0, jnp.maximum(i - 1, 0))),
             _full_spec((N_EXPERTS, LANES)))
    return shapes, specs


def _even_out_kernel(y_ref, u_ref, o_ref, x_ref, gate_ref, d_ref, gw_ref, gb_ref, w_ref,
                     g2_ref, sh_ref, sc_ref, rw_ref, rb_ref, tri_ref, xo_ref, h_ref, route_ref, cnt_ref, lg_scr):
    _route_init(cnt_ref, lg_scr)
    yy = y_ref[...] + d_ref[...] * u_ref[...]
    g = jax.nn.gelu(yy)
    a = g * jax.nn.sigmoid(_dot(g.astype(BF16), gw_ref[...]) + gb_ref[...])
    mix = _dot(a.astype(BF16), w_ref[0:A_WIDTH, :]) + _dot(o_ref[...], w_ref[A_WIDTH:, :])
    x = x_ref[...] + gate_ref[0] * mix
    xo_ref[...] = x
    _moe_input(x, g2_ref, sh_ref, sc_ref, rw_ref, rb_ref, tri_ref, h_ref, route_ref, cnt_ref, lg_scr)


def _even_out(y_t, u_t, o, x, modt, d_skip, glu_w, glu_b, w_out, g2, rw, rb, tri):
    tm_spec = pl.BlockSpec((TILE, A_WIDTH), lambda i: (0, _last_tile(i)))
    out_shape, out_specs = _mixer_out_shapes()
    return pl.pallas_call(
        _even_out_kernel,
        out_shape=out_shape,
        grid=(OUT_STEPS,),
        in_specs=[
            tm_spec, tm_spec, _out_tile_spec(B_WIDTH), _out_tile_spec(D_MODEL), _out_mod_spec(2),
            _full_spec((1, A_WIDTH)), _full_spec((A_WIDTH, A_WIDTH)), _full_spec((1, A_WIDTH)),
            _full_spec((A_WIDTH + B_WIDTH, D_MODEL)),
        ] + _moe_input_specs(),
        out_specs=out_specs,
        scratch_shapes=[pltpu.VMEM((N_EXPERTS, TILE), F32)],
        compiler_params=_cparams("arbitrary"),
        name="even_out",
    )(y_t, u_t, o, x, modt, d_skip, glu_w, glu_b, w_out, g2, modt, modt, rw, rb, tri)


def _rope(x, cos, sin):
    lane = lax.broadcasted_iota(jnp.int32, x.shape, 1)
    first = (lane % (HEAD_DIM // 2)) < (HEAD_DIM // 4)
    partner = jnp.where(first, pltpu.roll(x, LANES - HEAD_DIM // 4, 1), pltpu.roll(x, HEAD_DIM // 4, 1))
    return x * cos + partner * sin


def _odd_in_kernel(n_comb, slot, n_prev, x_ref, *refs):
    comb, refs = refs[:n_comb], refs[n_comb:]
    g_ref, sh_ref, sc_ref, w_ref, bd_ref, qn_ref, kn_ref, cos_ref, sin_ref = refs[:9]
    outs = refs[9 + n_prev:]
    x = _combined(x_ref, comb)
    if n_comb:
        outs[0][...] = x
        outs = outs[1:]
    q_ref, kr_ref, v_ref, kcache_ref, vcache_ref = outs
    i = pl.program_id(0)
    h = _norm_mod(x, g_ref[...], sh_ref[0], sc_ref[0]).astype(BF16)
    bd2 = bd_ref[...]
    lat = i >= N_CTX_TILES
    cos = jnp.where(lat, cos_ref[...], 1.0)
    sin = jnp.where(lat, sin_ref[...], 0.0)
    nq = C_HEADS * HEAD_DIM
    z = _dot(h, w_ref[...])
    for s in range(nq // LANES):
        q = _head_rms(z[:, s * LANES:(s + 1) * LANES], bd2, qn_ref[...])
        q_ref[:, s * LANES:(s + 1) * LANES] = (_rope(q, cos, sin) * ATT_SCALE).astype(BF16)
    k = _head_rms(z[:, nq:nq + LANES], bd2, kn_ref[...])
    v = z[:, nq + LANES:]
    kr_ref[...] = _rope(k, cos, sin)
    v_ref[...] = v
    _cache_store(slot, N_ODD, kcache_ref, k)
    _cache_store(slot, N_ODD, vcache_ref, v)


def _odd_in(x, comb, prev_caches, slot, g, modt, w, bd2, qn, kn, cos, sin):
    nq, nkv = C_HEADS * HEAD_DIM, C_KV_HEADS * HEAD_DIM
    lat_spec = pl.BlockSpec((TILE, LANES), lambda i: (jnp.maximum(i - N_CTX_TILES, 0) % LAT_TILES_PER_SEQ, 0))
    cache_shape, cache_spec = _cache_specs(slot, N_ODD, nkv)
    out_shape = [
        jax.ShapeDtypeStruct((N_TOK, nq), BF16),
        jax.ShapeDtypeStruct((N_TOK, nkv), F32),
        jax.ShapeDtypeStruct((N_TOK, nkv), F32),
        cache_shape, cache_shape,
    ]
    out_specs = [_tile_spec(nq), _tile_spec(nkv), _tile_spec(nkv), cache_spec, cache_spec]
    if comb:
        out_shape.insert(0, jax.ShapeDtypeStruct((N_TOK, D_MODEL), F32))
        out_specs.insert(0, _tile_spec(D_MODEL))
    n_in = 1 + len(comb) + 9
    aliases = {n_in + j: len(out_shape) - 2 + j for j in range(len(prev_caches))}
    return pl.pallas_call(
        functools.partial(_odd_in_kernel, len(comb), slot, len(prev_caches)),
        out_shape=tuple(out_shape),
        grid=(N_TILES,),
        in_specs=[_tile_spec(D_MODEL)] + (_comb_specs() if comb else []) + [
            _full_spec((1, D_MODEL)), _mod_spec(0), _mod_spec(1),
            _full_spec((D_MODEL, nq + 2 * nkv)), _full_spec((2 * LANES, LANES)),
            _full_spec((1, LANES)), _full_spec((1, LANES)), lat_spec, lat_spec,
        ] + [pl.BlockSpec(memory_space=pl.ANY)] * len(prev_caches),
        out_specs=tuple(out_specs),
        input_output_aliases=aliases,
        compiler_params=_cparams("arbitrary"),
        name="odd_in",
    )(x, *comb, g, modt, modt, w, bd2, qn, kn, cos, sin, *prev_caches)


def _rope_tables():
    nf = HEAD_DIM // 4
    inv = ROPE_BASE ** (-jnp.arange(nf, dtype=F32) / nf)
    t = jnp.arange(DEC_SEQ)
    pos = jnp.stack([t // GRID_W, t % GRID_W], axis=-1).astype(F32)
    ang = pos[:, :, None] * inv
    cos, sin = jnp.cos(ang), jnp.sin(ang)
    cos_h = jnp.stack([cos, cos], axis=2).reshape(DEC_SEQ, HEAD_DIM)
    sin_h = jnp.stack([-sin, sin], axis=2).reshape(DEC_SEQ, HEAD_DIM)
    return jnp.tile(cos_h, (1, 2)), jnp.tile(sin_h, (1, 2))


def _odd_out_kernel(o_ref, x_ref, gate_ref, w_ref, g2_ref, sh_ref, sc_ref, rw_ref, rb_ref, tri_ref,
                    xo_ref, h_ref, route_ref, cnt_ref, lg_scr):
    _route_init(cnt_ref, lg_scr)
    x = x_ref[...] + gate_ref[0] * _dot(o_ref[...], w_ref[...])
    xo_ref[...] = x
    _moe_input(x, g2_ref, sh_ref, sc_ref, rw_ref, rb_ref, tri_ref, h_ref, route_ref, cnt_ref, lg_scr)


def _odd_out(o, x, modt, w_o, g2, rw, rb, tri):
    nq = C_HEADS * HEAD_DIM
    out_shape, out_specs = _mixer_out_shapes()
    return pl.pallas_call(
        _odd_out_kernel,
        out_shape=out_shape,
        grid=(OUT_STEPS,),
        in_specs=[_out_tile_spec(nq), _out_tile_spec(D_MODEL), _out_mod_spec(2), _full_spec((nq, D_MODEL))]
        + _moe_input_specs(),
        out_specs=out_specs,
        scratch_shapes=[pltpu.VMEM((N_EXPERTS, TILE), F32)],
        compiler_params=_cparams("arbitrary"),
        name="odd_out",
    )(o, x, modt, w_o, g2, modt, modt, rw, rb, tri)


def _pack_bf16_pair(lo, hi):
    lo_bits = lax.bitcast_convert_type(lo.astype(BF16).astype(F32), U32)
    hi_bits = lax.bitcast_convert_type(hi.astype(BF16).astype(F32), U32)
    return (hi_bits & jnp.uint32(0xFFFF0000)) | (lo_bits >> 16)


def _unpack_bf16_pair(packed):
    lo = lax.bitcast_convert_type(packed << 16, F32).astype(BF16)
    hi = lax.bitcast_convert_type(packed & jnp.uint32(0xFFFF0000), F32).astype(BF16)
    return lo, hi


def _moe_expert_kernel(layer, be_ref, nb_ref, first_ref, slot_ref, nxt_ref, src_cur_ref, src_nxt_ref,
                       h_ref, wgu_hbm, bgu_ref, wd_hbm, bd_ref, o_ref, xg0, xg1, wgu_buf, wd_buf, wsem):
    i = pl.program_id(0)
    nb = nb_ref[0]
    half = D_MODEL // 2
    xg = (xg0, xg1)

    def gather(idx_ref, u, dst):
        for r in range(MOE_TM):
            dst[pl.ds(r, 1), :] = h_ref[pl.ds(idx_ref[0, 0, u * MOE_TM + r], 1), :]

    def weight_copies(e, slot):
        return (pltpu.make_async_copy(wgu_hbm.at[layer, e], wgu_buf.at[slot], wsem.at[0, slot]),
                pltpu.make_async_copy(wd_hbm.at[layer, e], wd_buf.at[slot], wsem.at[1, slot]))

    @pl.when(i == 0)
    def _():
        for cp in weight_copies(be_ref[0], 0):
            cp.start()
        gather(src_cur_ref, 0, xg0)

    for u in range(MOE_STEP_BLOCKS):
        blk = i * MOE_STEP_BLOCKS + u
        rows = pl.ds(u * MOE_TM, MOE_TM)
        x_cur, x_nxt = xg[u % 2], xg[(u + 1) % 2]

        @pl.when(blk < nb)
        def _():
            ws = slot_ref[blk]
            e = be_ref[blk]

            @pl.when(first_ref[blk] == 1)
            def _():
                for cp in weight_copies(e, ws):
                    cp.wait()

                @pl.when(nxt_ref[blk] >= 0)
                def _():
                    for cp in weight_copies(nxt_ref[blk], 1 - ws):
                        cp.start()

            x_lo, x_hi = _unpack_bf16_pair(x_cur[...])
            acc = None
            for c in range(2):
                cw = D_EXPERT // 2
                gc = slice(c * cw, (c + 1) * cw)
                lc = slice(D_EXPERT + c * cw, D_EXPERT + (c + 1) * cw)
                g = (_dot(x_lo, wgu_buf[ws, 0:half, gc].astype(BF16)) + _dot(x_hi, wgu_buf[ws, half:, gc].astype(BF16))
                     + bgu_ref[0, pl.ds(e, 1), gc])
                lin = (_dot(x_lo, wgu_buf[ws, 0:half, lc].astype(BF16)) + _dot(x_hi, wgu_buf[ws, half:, lc].astype(BF16))
                       + bgu_ref[0, pl.ds(e, 1), lc])
                if c == 0:
                    if u + 1 < MOE_STEP_BLOCKS:
                        gather(src_cur_ref, u + 1, x_nxt)
                    else:
                        gather(src_nxt_ref, 0, x_nxt)
                g = jnp.minimum(g, SWIGLU_LIMIT)
                lin = jnp.clip(lin, -SWIGLU_LIMIT, SWIGLU_LIMIT)
                act = g * jax.nn.sigmoid(SWIGLU_ALPHA * g) * (lin + 1.0)
                part = _dot(act.astype(BF16), wd_buf[ws, gc, :].astype(BF16))
                acc = part if acc is None else acc + part
            o_ref[rows, :] = acc + bd_ref[0, pl.ds(e, 1), :]

        @pl.when(blk >= nb)
        def _():
            o_ref[rows, :] = jnp.zeros((MOE_TM, D_MODEL), F32)


def _moe_experts(layer, plan, h_packed, w_gu, b_gu, w_down, b_down):
    block_e, n_used, first, slot, nxt, row_tok = plan
    half = D_MODEL // 2
    steps = MOE_BLOCKS // MOE_STEP_BLOCKS
    step_rows = MOE_STEP_BLOCKS * MOE_TM
    idx_spec = lambda f: pl.BlockSpec((1, 1, step_rows), f, memory_space=pltpu.SMEM)
    hbm = pl.BlockSpec(memory_space=pl.ANY)
    grid_spec = pltpu.PrefetchScalarGridSpec(
        num_scalar_prefetch=5,
        grid=(steps,),
        in_specs=[
            idx_spec(lambda i, *_: (i, 0, 0)),
            idx_spec(lambda i, *_: (jnp.minimum(i + 1, steps - 1), 0, 0)),
            pl.BlockSpec((N_TOK, half), lambda i, *_: (0, 0), pipeline_mode=pl.Buffered(1)),
            hbm,
            pl.BlockSpec((1, N_EXPERTS, 2 * D_EXPERT), lambda i, *_: (layer, 0, 0)),
            hbm,
            pl.BlockSpec((1, N_EXPERTS, D_MODEL), lambda i, *_: (layer, 0, 0)),
        ],
        out_specs=pl.BlockSpec((step_rows, D_MODEL), lambda i, *_: (i, 0)),
        scratch_shapes=[
            pltpu.VMEM((MOE_TM, half), U32), pltpu.VMEM((MOE_TM, half), U32),
            pltpu.VMEM((2, D_MODEL, 2 * D_EXPERT), F32), pltpu.VMEM((2, D_EXPERT, D_MODEL), F32),
            pltpu.SemaphoreType.DMA((2, 2)),
        ],
    )
    rows = row_tok.reshape(steps, 1, step_rows)
    return pl.pallas_call(
        functools.partial(_moe_expert_kernel, layer),
        out_shape=jax.ShapeDtypeStruct((MOE_ROWS, D_MODEL), F32),
        grid_spec=grid_spec,
        compiler_params=pltpu.CompilerParams(dimension_semantics=("arbitrary",), vmem_limit_bytes=MOE_VMEM_LIMIT),
        name="moe_experts",
    )(block_e, n_used, first, slot, nxt, rows, rows, h_packed, w_gu, b_gu, w_down, b_down)


def _moe_combine_kernel(x_ref, *refs):
    ctx_ref, lat_ref = refs[N_COMB:]
    x = _combined(x_ref, refs[:N_COMB])

    @pl.when(pl.program_id(0) < N_CTX_TILES // 2)
    def _():
        ctx_ref[...] = x

    @pl.when(pl.program_id(0) >= N_CTX_TILES // 2)
    def _():
        lat_ref[...] = x


def _moe_combine(x, comb):
    rows = 2 * TILE
    pair = lambda width: pl.BlockSpec((rows, width), lambda i: (i, 0))
    n_ctx = N_CTX_TILES // 2
    return pl.pallas_call(
        _moe_combine_kernel,
        out_shape=(jax.ShapeDtypeStruct((N_CTX_TOK, D_MODEL), F32), jax.ShapeDtypeStruct((N_LAT_TOK, D_MODEL), F32)),
        grid=(N_TILES // 2,),
        in_specs=[pair(D_MODEL), pl.BlockSpec((1, 1, D_MODEL), lambda i: (2 * i, 0, 5)), pair(TOP_K)]
        + [pair(D_MODEL)] * TOP_K,
        out_specs=(pl.BlockSpec((rows, D_MODEL), lambda i: (jnp.minimum(i, n_ctx - 1), 0)),
                   pl.BlockSpec((rows, D_MODEL), lambda i: (jnp.maximum(i - n_ctx, 0), 0))),
        compiler_params=_cparams("arbitrary"),
        name="moe_combine",
    )(x, *comb)


def _lut(idx, table):
    n = table.shape[0]
    hit = idx[:, None] == jnp.arange(n, dtype=idx.dtype)[None, :]
    return jnp.sum(jnp.where(hit, table[None, :], 0), axis=1)


def _moe_route(route, cnt):
    i32 = jnp.int32
    route = route.T
    gates = route[:, TOP_K:2 * TOP_K]
    flat_e = route[:, 0:TOP_K].astype(i32).reshape(N_ASG)
    rank = route[:, 2 * TOP_K:3 * TOP_K].astype(i32).reshape(N_ASG)
    asg = jnp.arange(N_ASG, dtype=i32)
    experts = jnp.arange(N_EXPERTS, dtype=i32)
    counts = cnt[:, 0].astype(i32)
    pad_counts = (counts + MOE_TM - 1) // MOE_TM * MOE_TM
    pad_end = jnp.cumsum(pad_counts)
    n_used = pad_end[-1] // MOE_TM
    dest = _lut(flat_e, pad_end - pad_counts) + rank
    fill_end = jnp.cumsum(pad_counts - counts)
    filler = jnp.arange(MOE_ROWS - N_ASG, dtype=i32)
    fill_e = jnp.sum((filler[:, None] >= fill_end[None, :]).astype(i32), axis=1)
    keys = jnp.concatenate([flat_e * ASG_STRIDE + asg, fill_e * ASG_STRIDE + (ASG_STRIDE - 1)])
    low = jnp.sort(keys) % ASG_STRIDE
    row_tok = jnp.where(low < N_ASG, low // TOP_K, 0)
    blk = jnp.arange(MOE_BLOCKS, dtype=i32)
    block_e = jnp.sum((blk[:, None] * MOE_TM >= pad_end[None, :]).astype(i32), axis=1)
    last_e = jnp.max(jnp.where(counts > 0, experts, 0))
    block_e = jnp.where(blk < n_used, jnp.minimum(block_e, N_EXPERTS - 1), last_e)
    first = jnp.concatenate([jnp.ones((1,), i32), (block_e[1:] != block_e[:-1]).astype(i32)])
    slot = (jnp.cumsum(first) - 1) % 2
    later = (experts[None, :] > experts[:, None]) & (counts[None, :] > 0)
    nxt_of = jnp.min(jnp.where(later, experts[None, :], N_EXPERTS), axis=1)
    nxt = _lut(block_e, jnp.where(nxt_of < N_EXPERTS, nxt_of, -1))
    plan = (block_e, n_used.reshape(1).astype(i32), first, slot.astype(i32), nxt.astype(i32), row_tok)
    return gates, dest.reshape(N_TOK, TOP_K), plan


def _moe(layer, h_packed, route, cnt, modt, w_gu, b_gu, w_down, b_down):
    gates, dest, plan = _moe_route(route, cnt)
    y_rows = _moe_experts(layer, plan, h_packed, w_gu, b_gu, w_down, b_down)
    return (modt, gates) + tuple(y_rows[dest[:, k]] for k in range(TOP_K))


def kernel(x_prompt, x_sample, cache_nat_k, cache_nat_v, cache_swa_k, cache_swa_v, state_ssm, c, c_ctx,
           norm1_g, norm2_g, mod_w, mod_b,
           ab_w_in, ab_w_out, ssm_lam_re, ssm_lam_im, ssm_log_dt, ssm_b_re, ssm_b_im, ssm_c_re, ssm_c_im,
           ssm_d, ssm_glu_w, ssm_glu_b, nat_qn, nat_kn, nat_rpb,
           swa_w_qkv, swa_w_o, swa_qn, swa_kn, swa_sink,
           moe_router_w, moe_router_b, moe_w_gu, moe_b_gu, moe_w_down, moe_b_down):
    x = jnp.concatenate([x_prompt.reshape(N_CTX_TOK, D_MODEL), x_sample.reshape(N_LAT_TOK, D_MODEL)], axis=0)
    cond = jnp.zeros((SUBLANES, D_MODEL), F32).at[0].set(c_ctx).at[1:1 + DEC_BATCH].set(c)
    mod = _modulation(cond, mod_w, mod_b)
    tile_row = np.concatenate([np.zeros(N_CTX_TILES, np.int32),
                               1 + np.arange(N_TILES - N_CTX_TILES, dtype=np.int32) // LAT_TILES_PER_SEQ])
    head_gain = lambda gn: jnp.tile(gn, 2).reshape(1, LANES)
    bd = np.kron(np.eye(2, dtype=np.float32), np.full((HEAD_DIM, HEAD_DIM), 1.0 / HEAD_DIM, np.float32))
    bd2 = jnp.asarray(np.concatenate([bd, bd], axis=0), BF16)
    rope_cos, rope_sin = _rope_tables()
    tri = jnp.asarray(np.triu(np.ones((TILE, TILE), np.float32), 1), BF16)
    router_w = jnp.pad(moe_router_w, ((0, 0), (0, 0), (0, ROUTER_PAD - N_EXPERTS))).transpose(0, 2, 1)
    router_w = jnp.stack(_split(router_w), axis=1)
    router_b = jnp.broadcast_to(jnp.pad(moe_router_b, ((0, 0), (0, ROUTER_PAD - N_EXPERTS)))[:, :, None],
                                (DEPTH, ROUTER_PAD, TILE))
    nkv = C_KV_HEADS * HEAD_DIM

    ssm_out = []
    nat_caches, swa_caches = (), ()
    comb = ()
    for l in range(DEPTH):
        modt = mod[l][tile_row].reshape(N_TILES, 1, N_MOD * D_MODEL)
        g1 = norm1_g[l].reshape(1, D_MODEL)
        moe_in = (norm2_g[l].reshape(1, D_MODEL), router_w[l], router_b[l], tri)
        i = l // 2
        if l % 2 == 0:
            outs = _even_in(x, comb, nat_caches, i, g1, modt, ab_w_in[i].astype(BF16), bd2,
                            head_gain(nat_qn[i]), head_gain(nat_kn[i]))
            if comb:
                x, outs = outs[0], outs[1:]
            u_t, q, k, v = outs[:4]
            nat_caches = tuple(outs[4:])
            a_b, w_bd, c_bd = _s5_params(ssm_lam_re[i], ssm_lam_im[i], ssm_log_dt[i], ssm_b_re[i], ssm_b_im[i],
                                         ssm_c_re[i], ssm_c_im[i])
            st = state_ssm[:, i].reshape(DEC_BATCH, 2, 2, SSM_LANES).transpose(1, 2, 0, 3)
            s0 = jnp.zeros((2, 2, N_TILES, SSM_LANES), F32)
            first = N_CTX_TILES + LAT_TILES_PER_SEQ * np.arange(DEC_BATCH)
            s0 = s0.at[0, :, first].set(st[0].transpose(1, 0, 2))
            s0 = s0.at[1, :, first + LAT_TILES_PER_SEQ - 1].set(st[1].transpose(1, 0, 2))
            y_t, fin = _s5_scan(u_t.reshape(TILE, N_TILES, A_WIDTH), a_b, w_bd, c_bd, s0)
            o = _nat_attention(q, k, v,
                               cache_nat_k[:, i].reshape(DEC_BATCH, PAST_LEN, B_WIDTH),
                               cache_nat_v[:, i].reshape(DEC_BATCH, PAST_LEN, B_WIDTH),
                               _nat_bias(nat_rpb[i]))
            x, h_packed, route, cnt = _even_out(y_t.reshape(TILE, N_TILES * A_WIDTH), u_t, o, x, modt,
                                            ssm_d[i].reshape(1, A_WIDTH), ssm_glu_w[i].astype(BF16),
                                            ssm_glu_b[i].reshape(1, A_WIDTH), ab_w_out[i].astype(BF16), *moe_in)
            ssm_out.append(fin[:, :, :N_CTX_TILES].transpose(2, 0, 1, 3).reshape(BATCH, 2, 2, A_GROUPS, A_STATE))
        else:
            outs = _odd_in(x, comb, swa_caches, i, g1, modt, swa_w_qkv[i].astype(BF16), bd2,
                           head_gain(swa_qn[i]), head_gain(swa_kn[i]), rope_cos, rope_sin)
            if comb:
                x, outs = outs[0], outs[1:]
            q, k_rot, v = outs[:3]
            swa_caches = tuple(outs[3:])
            o = _swa_attention(q, k_rot, v,
                               cache_swa_k[:, i].reshape(DEC_BATCH, PAST_LEN, nkv),
                               cache_swa_v[:, i].reshape(DEC_BATCH, PAST_LEN, nkv), swa_sink[i])
            x, h_packed, route, cnt = _odd_out(o, x, modt, swa_w_o[i].astype(BF16), *moe_in)
        comb = _moe(l, h_packed, route, cnt, modt, moe_w_gu, moe_b_gu, moe_w_down, moe_b_down)

    y_ctx, y_lat = _moe_combine(x, comb)
    y_prompt = y_ctx.reshape(BATCH, SEQ, D_MODEL)
    y_sample = y_lat.reshape(DEC_BATCH, DEC_SEQ, D_MODEL)
    nat_shape = (BATCH, N_EVEN, SEQ, B_HEADS, HEAD_DIM)
    swa_shape = (BATCH, N_ODD, SEQ, C_KV_HEADS, HEAD_DIM)
    return (y_prompt, y_sample,
            nat_caches[0].reshape(nat_shape), nat_caches[1].reshape(nat_shape),
            swa_caches[0].reshape(swa_shape), swa_caches[1].reshape(swa_shape),
            jnp.stack(ssm_out, axis=1))
```
